```python
import jax
import jax.numpy as jnp
from jax import lax
import numpy as np

D_MODEL = 2048
BATCH = 2
SEQ = 4096
DEPTH = 2

Q_BLOCK = 128
LN_EPS = 1e-5
RMS_EPS = 1e-6
ALPHA = (2 * DEPTH) ** 0.25
BETA = (8 * DEPTH) ** -0.25

MLA_HEADS = 8
MLA_Q_LORA = 512
MLA_KV_LORA = 256
MLA_NOPE = 128
MLA_ROPE = 64
MLA_V = 128
ROPE_THETA = 10000.0

FOX_HEADS = 8
FOX_HEAD_DIM = 128

SWA_Q_HEADS = 16
SWA_KV_HEADS = 2
SWA_HEAD_DIM = 64
SWA_WINDOW = 128

MOBA_HEADS = 8
MOBA_HEAD_DIM = 128
MOBA_BLOCK = 256
MOBA_TOPK = 3
MOBA_Q_CHUNK = 64

N_EXPERTS = 32
N_GROUPS = 8
EXPERTS_PER_GROUP = N_EXPERTS // N_GROUPS
MOE_TOPK = 2
EXPERT_FF = 512

EVEN_IN_SIZES = (MLA_Q_LORA, MLA_KV_LORA, MLA_ROPE, FOX_HEADS * FOX_HEAD_DIM, FOX_HEADS * FOX_HEAD_DIM, FOX_HEADS * FOX_HEAD_DIM, FOX_HEADS)
EVEN_IN = sum(EVEN_IN_SIZES)
EVEN_MIX = MLA_HEADS * MLA_V + FOX_HEADS * FOX_HEAD_DIM
ODD_IN_SIZES = (SWA_Q_HEADS * SWA_HEAD_DIM, SWA_KV_HEADS * SWA_HEAD_DIM, SWA_KV_HEADS * SWA_HEAD_DIM, MOBA_HEADS * MOBA_HEAD_DIM, MOBA_HEADS * MOBA_HEAD_DIM, MOBA_HEADS * MOBA_HEAD_DIM)
ODD_IN = sum(ODD_IN_SIZES)
ODD_MIX = SWA_Q_HEADS * SWA_HEAD_DIM + MOBA_HEADS * MOBA_HEAD_DIM

kernel_name = "hybrid_mla_fox_swa_moba_moe_deepnorm"


def _split(t, sizes):
    offsets = [int(o) for o in np.cumsum(sizes)[:-1]]
    return jnp.split(t, offsets, axis=-1)


def layer_norm(x, g, b):
    xf = x.astype(jnp.float32)
    mu = jnp.mean(xf, axis=-1, keepdims=True)
    var = jnp.mean(jnp.square(xf - mu), axis=-1, keepdims=True)
    y = (xf - mu) * lax.rsqrt(var + LN_EPS) * g.astype(jnp.float32) + b.astype(jnp.float32)
    return y.astype(x.dtype)


def rms_norm(x, g):
    xf = x.astype(jnp.float32)
    y = xf * lax.rsqrt(jnp.mean(jnp.square(xf), axis=-1, keepdims=True) + RMS_EPS) * g.astype(jnp.float32)
    return y.astype(x.dtype)


def alibi_slopes(n):
    return jnp.asarray(2.0 ** (-8.0 * np.arange(1, n + 1) / n), dtype=jnp.float32)


def rope(t, pos):
    half = t.shape[-1] // 2
    inv_freq = ROPE_THETA ** (-jnp.arange(half, dtype=jnp.float32) / half)
    ang = pos.astype(jnp.float32)[:, None] * inv_freq[None, :]
    cos, sin = jnp.cos(ang), jnp.sin(ang)
    t1 = t[..., :half].astype(jnp.float32)
    t2 = t[..., half:].astype(jnp.float32)
    return jnp.concatenate([t1 * cos - t2 * sin, t1 * sin + t2 * cos], axis=-1).astype(t.dtype)


def to_heads(t, n_heads, head_dim):
    b, s, _ = t.shape
    return t.reshape(b, s, n_heads, head_dim).transpose(0, 2, 1, 3)


def from_heads(t):
    b, h, s, d = t.shape
    return t.transpose(0, 2, 1, 3).reshape(b, s, h * d)


def blocked_causal_attention(q, k, v, scale, q_bias=None, k_bias=None):
    b, h, s, _ = q.shape
    kpos = jnp.arange(s)

    def one_block(i):
        start = i * Q_BLOCK
        qb = lax.dynamic_slice_in_dim(q, start, Q_BLOCK, axis=2)
        sc = jnp.einsum("bhqd,bhkd->bhqk", qb, k, preferred_element_type=jnp.float32) * scale
        if q_bias is not None:
            qbb = lax.dynamic_slice_in_dim(q_bias, start, Q_BLOCK, axis=2)
            sc = sc + qbb[..., :, None] - k_bias[..., None, :]
        qpos = start + jnp.arange(Q_BLOCK)
        sc = jnp.where(kpos[None, :] <= qpos[:, None], sc, -jnp.inf)
        p = jax.nn.softmax(sc, axis=-1).astype(v.dtype)
        return jnp.einsum("bhqk,bhkd->bhqd", p, v)

    out = lax.map(one_block, jnp.arange(s // Q_BLOCK))
    return jnp.moveaxis(out, 0, 2).reshape(b, h, s, -1)


def mla_attention(c_q, c_kv, k_rope_raw, q_norm, w_q_up, kv_norm, w_kv_up, pos):
    b, s, _ = c_q.shape
    q = to_heads(rms_norm(c_q, q_norm) @ w_q_up, MLA_HEADS, MLA_NOPE + MLA_ROPE)
    q = jnp.concatenate([q[..., :MLA_NOPE], rope(q[..., MLA_NOPE:], pos)], axis=-1)
    kv = to_heads(rms_norm(c_kv, kv_norm) @ w_kv_up, MLA_HEADS, MLA_NOPE + MLA_V)
    k_pe = jnp.broadcast_to(rope(k_rope_raw, pos)[:, None], (b, MLA_HEADS, s, MLA_ROPE))
    k = jnp.concatenate([kv[..., :MLA_NOPE], k_pe], axis=-1)
    v = kv[..., MLA_NOPE:]
    out = blocked_causal_attention(q, k, v, (MLA_NOPE + MLA_ROPE) ** -0.5)
    return from_heads(out)


def fox_attention(q, k, v, f_logit, forget_bias):
    q = to_heads(q, FOX_HEADS, FOX_HEAD_DIM)
    k = to_heads(k, FOX_HEADS, FOX_HEAD_DIM)
    v = to_heads(v, FOX_HEADS, FOX_HEAD_DIM)
    log_f = jax.nn.log_sigmoid(f_logit.astype(jnp.float32) + forget_bias.astype(jnp.float32))
    cum = jnp.cumsum(log_f, axis=1).transpose(0, 2, 1)
    out = blocked_causal_attention(q, k, v, FOX_HEAD_DIM ** -0.5, cum, cum)
    return from_heads(out)


def swa_sink_attention(q, k, v, sinks):
    b, s, _ = q.shape
    w, g, dh = SWA_WINDOW, SWA_Q_HEADS // SWA_KV_HEADS, SWA_HEAD_DIM
    nb = s // w
    q = q.reshape(b, nb, w, SWA_KV_HEADS, g, dh)
    k = k.reshape(b, nb, w, SWA_KV_HEADS, dh)
    v = v.reshape(b, nb, w, SWA_KV_HEADS, dh)
    pad = ((0, 0), (1, 0), (0, 0), (0, 0), (0, 0))
    kb = jnp.concatenate([jnp.pad(k, pad)[:, :-1], k], axis=2)
    vb = jnp.concatenate([jnp.pad(v, pad)[:, :-1], v], axis=2)
    sc = jnp.einsum("bnqhgd,bnkhd->bhgnqk", q, kb, preferred_element_type=jnp.float32) * dh ** -0.5
    dist = (jnp.arange(w)[:, None] + w - jnp.arange(2 * w)[None, :])
    abs_k = (jnp.arange(nb) * w - w)[:, None] + jnp.arange(2 * w)[None, :]
    valid = (dist >= 0)[None] & (dist < w)[None] & (abs_k >= 0)[:, None, :]
    slopes = alibi_slopes(SWA_Q_HEADS).reshape(SWA_KV_HEADS, g)
    sc = sc - slopes[None, :, :, None, None, None] * dist.astype(jnp.float32)
    sc = jnp.where(valid, sc, -jnp.inf)
    sink = sinks.astype(jnp.float32).reshape(SWA_KV_HEADS, g)[None, :, :, None, None, None]
    m = jnp.maximum(jnp.max(sc, axis=-1, keepdims=True), sink)
    p = jnp.exp(sc - m)
    p = p / (jnp.sum(p, axis=-1, keepdims=True) + jnp.exp(sink - m))
    out = jnp.einsum("bhgnqk,bnkhd->bnqhgd", p.astype(v.dtype), vb)
    return out.reshape(b, s, SWA_Q_HEADS * dh)


def moba_attention(q, k, v):
    b, s, _ = q.shape
    h, dh, bs, qc_len = MOBA_HEADS, MOBA_HEAD_DIM, MOBA_BLOCK, MOBA_Q_CHUNK
    n_kb = -(-s // bs)
    top = min(MOBA_TOPK, n_kb)
    pad = n_kb * bs - s
    q = to_heads(q, h, dh)
    padw = ((0, 0), (0, 0), (0, pad), (0, 0))
    k = jnp.pad(to_heads(k, h, dh), padw)
    v = jnp.pad(to_heads(v, h, dh), padw)
    k_blk = k.reshape(b, h, n_kb, bs, dh)
    v_blk = v.reshape(b, h, n_kb, bs, dh)
    k_mean = jnp.mean(k_blk.astype(jnp.float32), axis=3)
    slopes = alibi_slopes(h)
    scale = dh ** -0.5
    bi = jnp.arange(b)[:, None, None, None]
    hi = jnp.arange(h)[None, :, None, None]

    def one_chunk(ci):
        start = ci * qc_len
        blk = start // bs
        qc = lax.dynamic_slice_in_dim(q, start, qc_len, axis=2)
        qpos = (start + jnp.arange(qc_len)).astype(jnp.float32)
        gate = jnp.einsum("bhqd,bhnd->bhqn", qc.astype(jnp.float32), k_mean)
        gate = jnp.where(jnp.arange(n_kb) < blk, gate, -jnp.inf)
        _, idx = lax.top_k(gate, top)
        sel_ok = idx < blk
        k_sel = k_blk[bi, hi, idx]
        v_sel = v_blk[bi, hi, idx]
        s_sel = jnp.einsum("bhqd,bhqnkd->bhqnk", qc, k_sel, preferred_element_type=jnp.float32) * scale
        kpos_sel = (idx[..., None] * bs + jnp.arange(bs)).astype(jnp.float32)
        s_sel = s_sel - slopes[None, :, None, None, None] * (qpos[None, None, :, None, None] - kpos_sel)
        s_sel = jnp.where(sel_ok[..., None], s_sel, -jnp.inf)
        k_own = lax.dynamic_slice_in_dim(k, blk * bs, bs, axis=2)
        v_own = lax.dynamic_slice_in_dim(v, blk * bs, bs, axis=2)
        dist_own = qpos[:, None] - (blk * bs + jnp.arange(bs)).astype(jnp.float32)[None, :]
        s_own = jnp.einsum("bhqd,bhkd->bhqk", qc, k_own, preferred_element_type=jnp.float32) * scale
        s_own = s_own - slopes[None, :, None, None] * dist_own
        s_own = jnp.where(dist_own >= 0, s_own, -jnp.inf)
        p = jax.nn.softmax(jnp.concatenate([s_sel.reshape(b, h, qc_len, top * bs), s_own], axis=-1), axis=-1).astype(v.dtype)
        p_sel = p[..., :top * bs].reshape(b, h, qc_len, top, bs)
        p_own = p[..., top * bs:]
        return jnp.einsum("bhqnk,bhqnkd->bhqd", p_sel, v_sel) + jnp.einsum("bhqk,bhkd->bhqd", p_own, v_own)

    out = lax.map(one_chunk, jnp.arange(s // qc_len))
    out = jnp.moveaxis(out, 0, 2).reshape(b, h, s, dh)
    return from_heads(out)


def even_mixer(x, w_in, q_norm, w_q_up, kv_norm, w_kv_up, forget_bias, w_out):
    pos = jnp.arange(x.shape[1])
    c_q, c_kv, k_r, fq, fk, fv, f_logit = _split(x @ w_in, EVEN_IN_SIZES)
    a = mla_attention(c_q, c_kv, k_r, q_norm, w_q_up, kv_norm, w_kv_up, pos)
    bo = fox_attention(fq, fk, fv, f_logit, forget_bias)
    return jnp.concatenate([a, bo], axis=-1) @ w_out


def odd_mixer(x, w_in, sinks, w_out):
    sq, sk, sv, mq, mk, mv = _split(x @ w_in, ODD_IN_SIZES)
    c = swa_sink_attention(sq, sk, sv, sinks)
    d = moba_attention(mq, mk, mv)
    return jnp.concatenate([c, d], axis=-1) @ w_out


def moe(x, w_router, router_bias, w_gate_up, w_down):
    b, s, d = x.shape
    xt = x.reshape(b * s, d)
    aff = jax.nn.sigmoid(jnp.dot(xt, w_router, preferred_element_type=jnp.float32))
    sel = aff + router_bias.astype(jnp.float32)
    grp = sel.reshape(-1, N_GROUPS, EXPERTS_PER_GROUP)
    grp_score = jnp.sum(lax.top_k(grp, MOE_TOPK)[0], axis=-1)
    best = jnp.argmax(grp_score, axis=-1)
    in_grp = (jnp.arange(N_GROUPS)[None, :] == best[:, None])[:, :, None]
    masked = jnp.where(in_grp, grp, -jnp.inf).reshape(-1, N_EXPERTS)
    _, e_idx = lax.top_k(masked, MOE_TOPK)
    wts = jnp.take_along_axis(aff, e_idx, axis=-1)
    wts = wts / jnp.sum(wts, axis=-1, keepdims=True)
    gates = jnp.sum(jax.nn.one_hot(e_idx, N_EXPERTS, dtype=jnp.float32) * wts[..., None], axis=1).astype(x.dtype)
    y = jnp.zeros_like(xt)
    for e in range(N_EXPERTS):
        gu = xt @ w_gate_up[e]
        hdn = jax.nn.silu(gu[:, :EXPERT_FF]) * gu[:, EXPERT_FF:]
        y = y + gates[:, e:e + 1] * (hdn @ w_down[e])
    return y.reshape(b, s, d)


def setup_inputs(seed: int = 0) -> dict:
    key = jax.random.key(seed)
    ks = jax.random.split(key, 20)
    n_even = (DEPTH + 1) // 2
    n_odd = DEPTH // 2

    def nrm(k, shape, scale):
        return jax.random.normal(k, shape, jnp.float32) * scale

    return {
        "x": nrm(ks[0], (BATCH, SEQ, D_MODEL), 1.0),
        "w_router": nrm(ks[1], (D_MODEL, N_EXPERTS), D_MODEL ** -0.5),
        "router_bias": nrm(ks[2], (N_EXPERTS,), 0.01),
        "even_w_in": nrm(ks[3], (n_even, D_MODEL, EVEN_IN), D_MODEL ** -0.5),
        "even_q_norm": 1.0 + nrm(ks[4], (n_even, MLA_Q_LORA), 0.02),
        "even_w_q_up": nrm(ks[5], (n_even, MLA_Q_LORA, MLA_HEADS * (MLA_NOPE + MLA_ROPE)), MLA_Q_LORA ** -0.5),
        "even_kv_norm": 1.0 + nrm(ks[6], (n_even, MLA_KV_LORA), 0.02),
        "even_w_kv_up": nrm(ks[7], (n_even, MLA_KV_LORA, MLA_HEADS * (MLA_NOPE + MLA_V)), MLA_KV_LORA ** -0.5),
        "even_forget_bias": nrm(ks[8], (n_even, FOX_HEADS), 0.1),
        "even_w_out": nrm(ks[9], (n_even, EVEN_MIX, D_MODEL), BETA * EVEN_MIX ** -0.5),
        "odd_w_in": nrm(ks[10], (n_odd, D_MODEL, ODD_IN), D_MODEL ** -0.5),
        "odd_sinks": nrm(ks[11], (n_odd, SWA_Q_HEADS), 0.1),
        "odd_w_out": nrm(ks[12], (n_odd, ODD_MIX, D_MODEL), BETA * ODD_MIX ** -0.5),
        "ln_mix_g": 1.0 + nrm(ks[13], (DEPTH, D_MODEL), 0.02),
        "ln_mix_b": nrm(ks[14], (DEPTH, D_MODEL), 0.02),
        "ln_ffn_g": 1.0 + nrm(ks[15], (DEPTH, D_MODEL), 0.02),
        "ln_ffn_b": nrm(ks[16], (DEPTH, D_MODEL), 0.02),
        "w_gate_up": nrm(ks[17], (DEPTH, N_EXPERTS, D_MODEL, 2 * EXPERT_FF), D_MODEL ** -0.5),
        "w_down": nrm(ks[18], (DEPTH, N_EXPERTS, EXPERT_FF, D_MODEL), BETA * EXPERT_FF ** -0.5),
    }


def reference(x, w_router, router_bias, even_w_in, even_q_norm, even_w_q_up, even_kv_norm, even_w_kv_up, even_forget_bias, even_w_out, odd_w_in, odd_sinks, odd_w_out, ln_mix_g, ln_mix_b, ln_ffn_g, ln_ffn_b, w_gate_up, w_down):
    h = x
    for layer in range(DEPTH):
        i = layer // 2
        if layer % 2 == 0:
            mix = even_mixer(h, even_w_in[i], even_q_norm[i], even_w_q_up[i], even_kv_norm[i], even_w_kv_up[i], even_forget_bias[i], even_w_out[i])
        else:
            mix = odd_mixer(h, odd_w_in[i], odd_sinks[i], odd_w_out[i])
        h = layer_norm(ALPHA * h + mix, ln_mix_g[layer], ln_mix_b[layer])
        h = layer_norm(ALPHA * h + moe(h, w_router, router_bias, w_gate_up[layer], w_down[layer]), ln_ffn_g[layer], ln_ffn_b[layer])
    return h
```

```python
import functools

import numpy as np
import jax
import jax.numpy as jnp
from jax import lax
from jax.experimental import pallas as pl
from jax.experimental.pallas import tpu as pltpu

BF = jnp.bfloat16
F32 = jnp.float32
I32 = jnp.int32
HIGHEST = lax.Precision.HIGHEST

LANES = 128
SUBLANES = 8
VMEM_LIMIT = 56 * 1024 * 1024

D_MODEL = 2048
DEPTH = 2
LN_EPS = 1e-5
RMS_EPS = 1e-6
ALPHA = (2 * DEPTH) ** 0.25
MLA_HEADS, MLA_Q_LORA, MLA_KV_LORA, MLA_NOPE, MLA_ROPE, MLA_V = 8, 512, 256, 128, 64, 128
ROPE_THETA = 10000.0
FOX_HEADS, FOX_HEAD_DIM = 8, 128
SWA_Q_HEADS, SWA_KV_HEADS, SWA_HEAD_DIM, SWA_WINDOW = 16, 2, 64, 128
MOBA_HEADS, MOBA_HEAD_DIM, MOBA_BLOCK, MOBA_TOPK = 8, 128, 256, 3
N_EXPERTS, N_GROUPS, MOE_TOPK, EXPERT_FF = 32, 8, 2, 512
EXPERTS_PER_GROUP = N_EXPERTS // N_GROUPS

MLA_SCALE = (MLA_NOPE + MLA_ROPE) ** -0.5
FOX_SCALE = FOX_HEAD_DIM ** -0.5
SWA_SCALE = SWA_HEAD_DIM ** -0.5
MOBA_SCALE = MOBA_HEAD_DIM ** -0.5
NEG = -1e30

NT_DIMS = (((1,), (1,)), ((), ()))


def _params(*sem):
    return pltpu.CompilerParams(dimension_semantics=sem, vmem_limit_bytes=VMEM_LIMIT)


def _mm_body(x_ref, w_ref, cs_ref, o_ref):
    acc = jnp.dot(x_ref[...].astype(BF), w_ref[...], preferred_element_type=F32)
    o_ref[...] = (acc * cs_ref[...]).astype(o_ref.dtype)


def _mm(x, w, col_scale, out_dtype, tm, tn):
    m, k = x.shape
    n = w.shape[1]
    return pl.pallas_call(
        _mm_body,
        grid=(m // tm, n // tn),
        in_specs=[
            pl.BlockSpec((tm, k), lambda i, j: (i, 0)),
            pl.BlockSpec((k, tn), lambda i, j: (0, j)),
            pl.BlockSpec((1, tn), lambda i, j: (0, j)),
        ],
        out_specs=pl.BlockSpec((tm, tn), lambda i, j: (i, j)),
        out_shape=jax.ShapeDtypeStruct((m, n), out_dtype),
        compiler_params=_params("parallel", "arbitrary"),
        name="proj",
    )(x, w, col_scale)


def _rms(x, g):
    return x * lax.rsqrt(jnp.mean(x * x, axis=-1, keepdims=True) + RMS_EPS) * g


def _rope_pair(slab, table):
    r = slab * table
    return r + pltpu.roll(r, MLA_ROPE, axis=1)


def _mla_prep_body(cq_ref, ckv_ref, krp_ref, rope_ref, qg_ref, kvg_ref, wqn_ref, wqp_ref, wk_ref, wv_ref,
                   qn_ref, qp_ref, kn_ref, v_ref, kp_ref):
    table = rope_ref[...]
    cqn = _rms(cq_ref[...], qg_ref[...]).astype(BF)
    qn_ref[...] = (jnp.dot(cqn, wqn_ref[...], preferred_element_type=F32) * MLA_SCALE).astype(BF)
    qp = jnp.dot(cqn, wqp_ref[...], preferred_element_type=F32)
    for h in range(MLA_HEADS):
        sl = slice(h * LANES, (h + 1) * LANES)
        qp_ref[:, sl] = (_rope_pair(qp[:, sl], table) * MLA_SCALE).astype(BF)
    ckvn = _rms(ckv_ref[...], kvg_ref[...]).astype(BF)
    kn_ref[...] = jnp.dot(ckvn, wk_ref[...], preferred_element_type=F32).astype(BF)
    v_ref[...] = jnp.dot(ckvn, wv_ref[...], preferred_element_type=F32).astype(BF)
    kr = _rope_pair(krp_ref[...], table)
    lane = lax.broadcasted_iota(I32, kr.shape, 1)
    kp_ref[...] = jnp.where(lane < MLA_ROPE, kr, 0.0).astype(BF)


def _mla_prep(small, rope_table, q_norm, kv_norm, wqn, wqp, wk, wv, seq, tm):
    t = small.shape[0]
    n_s = seq // tm
    hd = MLA_HEADS * LANES
    row = lambda i: (i, 0)
    const = lambda i: (0, 0)
    out = lambda w: pl.BlockSpec((tm, w), row)
    return pl.pallas_call(
        _mla_prep_body,
        grid=(t // tm,),
        in_specs=[
            pl.BlockSpec((tm, MLA_Q_LORA), lambda i: (i, 0)),
            pl.BlockSpec((tm, MLA_KV_LORA), lambda i: (i, MLA_Q_LORA // MLA_KV_LORA)),
            pl.BlockSpec((tm, LANES), lambda i: (i, (MLA_Q_LORA + MLA_KV_LORA) // LANES)),
            pl.BlockSpec((tm, LANES), lambda i: (i % n_s, 0)),
            pl.BlockSpec((1, MLA_Q_LORA), const),
            pl.BlockSpec((1, MLA_KV_LORA), const),
            pl.BlockSpec((MLA_Q_LORA, hd), const),
            pl.BlockSpec((MLA_Q_LORA, hd), const),
            pl.BlockSpec((MLA_KV_LORA, hd), const),
            pl.BlockSpec((MLA_KV_LORA, hd), const),
        ],
        out_specs=[out(hd), out(hd), out(hd), out(hd), out(LANES)],
        out_shape=[jax.ShapeDtypeStruct((t, hd), BF)] * 4 + [jax.ShapeDtypeStruct((t, LANES), BF)],
        compiler_params=_params("parallel"),
        name="mla_prep",
    )(small, small, small, rope_table, q_norm, kv_norm, wqn, wqp, wk, wv)


def _fox_cum_body(fl_ref, fb_ref, col_ref, row_ref, carry_ref):
    @pl.when(pl.program_id(1) == 0)
    def _():
        carry_ref[...] = jnp.zeros_like(carry_ref)

    z = fl_ref[...] + fb_ref[...]
    log_f = jnp.minimum(z, 0.0) - jnp.log1p(jnp.exp(-jnp.abs(z)))
    tm = z.shape[0]
    r = lax.broadcasted_iota(I32, (tm, tm), 0)
    c = lax.broadcasted_iota(I32, (tm, tm), 1)
    tri = jnp.where(r >= c, 1.0, 0.0).astype(F32)
    cum = jnp.dot(tri, log_f, precision=HIGHEST, preferred_element_type=F32) + carry_ref[...]
    col_ref[...] = cum
    carry_ref[...] = cum[tm - 1:tm, :]
    er = lax.broadcasted_iota(I32, (SUBLANES, LANES), 0)
    ec = lax.broadcasted_iota(I32, (SUBLANES, LANES), 1)
    pick = jnp.where(er == ec, 1.0, 0.0).astype(F32)
    row_ref[0] = lax.dot_general(pick, cum, NT_DIMS, precision=HIGHEST, preferred_element_type=F32)


def _fox_cum(small, forget_bias_row, batch, seq, tm):
    t = small.shape[0]
    n_s = seq // tm
    fl_block = (MLA_Q_LORA + MLA_KV_LORA + LANES) // LANES
    return pl.pallas_call(
        _fox_cum_body,
        grid=(batch, n_s),
        in_specs=[
            pl.BlockSpec((tm, LANES), lambda b, i: (b * n_s + i, fl_block)),
            pl.BlockSpec((1, LANES), lambda b, i: (0, 0)),
        ],
        out_specs=[
            pl.BlockSpec((tm, LANES), lambda b, i: (b * n_s + i, 0)),
            pl.BlockSpec((1, SUBLANES, tm), lambda b, i: (b, 0, i)),
        ],
        out_shape=[jax.ShapeDtypeStruct((t, LANES), F32), jax.ShapeDtypeStruct((batch, SUBLANES, seq), F32)],
        scratch_shapes=[pltpu.VMEM((1, LANES), F32)],
        compiler_params=_params("parallel", "arbitrary"),
        name="fox_cum",
    )(small, forget_bias_row)


def _softmax_update(s, v_blk, m_ref, l_ref, acc_ref):
    m_prev = m_ref[...]
    m_new = jnp.maximum(m_prev, jnp.max(s, axis=1, keepdims=True))
    alpha = jnp.exp(m_prev - m_new)
    p = jnp.exp(s - m_new)
    l_ref[...] = alpha * l_ref[...] + jnp.sum(p, axis=1, keepdims=True)
    acc_ref[...] = alpha * acc_ref[...] + jnp.dot(p.astype(BF), v_blk, preferred_element_type=F32)
    m_ref[...] = m_new


def _attn_body(*refs, mode, t):
    h = pl.program_id(1)
    i = pl.program_id(2)
    if mode == "mla":
        qn_ref, qp_ref, kn_ref, kp_ref, v_ref, o_ref, m_ref, l_ref, acc_ref = refs
        q = jnp.concatenate([qn_ref[...], qp_ref[...]], axis=1)
    elif mode == "fox":
        q_ref, k_ref, v_ref, cq_ref, ck_ref, o_ref, m_ref, l_ref, acc_ref = refs
        q = q_ref[...]
        lane = lax.broadcasted_iota(I32, (t, LANES), 1)
        cq = jnp.sum(jnp.where(lane == h, cq_ref[...], 0.0), axis=1, keepdims=True)
    else:
        q_ref, k_ref, v_ref, km_ref, slope_ref, o_ref, m_ref, l_ref, acc_ref, sel_ref, dist_ref = refs
        q = q_ref[...]
        slope = slope_ref[h]
        n_kb = km_ref.shape[0]
        gate = lax.dot_general(q, km_ref[...].astype(BF), NT_DIMS, preferred_element_type=F32)
        blk = lax.broadcasted_iota(I32, (t, n_kb), 1)
        beaten = jnp.zeros((t, n_kb), F32)
        for n in range(n_kb):
            g_n = gate[:, n:n + 1]
            wins = (g_n > gate) | ((g_n == gate) & (blk > n))
            beaten = beaten + jnp.where(wins, 1.0, 0.0) * jnp.where(i > n, 1.0, 0.0)
        sel_ref[...] = jnp.where((blk < i) & (beaten < MOBA_TOPK), 1.0, 0.0)
        r = lax.broadcasted_iota(I32, (t, t), 0)
        c = lax.broadcasted_iota(I32, (t, t), 1)
        dist_ref[...] = slope * (r - c).astype(F32)

    def key_tile(j):
        ks = pl.multiple_of(j * t, t)
        if mode == "mla":
            k = jnp.concatenate([kn_ref[pl.ds(ks, t), :], kp_ref[pl.ds(ks, t), :]], axis=1)
        else:
            k = k_ref[pl.ds(ks, t), :]
        return ks, lax.dot_general(q, k, NT_DIMS, preferred_element_type=F32)

    def biased(s, ks, j):
        if mode == "fox":
            return s + (cq - ck_ref[0, pl.ds(h, 1), pl.ds(ks, t)])
        if mode == "moba":
            return s - dist_ref[...]
        return s

    m_ref[...] = jnp.full(m_ref.shape, NEG, F32)
    l_ref[...] = jnp.zeros(l_ref.shape, F32)
    acc_ref[...] = jnp.zeros(acc_ref.shape, F32)

    ks, s = key_tile(i)
    s = biased(s, ks, i)
    r = lax.broadcasted_iota(I32, (t, t), 0)
    c = lax.broadcasted_iota(I32, (t, t), 1)
    s = jnp.where(c <= r, s, NEG)
    _softmax_update(s, v_ref[pl.ds(ks, t), :], m_ref, l_ref, acc_ref)

    def past_tile(j, carry):
        ks, s = key_tile(j)
        s = biased(s, ks, j)
        if mode == "moba":
            blk = lax.broadcasted_iota(I32, sel_ref.shape, 1)
            chosen = jnp.sum(jnp.where(blk == j, sel_ref[...], 0.0), axis=1, keepdims=True)
            far = slope * ((i - j) * t).astype(F32)
            s = s + jnp.where(chosen > 0.0, -far, NEG)
        _softmax_update(s, v_ref[pl.ds(ks, t), :], m_ref, l_ref, acc_ref)
        return carry

    lax.fori_loop(0, i, past_tile, 0)
    o_ref[...] = (acc_ref[...] / l_ref[...]).astype(o_ref.dtype)


def _attention(mode, batch, seq, heads, t, operands, in_specs, extra_scratch=()):
    n_q = seq // t
    body = functools.partial(_attn_body, mode=mode, t=t)
    return pl.pallas_call(
        body,
        grid=(batch, heads, n_q),
        in_specs=in_specs,
        out_specs=pl.BlockSpec((t, LANES), lambda b, h, i: (b * n_q + i, h)),
        out_shape=jax.ShapeDtypeStruct((batch * seq, heads * LANES), BF),
        scratch_shapes=[pltpu.VMEM((t, 1), F32), pltpu.VMEM((t, 1), F32), pltpu.VMEM((t, LANES), F32),
                        *extra_scratch],
        compiler_params=_params("parallel", "parallel", "arbitrary"),
        name="attn_" + mode,
    )(*operands)


def _q_spec(t, n_q, col0=0):
    return pl.BlockSpec((t, LANES), lambda b, h, i: (b * n_q + i, col0 + h))


def _kv_spec(seq, col0=0):
    return pl.BlockSpec((seq, LANES), lambda b, h, i: (b, col0 + h))


def _mla_attention(qn, qp, kn, v, kp, batch, seq, t):
    n_q = seq // t
    specs = [_q_spec(t, n_q), _q_spec(t, n_q), _kv_spec(seq),
             pl.BlockSpec((seq, LANES), lambda b, h, i: (b, 0)), _kv_spec(seq)]
    return _attention("mla", batch, seq, MLA_HEADS, t, (qn, qp, kn, kp, v), specs)


def _fox_attention(qkv, cum_col, cum_row, batch, seq, t):
    n_q = seq // t
    specs = [_q_spec(t, n_q, 0), _kv_spec(seq, FOX_HEADS), _kv_spec(seq, 2 * FOX_HEADS),
             pl.BlockSpec((t, LANES), lambda b, h, i: (b * n_q + i, 0)),
             pl.BlockSpec((1, SUBLANES, seq), lambda b, h, i: (b, 0, 0))]
    return _attention("fox", batch, seq, FOX_HEADS, t, (qkv, qkv, qkv, cum_col, cum_row), specs)


def _moba_attention(big, k_mean, slopes, batch, seq, col_q, col_k, col_v):
    t = MOBA_BLOCK
    n_q = seq // t
    specs = [_q_spec(t, n_q, col_q), _kv_spec(seq, col_k), _kv_spec(seq, col_v),
             pl.BlockSpec((n_q, LANES), lambda b, h, i: (b, h)),
             pl.BlockSpec(memory_space=pltpu.SMEM)]
    scratch = (pltpu.VMEM((t, n_q), F32), pltpu.VMEM((t, t), F32))
    return _attention("moba", batch, seq, MOBA_HEADS, t, (big, big, big, k_mean, slopes), specs, scratch)


def _kmean_body(k_ref, o_ref):
    n_kb = o_ref.shape[0]
    for n in range(n_kb):
        blk = k_ref[n * MOBA_BLOCK:(n + 1) * MOBA_BLOCK, :].astype(F32)
        o_ref[n:n + 1, :] = jnp.mean(blk, axis=0, keepdims=True)


def _moba_kmean(big, batch, seq, col_block):
    n_kb = seq // MOBA_BLOCK
    width = MOBA_HEADS * MOBA_HEAD_DIM
    return pl.pallas_call(
        _kmean_body,
        grid=(batch,),
        in_specs=[pl.BlockSpec((seq, width), lambda b: (b, col_block))],
        out_specs=pl.BlockSpec((n_kb, width), lambda b: (b, 0)),
        out_shape=jax.ShapeDtypeStruct((batch * n_kb, width), F32),
        compiler_params=_params("parallel"),
        name="moba_kmean",
    )(big)


def _swa_body(sink_ref, q_ref, kp_ref, kc_ref, vp_ref, vc_ref, o_ref):
    i = pl.program_id(1)
    w = SWA_WINDOW
    half = SWA_HEAD_DIM
    lane = lax.broadcasted_iota(I32, (2 * w, LANES), 1)
    kcat = jnp.concatenate([kp_ref[...], kc_ref[...]], axis=0).astype(F32)
    vcat = jnp.concatenate([vp_ref[...], vc_ref[...]], axis=0).astype(F32)

    def lo_hi(x, kvh):
        own = jnp.where((lane < half) == (kvh == 0), x, 0.0)
        other = pltpu.roll(own, half, axis=1)
        lo, hi = (own, other) if kvh == 0 else (other, own)
        return lo.astype(BF), hi.astype(BF)

    r = lax.broadcasted_iota(I32, (w, 2 * w), 0)
    c = lax.broadcasted_iota(I32, (w, 2 * w), 1)
    dist = r + w - c
    valid = (dist >= 0) & (dist < w) & ((c >= w) | (i > 0))
    dist_f = dist.astype(F32)
    for kvh in range(SWA_KV_HEADS):
        k_lo, k_hi = lo_hi(kcat, kvh)
        v_lo, v_hi = lo_hi(vcat, kvh)
        pairs_per_kv = SWA_Q_HEADS // SWA_KV_HEADS // 2
        for pp in range(pairs_per_kv):
            pair = kvh * pairs_per_kv + pp
            q = q_ref[:, pair * LANES:(pair + 1) * LANES]
            out = jnp.zeros((w, LANES), F32)
            for k_side, v_side, head in ((k_lo, v_lo, 2 * pair), (k_hi, v_hi, 2 * pair + 1)):
                slope = float(2.0 ** (-8.0 * (head + 1) / SWA_Q_HEADS))
                s = lax.dot_general(q, k_side, NT_DIMS, preferred_element_type=F32)
                s = jnp.where(valid, s - slope * dist_f, NEG)
                sink = sink_ref[head]
                m = jnp.maximum(jnp.max(s, axis=1, keepdims=True), sink)
                p = jnp.exp(s - m)
                p = p / (jnp.sum(p, axis=1, keepdims=True) + jnp.exp(sink - m))
                out = out + jnp.dot(p.astype(BF), v_side, preferred_element_type=F32)
            o_ref[:, pair * LANES:(pair + 1) * LANES] = out.astype(o_ref.dtype)


def _swa_attention(big, sinks, batch, seq, col_q, col_k, col_v):
    w = SWA_WINDOW
    n_q = seq // w
    width = SWA_Q_HEADS * SWA_HEAD_DIM
    prev = lambda col: pl.BlockSpec((w, LANES), lambda b, i: (b * n_q + jnp.maximum(i - 1, 0), col))
    cur = lambda col: pl.BlockSpec((w, LANES), lambda b, i: (b * n_q + i, col))
    return pl.pallas_call(
        _swa_body,
        grid=(batch, n_q),
        in_specs=[pl.BlockSpec(memory_space=pltpu.SMEM),
                  pl.BlockSpec((w, width), lambda b, i: (b * n_q + i, col_q)),
                  prev(col_k), cur(col_k), prev(col_v), cur(col_v)],
        out_specs=pl.BlockSpec((w, width), lambda b, i: (b * n_q + i, 0)),
        out_shape=jax.ShapeDtypeStruct((batch * seq, width), BF),
        compiler_params=_params("parallel", "parallel"),
        name="swa",
    )(sinks, big, big, big, big, big)


def _layer_norm(z, g, b):
    mu = jnp.mean(z, axis=-1, keepdims=True)
    zc = z - mu
    var = jnp.mean(zc * zc, axis=-1, keepdims=True)
    return zc * lax.rsqrt(var + LN_EPS) * g + b


def _outproj_body(a_ref, b_ref, wa_ref, wb_ref, h_ref, g_ref, beta_ref, o_ref):
    mix = jnp.dot(a_ref[...], wa_ref[...], preferred_element_type=F32)
    mix = mix + jnp.dot(b_ref[...], wb_ref[...], preferred_element_type=F32)
    o_ref[...] = _layer_norm(ALPHA * h_ref[...] + mix, g_ref[...], beta_ref[...])


def _outproj_ln(a, b, w_out, h, g, beta, tm):
    t, d = h.shape
    ka = a.shape[1]
    row = lambda i: (i, 0)
    return pl.pallas_call(
        _outproj_body,
        grid=(t // tm,),
        in_specs=[pl.BlockSpec((tm, ka), row), pl.BlockSpec((tm, ka), row),
                  pl.BlockSpec((ka, d), lambda i: (0, 0)), pl.BlockSpec((ka, d), lambda i: (1, 0)),
                  pl.BlockSpec((tm, d), row), pl.BlockSpec((1, d), lambda i: (0, 0)),
                  pl.BlockSpec((1, d), lambda i: (0, 0))],
        out_specs=pl.BlockSpec((tm, d), row),
        out_shape=jax.ShapeDtypeStruct((t, d), F32),
        compiler_params=_params("parallel"),
        name="outproj_ln",
    )(a, b, w_out, w_out, h, g, beta)


def _second_of_four(a, b, c, d):
    hi_ab, lo_ab = jnp.maximum(a, b), jnp.minimum(a, b)
    hi_cd, lo_cd = jnp.maximum(c, d), jnp.minimum(c, d)
    return jnp.maximum(jnp.maximum(lo_ab, lo_cd), jnp.minimum(hi_ab, hi_cd))


def _router_body(h_ref, wr_ref, bias_ref, e_ref, r_ref):
    tm = h_ref.shape[0]
    logits = jnp.dot(h_ref[...], wr_ref[...], precision=HIGHEST, preferred_element_type=F32)
    lt = logits.T
    aff = [jax.nn.sigmoid(lt[SUBLANES * j:SUBLANES * (j + 1), :]) for j in range(EXPERTS_PER_GROUP)]
    sel = [aff[j] + bias_ref[SUBLANES * j:SUBLANES * (j + 1), :] for j in range(EXPERTS_PER_GROUP)]
    top1 = jnp.maximum(jnp.maximum(sel[0], sel[1]), jnp.maximum(sel[2], sel[3]))
    score = top1 + _second_of_four(*sel)
    gid = lax.broadcasted_iota(I32, (N_GROUPS, tm), 0)
    best = jnp.min(jnp.where(score == jnp.max(score, axis=0, keepdims=True), gid, N_GROUPS),
                   axis=0, keepdims=True)
    in_grp = gid == best
    pick = lambda x: jnp.sum(jnp.where(in_grp, x, 0.0), axis=0, keepdims=True)
    s4 = [pick(x) for x in sel]
    a4 = [pick(x) for x in aff]

    def argmax4(vals):
        j, v = jnp.zeros((1, tm), I32), vals[0]
        for n in range(1, EXPERTS_PER_GROUP):
            better = vals[n] > v
            j, v = jnp.where(better, n, j), jnp.where(better, vals[n], v)
        return j

    j0 = argmax4(s4)
    j1 = argmax4([jnp.where(j0 == n, -jnp.inf, s4[n]) for n in range(EXPERTS_PER_GROUP)])
    take = lambda j: sum(jnp.where(j == n, a4[n], 0.0) for n in range(EXPERTS_PER_GROUP))
    w0, w1 = take(j0), take(j1)
    total = w0 + w1
    e0 = best * EXPERTS_PER_GROUP + j0
    e1 = best * EXPERTS_PER_GROUP + j1
    rid = lax.broadcasted_iota(I32, (SUBLANES, tm), 0)
    e_ref[...] = jnp.where(rid == 0, e0, jnp.where(rid == 1, e1, 0))
    rid = lax.broadcasted_iota(I32, (LANES, tm), 0)
    rows = jnp.where(rid == 0, w0 / total, jnp.where(rid == 1, w1 / total, 0.0))
    r_ref[...] = rows.T


def _router(h, wr_perm, bias_perm, tm):
    t, d = h.shape
    return pl.pallas_call(
        _router_body,
        grid=(t // tm,),
        in_specs=[pl.BlockSpec((tm, d), lambda i: (i, 0)), pl.BlockSpec((d, LANES), lambda i: (0, 0)),
                  pl.BlockSpec((N_EXPERTS, 1), lambda i: (0, 0))],
        out_specs=[pl.BlockSpec((SUBLANES, tm), lambda i: (0, i)), pl.BlockSpec((tm, LANES), lambda i: (i, 0))],
        out_shape=[jax.ShapeDtypeStruct((SUBLANES, t), I32), jax.ShapeDtypeStruct((t, LANES), F32)],
        compiler_params=_params("parallel"),
        name="router",
    )(h, wr_perm, bias_perm)


def _dispatch_plan(e01, tm):
    t = e01.shape[1]
    flat = e01.reshape(-1)
    onehot = (flat[:, None] == jnp.arange(N_EXPERTS, dtype=I32)[None, :]).astype(I32)
    cum = jnp.cumsum(onehot, axis=0)
    rank = jnp.sum(onehot * cum, axis=1) - 1
    counts = cum[-1]
    padded = ((counts + tm - 1) // tm) * tm
    starts = jnp.concatenate([jnp.zeros((1,), I32), jnp.cumsum(padded).astype(I32)])
    pos = jnp.sum(onehot * starts[None, :N_EXPERTS], axis=1) + rank
    tok = jnp.tile(jnp.arange(t, dtype=I32), 2)
    n_rows = 2 * t + N_EXPERTS * tm
    row_token = jnp.zeros((n_rows,), I32).at[pos].set(tok)
    return row_token, starts, pos[:t], pos[t:]


def _experts_body(rt_ref, gs_ref, h_hbm, wgu_ref, wd_ref, ys_hbm, xbuf, ybuf, wgu_bf, wd_bf, gsem, osem, *, tm):
    e = pl.program_id(0)
    start = gs_ref[e]
    n_tiles = (gs_ref[e + 1] - start) // tm

    def gather_rows(k, slot):
        base = start + k * tm

        def one(r, carry):
            tok = rt_ref[base + r]
            pltpu.make_async_copy(h_hbm.at[pl.ds(tok, 1), :], xbuf.at[slot, pl.ds(r, 1), :],
                                  gsem.at[slot]).start()
            return carry

        lax.fori_loop(0, tm, one, 0)

    def wait_rows(slot):
        pltpu.make_async_copy(h_hbm.at[pl.ds(0, tm), :], xbuf.at[slot], gsem.at[slot]).wait()

    def out_copy(k, slot):
        row0 = pl.multiple_of(start + k * tm, tm)
        return pltpu.make_async_copy(ybuf.at[slot], ys_hbm.at[pl.ds(row0, tm), :], osem.at[slot])

    @pl.when(n_tiles > 0)
    def _():
        gather_rows(0, 0)
        wgu_bf[...] = wgu_ref[0, 0].astype(BF)
        wd_bf[...] = wd_ref[0, 0].astype(BF)

        def tile(k, carry):
            slot = k % 2

            @pl.when(k + 1 < n_tiles)
            def _():
                gather_rows(k + 1, 1 - slot)

            wait_rows(slot)
            x = xbuf[slot].astype(BF)
            gu = jnp.dot(x, wgu_bf[...], preferred_element_type=F32)
            gate, up = gu[:, :EXPERT_FF], gu[:, EXPERT_FF:]
            hidden = (gate * jax.nn.sigmoid(gate) * up).astype(BF)
            y = jnp.dot(hidden, wd_bf[...], preferred_element_type=F32)

            @pl.when(k >= 2)
            def _():
                out_copy(k - 2, slot).wait()

            ybuf[slot] = y
            out_copy(k, slot).start()
            return carry

        lax.fori_loop(0, n_tiles, tile, 0)

        @pl.when(n_tiles >= 2)
        def _():
            out_copy(n_tiles - 2, n_tiles % 2).wait()

        out_copy(n_tiles - 1, (n_tiles - 1) % 2).wait()

    @pl.when(e == N_EXPERTS - 1)
    def _():
        end = gs_ref[N_EXPERTS]
        n_tail = (ys_hbm.shape[0] - end) // tm
        ybuf[0] = jnp.zeros(ybuf.shape[1:], F32)

        def tail_copy(k):
            row0 = pl.multiple_of(end + k * tm, tm)
            return pltpu.make_async_copy(ybuf.at[0], ys_hbm.at[pl.ds(row0, tm), :], osem.at[0])

        def start_one(k, carry):
            tail_copy(k).start()
            return carry

        def wait_one(k, carry):
            tail_copy(k).wait()
            return carry

        lax.fori_loop(0, n_tail, start_one, 0)
        lax.fori_loop(0, n_tail, wait_one, 0)


def _experts(h, row_token, starts, w_gate_up, w_down, layer, tm):
    t, d = h.shape
    n_rows = row_token.shape[0]
    body = functools.partial(_experts_body, tm=tm)
    grid_spec = pltpu.PrefetchScalarGridSpec(
        num_scalar_prefetch=2,
        grid=(N_EXPERTS,),
        in_specs=[pl.BlockSpec(memory_space=pl.ANY),
                  pl.BlockSpec((1, 1, d, 2 * EXPERT_FF), lambda e, rt, gs: (layer, e, 0, 0)),
                  pl.BlockSpec((1, 1, EXPERT_FF, d), lambda e, rt, gs: (layer, e, 0, 0))],
        out_specs=pl.BlockSpec(memory_space=pl.ANY),
        scratch_shapes=[pltpu.VMEM((2, tm, d), F32), pltpu.VMEM((2, tm, d), F32),
                        pltpu.VMEM((d, 2 * EXPERT_FF), BF), pltpu.VMEM((EXPERT_FF, d), BF),
                        pltpu.SemaphoreType.DMA((2,)), pltpu.SemaphoreType.DMA((2,))],
    )
    return pl.pallas_call(
        body,
        grid_spec=grid_spec,
        out_shape=jax.ShapeDtypeStruct((n_rows, d), F32),
        compiler_params=_params("arbitrary"),
        name="experts",
    )(row_token, starts, h, w_gate_up, w_down)


def _combine_body(p0_ref, p1_ref, ys_hbm, h_ref, r_ref, g_ref, beta_ref, o_ref, buf0, buf1, sem, *, tc, n_steps):
    i = pl.program_id(0)
    slot = i % 2

    def gather(step, sl):
        base = step * tc

        def one(r, carry):
            pltpu.make_async_copy(ys_hbm.at[pl.ds(p0_ref[base + r], 1), :], buf0.at[sl, pl.ds(r, 1), :],
                                  sem.at[sl]).start()
            pltpu.make_async_copy(ys_hbm.at[pl.ds(p1_ref[base + r], 1), :], buf1.at[sl, pl.ds(r, 1), :],
                                  sem.at[sl]).start()
            return carry

        lax.fori_loop(0, tc, one, 0)

    @pl.when(i == 0)
    def _():
        gather(0, 0)

    @pl.when(i + 1 < n_steps)
    def _():
        gather(i + 1, 1 - slot)

    pltpu.make_async_copy(ys_hbm.at[pl.ds(0, tc), :], buf0.at[slot], sem.at[slot]).wait()
    pltpu.make_async_copy(ys_hbm.at[pl.ds(0, tc), :], buf1.at[slot], sem.at[slot]).wait()
    gates = r_ref[...]
    y = gates[:, 0:1] * buf0[slot] + gates[:, 1:2] * buf1[slot]
    o_ref[...] = _layer_norm(ALPHA * h_ref[...] + y, g_ref[...], beta_ref[...])


def _combine_ln(ys, pos0, pos1, h, gates, g, beta, tc):
    t, d = h.shape
    n_steps = t // tc
    body = functools.partial(_combine_body, tc=tc, n_steps=n_steps)
    row = lambda i, p0, p1: (i, 0)
    const = lambda i, p0, p1: (0, 0)
    grid_spec = pltpu.PrefetchScalarGridSpec(
        num_scalar_prefetch=2,
        grid=(n_steps,),
        in_specs=[pl.BlockSpec(memory_space=pl.ANY), pl.BlockSpec((tc, d), row), pl.BlockSpec((tc, LANES), row),
                  pl.BlockSpec((1, d), const), pl.BlockSpec((1, d), const)],
        out_specs=pl.BlockSpec((tc, d), row),
        scratch_shapes=[pltpu.VMEM((2, tc, d), F32), pltpu.VMEM((2, tc, d), F32), pltpu.SemaphoreType.DMA((2,))],
    )
    return pl.pallas_call(
        body,
        grid_spec=grid_spec,
        out_shape=jax.ShapeDtypeStruct((t, d), F32),
        compiler_params=_params("arbitrary"),
        name="combine_ln",
    )(pos0, pos1, ys, h, gates, g, beta)


def _moe_ln(h, wr_perm, bias_perm, w_gate_up, w_down, layer, g, beta, tm_router=512, tm_expert=256, tc=256):
    e01, gates = _router(h, wr_perm, bias_perm, tm_router)
    row_token, starts, pos0, pos1 = _dispatch_plan(e01[:MOE_TOPK], tm_expert)
    ys = _experts(h, row_token, starts, w_gate_up, w_down, layer, tm_expert)
    return _combine_ln(ys, pos0, pos1, h, gates, g, beta, tc)


def _swap_halves(w):
    half = w.shape[-1] // 2
    return jnp.concatenate([w[..., half:], w[..., :half]], axis=-1)


def _even_weights(w_in, w_q_up, w_kv_up, forget_bias):
    d = w_in.shape[0]
    o_kv = MLA_Q_LORA
    o_kr = o_kv + MLA_KV_LORA
    o_fq = o_kr + MLA_ROPE
    hd = FOX_HEADS * FOX_HEAD_DIM
    o_fl = o_fq + 3 * hd
    k_r = w_in[:, o_kr:o_fq]
    f_l = jnp.pad(w_in[:, o_fl:], ((0, 0), (0, LANES - FOX_HEADS)))
    w_small = jnp.concatenate([w_in[:, :o_kr], k_r, _swap_halves(k_r), f_l], axis=1).astype(BF)
    w_fox = w_in[:, o_fq:o_fl].astype(BF)
    cs_fox = jnp.concatenate([jnp.full((1, hd), FOX_SCALE, F32), jnp.ones((1, 2 * hd), F32)], axis=1)
    wq = w_q_up.reshape(MLA_Q_LORA, MLA_HEADS, MLA_NOPE + MLA_ROPE)
    wqn = wq[:, :, :MLA_NOPE].reshape(MLA_Q_LORA, -1).astype(BF)
    pe = wq[:, :, MLA_NOPE:]
    wqp = jnp.concatenate([pe, _swap_halves(pe)], axis=-1).reshape(MLA_Q_LORA, -1).astype(BF)
    wkv = w_kv_up.reshape(MLA_KV_LORA, MLA_HEADS, MLA_NOPE + MLA_V)
    wk = wkv[:, :, :MLA_NOPE].reshape(MLA_KV_LORA, -1).astype(BF)
    wv = wkv[:, :, MLA_NOPE:].reshape(MLA_KV_LORA, -1).astype(BF)
    fb = jnp.pad(forget_bias.astype(F32), (0, LANES - FOX_HEADS)).reshape(1, LANES)
    return w_small, w_fox, cs_fox, wqn, wqp, wk, wv, fb


def _rope_table(seq):
    half = MLA_ROPE // 2
    inv_freq = ROPE_THETA ** (-jnp.arange(half, dtype=F32) / half)
    ang = jnp.arange(seq).astype(F32)[:, None] * inv_freq[None, :]
    cos, sin = jnp.cos(ang), jnp.sin(ang)
    return jnp.concatenate([cos, cos, -sin, sin], axis=1)


def _router_weights(w_router, router_bias):
    r = np.arange(N_EXPERTS)
    perm = (r % N_GROUPS) * EXPERTS_PER_GROUP + r // N_GROUPS
    wr = jnp.pad(w_router[:, perm], ((0, 0), (0, LANES - N_EXPERTS)))
    return wr, router_bias.astype(F32)[perm].reshape(N_EXPERTS, 1)


def _even_layer(h, batch, seq, w_in, q_norm, w_q_up, kv_norm, w_kv_up, forget_bias, w_out, g, beta):
    w_small, w_fox, cs_fox, wqn, wqp, wk, wv, fb = _even_weights(w_in, w_q_up, w_kv_up, forget_bias)
    small = _mm(h, w_small, jnp.ones((1, w_small.shape[1]), F32), F32, 1024, 512)
    fox_qkv = _mm(h, w_fox, cs_fox, BF, 1024, 512)
    qn, qp, kn, v, kp = _mla_prep(small, _rope_table(seq), q_norm.reshape(1, -1), kv_norm.reshape(1, -1),
                                  wqn, wqp, wk, wv, seq, 512)
    cum_col, cum_row = _fox_cum(small, fb, batch, seq, 512)
    a = _mla_attention(qn, qp, kn, v, kp, batch, seq, 512)
    bo = _fox_attention(fox_qkv, cum_col, cum_row, batch, seq, 512)
    return _outproj_ln(a, bo, w_out.astype(BF), h, g, beta, 512)


def _odd_layer(h, batch, seq, w_in, sinks, w_out, g, beta):
    n_sq = SWA_Q_HEADS * SWA_HEAD_DIM
    n_skv = SWA_KV_HEADS * SWA_HEAD_DIM
    n_m = MOBA_HEADS * MOBA_HEAD_DIM
    o_mq = n_sq + 2 * n_skv
    w_big = jnp.concatenate([w_in[:, :n_sq], w_in[:, o_mq:], w_in[:, n_sq:o_mq]], axis=1).astype(BF)
    cs = jnp.concatenate([jnp.full((1, n_sq), SWA_SCALE, F32), jnp.full((1, n_m), MOBA_SCALE, F32),
                          jnp.ones((1, 2 * n_m + 2 * n_skv), F32)], axis=1)
    big = _mm(h, w_big, cs, BF, 1024, 256)
    blocks = lambda cols: cols // LANES
    k_mean = _moba_kmean(big, batch, seq, 2)
    slopes = jnp.asarray(2.0 ** (-8.0 * np.arange(1, MOBA_HEADS + 1) / MOBA_HEADS), dtype=F32)
    c = _swa_attention(big, sinks.astype(F32), batch, seq, 0, blocks(n_sq + 3 * n_m), blocks(n_sq + 3 * n_m) + 1)
    dd = _moba_attention(big, k_mean, slopes, batch, seq, blocks(n_sq), blocks(n_sq + n_m), blocks(n_sq + 2 * n_m))
    return _outproj_ln(c, dd, w_out.astype(BF), h, g, beta, 512)


def kernel(x, w_router, router_bias, even_w_in, even_q_norm, even_w_q_up, even_kv_norm, even_w_kv_up,
           even_forget_bias, even_w_out, odd_w_in, odd_sinks, odd_w_out, ln_mix_g, ln_mix_b, ln_ffn_g, ln_ffn_b,
           w_gate_up, w_down):
    batch, seq, d = x.shape
    h = x.reshape(batch * seq, d)
    wr_perm, bias_perm = _router_weights(w_router, router_bias)
    row = lambda p, layer: p[layer].reshape(1, d)
    for layer in range(DEPTH):
        i = layer // 2
        if layer % 2 == 0:
            h = _even_layer(h, batch, seq, even_w_in[i], even_q_norm[i], even_w_q_up[i], even_kv_norm[i],
                            even_w_kv_up[i], even_forget_bias[i], even_w_out[i], row(ln_mix_g, layer),
                            row(ln_mix_b, layer))
        else:
            h = _odd_layer(h, batch, seq, odd_w_in[i], odd_sinks[i], odd_w_out[i], row(ln_mix_g, layer),
                           row(ln_mix_b, layer))
        h = _moe_ln(h, wr_perm, bias_perm, w_gate_up, w_down, layer, row(ln_ffn_g, layer), row(ln_ffn_b, layer))
    return h.reshape(batch, seq, d)
```

```python
import functools

import numpy as np
import jax
import jax.numpy as jnp
from jax import lax
from jax.experimental import pallas as pl
from jax.experimental.pallas import tpu as pltpu

BF = jnp.bfloat16
F32 = jnp.float32
I32 = jnp.int32
HIGHEST = lax.Precision.HIGHEST

LANES = 128
SUBLANES = 8
VMEM_LIMIT = 56 * 1024 * 1024

D_MODEL = 2048
DEPTH = 2
LN_EPS = 1e-5
RMS_EPS = 1e-6
ALPHA = (2 * DEPTH) ** 0.25
MLA_HEADS, MLA_Q_LORA, MLA_KV_LORA, MLA_NOPE, MLA_ROPE, MLA_V = 8, 512, 256, 128, 64, 128
ROPE_THETA = 10000.0
FOX_HEADS, FOX_HEAD_DIM = 8, 128
SWA_Q_HEADS, SWA_KV_HEADS, SWA_HEAD_DIM, SWA_WINDOW = 16, 2, 64, 128
MOBA_HEADS, MOBA_HEAD_DIM, MOBA_BLOCK, MOBA_TOPK = 8, 128, 256, 3
N_EXPERTS, N_GROUPS, MOE_TOPK, EXPERT_FF = 32, 8, 2, 512
EXPERTS_PER_GROUP = N_EXPERTS // N_GROUPS

LOG2E = 1.4426950408889634
MLA_SCALE = (MLA_NOPE + MLA_ROPE) ** -0.5 * LOG2E
FOX_SCALE = FOX_HEAD_DIM ** -0.5 * LOG2E
MOBA_SCALE = MOBA_HEAD_DIM ** -0.5 * LOG2E
SWA_SCALE = SWA_HEAD_DIM ** -0.5
NEG = -1e30

NT_DIMS = (((1,), (1,)), ((), ()))


def _params(*sem):
    return pltpu.CompilerParams(dimension_semantics=sem, vmem_limit_bytes=VMEM_LIMIT)


def _mm_body(x_ref, w_ref, cs_ref, o_ref):
    acc = jnp.dot(x_ref[...].astype(BF), w_ref[...], preferred_element_type=F32)
    o_ref[...] = (acc * cs_ref[...]).astype(o_ref.dtype)


def _mm(x, w, col_scale, out_dtype, tm, tn):
    m, k = x.shape
    n = w.shape[1]
    return pl.pallas_call(
        _mm_body,
        grid=(n // tn, m // tm),
        in_specs=[
            pl.BlockSpec((tm, k), lambda j, i: (i, 0)),
            pl.BlockSpec((k, tn), lambda j, i: (0, j)),
            pl.BlockSpec((1, tn), lambda j, i: (0, j)),
        ],
        out_specs=pl.BlockSpec((tm, tn), lambda j, i: (i, j)),
        out_shape=jax.ShapeDtypeStruct((m, n), out_dtype),
        compiler_params=_params("parallel", "parallel"),
        name="proj",
    )(x, w, col_scale)


def _rms(x, g):
    return x * lax.rsqrt(jnp.mean(x * x, axis=-1, keepdims=True) + RMS_EPS) * g


def _rope_pair(slab, table):
    r = slab * table
    return r + pltpu.roll(r, MLA_ROPE, axis=1)


def _mla_prep_body(cq_ref, ckv_ref, krp_ref, rope_ref, qg_ref, kvg_ref, wqn_ref, wqp_ref, wk_ref, wv_ref,
                   qn_ref, qp_ref, kn_ref, v_ref, kp_ref):
    table = rope_ref[...]
    cqn = _rms(cq_ref[...], qg_ref[...]).astype(BF)
    qn_ref[...] = (jnp.dot(cqn, wqn_ref[...], preferred_element_type=F32) * MLA_SCALE).astype(BF)
    qp = jnp.dot(cqn, wqp_ref[...], preferred_element_type=F32)
    for h in range(MLA_HEADS):
        sl = slice(h * LANES, (h + 1) * LANES)
        qp_ref[:, sl] = (_rope_pair(qp[:, sl], table) * MLA_SCALE).astype(BF)
    ckvn = _rms(ckv_ref[...], kvg_ref[...]).astype(BF)
    kn_ref[...] = jnp.dot(ckvn, wk_ref[...], preferred_element_type=F32).astype(BF)
    v_ref[...] = jnp.dot(ckvn, wv_ref[...], preferred_element_type=F32).astype(BF)
    kr = _rope_pair(krp_ref[...], table)
    lane = lax.broadcasted_iota(I32, kr.shape, 1)
    kp_ref[...] = jnp.where(lane < MLA_ROPE, kr, 0.0).astype(BF)


def _mla_prep(small, rope_table, q_norm, kv_norm, wqn, wqp, wk, wv, seq, tm):
    t = small.shape[0]
    n_s = seq // tm
    hd = MLA_HEADS * LANES
    row = lambda i: (i, 0)
    const = lambda i: (0, 0)
    out = lambda w: pl.BlockSpec((tm, w), row)
    return pl.pallas_call(
        _mla_prep_body,
        grid=(t // tm,),
        in_specs=[
            pl.BlockSpec((tm, MLA_Q_LORA), lambda i: (i, 0)),
            pl.BlockSpec((tm, MLA_KV_LORA), lambda i: (i, MLA_Q_LORA // MLA_KV_LORA)),
            pl.BlockSpec((tm, LANES), lambda i: (i, (MLA_Q_LORA + MLA_KV_LORA) // LANES)),
            pl.BlockSpec((tm, LANES), lambda i: (i % n_s, 0)),
            pl.BlockSpec((1, MLA_Q_LORA), const),
            pl.BlockSpec((1, MLA_KV_LORA), const),
            pl.BlockSpec((MLA_Q_LORA, hd), const),
            pl.BlockSpec((MLA_Q_LORA, hd), const),
            pl.BlockSpec((MLA_KV_LORA, hd), const),
            pl.BlockSpec((MLA_KV_LORA, hd), const),
        ],
        out_specs=[out(hd), out(hd), out(hd), out(hd), out(LANES)],
        out_shape=[jax.ShapeDtypeStruct((t, hd), BF)] * 4 + [jax.ShapeDtypeStruct((t, LANES), BF)],
        compiler_params=_params("parallel"),
        name="mla_prep",
    )(small, small, small, rope_table, q_norm, kv_norm, wqn, wqp, wk, wv)


def _fox_cum_body(fl_ref, fb_ref, col_ref, row_ref, carry_ref):
    @pl.when(pl.program_id(1) == 0)
    def _():
        carry_ref[...] = jnp.zeros_like(carry_ref)

    z = fl_ref[...] + fb_ref[...]
    log_f = jnp.minimum(z, 0.0) - jnp.log1p(jnp.exp(-jnp.abs(z)))
    tm = z.shape[0]
    r = lax.broadcasted_iota(I32, (tm, tm), 0)
    c = lax.broadcasted_iota(I32, (tm, tm), 1)
    tri = jnp.where(r >= c, 1.0, 0.0).astype(F32)
    cum = jnp.dot(tri, log_f, precision=HIGHEST, preferred_element_type=F32) + carry_ref[...]
    carry_ref[...] = cum[tm - 1:tm, :]
    cum = cum * LOG2E
    col_ref[...] = cum
    er = lax.broadcasted_iota(I32, (SUBLANES, LANES), 0)
    ec = lax.broadcasted_iota(I32, (SUBLANES, LANES), 1)
    pick = jnp.where(er == ec, 1.0, 0.0).astype(F32)
    row_ref[0] = lax.dot_general(pick, cum, NT_DIMS, precision=HIGHEST, preferred_element_type=F32)


def _fox_cum(small, forget_bias_row, batch, seq, tm):
    t = small.shape[0]
    n_s = seq // tm
    fl_block = (MLA_Q_LORA + MLA_KV_LORA + LANES) // LANES
    return pl.pallas_call(
        _fox_cum_body,
        grid=(batch, n_s),
        in_specs=[
            pl.BlockSpec((tm, LANES), lambda b, i: (b * n_s + i, fl_block)),
            pl.BlockSpec((1, LANES), lambda b, i: (0, 0)),
        ],
        out_specs=[
            pl.BlockSpec((tm, LANES), lambda b, i: (b * n_s + i, 0)),
            pl.BlockSpec((1, SUBLANES, tm), lambda b, i: (b, 0, i)),
        ],
        out_shape=[jax.ShapeDtypeStruct((t, LANES), F32), jax.ShapeDtypeStruct((batch, SUBLANES, seq), F32)],
        scratch_shapes=[pltpu.VMEM((1, LANES), F32)],
        compiler_params=_params("parallel", "arbitrary"),
        name="fox_cum",
    )(small, forget_bias_row)


def _softmax_update(s, v_blk, m_ref, l_ref, acc_ref):
    m_prev = m_ref[...]
    m_new = jnp.maximum(m_prev, jnp.max(s, axis=1, keepdims=True))
    alpha = jnp.exp2(m_prev - m_new)
    p = jnp.exp2(s - jnp.concatenate([m_new] * (s.shape[1] // LANES), axis=1))
    l_ref[...] = alpha * l_ref[...] + jnp.sum(p, axis=1, keepdims=True)
    acc_ref[...] = alpha * acc_ref[...] + jnp.dot(p.astype(BF), v_blk, preferred_element_type=F32)
    m_ref[...] = m_new


def _attn_body(*refs, mode, t, group):
    hg = pl.program_id(1)
    i = pl.program_id(2)
    heads = range(group)
    sl = lambda g: slice(g * LANES, (g + 1) * LANES)
    if mode == "mla":
        qn_ref, qp_ref, kn_ref, kp_ref, v_ref, o_ref, m_ref, l_ref, acc_ref = refs
    elif mode == "fox":
        q_ref, k_ref, v_ref, cq_ref, ck_ref, o_ref, m_ref, l_ref, acc_ref = refs
        lane = lax.broadcasted_iota(I32, (t, LANES), 1)
        cq = [jnp.sum(jnp.where(lane == hg * group + g, cq_ref[...], 0.0), axis=1, keepdims=True)
              for g in heads]
    else:
        q_ref, k_ref, v_ref, km_ref, slope_ref, o_ref, m_ref, l_ref, acc_ref, sel_ref, dist_ref = refs
        n_kb = km_ref.shape[0]
        blk = lax.broadcasted_iota(I32, (t, n_kb), 1)
        r = lax.broadcasted_iota(I32, (t, t), 0)
        c = lax.broadcasted_iota(I32, (t, t), 1)
        slope = [slope_ref[hg * group + g] for g in heads]
        for g in heads:
            gate = lax.dot_general(q_ref[:, sl(g)], km_ref[:, sl(g)].astype(BF), NT_DIMS,
                                   preferred_element_type=F32)
            beaten = jnp.zeros((t, n_kb), F32)
            for n in range(n_kb):
                g_n = gate[:, n:n + 1]
                wins = (g_n > gate) | ((g_n == gate) & (blk > n))
                beaten = beaten + jnp.where(wins, 1.0, 0.0) * jnp.where(i > n, 1.0, 0.0)
            sel_ref[g] = jnp.where((blk < i) & (beaten < MOBA_TOPK), 1.0, 0.0)
            dist_ref[g] = slope[g] * (r - c).astype(F32)

    def scores(g, ks):
        if mode == "mla":
            q = jnp.concatenate([qn_ref[:, sl(g)], qp_ref[:, sl(g)]], axis=1)
            k = jnp.concatenate([kn_ref[pl.ds(ks, t), sl(g)], kp_ref[pl.ds(ks, t), :]], axis=1)
        else:
            q = q_ref[:, sl(g)]
            k = k_ref[pl.ds(ks, t), sl(g)]
        s = lax.dot_general(q, k, NT_DIMS, preferred_element_type=F32)
        if mode == "fox":
            s = s + (cq[g] - ck_ref[0, pl.ds(hg * group + g, 1), pl.ds(ks, t)])
        if mode == "moba":
            s = s - dist_ref[g]
        return s

    m_ref[...] = jnp.full(m_ref.shape, NEG, F32)
    l_ref[...] = jnp.zeros(l_ref.shape, F32)
    acc_ref[...] = jnp.zeros(acc_ref.shape, F32)

    ks = pl.multiple_of(i * t, t)
    r = lax.broadcasted_iota(I32, (t, t), 0)
    c = lax.broadcasted_iota(I32, (t, t), 1)
    for g in heads:
        s = jnp.where(c <= r, scores(g, ks), NEG)
        _softmax_update(s, v_ref[pl.ds(ks, t), sl(g)], m_ref.at[g], l_ref.at[g], acc_ref.at[g])

    def past_tile(j, carry):
        ks = pl.multiple_of(j * t, t)
        for g in heads:
            s = scores(g, ks)
            if mode == "moba":
                blk = lax.broadcasted_iota(I32, sel_ref.shape[1:], 1)
                chosen = jnp.sum(jnp.where(blk == j, sel_ref[g], 0.0), axis=1, keepdims=True)
                far = slope[g] * ((i - j) * t).astype(F32)
                s = s + jnp.where(chosen > 0.0, -far, NEG)
            _softmax_update(s, v_ref[pl.ds(ks, t), sl(g)], m_ref.at[g], l_ref.at[g], acc_ref.at[g])
        return carry

    lax.fori_loop(0, i, past_tile, 0)
    for g in heads:
        o_ref[:, sl(g)] = (acc_ref[g] / l_ref[g]).astype(o_ref.dtype)


def _attention(mode, batch, seq, heads, t, group, operands, in_specs, extra_scratch=()):
    n_q = seq // t
    body = functools.partial(_attn_body, mode=mode, t=t, group=group)
    return pl.pallas_call(
        body,
        grid=(batch, heads // group, n_q),
        in_specs=in_specs,
        out_specs=pl.BlockSpec((t, group * LANES), lambda b, h, i: (b * n_q + i, h)),
        out_shape=jax.ShapeDtypeStruct((batch * seq, heads * LANES), BF),
        scratch_shapes=[pltpu.VMEM((group, t, LANES), F32), pltpu.VMEM((group, t, LANES), F32),
                        pltpu.VMEM((group, t, LANES), F32), *extra_scratch],
        compiler_params=_params("parallel", "parallel", "arbitrary"),
        name="attn_" + mode,
    )(*operands)


def _q_spec(t, n_q, group, col0=0):
    return pl.BlockSpec((t, group * LANES), lambda b, h, i: (b * n_q + i, col0 + h))


def _kv_spec(seq, group, col0=0):
    return pl.BlockSpec((seq, group * LANES), lambda b, h, i: (b, col0 + h))


def _mla_attention(qn, qp, kn, v, kp, batch, seq, t, group):
    n_q = seq // t
    specs = [_q_spec(t, n_q, group), _q_spec(t, n_q, group), _kv_spec(seq, group),
             pl.BlockSpec((seq, LANES), lambda b, h, i: (b, 0)), _kv_spec(seq, group)]
    return _attention("mla", batch, seq, MLA_HEADS, t, group, (qn, qp, kn, kp, v), specs)


def _fox_attention(qkv, cum_col, cum_row, batch, seq, t, group):
    n_q = seq // t
    n_hg = FOX_HEADS // group
    specs = [_q_spec(t, n_q, group, 0), _kv_spec(seq, group, n_hg), _kv_spec(seq, group, 2 * n_hg),
             pl.BlockSpec((t, LANES), lambda b, h, i: (b * n_q + i, 0)),
             pl.BlockSpec((1, SUBLANES, seq), lambda b, h, i: (b, 0, 0))]
    return _attention("fox", batch, seq, FOX_HEADS, t, group, (qkv, qkv, qkv, cum_col, cum_row), specs)


def _moba_attention(big, k_mean, slopes, batch, seq, col_q, col_k, col_v, group):
    t = MOBA_BLOCK
    n_q = seq // t
    specs = [_q_spec(t, n_q, group, col_q // group), _kv_spec(seq, group, col_k // group),
             _kv_spec(seq, group, col_v // group),
             pl.BlockSpec((n_q, group * LANES), lambda b, h, i: (b, h)),
             pl.BlockSpec(memory_space=pltpu.SMEM)]
    scratch = (pltpu.VMEM((group, t, n_q), F32), pltpu.VMEM((group, t, t), F32))
    return _attention("moba", batch, seq, MOBA_HEADS, t, group, (big, big, big, k_mean, slopes), specs, scratch)


def _kmean_body(k_ref, o_ref):
    n_kb = o_ref.shape[0]
    for n in range(n_kb):
        blk = k_ref[n * MOBA_BLOCK:(n + 1) * MOBA_BLOCK, :].astype(F32)
        o_ref[n:n + 1, :] = jnp.mean(blk, axis=0, keepdims=True)


def _moba_kmean(big, batch, seq, col_block):
    n_kb = seq // MOBA_BLOCK
    width = MOBA_HEADS * MOBA_HEAD_DIM
    return pl.pallas_call(
        _kmean_body,
        grid=(batch,),
        in_specs=[pl.BlockSpec((seq, width), lambda b: (b, col_block))],
        out_specs=pl.BlockSpec((n_kb, width), lambda b: (b, 0)),
        out_shape=jax.ShapeDtypeStruct((batch * n_kb, width), F32),
        compiler_params=_params("parallel"),
        name="moba_kmean",
    )(big)


def _swa_body(sink_ref, q_ref, kp_ref, kc_ref, vp_ref, vc_ref, o_ref):
    i = pl.program_id(1)
    w = SWA_WINDOW
    half = SWA_HEAD_DIM
    lane = lax.broadcasted_iota(I32, (2 * w, LANES), 1)
    kcat = jnp.concatenate([kp_ref[...], kc_ref[...]], axis=0).astype(F32)
    vcat = jnp.concatenate([vp_ref[...], vc_ref[...]], axis=0).astype(F32)

    def lo_hi(x, kvh):
        own = jnp.where((lane < half) == (kvh == 0), x, 0.0)
        other = pltpu.roll(own, half, axis=1)
        lo, hi = (own, other) if kvh == 0 else (other, own)
        return lo.astype(BF), hi.astype(BF)

    r = lax.broadcasted_iota(I32, (w, 2 * w), 0)
    c = lax.broadcasted_iota(I32, (w, 2 * w), 1)
    dist = r + w - c
    valid = (dist >= 0) & (dist < w) & ((c >= w) | (i > 0))
    dist_f = dist.astype(F32)
    for kvh in range(SWA_KV_HEADS):
        k_lo, k_hi = lo_hi(kcat, kvh)
        v_lo, v_hi = lo_hi(vcat, kvh)
        pairs_per_kv = SWA_Q_HEADS // SWA_KV_HEADS // 2
        for pp in range(pairs_per_kv):
            pair = kvh * pairs_per_kv + pp
            q = q_ref[:, pair * LANES:(pair + 1) * LANES]
            out = jnp.zeros((w, LANES), F32)
            for k_side, v_side, head in ((k_lo, v_lo, 2 * pair), (k_hi, v_hi, 2 * pair + 1)):
                slope = float(2.0 ** (-8.0 * (head + 1) / SWA_Q_HEADS))
                s = lax.dot_general(q, k_side, NT_DIMS, preferred_element_type=F32)
                s = jnp.where(valid, s - slope * dist_f, NEG)
                sink = sink_ref[head]
                m = jnp.maximum(jnp.max(s, axis=1, keepdims=True), sink)
                p = jnp.exp(s - m)
                p = p / (jnp.sum(p, axis=1, keepdims=True) + jnp.exp(sink - m))
                out = out + jnp.dot(p.astype(BF), v_side, preferred_element_type=F32)
            o_ref[:, pair * LANES:(pair + 1) * LANES] = out.astype(o_ref.dtype)


def _swa_attention(big, sinks, batch, seq, col_q, col_k, col_v):
    w = SWA_WINDOW
    n_q = seq // w
    width = SWA_Q_HEADS * SWA_HEAD_DIM
    prev = lambda col: pl.BlockSpec((w, LANES), lambda b, i: (b * n_q + jnp.maximum(i - 1, 0), col))
    cur = lambda col: pl.BlockSpec((w, LANES), lambda b, i: (b * n_q + i, col))
    return pl.pallas_call(
        _swa_body,
        grid=(batch, n_q),
        in_specs=[pl.BlockSpec(memory_space=pltpu.SMEM),
                  pl.BlockSpec((w, width), lambda b, i: (b * n_q + i, col_q)),
                  prev(col_k), cur(col_k), prev(col_v), cur(col_v)],
        out_specs=pl.BlockSpec((w, width), lambda b, i: (b * n_q + i, 0)),
        out_shape=jax.ShapeDtypeStruct((batch * seq, width), BF),
        compiler_params=_params("parallel", "parallel"),
        name="swa",
    )(sinks, big, big, big, big, big)


def _layer_norm(z, g, b):
    mu = jnp.mean(z, axis=-1, keepdims=True)
    zc = z - mu
    var = jnp.mean(zc * zc, axis=-1, keepdims=True)
    return zc * lax.rsqrt(var + LN_EPS) * g + b


def _outproj_body(a_ref, b_ref, wa_ref, wb_ref, h_ref, g_ref, beta_ref, o_ref):
    mix = jnp.dot(a_ref[...], wa_ref[...], preferred_element_type=F32)
    mix = mix + jnp.dot(b_ref[...], wb_ref[...], preferred_element_type=F32)
    o_ref[...] = _layer_norm(ALPHA * h_ref[...] + mix, g_ref[...], beta_ref[...])


def _outproj_ln(a, b, w_out, h, g, beta, tm):
    t, d = h.shape
    ka = a.shape[1]
    row = lambda i: (i, 0)
    return pl.pallas_call(
        _outproj_body,
        grid=(t // tm,),
        in_specs=[pl.BlockSpec((tm, ka), row), pl.BlockSpec((tm, ka), row),
                  pl.BlockSpec((ka, d), lambda i: (0, 0)), pl.BlockSpec((ka, d), lambda i: (1, 0)),
                  pl.BlockSpec((tm, d), row), pl.BlockSpec((1, d), lambda i: (0, 0)),
                  pl.BlockSpec((1, d), lambda i: (0, 0))],
        out_specs=pl.BlockSpec((tm, d), row),
        out_shape=jax.ShapeDtypeStruct((t, d), F32),
        compiler_params=_params("parallel"),
        name="outproj_ln",
    )(a, b, w_out, w_out, h, g, beta)


def _second_of_four(a, b, c, d):
    hi_ab, lo_ab = jnp.maximum(a, b), jnp.minimum(a, b)
    hi_cd, lo_cd = jnp.maximum(c, d), jnp.minimum(c, d)
    return jnp.maximum(jnp.maximum(lo_ab, lo_cd), jnp.minimum(hi_ab, hi_cd))


def _router_body(h_ref, wr_ref, bias_ref, e_ref, r_ref):
    tm = h_ref.shape[0]
    logits = jnp.dot(h_ref[...], wr_ref[...], precision=HIGHEST, preferred_element_type=F32)
    lt = logits.T
    aff = [jax.nn.sigmoid(lt[SUBLANES * j:SUBLANES * (j + 1), :]) for j in range(EXPERTS_PER_GROUP)]
    sel = [aff[j] + bias_ref[SUBLANES * j:SUBLANES * (j + 1), :] for j in range(EXPERTS_PER_GROUP)]
    top1 = jnp.maximum(jnp.maximum(sel[0], sel[1]), jnp.maximum(sel[2], sel[3]))
    score = top1 + _second_of_four(*sel)
    gid = lax.broadcasted_iota(I32, (N_GROUPS, tm), 0)
    best = jnp.min(jnp.where(score == jnp.max(score, axis=0, keepdims=True), gid, N_GROUPS),
                   axis=0, keepdims=True)
    in_grp = gid == best
    pick = lambda x: jnp.sum(jnp.where(in_grp, x, 0.0), axis=0, keepdims=True)
    s4 = [pick(x) for x in sel]
    a4 = [pick(x) for x in aff]

    def argmax4(vals):
        j, v = jnp.zeros((1, tm), I32), vals[0]
        for n in range(1, EXPERTS_PER_GROUP):
            better = vals[n] > v
            j, v = jnp.where(better, n, j), jnp.where(better, vals[n], v)
        return j

    j0 = argmax4(s4)
    j1 = argmax4([jnp.where(j0 == n, -jnp.inf, s4[n]) for n in range(EXPERTS_PER_GROUP)])
    take = lambda j: sum(jnp.where(j == n, a4[n], 0.0) for n in range(EXPERTS_PER_GROUP))
    w0, w1 = take(j0), take(j1)
    total = w0 + w1
    e0 = best * EXPERTS_PER_GROUP + j0
    e1 = best * EXPERTS_PER_GROUP + j1
    rid = lax.broadcasted_iota(I32, (SUBLANES, tm), 0)
    e_ref[...] = jnp.where(rid == 0, e0, jnp.where(rid == 1, e1, 0))
    rid = lax.broadcasted_iota(I32, (LANES, tm), 0)
    rows = jnp.where(rid == 0, w0 / total, jnp.where(rid == 1, w1 / total, 0.0))
    r_ref[...] = rows.T


def _router(h, wr_perm, bias_perm, tm):
    t, d = h.shape
    return pl.pallas_call(
        _router_body,
        grid=(t // tm,),
        in_specs=[pl.BlockSpec((tm, d), lambda i: (i, 0)), pl.BlockSpec((d, LANES), lambda i: (0, 0)),
                  pl.BlockSpec((N_EXPERTS, 1), lambda i: (0, 0))],
        out_specs=[pl.BlockSpec((SUBLANES, tm), lambda i: (0, i)), pl.BlockSpec((tm, LANES), lambda i: (i, 0))],
        out_shape=[jax.ShapeDtypeStruct((SUBLANES, t), I32), jax.ShapeDtypeStruct((t, LANES), F32)],
        compiler_params=_params("parallel"),
        name="router",
    )(h, wr_perm, bias_perm)


def _dispatch_plan(e01, tm):
    t = e01.shape[1]
    flat = e01.reshape(-1)
    onehot = (flat[:, None] == jnp.arange(N_EXPERTS, dtype=I32)[None, :]).astype(I32)
    cum = jnp.cumsum(onehot, axis=0)
    rank = jnp.sum(onehot * cum, axis=1) - 1
    counts = cum[-1]
    padded = ((counts + tm - 1) // tm) * tm
    starts = jnp.concatenate([jnp.zeros((1,), I32), jnp.cumsum(padded).astype(I32)])
    pos = jnp.sum(onehot * starts[None, :N_EXPERTS], axis=1) + rank
    tok = jnp.tile(jnp.arange(t, dtype=I32), 2)
    n_rows = 2 * t + N_EXPERTS * tm
    row_token = jnp.zeros((n_rows,), I32).at[pos].set(tok)
    return row_token, starts, pos[:t], pos[t:]


def _experts_body(rt_ref, gs_ref, h_hbm, wgu_ref, wd_ref, ys_hbm, xbuf, ybuf, wgu_bf, wd_bf, gsem, osem, *, tm):
    e = pl.program_id(0)
    first = gs_ref[e] // tm
    n_tiles = gs_ref[e + 1] // tm - first
    total = gs_ref[N_EXPERTS] // tm

    def row_copy(base, r, slot):
        return pltpu.make_async_copy(h_hbm.at[pl.ds(rt_ref[base + r], 1), :], xbuf.at[slot, pl.ds(r, 1), :],
                                     gsem.at[slot])

    def wait_rows(slot):
        pltpu.make_async_copy(h_hbm.at[pl.ds(0, tm), :], xbuf.at[slot], gsem.at[slot]).wait()

    def out_copy(g, slot):
        row0 = pl.multiple_of(g * tm, tm)
        return pltpu.make_async_copy(ybuf.at[slot], ys_hbm.at[pl.ds(row0, tm), :], osem.at[slot])

    @pl.when(e == 0)
    def _():
        def one(r, carry):
            row_copy(0, r, 0).start()
            return carry

        lax.fori_loop(0, tm, one, 0)
        ybuf[...] = jnp.zeros(ybuf.shape, F32)
        out_copy(0, 0).start()
        out_copy(1, 1).start()

    @pl.when(n_tiles > 0)
    def _():
        wgu_bf[...] = wgu_ref[0, 0].astype(BF)
        wd_bf[...] = wd_ref[0, 0].astype(BF)

        def tile(k, carry):
            g = first + k
            slot = g % 2
            wait_rows(slot)
            out_copy(g, slot).wait()
            nxt = jnp.minimum(g + 1, total - 1) * tm
            for r in range(tm):
                row_copy(nxt, r, 1 - slot).start()
            x = xbuf[slot].astype(BF)
            gu = jnp.dot(x, wgu_bf[...], preferred_element_type=F32)
            gate, up = gu[:, :EXPERT_FF], gu[:, EXPERT_FF:]
            hidden = (gate * jax.nn.sigmoid(gate) * up).astype(BF)
            ybuf[slot] = jnp.dot(hidden, wd_bf[...], preferred_element_type=F32)
            out_copy(g, slot).start()
            return carry

        lax.fori_loop(0, n_tiles, tile, 0)

    @pl.when(e == N_EXPERTS - 1)
    def _():
        wait_rows(total % 2)
        out_copy(0, 0).wait()
        out_copy(1, 1).wait()
        end = gs_ref[N_EXPERTS]
        n_tail = (ys_hbm.shape[0] - end) // tm
        ybuf[0] = jnp.zeros(ybuf.shape[1:], F32)

        def tail_copy(k):
            row0 = pl.multiple_of(end + k * tm, tm)
            return pltpu.make_async_copy(ybuf.at[0], ys_hbm.at[pl.ds(row0, tm), :], osem.at[0])

        def start_one(k, carry):
            tail_copy(k).start()
            return carry

        def wait_one(k, carry):
            tail_copy(k).wait()
            return carry

        lax.fori_loop(0, n_tail, start_one, 0)
        lax.fori_loop(0, n_tail, wait_one, 0)


def _experts(h, row_token, starts, w_gate_up, w_down, layer, tm):
    t, d = h.shape
    n_rows = row_token.shape[0]
    body = functools.partial(_experts_body, tm=tm)
    grid_spec = pltpu.PrefetchScalarGridSpec(
        num_scalar_prefetch=2,
        grid=(N_EXPERTS,),
        in_specs=[pl.BlockSpec(memory_space=pl.ANY),
                  pl.BlockSpec((1, 1, d, 2 * EXPERT_FF), lambda e, rt, gs: (layer, e, 0, 0)),
                  pl.BlockSpec((1, 1, EXPERT_FF, d), lambda e, rt, gs: (layer, e, 0, 0))],
        out_specs=pl.BlockSpec(memory_space=pl.ANY),
        scratch_shapes=[pltpu.VMEM((2, tm, d), F32), pltpu.VMEM((2, tm, d), F32),
                        pltpu.VMEM((d, 2 * EXPERT_FF), BF), pltpu.VMEM((EXPERT_FF, d), BF),
                        pltpu.SemaphoreType.DMA((2,)), pltpu.SemaphoreType.DMA((2,))],
    )
    return pl.pallas_call(
        body,
        grid_spec=grid_spec,
        out_shape=jax.ShapeDtypeStruct((n_rows, d), F32),
        compiler_params=_params("arbitrary"),
        name="experts",
    )(row_token, starts, h, w_gate_up, w_down)


def _combine_body(p0_ref, p1_ref, ys_hbm, h_ref, r_ref, g_ref, beta_ref, o_ref, buf0, buf1, sem, *, tc, n_steps):
    i = pl.program_id(0)
    slot = i % 2

    def start_row(base, r, sl):
        pltpu.make_async_copy(ys_hbm.at[pl.ds(p0_ref[base + r], 1), :], buf0.at[sl, pl.ds(r, 1), :],
                              sem.at[sl]).start()
        pltpu.make_async_copy(ys_hbm.at[pl.ds(p1_ref[base + r], 1), :], buf1.at[sl, pl.ds(r, 1), :],
                              sem.at[sl]).start()

    def wait_rows(sl):
        pltpu.make_async_copy(ys_hbm.at[pl.ds(0, tc), :], buf0.at[sl], sem.at[sl]).wait()
        pltpu.make_async_copy(ys_hbm.at[pl.ds(0, tc), :], buf1.at[sl], sem.at[sl]).wait()

    @pl.when(i == 0)
    def _():
        def one(r, carry):
            start_row(0, r, 0)
            return carry

        lax.fori_loop(0, tc, one, 0)

    wait_rows(slot)
    nxt = jnp.minimum(i + 1, n_steps - 1) * tc
    for r in range(tc):
        start_row(nxt, r, 1 - slot)
    gates = r_ref[...]
    y = gates[:, 0:1] * buf0[slot] + gates[:, 1:2] * buf1[slot]
    o_ref[...] = _layer_norm(ALPHA * h_ref[...] + y, g_ref[...], beta_ref[...])

    @pl.when(i == n_steps - 1)
    def _():
        wait_rows(1 - slot)


def _combine_ln(ys, pos0, pos1, h, gates, g, beta, tc):
    t, d = h.shape
    n_steps = t // tc
    body = functools.partial(_combine_body, tc=tc, n_steps=n_steps)
    row = lambda i, p0, p1: (i, 0)
    const = lambda i, p0, p1: (0, 0)
    grid_spec = pltpu.PrefetchScalarGridSpec(
        num_scalar_prefetch=2,
        grid=(n_steps,),
        in_specs=[pl.BlockSpec(memory_space=pl.ANY), pl.BlockSpec((tc, d), row), pl.BlockSpec((tc, LANES), row),
                  pl.BlockSpec((1, d), const), pl.BlockSpec((1, d), const)],
        out_specs=pl.BlockSpec((tc, d), row),
        scratch_shapes=[pltpu.VMEM((2, tc, d), F32), pltpu.VMEM((2, tc, d), F32), pltpu.SemaphoreType.DMA((2,))],
    )
    return pl.pallas_call(
        body,
        grid_spec=grid_spec,
        out_shape=jax.ShapeDtypeStruct((t, d), F32),
        compiler_params=_params("arbitrary"),
        name="combine_ln",
    )(pos0, pos1, ys, h, gates, g, beta)


def _moe_ln(h, wr_perm, bias_perm, w_gate_up, w_down, layer, g, beta, tm_router=512, tm_expert=256, tc=256):
    e01, gates = _router(h, wr_perm, bias_perm, tm_router)
    row_token, starts, pos0, pos1 = _dispatch_plan(e01[:MOE_TOPK], tm_expert)
    ys = _experts(h, row_token, starts, w_gate_up, w_down, layer, tm_expert)
    return _combine_ln(ys, pos0, pos1, h, gates, g, beta, tc)


def _swap_halves(w):
    half = w.shape[-1] // 2
    return jnp.concatenate([w[..., half:], w[..., :half]], axis=-1)


def _even_weights(w_in, w_q_up, w_kv_up, forget_bias):
    d = w_in.shape[0]
    o_kv = MLA_Q_LORA
    o_kr = o_kv + MLA_KV_LORA
    o_fq = o_kr + MLA_ROPE
    hd = FOX_HEADS * FOX_HEAD_DIM
    o_fl = o_fq + 3 * hd
    k_r = w_in[:, o_kr:o_fq]
    f_l = jnp.pad(w_in[:, o_fl:], ((0, 0), (0, LANES - FOX_HEADS)))
    w_small = jnp.concatenate([w_in[:, :o_kr], k_r, _swap_halves(k_r), f_l], axis=1).astype(BF)
    w_fox = w_in[:, o_fq:o_fl].astype(BF)
    cs_fox = jnp.concatenate([jnp.full((1, hd), FOX_SCALE, F32), jnp.ones((1, 2 * hd), F32)], axis=1)
    wq = w_q_up.reshape(MLA_Q_LORA, MLA_HEADS, MLA_NOPE + MLA_ROPE)
    wqn = wq[:, :, :MLA_NOPE].reshape(MLA_Q_LORA, -1).astype(BF)
    pe = wq[:, :, MLA_NOPE:]
    wqp = jnp.concatenate([pe, _swap_halves(pe)], axis=-1).reshape(MLA_Q_LORA, -1).astype(BF)
    wkv = w_kv_up.reshape(MLA_KV_LORA, MLA_HEADS, MLA_NOPE + MLA_V)
    wk = wkv[:, :, :MLA_NOPE].reshape(MLA_KV_LORA, -1).astype(BF)
    wv = wkv[:, :, MLA_NOPE:].reshape(MLA_KV_LORA, -1).astype(BF)
    fb = jnp.pad(forget_bias.astype(F32), (0, LANES - FOX_HEADS)).reshape(1, LANES)
    return w_small, w_fox, cs_fox, wqn, wqp, wk, wv, fb


def _rope_table(seq):
    half = MLA_ROPE // 2
    inv_freq = ROPE_THETA ** (-jnp.arange(half, dtype=F32) / half)
    ang = jnp.arange(seq).astype(F32)[:, None] * inv_freq[None, :]
    cos, sin = jnp.cos(ang), jnp.sin(ang)
    return jnp.concatenate([cos, cos, -sin, sin], axis=1)


def _router_weights(w_router, router_bias):
    r = np.arange(N_EXPERTS)
    perm = (r % N_GROUPS) * EXPERTS_PER_GROUP + r // N_GROUPS
    wr = jnp.pad(w_router[:, perm], ((0, 0), (0, LANES - N_EXPERTS)))
    return wr, router_bias.astype(F32)[perm].reshape(N_EXPERTS, 1)


def _even_layer(h, batch, seq, w_in, q_norm, w_q_up, kv_norm, w_kv_up, forget_bias, w_out, g, beta):
    w_small, w_fox, cs_fox, wqn, wqp, wk, wv, fb = _even_weights(w_in, w_q_up, w_kv_up, forget_bias)
    small = _mm(h, w_small, jnp.ones((1, w_small.shape[1]), F32), F32, 512, 1024)
    fox_qkv = _mm(h, w_fox, cs_fox, BF, 512, 1536)
    qn, qp, kn, v, kp = _mla_prep(small, _rope_table(seq), q_norm.reshape(1, -1), kv_norm.reshape(1, -1),
                                  wqn, wqp, wk, wv, seq, 512)
    cum_col, cum_row = _fox_cum(small, fb, batch, seq, 512)
    a = _mla_attention(qn, qp, kn, v, kp, batch, seq, 512, 2)
    bo = _fox_attention(fox_qkv, cum_col, cum_row, batch, seq, 512, 2)
    return _outproj_ln(a, bo, w_out.astype(BF), h, g, beta, 512)


def _odd_layer(h, batch, seq, w_in, sinks, w_out, g, beta):
    n_sq = SWA_Q_HEADS * SWA_HEAD_DIM
    n_skv = SWA_KV_HEADS * SWA_HEAD_DIM
    n_m = MOBA_HEADS * MOBA_HEAD_DIM
    o_mq = n_sq + 2 * n_skv
    w_big = jnp.concatenate([w_in[:, :n_sq], w_in[:, o_mq:], w_in[:, n_sq:o_mq]], axis=1).astype(BF)
    cs = jnp.concatenate([jnp.full((1, n_sq), SWA_SCALE, F32), jnp.full((1, n_m), MOBA_SCALE, F32),
                          jnp.ones((1, 2 * n_m + 2 * n_skv), F32)], axis=1)
    big = _mm(h, w_big, cs, BF, 512, 2176)
    blocks = lambda cols: cols // LANES
    k_mean = _moba_kmean(big, batch, seq, 2)
    slopes = jnp.asarray(LOG2E * 2.0 ** (-8.0 * np.arange(1, MOBA_HEADS + 1) / MOBA_HEADS), dtype=F32)
    c = _swa_attention(big, sinks.astype(F32), batch, seq, 0, blocks(n_sq + 3 * n_m), blocks(n_sq + 3 * n_m) + 1)
    dd = _moba_attention(big, k_mean, slopes, batch, seq, blocks(n_sq), blocks(n_sq + n_m), blocks(n_sq + 2 * n_m),
                         4)
    return _outproj_ln(c, dd, w_out.astype(BF), h, g, beta, 512)


def kernel(x, w_router, router_bias, even_w_in, even_q_norm, even_w_q_up, even_kv_norm, even_w_kv_up,
           even_forget_bias, even_w_out, odd_w_in, odd_sinks, odd_w_out, ln_mix_g, ln_mix_b, ln_ffn_g, ln_ffn_b,
           w_gate_up, w_down):
    batch, seq, d = x.shape
    h = x.reshape(batch * seq, d)
    wr_perm, bias_perm = _router_weights(w_router, router_bias)
    row = lambda p, layer: p[layer].reshape(1, d)
    for layer in range(DEPTH):
        i = layer // 2
        if layer % 2 == 0:
            h = _even_layer(h, batch, seq, even_w_in[i], even_q_norm[i], even_w_q_up[i], even_kv_norm[i],
                            even_w_kv_up[i], even_forget_bias[i], even_w_out[i], row(ln_mix_g, layer),
                            row(ln_mix_b, layer))
        else:
            h = _odd_layer(h, batch, seq, odd_w_in[i], odd_sinks[i], odd_w_out[i], row(ln_mix_g, layer),
                           row(ln_mix_b, layer))
        h = _moe_ln(h, wr_perm, bias_perm, w_gate_up, w_down, layer, row(ln_ffn_g, layer), row(ln_ffn_b, layer))
    return h.reshape(batch, seq, d)
```

```python
import functools

import numpy as np
import jax
import jax.numpy as jnp
from jax import lax
from jax.experimental import pallas as pl
from jax.experimental.pallas import tpu as pltpu

BF = jnp.bfloat16
F32 = jnp.float32
I32 = jnp.int32
HIGHEST = lax.Precision.HIGHEST

LANES = 128
SUBLANES = 8
VMEM_LIMIT = 56 * 1024 * 1024

D_MODEL = 2048
DEPTH = 2
LN_EPS = 1e-5
RMS_EPS = 1e-6
ALPHA = (2 * DEPTH) ** 0.25
MLA_HEADS, MLA_Q_LORA, MLA_KV_LORA, MLA_NOPE, MLA_ROPE, MLA_V = 8, 512, 256, 128, 64, 128
ROPE_THETA = 10000.0
FOX_HEADS, FOX_HEAD_DIM = 8, 128
SWA_Q_HEADS, SWA_KV_HEADS, SWA_HEAD_DIM, SWA_WINDOW = 16, 2, 64, 128
MOBA_HEADS, MOBA_HEAD_DIM, MOBA_BLOCK, MOBA_TOPK = 8, 128, 256, 3
N_EXPERTS, N_GROUPS, MOE_TOPK, EXPERT_FF = 32, 8, 2, 512
EXPERTS_PER_GROUP = N_EXPERTS // N_GROUPS

LOG2E = 1.4426950408889634
MLA_SCALE = (MLA_NOPE + MLA_ROPE) ** -0.5 * LOG2E
FOX_SCALE = FOX_HEAD_DIM ** -0.5 * LOG2E
MOBA_SCALE = MOBA_HEAD_DIM ** -0.5 * LOG2E
SWA_SCALE = SWA_HEAD_DIM ** -0.5
NEG = -1e30

NT_DIMS = (((1,), (1,)), ((), ()))


def _params(*sem):
    return pltpu.CompilerParams(dimension_semantics=sem, vmem_limit_bytes=VMEM_LIMIT)


def _mm_body(x_ref, w_ref, cs_ref, o_ref):
    acc = jnp.dot(x_ref[...].astype(BF), w_ref[...], preferred_element_type=F32)
    o_ref[...] = (acc * cs_ref[...]).astype(o_ref.dtype)


def _mm(x, w, col_scale, out_dtype, tm, tn):
    m, k = x.shape
    n = w.shape[1]
    return pl.pallas_call(
        _mm_body,
        grid=(n // tn, m // tm),
        in_specs=[
            pl.BlockSpec((tm, k), lambda j, i: (i, 0)),
            pl.BlockSpec((k, tn), lambda j, i: (0, j)),
            pl.BlockSpec((1, tn), lambda j, i: (0, j)),
        ],
        out_specs=pl.BlockSpec((tm, tn), lambda j, i: (i, j)),
        out_shape=jax.ShapeDtypeStruct((m, n), out_dtype),
        compiler_params=_params("parallel", "parallel"),
        name="proj",
    )(x, w, col_scale)


def _rms(x, g):
    return x * lax.rsqrt(jnp.mean(x * x, axis=-1, keepdims=True) + RMS_EPS) * g


def _rope_pair(slab, table):
    r = slab * table
    return r + pltpu.roll(r, MLA_ROPE, axis=1)


def _mla_prep_body(cq_ref, ckv_ref, krp_ref, rope_ref, qg_ref, kvg_ref, wqn_ref, wqp_ref, wk_ref, wv_ref,
                   qn_ref, qp_ref, kn_ref, v_ref, kp_ref):
    table = rope_ref[...]
    cqn = _rms(cq_ref[...], qg_ref[...]).astype(BF)
    qn_ref[...] = (jnp.dot(cqn, wqn_ref[...], preferred_element_type=F32) * MLA_SCALE).astype(BF)
    qp = jnp.dot(cqn, wqp_ref[...], preferred_element_type=F32)
    for h in range(MLA_HEADS):
        sl = slice(h * LANES, (h + 1) * LANES)
        qp_ref[:, sl] = (_rope_pair(qp[:, sl], table) * MLA_SCALE).astype(BF)
    ckvn = _rms(ckv_ref[...], kvg_ref[...]).astype(BF)
    kn_ref[...] = jnp.dot(ckvn, wk_ref[...], preferred_element_type=F32).astype(BF)
    v_ref[...] = jnp.dot(ckvn, wv_ref[...], preferred_element_type=F32).astype(BF)
    kr = _rope_pair(krp_ref[...], table)
    lane = lax.broadcasted_iota(I32, kr.shape, 1)
    kp_ref[...] = jnp.where(lane < MLA_ROPE, kr, 0.0).astype(BF)


def _mla_prep(small, rope_table, q_norm, kv_norm, wqn, wqp, wk, wv, seq, tm):
    t = small.shape[0]
    n_s = seq // tm
    hd = MLA_HEADS * LANES
    row = lambda i: (i, 0)
    const = lambda i: (0, 0)
    out = lambda w: pl.BlockSpec((tm, w), row)
    return pl.pallas_call(
        _mla_prep_body,
        grid=(t // tm,),
        in_specs=[
            pl.BlockSpec((tm, MLA_Q_LORA), lambda i: (i, 0)),
            pl.BlockSpec((tm, MLA_KV_LORA), lambda i: (i, MLA_Q_LORA // MLA_KV_LORA)),
            pl.BlockSpec((tm, LANES), lambda i: (i, (MLA_Q_LORA + MLA_KV_LORA) // LANES)),
            pl.BlockSpec((tm, LANES), lambda i: (i % n_s, 0)),
            pl.BlockSpec((1, MLA_Q_LORA), const),
            pl.BlockSpec((1, MLA_KV_LORA), const),
            pl.BlockSpec((MLA_Q_LORA, hd), const),
            pl.BlockSpec((MLA_Q_LORA, hd), const),
            pl.BlockSpec((MLA_KV_LORA, hd), const),
            pl.BlockSpec((MLA_KV_LORA, hd), const),
        ],
        out_specs=[out(hd), out(hd), out(hd), out(hd), out(LANES)],
        out_shape=[jax.ShapeDtypeStruct((t, hd), BF)] * 4 + [jax.ShapeDtypeStruct((t, LANES), BF)],
        compiler_params=_params("parallel"),
        name="mla_prep",
    )(small, small, small, rope_table, q_norm, kv_norm, wqn, wqp, wk, wv)


N_SPLIT = 3


def _split_piece(x, which):
    hi = x.astype(BF).astype(F32)
    rest = x - hi
    mid = rest.astype(BF).astype(F32)
    return jnp.where(which == 0, hi, jnp.where(which == 1, mid, rest - mid))


def _fox_cum_body(fl_ref, fb_ref, qe_ref, ke_ref, carry_ref):
    @pl.when(pl.program_id(1) == 0)
    def _():
        carry_ref[...] = jnp.zeros_like(carry_ref)

    z = fl_ref[...] + fb_ref[...]
    log_f = jnp.minimum(z, 0.0) - jnp.log1p(jnp.exp(-jnp.abs(z)))
    tm = z.shape[0]
    r = lax.broadcasted_iota(I32, (tm, tm), 0)
    c = lax.broadcasted_iota(I32, (tm, tm), 1)
    tri = jnp.where(r >= c, 1.0, 0.0).astype(F32)
    cum = jnp.dot(tri, log_f, precision=HIGHEST, preferred_element_type=F32) + carry_ref[...]
    carry_ref[...] = cum[tm - 1:tm, :]
    cum = cum * LOG2E
    width = qe_ref.shape[1]
    hr = lax.broadcasted_iota(I32, (LANES, width), 0)
    hc = lax.broadcasted_iota(I32, (LANES, width), 1)
    spread = jnp.where((hc // LANES == hr) & (hc % LANES < 2 * N_SPLIT), 1.0, 0.0).astype(F32)
    rep = jnp.dot(cum, spread, precision=HIGHEST, preferred_element_type=F32)
    piece = _split_piece(rep, lax.broadcasted_iota(I32, rep.shape, 1) % N_SPLIT)
    li = lax.broadcasted_iota(I32, rep.shape, 1) % LANES
    qe_ref[...] = jnp.where(li < N_SPLIT, piece, jnp.where(li < 2 * N_SPLIT, 1.0, 0.0)).astype(BF)
    ke_ref[...] = jnp.where(li < N_SPLIT, 1.0, jnp.where(li < 2 * N_SPLIT, -piece, 0.0)).astype(BF)


def _fox_cum(small, forget_bias_row, batch, seq, tm):
    t = small.shape[0]
    n_s = seq // tm
    width = FOX_HEADS * LANES
    fl_block = (MLA_Q_LORA + MLA_KV_LORA + LANES) // LANES
    return pl.pallas_call(
        _fox_cum_body,
        grid=(batch, n_s),
        in_specs=[
            pl.BlockSpec((tm, LANES), lambda b, i: (b * n_s + i, fl_block)),
            pl.BlockSpec((1, LANES), lambda b, i: (0, 0)),
        ],
        out_specs=[
            pl.BlockSpec((tm, width), lambda b, i: (b * n_s + i, 0)),
            pl.BlockSpec((tm, width), lambda b, i: (b * n_s + i, 0)),
        ],
        out_shape=[jax.ShapeDtypeStruct((t, width), BF), jax.ShapeDtypeStruct((t, width), BF)],
        scratch_shapes=[pltpu.VMEM((1, LANES), F32)],
        compiler_params=_params("parallel", "arbitrary"),
        name="fox_cum",
    )(small, forget_bias_row)


def _softmax_update(s_all, v_all, m_ref, l_ref, acc_ref):
    heads = range(len(s_all))
    m_prev = [m_ref[g] for g in heads]
    m_new = [jnp.maximum(m_prev[g], jnp.max(s_all[g], axis=1, keepdims=True)) for g in heads]
    reps = s_all[0].shape[1] // LANES
    p = [jnp.exp2(s_all[g] - jnp.concatenate([m_new[g]] * reps, axis=1)) for g in heads]
    pv = [jnp.dot(p[g].astype(BF), v_all[g], preferred_element_type=F32) for g in heads]
    for g in heads:
        alpha = jnp.exp2(m_prev[g] - m_new[g])
        l_ref[g] = alpha * l_ref[g] + jnp.sum(p[g], axis=1, keepdims=True)
        acc_ref[g] = alpha * acc_ref[g] + pv[g]
        m_ref[g] = m_new[g]


def _attn_body(*refs, mode, t, group):
    hg = pl.program_id(1)
    i = pl.program_id(2)
    heads = range(group)
    sl = lambda g: slice(g * LANES, (g + 1) * LANES)
    if mode == "moba":
        q_ref, k_ref, kx_ref, v_ref, km_ref, slope_ref, o_ref, m_ref, l_ref, acc_ref, qx_ref = refs
        for g in heads:
            qx_ref[g] = _moba_query_lanes(q_ref[:, sl(g)], km_ref[:, sl(g)], slope_ref[hg * group + g], i, t)
        q_extra = lambda g: qx_ref[g]
        k_extra = lambda g, ks: kx_ref[pl.ds(ks, t), :]
    else:
        q_ref, qx_ref, k_ref, kx_ref, v_ref, o_ref, m_ref, l_ref, acc_ref = refs
        q_extra = lambda g: qx_ref[:, sl(g)]
        if mode == "mla":
            k_extra = lambda g, ks: kx_ref[pl.ds(ks, t), :]
        else:
            k_extra = lambda g, ks: kx_ref[pl.ds(ks, t), sl(g)]

    def scores(g, ks):
        q = jnp.concatenate([q_ref[:, sl(g)], q_extra(g)], axis=1)
        k = jnp.concatenate([k_ref[pl.ds(ks, t), sl(g)], k_extra(g, ks)], axis=1)
        return lax.dot_general(q, k, NT_DIMS, preferred_element_type=F32)

    m_ref[...] = jnp.full(m_ref.shape, NEG, F32)
    l_ref[...] = jnp.zeros(l_ref.shape, F32)
    acc_ref[...] = jnp.zeros(acc_ref.shape, F32)

    ks = pl.multiple_of(i * t, t)
    r = lax.broadcasted_iota(I32, (t, t), 0)
    c = lax.broadcasted_iota(I32, (t, t), 1)
    values = lambda ks: [v_ref[pl.ds(ks, t), sl(g)] for g in heads]
    _softmax_update([jnp.where(c <= r, scores(g, ks), NEG) for g in heads], values(ks), m_ref, l_ref, acc_ref)

    def past_tile(j, carry):
        ks = pl.multiple_of(j * t, t)
        _softmax_update([scores(g, ks) for g in heads], values(ks), m_ref, l_ref, acc_ref)
        return carry

    lax.fori_loop(0, i, past_tile, 0)
    for g in heads:
        o_ref[:, sl(g)] = (acc_ref[g] / l_ref[g]).astype(o_ref.dtype)


def _attention(mode, batch, seq, heads, t, group, operands, in_specs, extra_scratch=()):
    n_q = seq // t
    body = functools.partial(_attn_body, mode=mode, t=t, group=group)
    return pl.pallas_call(
        body,
        grid=(batch, heads // group, n_q),
        in_specs=in_specs,
        out_specs=pl.BlockSpec((t, group * LANES), lambda b, h, i: (b * n_q + i, h)),
        out_shape=jax.ShapeDtypeStruct((batch * seq, heads * LANES), BF),
        scratch_shapes=[pltpu.VMEM((group, t, LANES), F32), pltpu.VMEM((group, t, LANES), F32),
                        pltpu.VMEM((group, t, LANES), F32), *extra_scratch],
        compiler_params=_params("parallel", "parallel", "arbitrary"),
        name="attn_" + mode,
    )(*operands)


def _q_spec(t, n_q, group, col0=0):
    return pl.BlockSpec((t, group * LANES), lambda b, h, i: (b * n_q + i, col0 + h))


def _kv_spec(seq, group, col0=0):
    return pl.BlockSpec((seq, group * LANES), lambda b, h, i: (b, col0 + h))


def _mla_attention(qn, qp, kn, v, kp, batch, seq, t, group):
    n_q = seq // t
    specs = [_q_spec(t, n_q, group), _q_spec(t, n_q, group), _kv_spec(seq, group),
             pl.BlockSpec((seq, LANES), lambda b, h, i: (b, 0)), _kv_spec(seq, group)]
    return _attention("mla", batch, seq, MLA_HEADS, t, group, (qn, qp, kn, kp, v), specs)


def _fox_attention(qkv, q_bias, k_bias, batch, seq, t, group):
    n_q = seq // t
    n_hg = FOX_HEADS // group
    specs = [_q_spec(t, n_q, group, 0), _q_spec(t, n_q, group, 0), _kv_spec(seq, group, n_hg),
             _kv_spec(seq, group, 0), _kv_spec(seq, group, 2 * n_hg)]
    return _attention("fox", batch, seq, FOX_HEADS, t, group, (qkv, q_bias, qkv, k_bias, qkv), specs)


def _moba_key_lanes(seq):
    n_kb = seq // MOBA_BLOCK
    kpos = np.arange(seq)
    lanes = np.zeros((seq, LANES), np.float32)
    lanes[kpos, kpos // MOBA_BLOCK] = 1.0
    lanes[:, n_kb:n_kb + N_SPLIT] = 1.0
    lanes[:, n_kb + N_SPLIT:n_kb + 2 * N_SPLIT] = (MOBA_BLOCK * (kpos // MOBA_BLOCK))[:, None]
    lanes[:, n_kb + 2 * N_SPLIT:n_kb + 3 * N_SPLIT] = (kpos % MOBA_BLOCK)[:, None]
    return jnp.asarray(lanes, dtype=BF)


def _moba_query_lanes(q, k_mean, slope, i, t):
    n_kb = k_mean.shape[0]
    gate = lax.dot_general(k_mean.astype(BF), q, NT_DIMS, preferred_element_type=F32)
    blk = lax.broadcasted_iota(I32, (n_kb, t), 0)
    beaten = jnp.zeros((n_kb, t), F32)
    for n in range(n_kb):
        g_n = gate[n:n + 1, :]
        wins = (g_n > gate) | ((g_n == gate) & (blk > n))
        beaten = beaten + jnp.where(wins, 1.0, 0.0) * jnp.where(i > n, 1.0, 0.0)
    attended = (blk == i) | ((blk < i) & (beaten < MOBA_TOPK))
    mask = jnp.concatenate([jnp.where(attended, 0.0, NEG), jnp.zeros((LANES - n_kb, t), F32)], axis=0)
    row = lax.broadcasted_iota(I32, (LANES, t), 0) - n_kb
    qpos = (i * t + lax.broadcasted_iota(I32, (1, t), 1)).astype(F32)
    slope_row = jnp.full((1, t), slope, F32)
    which = row % N_SPLIT
    lanes = jnp.where(row < 0, mask,
                      jnp.where(row < N_SPLIT, _split_piece(-slope_row * qpos, which),
                                jnp.where(row < 3 * N_SPLIT, _split_piece(slope_row, which), 0.0)))
    return lanes.T.astype(BF)


def _moba_attention(big, k_mean, slopes, batch, seq, col_q, col_k, col_v, group):
    t = MOBA_BLOCK
    n_q = seq // t
    specs = [_q_spec(t, n_q, group, col_q // group), _kv_spec(seq, group, col_k // group),
             pl.BlockSpec((seq, LANES), lambda b, h, i: (0, 0)),
             _kv_spec(seq, group, col_v // group),
             pl.BlockSpec((n_q, group * LANES), lambda b, h, i: (b, h)),
             pl.BlockSpec(memory_space=pltpu.SMEM)]
    scratch = (pltpu.VMEM((group, t, LANES), BF),)
    return _attention("moba", batch, seq, MOBA_HEADS, t, group,
                      (big, big, _moba_key_lanes(seq), big, k_mean, slopes), specs, scratch)


def _kmean_body(k_ref, o_ref):
    n_kb = o_ref.shape[0]
    for n in range(n_kb):
        blk = k_ref[n * MOBA_BLOCK:(n + 1) * MOBA_BLOCK, :].astype(F32)
        o_ref[n:n + 1, :] = jnp.mean(blk, axis=0, keepdims=True)


def _moba_kmean(big, batch, seq, col_block):
    n_kb = seq // MOBA_BLOCK
    width = MOBA_HEADS * MOBA_HEAD_DIM
    return pl.pallas_call(
        _kmean_body,
        grid=(batch,),
        in_specs=[pl.BlockSpec((seq, width), lambda b: (b, col_block))],
        out_specs=pl.BlockSpec((n_kb, width), lambda b: (b, 0)),
        out_shape=jax.ShapeDtypeStruct((batch * n_kb, width), F32),
        compiler_params=_params("parallel"),
        name="moba_kmean",
    )(big)


def _swa_body(sink_ref, q_ref, kp_ref, kc_ref, vp_ref, vc_ref, o_ref):
    i = pl.program_id(1)
    w = SWA_WINDOW
    half = SWA_HEAD_DIM
    lane = lax.broadcasted_iota(I32, (2 * w, LANES), 1)
    kcat = jnp.concatenate([kp_ref[...], kc_ref[...]], axis=0).astype(F32)
    vcat = jnp.concatenate([vp_ref[...], vc_ref[...]], axis=0).astype(F32)

    def lo_hi(x, kvh):
        own = jnp.where((lane < half) == (kvh == 0), x, 0.0)
        other = pltpu.roll(own, half, axis=1)
        lo, hi = (own, other) if kvh == 0 else (other, own)
        return lo.astype(BF), hi.astype(BF)

    r = lax.broadcasted_iota(I32, (w, 2 * w), 0)
    c = lax.broadcasted_iota(I32, (w, 2 * w), 1)
    dist = r + w - c
    valid = (dist >= 0) & (dist < w) & ((c >= w) | (i > 0))
    dist_f = dist.astype(F32)
    for kvh in range(SWA_KV_HEADS):
        k_lo, k_hi = lo_hi(kcat, kvh)
        v_lo, v_hi = lo_hi(vcat, kvh)
        pairs_per_kv = SWA_Q_HEADS // SWA_KV_HEADS // 2
        for pp in range(pairs_per_kv):
            pair = kvh * pairs_per_kv + pp
            q = q_ref[:, pair * LANES:(pair + 1) * LANES]
            out = jnp.zeros((w, LANES), F32)
            for k_side, v_side, head in ((k_lo, v_lo, 2 * pair), (k_hi, v_hi, 2 * pair + 1)):
                slope = float(2.0 ** (-8.0 * (head + 1) / SWA_Q_HEADS))
                s = lax.dot_general(q, k_side, NT_DIMS, preferred_element_type=F32)
                s = jnp.where(valid, s - slope * dist_f, NEG)
                sink = sink_ref[head]
                m = jnp.maximum(jnp.max(s, axis=1, keepdims=True), sink)
                p = jnp.exp(s - m)
                p = p / (jnp.sum(p, axis=1, keepdims=True) + jnp.exp(sink - m))
                out = out + jnp.dot(p.astype(BF), v_side, preferred_element_type=F32)
            o_ref[:, pair * LANES:(pair + 1) * LANES] = out.astype(o_ref.dtype)


def _swa_attention(big, sinks, batch, seq, col_q, col_k, col_v):
    w = SWA_WINDOW
    n_q = seq // w
    width = SWA_Q_HEADS * SWA_HEAD_DIM
    prev = lambda col: pl.BlockSpec((w, LANES), lambda b, i: (b * n_q + jnp.maximum(i - 1, 0), col))
    cur = lambda col: pl.BlockSpec((w, LANES), lambda b, i: (b * n_q + i, col))
    return pl.pallas_call(
        _swa_body,
        grid=(batch, n_q),
        in_specs=[pl.BlockSpec(memory_space=pltpu.SMEM),
                  pl.BlockSpec((w, width), lambda b, i: (b * n_q + i, col_q)),
                  prev(col_k), cur(col_k), prev(col_v), cur(col_v)],
        out_specs=pl.BlockSpec((w, width), lambda b, i: (b * n_q + i, 0)),
        out_shape=jax.ShapeDtypeStruct((batch * seq, width), BF),
        compiler_params=_params("parallel", "parallel"),
        name="swa",
    )(sinks, big, big, big, big, big)


def _layer_norm(z, g, b):
    mu = jnp.mean(z, axis=-1, keepdims=True)
    zc = z - mu
    var = jnp.mean(zc * zc, axis=-1, keepdims=True)
    return zc * lax.rsqrt(var + LN_EPS) * g + b


def _outproj_body(a_ref, b_ref, wa_ref, wb_ref, h_ref, g_ref, beta_ref, o_ref):
    mix = jnp.dot(a_ref[...], wa_ref[...], preferred_element_type=F32)
    mix = mix + jnp.dot(b_ref[...], wb_ref[...], preferred_element_type=F32)
    o_ref[...] = _layer_norm(ALPHA * h_ref[...] + mix, g_ref[...], beta_ref[...])


def _outproj_ln(a, b, w_out, h, g, beta, tm):
    t, d = h.shape
    ka = a.shape[1]
    row = lambda i: (i, 0)
    return pl.pallas_call(
        _outproj_body,
        grid=(t // tm,),
        in_specs=[pl.BlockSpec((tm, ka), row), pl.BlockSpec((tm, ka), row),
                  pl.BlockSpec((ka, d), lambda i: (0, 0)), pl.BlockSpec((ka, d), lambda i: (1, 0)),
                  pl.BlockSpec((tm, d), row), pl.BlockSpec((1, d), lambda i: (0, 0)),
                  pl.BlockSpec((1, d), lambda i: (0, 0))],
        out_specs=pl.BlockSpec((tm, d), row),
        out_shape=jax.ShapeDtypeStruct((t, d), F32),
        compiler_params=_params("parallel"),
        name="outproj_ln",
    )(a, b, w_out, w_out, h, g, beta)


def _second_of_four(a, b, c, d):
    hi_ab, lo_ab = jnp.maximum(a, b), jnp.minimum(a, b)
    hi_cd, lo_cd = jnp.maximum(c, d), jnp.minimum(c, d)
    return jnp.maximum(jnp.maximum(lo_ab, lo_cd), jnp.minimum(hi_ab, hi_cd))


def _router_body(h_ref, wr_ref, bias_ref, e_ref, r_ref):
    tm = h_ref.shape[0]
    logits = jnp.dot(h_ref[...], wr_ref[...], precision=HIGHEST, preferred_element_type=F32)
    lt = logits.T
    aff = [jax.nn.sigmoid(lt[SUBLANES * j:SUBLANES * (j + 1), :]) for j in range(EXPERTS_PER_GROUP)]
    sel = [aff[j] + bias_ref[SUBLANES * j:SUBLANES * (j + 1), :] for j in range(EXPERTS_PER_GROUP)]
    top1 = jnp.maximum(jnp.maximum(sel[0], sel[1]), jnp.maximum(sel[2], sel[3]))
    score = top1 + _second_of_four(*sel)
    gid = lax.broadcasted_iota(I32, (N_GROUPS, tm), 0)
    best = jnp.min(jnp.where(score == jnp.max(score, axis=0, keepdims=True), gid, N_GROUPS),
                   axis=0, keepdims=True)
    in_grp = gid == best
    pick = lambda x: jnp.sum(jnp.where(in_grp, x, 0.0), axis=0, keepdims=True)
    s4 = [pick(x) for x in sel]
    a4 = [pick(x) for x in aff]

    def argmax4(vals):
        j, v = jnp.zeros((1, tm), I32), vals[0]
        for n in range(1, EXPERTS_PER_GROUP):
            better = vals[n] > v
            j, v = jnp.where(better, n, j), jnp.where(better, vals[n], v)
        return j

    j0 = argmax4(s4)
    j1 = argmax4([jnp.where(j0 == n, -jnp.inf, s4[n]) for n in range(EXPERTS_PER_GROUP)])
    take = lambda j: sum(jnp.where(j == n, a4[n], 0.0) for n in range(EXPERTS_PER_GROUP))
    w0, w1 = take(j0), take(j1)
    total = w0 + w1
    e0 = best * EXPERTS_PER_GROUP + j0
    e1 = best * EXPERTS_PER_GROUP + j1
    rid = lax.broadcasted_iota(I32, (SUBLANES, tm), 0)
    e_ref[...] = jnp.where(rid == 0, e0, jnp.where(rid == 1, e1, 0))
    rid = lax.broadcasted_iota(I32, (LANES, tm), 0)
    rows = jnp.where(rid == 0, w0 / total, jnp.where(rid == 1, w1 / total, 0.0))
    r_ref[...] = rows.T


def _router(h, wr_perm, bias_perm, tm):
    t, d = h.shape
    return pl.pallas_call(
        _router_body,
        grid=(t // tm,),
        in_specs=[pl.BlockSpec((tm, d), lambda i: (i, 0)), pl.BlockSpec((d, LANES), lambda i: (0, 0)),
                  pl.BlockSpec((N_EXPERTS, 1), lambda i: (0, 0))],
        out_specs=[pl.BlockSpec((SUBLANES, tm), lambda i: (0, i)), pl.BlockSpec((tm, LANES), lambda i: (i, 0))],
        out_shape=[jax.ShapeDtypeStruct((SUBLANES, t), I32), jax.ShapeDtypeStruct((t, LANES), F32)],
        compiler_params=_params("parallel"),
        name="router",
    )(h, wr_perm, bias_perm)


def _dispatch_plan(e01, tm):
    t = e01.shape[1]
    flat = e01.reshape(-1)
    onehot = (flat[:, None] == jnp.arange(N_EXPERTS, dtype=I32)[None, :]).astype(I32)
    cum = jnp.cumsum(onehot, axis=0)
    rank = jnp.sum(onehot * cum, axis=1) - 1
    counts = cum[-1]
    padded = ((counts + tm - 1) // tm) * tm
    starts = jnp.concatenate([jnp.zeros((1,), I32), jnp.cumsum(padded).astype(I32)])
    pos = jnp.sum(onehot * starts[None, :N_EXPERTS], axis=1) + rank
    tok = jnp.tile(jnp.arange(t, dtype=I32), 2)
    n_rows = 2 * t + N_EXPERTS * tm
    row_token = jnp.zeros((n_rows,), I32).at[pos].set(tok)
    return row_token, starts, pos[:t], pos[t:]


def _experts_body(rt_ref, gs_ref, h_hbm, wgu_ref, wd_ref, ys_hbm, xbuf, ybuf, wgu_bf, wd_bf, gsem, osem, *, tm):
    e = pl.program_id(0)
    first = gs_ref[e] // tm
    n_tiles = gs_ref[e + 1] // tm - first
    total = gs_ref[N_EXPERTS] // tm

    def row_copy(base, r, slot):
        return pltpu.make_async_copy(h_hbm.at[pl.ds(rt_ref[base + r], 1), :], xbuf.at[slot, pl.ds(r, 1), :],
                                     gsem.at[slot])

    def wait_rows(slot):
        pltpu.make_async_copy(h_hbm.at[pl.ds(0, tm), :], xbuf.at[slot], gsem.at[slot]).wait()

    def out_copy(g, slot):
        row0 = pl.multiple_of(g * tm, tm)
        return pltpu.make_async_copy(ybuf.at[slot], ys_hbm.at[pl.ds(row0, tm), :], osem.at[slot])

    @pl.when(e == 0)
    def _():
        def one(r, carry):
            row_copy(0, r, 0).start()
            return carry

        lax.fori_loop(0, tm, one, 0)
        ybuf[...] = jnp.zeros(ybuf.shape, F32)
        out_copy(0, 0).start()
        out_copy(1, 1).start()

    @pl.when(n_tiles > 0)
    def _():
        wgu_bf[...] = wgu_ref[0, 0].astype(BF)
        wd_bf[...] = wd_ref[0, 0].astype(BF)

        def tile(k, carry):
            g = first + k
            slot = g % 2
            wait_rows(slot)
            out_copy(g, slot).wait()
            nxt = jnp.minimum(g + 1, total - 1) * tm
            for r in range(tm):
                row_copy(nxt, r, 1 - slot).start(priority=r % 2)
            x = xbuf[slot].astype(BF)
            gu = jnp.dot(x, wgu_bf[...], preferred_element_type=F32)
            gate, up = gu[:, :EXPERT_FF], gu[:, EXPERT_FF:]
            hidden = (gate * jax.nn.sigmoid(gate) * up).astype(BF)
            ybuf[slot] = jnp.dot(hidden, wd_bf[...], preferred_element_type=F32)
            out_copy(g, slot).start()
            return carry

        lax.fori_loop(0, n_tiles, tile, 0)

    @pl.when(e == N_EXPERTS - 1)
    def _():
        wait_rows(total % 2)
        out_copy(0, 0).wait()
        out_copy(1, 1).wait()
        end = gs_ref[N_EXPERTS]
        n_tail = (ys_hbm.shape[0] - end) // tm
        ybuf[0] = jnp.zeros(ybuf.shape[1:], F32)

        def tail_copy(k):
            row0 = pl.multiple_of(end + k * tm, tm)
            return pltpu.make_async_copy(ybuf.at[0], ys_hbm.at[pl.ds(row0, tm), :], osem.at[0])

        def start_one(k, carry):
            tail_copy(k).start()
            return carry

        def wait_one(k, carry):
            tail_copy(k).wait()
            return carry

        lax.fori_loop(0, n_tail, start_one, 0)
        lax.fori_loop(0, n_tail, wait_one, 0)


def _experts(h, row_token, starts, w_gate_up, w_down, layer, tm):
    t, d = h.shape
    n_rows = row_token.shape[0]
    body = functools.partial(_experts_body, tm=tm)
    grid_spec = pltpu.PrefetchScalarGridSpec(
        num_scalar_prefetch=2,
        grid=(N_EXPERTS,),
        in_specs=[pl.BlockSpec(memory_space=pl.ANY),
                  pl.BlockSpec((1, 1, d, 2 * EXPERT_FF), lambda e, rt, gs: (layer, e, 0, 0)),
                  pl.BlockSpec((1, 1, EXPERT_FF, d), lambda e, rt, gs: (layer, e, 0, 0))],
        out_specs=pl.BlockSpec(memory_space=pl.ANY),
        scratch_shapes=[pltpu.VMEM((2, tm, d), F32), pltpu.VMEM((2, tm, d), F32),
                        pltpu.VMEM((d, 2 * EXPERT_FF), BF), pltpu.VMEM((EXPERT_FF, d), BF),
                        pltpu.SemaphoreType.DMA((2,)), pltpu.SemaphoreType.DMA((2,))],
    )
    return pl.pallas_call(
        body,
        grid_spec=grid_spec,
        out_shape=jax.ShapeDtypeStruct((n_rows, d), F32),
        compiler_params=_params("arbitrary"),
        name="experts",
    )(row_token, starts, h, w_gate_up, w_down)


def _combine_body(p0_ref, p1_ref, ys_hbm, h_ref, r_ref, g_ref, beta_ref, o_ref, buf0, buf1, sem, *, tc, n_steps):
    i = pl.program_id(0)
    slot = i % 2

    def start_row(base, r, sl):
        pltpu.make_async_copy(ys_hbm.at[pl.ds(p0_ref[base + r], 1), :], buf0.at[sl, pl.ds(r, 1), :],
                              sem.at[sl]).start(priority=0)
        pltpu.make_async_copy(ys_hbm.at[pl.ds(p1_ref[base + r], 1), :], buf1.at[sl, pl.ds(r, 1), :],
                              sem.at[sl]).start(priority=1)

    def wait_rows(sl):
        pltpu.make_async_copy(ys_hbm.at[pl.ds(0, tc), :], buf0.at[sl], sem.at[sl]).wait()
        pltpu.make_async_copy(ys_hbm.at[pl.ds(0, tc), :], buf1.at[sl], sem.at[sl]).wait()

    @pl.when(i == 0)
    def _():
        def one(r, carry):
            start_row(0, r, 0)
            return carry

        lax.fori_loop(0, tc, one, 0)

    wait_rows(slot)
    nxt = jnp.minimum(i + 1, n_steps - 1) * tc
    for r in range(tc):
        start_row(nxt, r, 1 - slot)
    gates = r_ref[...]
    y = gates[:, 0:1] * buf0[slot] + gates[:, 1:2] * buf1[slot]
    o_ref[...] = _layer_norm(ALPHA * h_ref[...] + y, g_ref[...], beta_ref[...])

    @pl.when(i == n_steps - 1)
    def _():
        wait_rows(1 - slot)


def _combine_ln(ys, pos0, pos1, h, gates, g, beta, tc):
    t, d = h.shape
    n_steps = t // tc
    body = functools.partial(_combine_body, tc=tc, n_steps=n_steps)
    row = lambda i, p0, p1: (i, 0)
    const = lambda i, p0, p1: (0, 0)
    grid_spec = pltpu.PrefetchScalarGridSpec(
        num_scalar_prefetch=2,
        grid=(n_steps,),
        in_specs=[pl.BlockSpec(memory_space=pl.ANY), pl.BlockSpec((tc, d), row), pl.BlockSpec((tc, LANES), row),
                  pl.BlockSpec((1, d), const), pl.BlockSpec((1, d), const)],
        out_specs=pl.BlockSpec((tc, d), row),
        scratch_shapes=[pltpu.VMEM((2, tc, d), F32), pltpu.VMEM((2, tc, d), F32), pltpu.SemaphoreType.DMA((2,))],
    )
    return pl.pallas_call(
        body,
        grid_spec=grid_spec,
        out_shape=jax.ShapeDtypeStruct((t, d), F32),
        compiler_params=_params("arbitrary"),
        name="combine_ln",
    )(pos0, pos1, ys, h, gates, g, beta)


def _moe_ln(h, wr_perm, bias_perm, w_gate_up, w_down, layer, g, beta, tm_router=512, tm_expert=256, tc=256):
    e01, gates = _router(h, wr_perm, bias_perm, tm_router)
    row_token, starts, pos0, pos1 = _dispatch_plan(e01[:MOE_TOPK], tm_expert)
    ys = _experts(h, row_token, starts, w_gate_up, w_down, layer, tm_expert)
    return _combine_ln(ys, pos0, pos1, h, gates, g, beta, tc)


def _swap_halves(w):
    half = w.shape[-1] // 2
    return jnp.concatenate([w[..., half:], w[..., :half]], axis=-1)


def _even_weights(w_in, w_q_up, w_kv_up, forget_bias):
    d = w_in.shape[0]
    o_kv = MLA_Q_LORA
    o_kr = o_kv + MLA_KV_LORA
    o_fq = o_kr + MLA_ROPE
    hd = FOX_HEADS * FOX_HEAD_DIM
    o_fl = o_fq + 3 * hd
    k_r = w_in[:, o_kr:o_fq]
    f_l = jnp.pad(w_in[:, o_fl:], ((0, 0), (0, LANES - FOX_HEADS)))
    w_small = jnp.concatenate([w_in[:, :o_kr], k_r, _swap_halves(k_r), f_l], axis=1).astype(BF)
    w_fox = w_in[:, o_fq:o_fl].astype(BF)
    cs_fox = jnp.concatenate([jnp.full((1, hd), FOX_SCALE, F32), jnp.ones((1, 2 * hd), F32)], axis=1)
    wq = w_q_up.reshape(MLA_Q_LORA, MLA_HEADS, MLA_NOPE + MLA_ROPE)
    wqn = wq[:, :, :MLA_NOPE].reshape(MLA_Q_LORA, -1).astype(BF)
    pe = wq[:, :, MLA_NOPE:]
    wqp = jnp.concatenate([pe, _swap_halves(pe)], axis=-1).reshape(MLA_Q_LORA, -1).astype(BF)
    wkv = w_kv_up.reshape(MLA_KV_LORA, MLA_HEADS, MLA_NOPE + MLA_V)
    wk = wkv[:, :, :MLA_NOPE].reshape(MLA_KV_LORA, -1).astype(BF)
    wv = wkv[:, :, MLA_NOPE:].reshape(MLA_KV_LORA, -1).astype(BF)
    fb = jnp.pad(forget_bias.astype(F32), (0, LANES - FOX_HEADS)).reshape(1, LANES)
    return w_small, w_fox, cs_fox, wqn, wqp, wk, wv, fb


def _rope_table(seq):
    half = MLA_ROPE // 2
    inv_freq = ROPE_THETA ** (-jnp.arange(half, dtype=F32) / half)
    ang = jnp.arange(seq).astype(F32)[:, None] * inv_freq[None, :]
    cos, sin = jnp.cos(ang), jnp.sin(ang)
    return jnp.concatenate([cos, cos, -sin, sin], axis=1)


def _router_weights(w_router, router_bias):
    r = np.arange(N_EXPERTS)
    perm = (r % N_GROUPS) * EXPERTS_PER_GROUP + r // N_GROUPS
    wr = jnp.pad(w_router[:, perm], ((0, 0), (0, LANES - N_EXPERTS)))
    return wr, router_bias.astype(F32)[perm].reshape(N_EXPERTS, 1)


def _even_layer(h, batch, seq, w_in, q_norm, w_q_up, kv_norm, w_kv_up, forget_bias, w_out, g, beta):
    w_small, w_fox, cs_fox, wqn, wqp, wk, wv, fb = _even_weights(w_in, w_q_up, w_kv_up, forget_bias)
    small = _mm(h, w_small, jnp.ones((1, w_small.shape[1]), F32), F32, 512, 1024)
    fox_qkv = _mm(h, w_fox, cs_fox, BF, 512, 1536)
    qn, qp, kn, v, kp = _mla_prep(small, _rope_table(seq), q_norm.reshape(1, -1), kv_norm.reshape(1, -1),
                                  wqn, wqp, wk, wv, seq, 512)
    q_bias, k_bias = _fox_cum(small, fb, batch, seq, 512)
    a = _mla_attention(qn, qp, kn, v, kp, batch, seq, 512, 4)
    bo = _fox_attention(fox_qkv, q_bias, k_bias, batch, seq, 512, 4)
    return _outproj_ln(a, bo, w_out.astype(BF), h, g, beta, 512)


def _odd_layer(h, batch, seq, w_in, sinks, w_out, g, beta):
    n_sq = SWA_Q_HEADS * SWA_HEAD_DIM
    n_skv = SWA_KV_HEADS * SWA_HEAD_DIM
    n_m = MOBA_HEADS * MOBA_HEAD_DIM
    o_mq = n_sq + 2 * n_skv
    w_big = jnp.concatenate([w_in[:, :n_sq], w_in[:, o_mq:], w_in[:, n_sq:o_mq]], axis=1).astype(BF)
    cs = jnp.concatenate([jnp.full((1, n_sq), SWA_SCALE, F32), jnp.full((1, n_m), MOBA_SCALE, F32),
                          jnp.ones((1, 2 * n_m + 2 * n_skv), F32)], axis=1)
    big = _mm(h, w_big, cs, BF, 512, 2176)
    blocks = lambda cols: cols // LANES
    k_mean = _moba_kmean(big, batch, seq, 2)
    slopes = jnp.asarray(LOG2E * 2.0 ** (-8.0 * np.arange(1, MOBA_HEADS + 1) / MOBA_HEADS), dtype=F32)
    c = _swa_attention(big, sinks.astype(F32), batch, seq, 0, blocks(n_sq + 3 * n_m), blocks(n_sq + 3 * n_m) + 1)
    dd = _moba_attention(big, k_mean, slopes, batch, seq, blocks(n_sq), blocks(n_sq + n_m), blocks(n_sq + 2 * n_m),
                         8)
    return _outproj_ln(c, dd, w_out.astype(BF), h, g, beta, 512)


def kernel(x, w_router, router_bias, even_w_in, even_q_norm, even_w_q_up, even_kv_norm, even_w_kv_up,
           even_forget_bias, even_w_out, odd_w_in, odd_sinks, odd_w_out, ln_mix_g, ln_mix_b, ln_ffn_g, ln_ffn_b,
           w_gate_up, w_down):
    batch, seq, d = x.shape
    h = x.reshape(batch * seq, d)
    wr_perm, bias_perm = _router_weights(w_router, router_bias)
    row = lambda p, layer: p[layer].reshape(1, d)
    for layer in range(DEPTH):
        i = layer // 2
        if layer % 2 == 0:
            h = _even_layer(h, batch, seq, even_w_in[i], even_q_norm[i], even_w_q_up[i], even_kv_norm[i],
                            even_w_kv_up[i], even_forget_bias[i], even_w_out[i], row(ln_mix_g, layer),
                            row(ln_mix_b, layer))
        else:
            h = _odd_layer(h, batch, seq, odd_w_in[i], odd_sinks[i], odd_w_out[i], row(ln_mix_g, layer),
                           row(ln_mix_b, layer))
        h = _moe_ln(h, wr_perm, bias_perm, w_gate_up, w_down, layer, row(ln_ffn_g, layer), row(ln_ffn_b, layer))
    return h.reshape(batch, seq, d)
```

```python
import functools

import numpy as np
import jax
import jax.numpy as jnp
from jax import lax
from jax.experimental import pallas as pl
from jax.experimental.pallas import tpu as pltpu

BF = jnp.bfloat16
F32 = jnp.float32
I32 = jnp.int32

LANES = 128
SUBLANES = 8
VMEM_LIMIT = 56 * 1024 * 1024

D_MODEL = 2048
DEPTH = 2
LN_EPS = 1e-5
RMS_EPS = 1e-6
ALPHA = (2 * DEPTH) ** 0.25
MLA_HEADS, MLA_Q_LORA, MLA_KV_LORA, MLA_NOPE, MLA_ROPE, MLA_V = 8, 512, 256, 128, 64, 128
ROPE_THETA = 10000.0
FOX_HEADS, FOX_HEAD_DIM = 8, 128
SWA_Q_HEADS, SWA_KV_HEADS, SWA_HEAD_DIM, SWA_WINDOW = 16, 2, 64, 128
MOBA_HEADS, MOBA_HEAD_DIM, MOBA_BLOCK, MOBA_TOPK = 8, 128, 256, 3
N_EXPERTS, N_GROUPS, MOE_TOPK, EXPERT_FF = 32, 8, 2, 512
EXPERTS_PER_GROUP = N_EXPERTS // N_GROUPS

LOG2E = 1.4426950408889634
MLA_SCALE = (MLA_NOPE + MLA_ROPE) ** -0.5 * LOG2E
FOX_SCALE = FOX_HEAD_DIM ** -0.5 * LOG2E
MOBA_SCALE = MOBA_HEAD_DIM ** -0.5 * LOG2E
SWA_SCALE = SWA_HEAD_DIM ** -0.5 * LOG2E
NEG = -1e30

NT_DIMS = (((1,), (1,)), ((), ()))


def _params(*sem):
    return pltpu.CompilerParams(dimension_semantics=sem, vmem_limit_bytes=VMEM_LIMIT)


def _mm_body(x_ref, w_ref, cs_ref, o_ref):
    acc = jnp.dot(x_ref[...].astype(BF), w_ref[...], preferred_element_type=F32)
    o_ref[...] = (acc * cs_ref[...]).astype(o_ref.dtype)


def _mm(x, w, col_scale, out_dtype, tm, tn):
    m, k = x.shape
    n = w.shape[1]
    return pl.pallas_call(
        _mm_body,
        grid=(n // tn, m // tm),
        in_specs=[
            pl.BlockSpec((tm, k), lambda j, i: (i, 0)),
            pl.BlockSpec((k, tn), lambda j, i: (0, j)),
            pl.BlockSpec((1, tn), lambda j, i: (0, j)),
        ],
        out_specs=pl.BlockSpec((tm, tn), lambda j, i: (i, j)),
        out_shape=jax.ShapeDtypeStruct((m, n), out_dtype),
        compiler_params=_params("parallel", "parallel"),
        name="proj",
    )(x, w, col_scale)


def _rms(x, g):
    return x * lax.rsqrt(jnp.mean(x * x, axis=-1, keepdims=True) + RMS_EPS) * g


def _rope_pair(slab, table):
    r = slab * table
    return r + pltpu.roll(r, MLA_ROPE, axis=1)


def _mla_prep_body(cq_ref, ckv_ref, krp_ref, rope_ref, qg_ref, kvg_ref, wqn_ref, wqp_ref, wk_ref, wv_ref,
                   qn_ref, qp_ref, kn_ref, v_ref, kp_ref):
    table = rope_ref[...]
    cqn = _rms(cq_ref[...], qg_ref[...]).astype(BF)
    qn_ref[...] = (jnp.dot(cqn, wqn_ref[...], preferred_element_type=F32) * MLA_SCALE).astype(BF)
    qp = jnp.dot(cqn, wqp_ref[...], preferred_element_type=F32)
    for h in range(MLA_HEADS):
        sl = slice(h * LANES, (h + 1) * LANES)
        qp_ref[:, sl] = (_rope_pair(qp[:, sl], table) * MLA_SCALE).astype(BF)
    ckvn = _rms(ckv_ref[...], kvg_ref[...]).astype(BF)
    kn_ref[...] = jnp.dot(ckvn, wk_ref[...], preferred_element_type=F32).astype(BF)
    v_ref[...] = jnp.dot(ckvn, wv_ref[...], preferred_element_type=F32).astype(BF)
    kr = _rope_pair(krp_ref[...], table)
    lane = lax.broadcasted_iota(I32, kr.shape, 1)
    kp_ref[...] = jnp.where(lane < MLA_ROPE, kr, 0.0).astype(BF)


def _mla_prep(small, rope_table, q_norm, kv_norm, wqn, wqp, wk, wv, seq, tm):
    t = small.shape[0]
    n_s = seq // tm
    hd = MLA_HEADS * LANES
    row = lambda i: (i, 0)
    const = lambda i: (0, 0)
    out = lambda w: pl.BlockSpec((tm, w), row)
    return pl.pallas_call(
        _mla_prep_body,
        grid=(t // tm,),
        in_specs=[
            pl.BlockSpec((tm, MLA_Q_LORA), lambda i: (i, 0)),
            pl.BlockSpec((tm, MLA_KV_LORA), lambda i: (i, MLA_Q_LORA // MLA_KV_LORA)),
            pl.BlockSpec((tm, LANES), lambda i: (i, (MLA_Q_LORA + MLA_KV_LORA) // LANES)),
            pl.BlockSpec((tm, LANES), lambda i: (i % n_s, 0)),
            pl.BlockSpec((1, MLA_Q_LORA), const),
            pl.BlockSpec((1, MLA_KV_LORA), const),
            pl.BlockSpec((MLA_Q_LORA, hd), const),
            pl.BlockSpec((MLA_Q_LORA, hd), const),
            pl.BlockSpec((MLA_KV_LORA, hd), const),
            pl.BlockSpec((MLA_KV_LORA, hd), const),
        ],
        out_specs=[out(hd), out(hd), out(hd), out(hd), out(LANES)],
        out_shape=[jax.ShapeDtypeStruct((t, hd), BF)] * 4 + [jax.ShapeDtypeStruct((t, LANES), BF)],
        compiler_params=_params("parallel"),
        name="mla_prep",
    )(small, small, small, rope_table, q_norm, kv_norm, wqn, wqp, wk, wv)


N_SPLIT = 3


def _split_piece(x, which):
    hi = x.astype(BF).astype(F32)
    rest = x - hi
    mid = rest.astype(BF).astype(F32)
    return jnp.where(which == 0, hi, jnp.where(which == 1, mid, rest - mid))


def _fox_cum_body(fl_ref, fb_ref, spread_ref, qe_ref, ke_ref, carry_ref):
    @pl.when(pl.program_id(1) == 0)
    def _():
        carry_ref[...] = jnp.zeros_like(carry_ref)

    z = fl_ref[...] + fb_ref[...]
    log_f = jnp.minimum(z, 0.0) - jnp.log1p(jnp.exp(-jnp.abs(z)))
    tm = z.shape[0]
    r = lax.broadcasted_iota(I32, (tm, tm), 0)
    c = lax.broadcasted_iota(I32, (tm, tm), 1)
    tri = jnp.where(r >= c, 1.0, 0.0).astype(BF)
    parts = jnp.dot(tri, jnp.concatenate([_split_piece(log_f, n).astype(BF) for n in range(N_SPLIT)], axis=1),
                    preferred_element_type=F32)
    cum = (parts[:, :LANES] + parts[:, LANES:2 * LANES]) + parts[:, 2 * LANES:] + carry_ref[...]
    carry_ref[...] = cum[tm - 1:tm, :]
    cum = cum * LOG2E
    width = qe_ref.shape[1]
    pieces = jnp.concatenate([_split_piece(cum, n).astype(BF) for n in range(N_SPLIT)], axis=1)
    placed = jnp.dot(pieces, spread_ref[...], preferred_element_type=F32)
    li = lax.broadcasted_iota(I32, (tm, width), 1) % LANES
    qe_ref[...] = jnp.where((li >= N_SPLIT) & (li < 2 * N_SPLIT), 1.0, placed[:, :width]).astype(BF)
    ke_ref[...] = jnp.where(li < N_SPLIT, 1.0, placed[:, width:]).astype(BF)


def _fox_spread():
    width = FOX_HEADS * LANES
    m = np.zeros((N_SPLIT * LANES, 2 * width), np.float32)
    for j in range(N_SPLIT):
        for h in range(FOX_HEADS):
            m[j * LANES + h, h * LANES + j] = 1.0
            m[j * LANES + h, width + h * LANES + N_SPLIT + j] = -1.0
    return jnp.asarray(m, dtype=BF)


def _fox_cum(small, forget_bias_row, batch, seq, tm):
    t = small.shape[0]
    n_s = seq // tm
    width = FOX_HEADS * LANES
    fl_block = (MLA_Q_LORA + MLA_KV_LORA + LANES) // LANES
    return pl.pallas_call(
        _fox_cum_body,
        grid=(batch, n_s),
        in_specs=[
            pl.BlockSpec((tm, LANES), lambda b, i: (b * n_s + i, fl_block)),
            pl.BlockSpec((1, LANES), lambda b, i: (0, 0)),
            pl.BlockSpec((N_SPLIT * LANES, 2 * width), lambda b, i: (0, 0)),
        ],
        out_specs=[
            pl.BlockSpec((tm, width), lambda b, i: (b * n_s + i, 0)),
            pl.BlockSpec((tm, width), lambda b, i: (b * n_s + i, 0)),
        ],
        out_shape=[jax.ShapeDtypeStruct((t, width), BF), jax.ShapeDtypeStruct((t, width), BF)],
        scratch_shapes=[pltpu.VMEM((1, LANES), F32)],
        compiler_params=_params("parallel", "arbitrary"),
        name="fox_cum",
    )(small, forget_bias_row, _fox_spread())


def _softmax_update(s_all, v_all, m_ref, l_ref, acc_ref):
    heads = range(len(s_all))
    m_prev = [m_ref[g] for g in heads]
    m_new = [jnp.maximum(m_prev[g], jnp.max(s_all[g], axis=1, keepdims=True)) for g in heads]
    reps = s_all[0].shape[1] // LANES
    p = [jnp.exp2(s_all[g] - jnp.concatenate([m_new[g]] * reps, axis=1)) for g in heads]
    pv = [jnp.dot(p[g].astype(BF), v_all[g], preferred_element_type=F32) for g in heads]
    for g in heads:
        alpha = jnp.exp2(m_prev[g] - m_new[g])
        l_ref[g] = alpha * l_ref[g] + jnp.sum(p[g], axis=1, keepdims=True)
        acc_ref[g] = alpha * acc_ref[g] + pv[g]
        m_ref[g] = m_new[g]


def _attn_body(*refs, mode, t, group):
    hg = pl.program_id(1)
    i = pl.program_id(2)
    heads = range(group)
    sl = lambda g: slice(g * LANES, (g + 1) * LANES)
    if mode == "moba":
        q_ref, k_ref, kx_ref, v_ref, km_ref, slope_ref, o_ref, m_ref, l_ref, acc_ref, qx_ref = refs
        for g in heads:
            qx_ref[g] = _moba_query_lanes(q_ref[:, sl(g)], km_ref[:, sl(g)], slope_ref[hg * group + g], i, t)
        q_extra = lambda g: qx_ref[g]
        k_extra = lambda g, ks: kx_ref[pl.ds(ks, t), :]
    else:
        q_ref, qx_ref, k_ref, kx_ref, v_ref, o_ref, m_ref, l_ref, acc_ref = refs
        q_extra = lambda g: qx_ref[:, sl(g)]
        if mode == "mla":
            k_extra = lambda g, ks: kx_ref[pl.ds(ks, t), :]
        else:
            k_extra = lambda g, ks: kx_ref[pl.ds(ks, t), sl(g)]

    def scores(g, ks):
        q = jnp.concatenate([q_ref[:, sl(g)], q_extra(g)], axis=1)
        k = jnp.concatenate([k_ref[pl.ds(ks, t), sl(g)], k_extra(g, ks)], axis=1)
        return lax.dot_general(q, k, NT_DIMS, preferred_element_type=F32)

    m_ref[...] = jnp.full(m_ref.shape, NEG, F32)
    l_ref[...] = jnp.zeros(l_ref.shape, F32)
    acc_ref[...] = jnp.zeros(acc_ref.shape, F32)

    ks = pl.multiple_of(i * t, t)
    r = lax.broadcasted_iota(I32, (t, t), 0)
    c = lax.broadcasted_iota(I32, (t, t), 1)
    values = lambda ks: [v_ref[pl.ds(ks, t), sl(g)] for g in heads]
    _softmax_update([jnp.where(c <= r, scores(g, ks), NEG) for g in heads], values(ks), m_ref, l_ref, acc_ref)

    def past_tile(j, carry):
        ks = pl.multiple_of(j * t, t)
        _softmax_update([scores(g, ks) for g in heads], values(ks), m_ref, l_ref, acc_ref)
        return carry

    lax.fori_loop(0, i, past_tile, 0)
    for g in heads:
        o_ref[:, sl(g)] = (acc_ref[g] / l_ref[g]).astype(o_ref.dtype)


def _attention(mode, batch, seq, heads, t, group, operands, in_specs, extra_scratch=()):
    n_q = seq // t
    body = functools.partial(_attn_body, mode=mode, t=t, group=group)
    return pl.pallas_call(
        body,
        grid=(batch, heads // group, n_q),
        in_specs=in_specs,
        out_specs=pl.BlockSpec((t, group * LANES), lambda b, h, i: (b * n_q + i, h)),
        out_shape=jax.ShapeDtypeStruct((batch * seq, heads * LANES), BF),
        scratch_shapes=[pltpu.VMEM((group, t, LANES), F32), pltpu.VMEM((group, t, LANES), F32),
                        pltpu.VMEM((group, t, LANES), F32), *extra_scratch],
        compiler_params=_params("parallel", "parallel", "arbitrary"),
        name="attn_" + mode,
    )(*operands)


def _q_spec(t, n_q, group, col0=0):
    return pl.BlockSpec((t, group * LANES), lambda b, h, i: (b * n_q + i, col0 + h))


def _kv_spec(seq, group, col0=0):
    return pl.BlockSpec((seq, group * LANES), lambda b, h, i: (b, col0 + h))


def _mla_attention(qn, qp, kn, v, kp, batch, seq, t, group):
    n_q = seq // t
    specs = [_q_spec(t, n_q, group), _q_spec(t, n_q, group), _kv_spec(seq, group),
             pl.BlockSpec((seq, LANES), lambda b, h, i: (b, 0)), _kv_spec(seq, group)]
    return _attention("mla", batch, seq, MLA_HEADS, t, group, (qn, qp, kn, kp, v), specs)


def _fox_attention(qkv, q_bias, k_bias, batch, seq, t, group):
    n_q = seq // t
    n_hg = FOX_HEADS // group
    specs = [_q_spec(t, n_q, group, 0), _q_spec(t, n_q, group, 0), _kv_spec(seq, group, n_hg),
             _kv_spec(seq, group, 0), _kv_spec(seq, group, 2 * n_hg)]
    return _attention("fox", batch, seq, FOX_HEADS, t, group, (qkv, q_bias, qkv, k_bias, qkv), specs)


def _moba_key_lanes(seq):
    n_kb = seq // MOBA_BLOCK
    kpos = np.arange(seq)
    lanes = np.zeros((seq, LANES), np.float32)
    lanes[kpos, kpos // MOBA_BLOCK] = 1.0
    lanes[:, n_kb:n_kb + N_SPLIT] = 1.0
    lanes[:, n_kb + N_SPLIT:n_kb + 2 * N_SPLIT] = (MOBA_BLOCK * (kpos // MOBA_BLOCK))[:, None]
    lanes[:, n_kb + 2 * N_SPLIT:n_kb + 3 * N_SPLIT] = (kpos % MOBA_BLOCK)[:, None]
    return jnp.asarray(lanes, dtype=BF)


def _moba_query_lanes(q, k_mean, slope, i, t):
    n_kb = k_mean.shape[0]
    gate = lax.dot_general(k_mean.astype(BF), q, NT_DIMS, preferred_element_type=F32)
    blk = lax.broadcasted_iota(I32, (n_kb, t), 0)
    beaten = jnp.zeros((n_kb, t), F32)
    for n in range(n_kb):
        g_n = gate[n:n + 1, :]
        wins = (g_n > gate) | ((g_n == gate) & (blk > n))
        beaten = beaten + jnp.where(wins, 1.0, 0.0) * jnp.where(i > n, 1.0, 0.0)
    attended = (blk == i) | ((blk < i) & (beaten < MOBA_TOPK))
    mask = jnp.concatenate([jnp.where(attended, 0.0, NEG), jnp.zeros((LANES - n_kb, t), F32)], axis=0)
    row = lax.broadcasted_iota(I32, (LANES, t), 0) - n_kb
    qpos = (i * t + lax.broadcasted_iota(I32, (1, t), 1)).astype(F32)
    slope_row = jnp.full((1, t), slope, F32)
    which = row % N_SPLIT
    lanes = jnp.where(row < 0, mask,
                      jnp.where(row < N_SPLIT, _split_piece(-slope_row * qpos, which),
                                jnp.where(row < 3 * N_SPLIT, _split_piece(slope_row, which), 0.0)))
    return lanes.T.astype(BF)


def _moba_attention(big, k_mean, slopes, batch, seq, col_q, col_k, col_v, group):
    t = MOBA_BLOCK
    n_q = seq // t
    specs = [_q_spec(t, n_q, group, col_q // group), _kv_spec(seq, group, col_k // group),
             pl.BlockSpec((seq, LANES), lambda b, h, i: (0, 0)),
             _kv_spec(seq, group, col_v // group),
             pl.BlockSpec((n_q, group * LANES), lambda b, h, i: (b, h)),
             pl.BlockSpec(memory_space=pltpu.SMEM)]
    scratch = (pltpu.VMEM((group, t, LANES), BF),)
    return _attention("moba", batch, seq, MOBA_HEADS, t, group,
                      (big, big, _moba_key_lanes(seq), big, k_mean, slopes), specs, scratch)


def _kmean_body(k_ref, o_ref):
    n_kb = o_ref.shape[0]
    for n in range(n_kb):
        blk = k_ref[n * MOBA_BLOCK:(n + 1) * MOBA_BLOCK, :].astype(F32)
        o_ref[n:n + 1, :] = jnp.mean(blk, axis=0, keepdims=True)


def _moba_kmean(big, batch, seq, col_block):
    n_kb = seq // MOBA_BLOCK
    width = MOBA_HEADS * MOBA_HEAD_DIM
    return pl.pallas_call(
        _kmean_body,
        grid=(batch,),
        in_specs=[pl.BlockSpec((seq, width), lambda b: (b, col_block))],
        out_specs=pl.BlockSpec((n_kb, width), lambda b: (b, 0)),
        out_shape=jax.ShapeDtypeStruct((batch * n_kb, width), F32),
        compiler_params=_params("parallel"),
        name="moba_kmean",
    )(big)


def _swa_body(sink_ref, q_ref, kp_ref, kc_ref, vp_ref, vc_ref, o_ref):
    i = pl.program_id(1)
    w = SWA_WINDOW
    half = SWA_HEAD_DIM
    lane = lax.broadcasted_iota(I32, (w, LANES), 1)
    r = lax.broadcasted_iota(I32, (w, w), 0)
    c = lax.broadcasted_iota(I32, (w, w), 1)
    from_prev = c > r
    dist = (r - c + jnp.where(from_prev, w, 0)).astype(F32)
    valid = jnp.logical_not(from_prev) | (i > 0)
    heads_per_kv = SWA_Q_HEADS // SWA_KV_HEADS

    def lo_hi(ref, kvh):
        own = jnp.where((lane < half) == (kvh == 0), ref[...].astype(F32), 0.0)
        other = pltpu.roll(own, half, axis=1)
        lo, hi = (own, other) if kvh == 0 else (other, own)
        return lo.astype(BF), hi.astype(BF)

    for kvh in range(SWA_KV_HEADS):
        kp, kc, vp, vc = (lo_hi(ref, kvh) for ref in (kp_ref, kc_ref, vp_ref, vc_ref))
        heads = range(kvh * heads_per_kv, (kvh + 1) * heads_per_kv)
        q = {h: q_ref[:, (h // 2) * LANES:(h // 2 + 1) * LANES] for h in heads}
        s = {h: jnp.where(from_prev,
                          lax.dot_general(q[h], kp[h % 2], NT_DIMS, preferred_element_type=F32),
                          lax.dot_general(q[h], kc[h % 2], NT_DIMS, preferred_element_type=F32)) for h in heads}
        p, inv = {}, {}
        for h in heads:
            slope = float(LOG2E * 2.0 ** (-8.0 * (h + 1) / SWA_Q_HEADS))
            sink = sink_ref[h] * LOG2E
            logits = jnp.where(valid, s[h] - slope * dist, NEG)
            m = jnp.maximum(jnp.max(logits, axis=1, keepdims=True), sink)
            p[h] = jnp.exp2(logits - m)
            inv[h] = 1.0 / (jnp.sum(p[h], axis=1, keepdims=True) + jnp.exp2(sink - m))
        for pair in range(kvh * heads_per_kv // 2, (kvh + 1) * heads_per_kv // 2):
            out = jnp.zeros((w, LANES), F32)
            for h in (2 * pair, 2 * pair + 1):
                pn = p[h] * inv[h]
                out = out + jnp.dot(jnp.where(from_prev, pn, 0.0).astype(BF), vp[h % 2], preferred_element_type=F32)
                out = out + jnp.dot(jnp.where(from_prev, 0.0, pn).astype(BF), vc[h % 2], preferred_element_type=F32)
            o_ref[:, pair * LANES:(pair + 1) * LANES] = out.astype(o_ref.dtype)


def _swa_attention(big, sinks, batch, seq, col_q, col_k, col_v):
    w = SWA_WINDOW
    n_q = seq // w
    width = SWA_Q_HEADS * SWA_HEAD_DIM
    prev = lambda col: pl.BlockSpec((w, LANES), lambda b, i: (b * n_q + jnp.maximum(i - 1, 0), col))
    cur = lambda col: pl.BlockSpec((w, LANES), lambda b, i: (b * n_q + i, col))
    return pl.pallas_call(
        _swa_body,
        grid=(batch, n_q),
        in_specs=[pl.BlockSpec(memory_space=pltpu.SMEM),
                  pl.BlockSpec((w, width), lambda b, i: (b * n_q + i, col_q)),
                  prev(col_k), cur(col_k), prev(col_v), cur(col_v)],
        out_specs=pl.BlockSpec((w, width), lambda b, i: (b * n_q + i, 0)),
        out_shape=jax.ShapeDtypeStruct((batch * seq, width), BF),
        compiler_params=_params("parallel", "parallel"),
        name="swa",
    )(sinks, big, big, big, big, big)


def _layer_norm(z, g, b):
    mu = jnp.mean(z, axis=-1, keepdims=True)
    zc = z - mu
    var = jnp.mean(zc * zc, axis=-1, keepdims=True)
    return zc * lax.rsqrt(var + LN_EPS) * g + b


def _outproj_body(a_ref, b_ref, wa_ref, wb_ref, h_ref, g_ref, beta_ref, o_ref):
    mix = jnp.dot(a_ref[...], wa_ref[...], preferred_element_type=F32)
    mix = mix + jnp.dot(b_ref[...], wb_ref[...], preferred_element_type=F32)
    o_ref[...] = _layer_norm(ALPHA * h_ref[...] + mix, g_ref[...], beta_ref[...])


def _outproj_ln(a, b, w_out, h, g, beta, tm):
    t, d = h.shape
    ka = a.shape[1]
    row = lambda i: (i, 0)
    return pl.pallas_call(
        _outproj_body,
        grid=(t // tm,),
        in_specs=[pl.BlockSpec((tm, ka), row), pl.BlockSpec((tm, ka), row),
                  pl.BlockSpec((ka, d), lambda i: (0, 0)), pl.BlockSpec((ka, d), lambda i: (1, 0)),
                  pl.BlockSpec((tm, d), row), pl.BlockSpec((1, d), lambda i: (0, 0)),
                  pl.BlockSpec((1, d), lambda i: (0, 0))],
        out_specs=pl.BlockSpec((tm, d), row),
        out_shape=jax.ShapeDtypeStruct((t, d), F32),
        compiler_params=_params("parallel"),
        name="outproj_ln",
    )(a, b, w_out, w_out, h, g, beta)


def _second_of_four(a, b, c, d):
    hi_ab, lo_ab = jnp.maximum(a, b), jnp.minimum(a, b)
    hi_cd, lo_cd = jnp.maximum(c, d), jnp.minimum(c, d)
    return jnp.maximum(jnp.maximum(lo_ab, lo_cd), jnp.minimum(hi_ab, hi_cd))


def _router_body(h_ref, wr_ref, bias_ref, e_ref, r_ref):
    tm = h_ref.shape[0]
    h = h_ref[...]
    h_hi = h.astype(BF)
    h_lo = (h - h_hi.astype(F32)).astype(BF)
    both = jnp.dot(h_hi, wr_ref[...], preferred_element_type=F32)
    logits = (both[:, :LANES] + both[:, LANES:]) + jnp.dot(h_lo, wr_ref[:, :LANES], preferred_element_type=F32)
    lt = logits.T
    aff = [jax.nn.sigmoid(lt[SUBLANES * j:SUBLANES * (j + 1), :]) for j in range(EXPERTS_PER_GROUP)]
    sel = [aff[j] + bias_ref[SUBLANES * j:SUBLANES * (j + 1), :] for j in range(EXPERTS_PER_GROUP)]
    top1 = jnp.maximum(jnp.maximum(sel[0], sel[1]), jnp.maximum(sel[2], sel[3]))
    score = top1 + _second_of_four(*sel)
    gid = lax.broadcasted_iota(I32, (N_GROUPS, tm), 0)
    best = jnp.min(jnp.where(score == jnp.max(score, axis=0, keepdims=True), gid, N_GROUPS),
                   axis=0, keepdims=True)
    in_grp = gid == best
    pick = lambda x: jnp.sum(jnp.where(in_grp, x, 0.0), axis=0, keepdims=True)
    s4 = [pick(x) for x in sel]
    a4 = [pick(x) for x in aff]

    def argmax4(vals):
        j, v = jnp.zeros((1, tm), I32), vals[0]
        for n in range(1, EXPERTS_PER_GROUP):
            better = vals[n] > v
            j, v = jnp.where(better, n, j), jnp.where(better, vals[n], v)
        return j

    j0 = argmax4(s4)
    j1 = argmax4([jnp.where(j0 == n, -jnp.inf, s4[n]) for n in range(EXPERTS_PER_GROUP)])
    take = lambda j: sum(jnp.where(j == n, a4[n], 0.0) for n in range(EXPERTS_PER_GROUP))
    w0, w1 = take(j0), take(j1)
    total = w0 + w1
    e0 = best * EXPERTS_PER_GROUP + j0
    e1 = best * EXPERTS_PER_GROUP + j1
    rid = lax.broadcasted_iota(I32, (SUBLANES, tm), 0)
    e_ref[...] = jnp.where(rid == 0, e0, jnp.where(rid == 1, e1, 0))
    rid = lax.broadcasted_iota(I32, (LANES, tm), 0)
    rows = jnp.where(rid == 0, w0 / total, jnp.where(rid == 1, w1 / total, 0.0))
    r_ref[...] = rows.T


def _router(h, wr_perm, bias_perm, tm):
    t, d = h.shape
    return pl.pallas_call(
        _router_body,
        grid=(t // tm,),
        in_specs=[pl.BlockSpec((tm, d), lambda i: (i, 0)), pl.BlockSpec((d, 2 * LANES), lambda i: (0, 0)),
                  pl.BlockSpec((N_EXPERTS, 1), lambda i: (0, 0))],
        out_specs=[pl.BlockSpec((SUBLANES, tm), lambda i: (0, i)), pl.BlockSpec((tm, LANES), lambda i: (i, 0))],
        out_shape=[jax.ShapeDtypeStruct((SUBLANES, t), I32), jax.ShapeDtypeStruct((t, LANES), F32)],
        compiler_params=_params("parallel"),
        name="router",
    )(h, wr_perm, bias_perm)


def _dispatch_plan(e01, tm):
    t = e01.shape[1]
    flat = e01.reshape(-1)
    onehot = (flat[:, None] == jnp.arange(N_EXPERTS, dtype=I32)[None, :]).astype(I32)
    cum = jnp.cumsum(onehot, axis=0)
    rank = jnp.sum(onehot * cum, axis=1) - 1
    counts = cum[-1]
    padded = ((counts + tm - 1) // tm) * tm
    starts = jnp.concatenate([jnp.zeros((1,), I32), jnp.cumsum(padded).astype(I32)])
    pos = jnp.sum(onehot * starts[None, :N_EXPERTS], axis=1) + rank
    return starts, starts[:N_EXPERTS] + counts, pos[:t], pos[t:]


def _dispatch_body(p0_ref, p1_ref, gs_ref, ge_ref, h_ref, xs_hbm, xbuf, zbuf, sem, zsem, *, tm, n_steps):
    i = pl.program_id(0)
    slot = i % 2
    n_tail = (xs_hbm.shape[0] - gs_ref[N_EXPERTS]) // tm

    def zero_copy(row0, size):
        return pltpu.make_async_copy(zbuf.at[pl.ds(0, size), :], xs_hbm.at[pl.ds(row0, size), :], zsem.at[0])

    def for_each_fill(fn):
        def group_pad(e, carry):
            end = gs_ref[e + 1]
            pad = end - ge_ref[e]
            covered = 0
            size = tm // 2
            while size >= SUBLANES:
                take = pad & size

                @pl.when(take != 0)
                def _():
                    fn(zero_copy(pl.multiple_of(end - covered - size, size), size))

                covered = covered + take
                size //= 2
            for k in range(SUBLANES - 1):
                @pl.when(k < (pad & (SUBLANES - 1)))
                def _():
                    fn(zero_copy(ge_ref[e] + k, 1))
            return carry

        def tail_tile(k, carry):
            fn(zero_copy(pl.multiple_of(gs_ref[N_EXPERTS] + k * tm, tm), tm))
            return carry

        lax.fori_loop(0, N_EXPERTS, group_pad, 0)
        lax.fori_loop(0, n_tail, tail_tile, 0)

    def wait_rows(sl):
        for _ in range(2):
            pltpu.make_async_copy(xbuf.at[sl], xs_hbm.at[pl.ds(0, tm), :], sem.at[sl]).wait()

    @pl.when(i == 0)
    def _():
        zbuf[...] = jnp.zeros(zbuf.shape, F32)
        for_each_fill(lambda cp: cp.start())

    @pl.when(i >= 2)
    def _():
        wait_rows(slot)

    xbuf[slot] = h_ref[...]
    base = i * tm
    for r in range(tm):
        src = xbuf.at[slot, pl.ds(r, 1), :]
        pltpu.make_async_copy(src, xs_hbm.at[pl.ds(p0_ref[base + r], 1), :], sem.at[slot]).start(priority=0)
        pltpu.make_async_copy(src, xs_hbm.at[pl.ds(p1_ref[base + r], 1), :], sem.at[slot]).start(priority=1)

    @pl.when(i == n_steps - 1)
    def _():
        wait_rows(slot)
        if n_steps > 1:
            wait_rows(1 - slot)
        for_each_fill(lambda cp: cp.wait())


def _dispatch(h, starts, ends, pos0, pos1, tm):
    t, d = h.shape
    n_steps = t // tm
    n_rows = 2 * t + N_EXPERTS * tm
    body = functools.partial(_dispatch_body, tm=tm, n_steps=n_steps)
    grid_spec = pltpu.PrefetchScalarGridSpec(
        num_scalar_prefetch=4,
        grid=(n_steps,),
        in_specs=[pl.BlockSpec((tm, d), lambda i, *_: (i, 0))],
        out_specs=pl.BlockSpec(memory_space=pl.ANY),
        scratch_shapes=[pltpu.VMEM((2, tm, d), F32), pltpu.VMEM((tm, d), F32),
                        pltpu.SemaphoreType.DMA((2,)), pltpu.SemaphoreType.DMA((1,))],
    )
    return pl.pallas_call(
        body,
        grid_spec=grid_spec,
        out_shape=jax.ShapeDtypeStruct((n_rows, d), F32),
        compiler_params=_params("arbitrary"),
        name="dispatch",
    )(pos0, pos1, starts, ends, h)


def _experts_body(gs_ref, xs_hbm, wgu_ref, wd_ref, ys_hbm, xbuf, ybuf, wgu_bf, wd_bf, xsem, osem, *, tm):
    e = pl.program_id(0)
    first = gs_ref[e] // tm
    n_tiles = gs_ref[e + 1] // tm - first
    total = gs_ref[N_EXPERTS] // tm

    def in_copy(g, slot):
        row0 = pl.multiple_of(g * tm, tm)
        return pltpu.make_async_copy(xs_hbm.at[pl.ds(row0, tm), :], xbuf.at[slot], xsem.at[slot])

    def out_copy(g, slot):
        row0 = pl.multiple_of(g * tm, tm)
        return pltpu.make_async_copy(ybuf.at[slot], ys_hbm.at[pl.ds(row0, tm), :], osem.at[slot])

    @pl.when(e == 0)
    def _():
        in_copy(0, 0).start()
        ybuf[...] = jnp.zeros(ybuf.shape, F32)
        out_copy(0, 0).start()
        out_copy(1, 1).start()

    @pl.when(n_tiles > 0)
    def _():
        wgu_bf[...] = wgu_ref[0, 0].astype(BF)
        wd_bf[...] = wd_ref[0, 0].astype(BF)

        def tile(k, carry):
            g = first + k
            slot = g % 2
            in_copy(g, slot).wait()
            out_copy(g, slot).wait()
            in_copy(jnp.minimum(g + 1, total - 1), 1 - slot).start()
            x = xbuf[slot].astype(BF)
            gu = jnp.dot(x, wgu_bf[...], preferred_element_type=F32)
            gate, up = gu[:, :EXPERT_FF], gu[:, EXPERT_FF:]
            hidden = (gate * jax.nn.sigmoid(gate) * up).astype(BF)
            ybuf[slot] = jnp.dot(hidden, wd_bf[...], preferred_element_type=F32)
            out_copy(g, slot).start()
            return carry

        lax.fori_loop(0, n_tiles, tile, 0)

    @pl.when(e == N_EXPERTS - 1)
    def _():
        in_copy(0, total % 2).wait()
        out_copy(0, 0).wait()
        out_copy(1, 1).wait()
        end = gs_ref[N_EXPERTS]
        n_tail = (ys_hbm.shape[0] - end) // tm
        ybuf[0] = jnp.zeros(ybuf.shape[1:], F32)

        def tail_copy(k):
            row0 = pl.multiple_of(end + k * tm, tm)
            return pltpu.make_async_copy(ybuf.at[0], ys_hbm.at[pl.ds(row0, tm), :], osem.at[0])

        def start_one(k, carry):
            tail_copy(k).start()
            return carry

        def wait_one(k, carry):
            tail_copy(k).wait()
            return carry

        lax.fori_loop(0, n_tail, start_one, 0)
        lax.fori_loop(0, n_tail, wait_one, 0)


def _experts(xs, starts, w_gate_up, w_down, layer, tm):
    n_rows, d = xs.shape
    body = functools.partial(_experts_body, tm=tm)
    grid_spec = pltpu.PrefetchScalarGridSpec(
        num_scalar_prefetch=1,
        grid=(N_EXPERTS,),
        in_specs=[pl.BlockSpec(memory_space=pl.ANY),
                  pl.BlockSpec((1, 1, d, 2 * EXPERT_FF), lambda e, gs: (layer, e, 0, 0)),
                  pl.BlockSpec((1, 1, EXPERT_FF, d), lambda e, gs: (layer, e, 0, 0))],
        out_specs=pl.BlockSpec(memory_space=pl.ANY),
        scratch_shapes=[pltpu.VMEM((2, tm, d), F32), pltpu.VMEM((2, tm, d), F32),
                        pltpu.VMEM((d, 2 * EXPERT_FF), BF), pltpu.VMEM((EXPERT_FF, d), BF),
                        pltpu.SemaphoreType.DMA((2,)), pltpu.SemaphoreType.DMA((2,))],
    )
    return pl.pallas_call(
        body,
        grid_spec=grid_spec,
        out_shape=jax.ShapeDtypeStruct((n_rows, d), F32),
        compiler_params=_params("arbitrary"),
        name="experts",
    )(starts, xs, w_gate_up, w_down)


def _combine_body(p0_ref, p1_ref, ys_hbm, h_ref, r_ref, g_ref, beta_ref, o_ref, buf0, buf1, sem, *, tc, n_steps):
    i = pl.program_id(0)
    slot = i % 2

    def start_row(base, r, sl):
        pltpu.make_async_copy(ys_hbm.at[pl.ds(p0_ref[base + r], 1), :], buf0.at[sl, pl.ds(r, 1), :],
                              sem.at[sl]).start(priority=0)
        pltpu.make_async_copy(ys_hbm.at[pl.ds(p1_ref[base + r], 1), :], buf1.at[sl, pl.ds(r, 1), :],
                              sem.at[sl]).start(priority=1)

    def wait_rows(sl):
        pltpu.make_async_copy(ys_hbm.at[pl.ds(0, tc), :], buf0.at[sl], sem.at[sl]).wait()
        pltpu.make_async_copy(ys_hbm.at[pl.ds(0, tc), :], buf1.at[sl], sem.at[sl]).wait()

    @pl.when(i == 0)
    def _():
        def one(r, carry):
            start_row(0, r, 0)
            return carry

        lax.fori_loop(0, tc, one, 0)

    wait_rows(slot)
    nxt = jnp.minimum(i + 1, n_steps - 1) * tc
    for r in range(tc):
        start_row(nxt, r, 1 - slot)
    gates = r_ref[...]
    y = gates[:, 0:1] * buf0[slot] + gates[:, 1:2] * buf1[slot]
    o_ref[...] = _layer_norm(ALPHA * h_ref[...] + y, g_ref[...], beta_ref[...])

    @pl.when(i == n_steps - 1)
    def _():
        wait_rows(1 - slot)


def _combine_ln(ys, pos0, pos1, h, gates, g, beta, tc):
    t, d = h.shape
    n_steps = t // tc
    body = functools.partial(_combine_body, tc=tc, n_steps=n_steps)
    row = lambda i, p0, p1: (i, 0)
    const = lambda i, p0, p1: (0, 0)
    grid_spec = pltpu.PrefetchScalarGridSpec(
        num_scalar_prefetch=2,
        grid=(n_steps,),
        in_specs=[pl.BlockSpec(memory_space=pl.ANY), pl.BlockSpec((tc, d), row), pl.BlockSpec((tc, LANES), row),
                  pl.BlockSpec((1, d), const), pl.BlockSpec((1, d), const)],
        out_specs=pl.BlockSpec((tc, d), row),
        scratch_shapes=[pltpu.VMEM((2, tc, d), F32), pltpu.VMEM((2, tc, d), F32), pltpu.SemaphoreType.DMA((2,))],
    )
    return pl.pallas_call(
        body,
        grid_spec=grid_spec,
        out_shape=jax.ShapeDtypeStruct((t, d), F32),
        compiler_params=_params("arbitrary"),
        name="combine_ln",
    )(pos0, pos1, ys, h, gates, g, beta)


def _moe_ln(h, wr_perm, bias_perm, w_gate_up, w_down, layer, g, beta, tm_router=512, tm_expert=256, tc=256):
    e01, gates = _router(h, wr_perm, bias_perm, tm_router)
    starts, ends, pos0, pos1 = _dispatch_plan(e01[:MOE_TOPK], tm_expert)
    xs = _dispatch(h, starts, ends, pos0, pos1, tm_expert)
    ys = _experts(xs, starts, w_gate_up, w_down, layer, tm_expert)
    return _combine_ln(ys, pos0, pos1, h, gates, g, beta, tc)


def _swap_halves(w):
    half = w.shape[-1] // 2
    return jnp.concatenate([w[..., half:], w[..., :half]], axis=-1)


def _even_weights(w_in, w_q_up, w_kv_up, forget_bias):
    d = w_in.shape[0]
    o_kv = MLA_Q_LORA
    o_kr = o_kv + MLA_KV_LORA
    o_fq = o_kr + MLA_ROPE
    hd = FOX_HEADS * FOX_HEAD_DIM
    o_fl = o_fq + 3 * hd
    k_r = w_in[:, o_kr:o_fq]
    f_l = jnp.pad(w_in[:, o_fl:], ((0, 0), (0, LANES - FOX_HEADS)))
    w_small = jnp.concatenate([w_in[:, :o_kr], k_r, _swap_halves(k_r), f_l], axis=1).astype(BF)
    w_fox = w_in[:, o_fq:o_fl].astype(BF)
    cs_fox = jnp.concatenate([jnp.full((1, hd), FOX_SCALE, F32), jnp.ones((1, 2 * hd), F32)], axis=1)
    wq = w_q_up.reshape(MLA_Q_LORA, MLA_HEADS, MLA_NOPE + MLA_ROPE)
    wqn = wq[:, :, :MLA_NOPE].reshape(MLA_Q_LORA, -1).astype(BF)
    pe = wq[:, :, MLA_NOPE:]
    wqp = jnp.concatenate([pe, _swap_halves(pe)], axis=-1).reshape(MLA_Q_LORA, -1).astype(BF)
    wkv = w_kv_up.reshape(MLA_KV_LORA, MLA_HEADS, MLA_NOPE + MLA_V)
    wk = wkv[:, :, :MLA_NOPE].reshape(MLA_KV_LORA, -1).astype(BF)
    wv = wkv[:, :, MLA_NOPE:].reshape(MLA_KV_LORA, -1).astype(BF)
    fb = jnp.pad(forget_bias.astype(F32), (0, LANES - FOX_HEADS)).reshape(1, LANES)
    return w_small, w_fox, cs_fox, wqn, wqp, wk, wv, fb


def _rope_table(seq):
    half = MLA_ROPE // 2
    inv_freq = ROPE_THETA ** (-jnp.arange(half, dtype=F32) / half)
    ang = jnp.arange(seq).astype(F32)[:, None] * inv_freq[None, :]
    cos, sin = jnp.cos(ang), jnp.sin(ang)
    return jnp.concatenate([cos, cos, -sin, sin], axis=1)


def _router_weights(w_router, router_bias):
    r = np.arange(N_EXPERTS)
    perm = (r % N_GROUPS) * EXPERTS_PER_GROUP + r // N_GROUPS
    wr = jnp.pad(w_router[:, perm], ((0, 0), (0, LANES - N_EXPERTS)))
    w_hi = wr.astype(BF)
    w_lo = (wr - w_hi.astype(F32)).astype(BF)
    return jnp.concatenate([w_hi, w_lo], axis=1), router_bias.astype(F32)[perm].reshape(N_EXPERTS, 1)


def _even_layer(h, batch, seq, w_in, q_norm, w_q_up, kv_norm, w_kv_up, forget_bias, w_out, g, beta):
    w_small, w_fox, cs_fox, wqn, wqp, wk, wv, fb = _even_weights(w_in, w_q_up, w_kv_up, forget_bias)
    small = _mm(h, w_small, jnp.ones((1, w_small.shape[1]), F32), F32, 512, 1024)
    fox_qkv = _mm(h, w_fox, cs_fox, BF, 512, 1536)
    qn, qp, kn, v, kp = _mla_prep(small, _rope_table(seq), q_norm.reshape(1, -1), kv_norm.reshape(1, -1),
                                  wqn, wqp, wk, wv, seq, 512)
    q_bias, k_bias = _fox_cum(small, fb, batch, seq, 512)
    a = _mla_attention(qn, qp, kn, v, kp, batch, seq, 512, 4)
    bo = _fox_attention(fox_qkv, q_bias, k_bias, batch, seq, 512, 4)
    return _outproj_ln(a, bo, w_out.astype(BF), h, g, beta, 512)


def _odd_layer(h, batch, seq, w_in, sinks, w_out, g, beta):
    n_sq = SWA_Q_HEADS * SWA_HEAD_DIM
    n_skv = SWA_KV_HEADS * SWA_HEAD_DIM
    n_m = MOBA_HEADS * MOBA_HEAD_DIM
    o_mq = n_sq + 2 * n_skv
    w_big = jnp.concatenate([w_in[:, :n_sq], w_in[:, o_mq:], w_in[:, n_sq:o_mq]], axis=1).astype(BF)
    cs = jnp.concatenate([jnp.full((1, n_sq), SWA_SCALE, F32), jnp.full((1, n_m), MOBA_SCALE, F32),
                          jnp.ones((1, 2 * n_m + 2 * n_skv), F32)], axis=1)
    big = _mm(h, w_big, cs, BF, 512, 2176)
    blocks = lambda cols: cols // LANES
    k_mean = _moba_kmean(big, batch, seq, 2)
    slopes = jnp.asarray(LOG2E * 2.0 ** (-8.0 * np.arange(1, MOBA_HEADS + 1) / MOBA_HEADS), dtype=F32)
    c = _swa_attention(big, sinks.astype(F32), batch, seq, 0, blocks(n_sq + 3 * n_m), blocks(n_sq + 3 * n_m) + 1)
    dd = _moba_attention(big, k_mean, slopes, batch, seq, blocks(n_sq), blocks(n_sq + n_m), blocks(n_sq + 2 * n_m),
                         8)
    return _outproj_ln(c, dd, w_out.astype(BF), h, g, beta, 512)


def kernel(x, w_router, router_bias, even_w_in, even_q_norm, even_w_q_up, even_kv_norm, even_w_kv_up,
           even_forget_bias, even_w_out, odd_w_in, odd_sinks, odd_w_out, ln_mix_g, ln_mix_b, ln_ffn_g, ln_ffn_b,
           w_gate_up, w_down):
    batch, seq, d = x.shape
    h = x.reshape(batch * seq, d)
    wr_perm, bias_perm = _router_weights(w_router, router_bias)
    row = lambda p, layer: p[layer].reshape(1, d)
    for layer in range(DEPTH):
        i = layer // 2
        if layer % 2 == 0:
            h = _even_layer(h, batch, seq, even_w_in[i], even_q_norm[i], even_w_q_up[i], even_kv_norm[i],
                            even_w_kv_up[i], even_forget_bias[i], even_w_out[i], row(ln_mix_g, layer),
                            row(ln_mix_b, layer))
        else:
            h = _odd_layer(h, batch, seq, odd_w_in[i], odd_sinks[i], odd_w_out[i], row(ln_mix_g, layer),
                           row(ln_mix_b, layer))
        h = _moe_ln(h, wr_perm, bias_perm, w_gate_up, w_down, layer, row(ln_ffn_g, layer), row(ln_ffn_b, layer))
    return h.reshape(batch, seq, d)
```

```python
import functools

import numpy as np
import jax
import jax.numpy as jnp
from jax import lax
from jax.experimental import pallas as pl
from jax.experimental.pallas import tpu as pltpu

BF = jnp.bfloat16
F32 = jnp.float32
I32 = jnp.int32

LANES = 128
SUBLANES = 8
VMEM_LIMIT = 56 * 1024 * 1024

D_MODEL = 2048
DEPTH = 2
LN_EPS = 1e-5
RMS_EPS = 1e-6
ALPHA = (2 * DEPTH) ** 0.25
MLA_HEADS, MLA_Q_LORA, MLA_KV_LORA, MLA_NOPE, MLA_ROPE, MLA_V = 8, 512, 256, 128, 64, 128
ROPE_THETA = 10000.0
FOX_HEADS, FOX_HEAD_DIM = 8, 128
SWA_Q_HEADS, SWA_KV_HEADS, SWA_HEAD_DIM, SWA_WINDOW = 16, 2, 64, 128
MOBA_HEADS, MOBA_HEAD_DIM, MOBA_BLOCK, MOBA_TOPK = 8, 128, 256, 3
N_EXPERTS, N_GROUPS, MOE_TOPK, EXPERT_FF = 32, 8, 2, 512
EXPERTS_PER_GROUP = N_EXPERTS // N_GROUPS

LOG2E = 1.4426950408889634
MLA_SCALE = (MLA_NOPE + MLA_ROPE) ** -0.5 * LOG2E
FOX_SCALE = FOX_HEAD_DIM ** -0.5 * LOG2E
MOBA_SCALE = MOBA_HEAD_DIM ** -0.5 * LOG2E
SWA_SCALE = SWA_HEAD_DIM ** -0.5 * LOG2E
NEG = -1e30

NT_DIMS = (((1,), (1,)), ((), ()))


def _params(*sem):
    return pltpu.CompilerParams(dimension_semantics=sem, vmem_limit_bytes=VMEM_LIMIT)


def _mm_body(x_ref, w_ref, cs_ref, o_ref):
    acc = jnp.dot(x_ref[...].astype(BF), w_ref[...], preferred_element_type=F32)
    o_ref[...] = (acc * cs_ref[...]).astype(o_ref.dtype)


def _mm(x, w, col_scale, out_dtype, tm, tn):
    m, k = x.shape
    n = w.shape[1]
    return pl.pallas_call(
        _mm_body,
        grid=(n // tn, m // tm),
        in_specs=[
            pl.BlockSpec((tm, k), lambda j, i: (i, 0)),
            pl.BlockSpec((k, tn), lambda j, i: (0, j)),
            pl.BlockSpec((1, tn), lambda j, i: (0, j)),
        ],
        out_specs=pl.BlockSpec((tm, tn), lambda j, i: (i, j)),
        out_shape=jax.ShapeDtypeStruct((m, n), out_dtype),
        compiler_params=_params("parallel", "parallel"),
        name="proj",
    )(x, w, col_scale)


def _rms(x, g):
    return x * lax.rsqrt(jnp.mean(x * x, axis=-1, keepdims=True) + RMS_EPS) * g


def _rope_pair(slab, table):
    r = slab * table
    return r + pltpu.roll(r, MLA_ROPE, axis=1)


def _mla_prep_body(cq_ref, ckv_ref, krp_ref, rope_ref, qg_ref, kvg_ref, wqn_ref, wqp_ref, wk_ref, wv_ref,
                   qn_ref, qp_ref, kn_ref, v_ref, kp_ref):
    table = rope_ref[...]
    cqn = _rms(cq_ref[...], qg_ref[...]).astype(BF)
    qn_ref[...] = (jnp.dot(cqn, wqn_ref[...], preferred_element_type=F32) * MLA_SCALE).astype(BF)
    qp = jnp.dot(cqn, wqp_ref[...], preferred_element_type=F32)
    for h in range(MLA_HEADS):
        sl = slice(h * LANES, (h + 1) * LANES)
        qp_ref[:, sl] = (_rope_pair(qp[:, sl], table) * MLA_SCALE).astype(BF)
    ckvn = _rms(ckv_ref[...], kvg_ref[...]).astype(BF)
    kn_ref[...] = jnp.dot(ckvn, wk_ref[...], preferred_element_type=F32).astype(BF)
    v_ref[...] = jnp.dot(ckvn, wv_ref[...], preferred_element_type=F32).astype(BF)
    kr = _rope_pair(krp_ref[...], table)
    lane = lax.broadcasted_iota(I32, kr.shape, 1)
    kp_ref[...] = jnp.where(lane < MLA_ROPE, kr, 0.0).astype(BF)


def _mla_prep(small, rope_table, q_norm, kv_norm, wqn, wqp, wk, wv, seq, tm):
    t = small.shape[0]
    n_s = seq // tm
    hd = MLA_HEADS * LANES
    row = lambda i: (i, 0)
    const = lambda i: (0, 0)
    out = lambda w: pl.BlockSpec((tm, w), row)
    return pl.pallas_call(
        _mla_prep_body,
        grid=(t // tm,),
        in_specs=[
            pl.BlockSpec((tm, MLA_Q_LORA), lambda i: (i, 0)),
            pl.BlockSpec((tm, MLA_KV_LORA), lambda i: (i, MLA_Q_LORA // MLA_KV_LORA)),
            pl.BlockSpec((tm, LANES), lambda i: (i, (MLA_Q_LORA + MLA_KV_LORA) // LANES)),
            pl.BlockSpec((tm, LANES), lambda i: (i % n_s, 0)),
            pl.BlockSpec((1, MLA_Q_LORA), const),
            pl.BlockSpec((1, MLA_KV_LORA), const),
            pl.BlockSpec((MLA_Q_LORA, hd), const),
            pl.BlockSpec((MLA_Q_LORA, hd), const),
            pl.BlockSpec((MLA_KV_LORA, hd), const),
            pl.BlockSpec((MLA_KV_LORA, hd), const),
        ],
        out_specs=[out(hd), out(hd), out(hd), out(hd), out(LANES)],
        out_shape=[jax.ShapeDtypeStruct((t, hd), BF)] * 4 + [jax.ShapeDtypeStruct((t, LANES), BF)],
        compiler_params=_params("parallel"),
        name="mla_prep",
    )(small, small, small, rope_table, q_norm, kv_norm, wqn, wqp, wk, wv)


N_SPLIT = 3


def _split_piece(x, which):
    hi = x.astype(BF).astype(F32)
    rest = x - hi
    mid = rest.astype(BF).astype(F32)
    return jnp.where(which == 0, hi, jnp.where(which == 1, mid, rest - mid))


def _fox_cum_body(fl_ref, fb_ref, spread_ref, qe_ref, ke_ref, edge_ref, carry_ref):
    @pl.when(pl.program_id(1) == 0)
    def _():
        carry_ref[...] = jnp.zeros_like(carry_ref)

    z = fl_ref[...] + fb_ref[...]
    log_f = jnp.minimum(z, 0.0) - jnp.log1p(jnp.exp(-jnp.abs(z)))
    tm = z.shape[0]
    r = lax.broadcasted_iota(I32, (tm, tm), 0)
    c = lax.broadcasted_iota(I32, (tm, tm), 1)
    tri = jnp.where(r >= c, 1.0, 0.0).astype(BF)
    parts = jnp.dot(tri, jnp.concatenate([_split_piece(log_f, n).astype(BF) for n in range(N_SPLIT)], axis=1),
                    preferred_element_type=F32)
    cum = (parts[:, :LANES] + parts[:, LANES:2 * LANES]) + parts[:, 2 * LANES:] + carry_ref[...]
    carry_ref[...] = cum[tm - 1:tm, :]
    cum = cum * LOG2E
    width = qe_ref.shape[1]
    pieces = jnp.concatenate([_split_piece(cum, n).astype(BF) for n in range(N_SPLIT)], axis=1)
    placed = jnp.dot(pieces, spread_ref[...], preferred_element_type=F32)
    li = lax.broadcasted_iota(I32, (tm, width), 1) % LANES
    qe_ref[...] = jnp.where((li >= N_SPLIT) & (li < 2 * N_SPLIT), 1.0, placed[:, :width]).astype(BF)
    ke_ref[...] = jnp.where(li < N_SPLIT, 1.0, placed[:, width:]).astype(BF)
    edge_ref[0, 0:1, :] = cum[0:1, :]
    edge_ref[0, 1:2, :] = cum[tm - 1:tm, :]


def _fox_spread():
    width = FOX_HEADS * LANES
    m = np.zeros((N_SPLIT * LANES, 2 * width), np.float32)
    for j in range(N_SPLIT):
        for h in range(FOX_HEADS):
            m[j * LANES + h, h * LANES + j] = 1.0
            m[j * LANES + h, width + h * LANES + N_SPLIT + j] = -1.0
    return jnp.asarray(m, dtype=BF)


def _fox_cum(small, forget_bias_row, batch, seq, tm):
    t = small.shape[0]
    n_s = seq // tm
    width = FOX_HEADS * LANES
    fl_block = (MLA_Q_LORA + MLA_KV_LORA + LANES) // LANES
    q_bias, k_bias, edge = pl.pallas_call(
        _fox_cum_body,
        grid=(batch, n_s),
        in_specs=[
            pl.BlockSpec((tm, LANES), lambda b, i: (b * n_s + i, fl_block)),
            pl.BlockSpec((1, LANES), lambda b, i: (0, 0)),
            pl.BlockSpec((N_SPLIT * LANES, 2 * width), lambda b, i: (0, 0)),
        ],
        out_specs=[
            pl.BlockSpec((tm, width), lambda b, i: (b * n_s + i, 0)),
            pl.BlockSpec((tm, width), lambda b, i: (b * n_s + i, 0)),
            pl.BlockSpec((1, 2, LANES), lambda b, i: (b * n_s + i, 0, 0)),
        ],
        out_shape=[jax.ShapeDtypeStruct((t, width), BF), jax.ShapeDtypeStruct((t, width), BF),
                   jax.ShapeDtypeStruct((batch * n_s, 2, LANES), F32)],
        scratch_shapes=[pltpu.VMEM((1, LANES), F32)],
        compiler_params=_params("parallel", "arbitrary"),
        name="fox_cum",
    )(small, forget_bias_row, _fox_spread())
    return q_bias, k_bias, edge[:, :, :FOX_HEADS].reshape(-1)


def _softmax_update(s_all, v_all, m_ref, l_ref, acc_ref):
    heads = range(len(s_all))
    m_prev = [m_ref[g] for g in heads]
    m_new = [jnp.maximum(m_prev[g], jnp.max(s_all[g], axis=1, keepdims=True)) for g in heads]
    reps = s_all[0].shape[1] // LANES
    p = [jnp.exp2(s_all[g] - jnp.concatenate([m_new[g]] * reps, axis=1)) for g in heads]
    pv = [jnp.dot(p[g].astype(BF), v_all[g], preferred_element_type=F32) for g in heads]
    for g in heads:
        alpha = jnp.exp2(m_prev[g] - m_new[g])
        l_ref[g] = alpha * l_ref[g] + jnp.sum(p[g], axis=1, keepdims=True)
        acc_ref[g] = alpha * acc_ref[g] + pv[g]
        m_ref[g] = m_new[g]


def _attn_body(*refs, mode, t, group):
    hg = pl.program_id(1)
    i = pl.program_id(2)
    heads = range(group)
    sl = lambda g: slice(g * LANES, (g + 1) * LANES)
    if mode == "moba":
        q_ref, k_ref, kx_ref, v_ref, km_ref, slope_ref, o_ref, m_ref, l_ref, acc_ref, qx_ref = refs
        for g in heads:
            qx_ref[g] = _moba_query_lanes(q_ref[:, sl(g)], km_ref[:, sl(g)], slope_ref[hg * group + g], i, t)
        q_extra = lambda g: qx_ref[g]
        k_extra = lambda g, ks: kx_ref[pl.ds(ks, t), :]
    elif mode == "mla":
        q_ref, qx_ref, k_ref, kx_ref, v_ref, o_ref, m_ref, l_ref, acc_ref = refs
        q_extra = lambda g: qx_ref[:, sl(g)]
        k_extra = lambda g, ks: kx_ref[pl.ds(ks, t), :]
    else:
        q_ref, qx_ref, k_ref, kx_ref, v_ref, edge_ref, o_ref, m_ref, l_ref, acc_ref, knorm_ref = refs
        q_extra = lambda g: qx_ref[:, sl(g)]
        k_extra = lambda g, ks: kx_ref[pl.ds(ks, t), sl(g)]
        first_tile = _fox_first_tile(q_ref, k_ref, edge_ref, knorm_ref, i, hg, t, group)

    def scores(g, ks):
        q = jnp.concatenate([q_ref[:, sl(g)], q_extra(g)], axis=1)
        k = jnp.concatenate([k_ref[pl.ds(ks, t), sl(g)], k_extra(g, ks)], axis=1)
        return lax.dot_general(q, k, NT_DIMS, preferred_element_type=F32)

    m_ref[...] = jnp.full(m_ref.shape, NEG, F32)
    l_ref[...] = jnp.zeros(l_ref.shape, F32)
    acc_ref[...] = jnp.zeros(acc_ref.shape, F32)

    ks = pl.multiple_of(i * t, t)
    r = lax.broadcasted_iota(I32, (t, t), 0)
    c = lax.broadcasted_iota(I32, (t, t), 1)
    values = lambda ks: [v_ref[pl.ds(ks, t), sl(g)] for g in heads]
    _softmax_update([jnp.where(c <= r, scores(g, ks), NEG) for g in heads], values(ks), m_ref, l_ref, acc_ref)

    def past_tile(j, carry):
        ks = pl.multiple_of(j * t, t)
        _softmax_update([scores(g, ks) for g in heads], values(ks), m_ref, l_ref, acc_ref)
        return carry

    lax.fori_loop(first_tile if mode == "fox" else 0, i, past_tile, 0)
    for g in heads:
        o_ref[:, sl(g)] = (acc_ref[g] / l_ref[g]).astype(o_ref.dtype)


FOX_SKIP_GAP = 160.0


def _fox_first_tile(q_ref, k_ref, edge_ref, knorm_ref, i, hg, t, group):
    b = pl.program_id(0)
    n_q = pl.num_programs(2)
    sl = lambda g: slice(g * LANES, (g + 1) * LANES)

    def max_row_norm(x):
        x = x.astype(F32)
        return jnp.sqrt(jnp.max(jnp.sum(x * x, axis=1, keepdims=True)))

    @pl.when(i == 0)
    def _():
        for g in range(group):
            knorm_ref[g] = max_row_norm(k_ref[:, sl(g)])

    def edge(tile, last, g):
        return edge_ref[((b * n_q + tile) * 2 + last) * FOX_HEADS + hg * group + g]

    limit = [-(2.0 * 1.001 * max_row_norm(q_ref[:, sl(g)]) * knorm_ref[g] + FOX_SKIP_GAP) for g in range(group)]
    first = [edge(i, 0, g) for g in range(group)]

    def scan(j, lo):
        needed = first[0] - edge(j, 1, 0) >= limit[0]
        for g in range(1, group):
            needed = needed | (first[g] - edge(j, 1, g) >= limit[g])
        return jnp.where(needed, jnp.minimum(lo, j), lo)

    return lax.fori_loop(0, i, scan, i)


def _attention(mode, batch, seq, heads, t, group, operands, in_specs, extra_scratch=()):
    n_q = seq // t
    body = functools.partial(_attn_body, mode=mode, t=t, group=group)
    return pl.pallas_call(
        body,
        grid=(batch, heads // group, n_q),
        in_specs=in_specs,
        out_specs=pl.BlockSpec((t, group * LANES), lambda b, h, i: (b * n_q + i, h)),
        out_shape=jax.ShapeDtypeStruct((batch * seq, heads * LANES), BF),
        scratch_shapes=[pltpu.VMEM((group, t, LANES), F32), pltpu.VMEM((group, t, LANES), F32),
                        pltpu.VMEM((group, t, LANES), F32), *extra_scratch],
        compiler_params=_params("parallel", "parallel", "arbitrary"),
        name="attn_" + mode,
    )(*operands)


def _q_spec(t, n_q, group, col0=0):
    return pl.BlockSpec((t, group * LANES), lambda b, h, i: (b * n_q + i, col0 + h))


def _kv_spec(seq, group, col0=0):
    return pl.BlockSpec((seq, group * LANES), lambda b, h, i: (b, col0 + h))


def _mla_attention(qn, qp, kn, v, kp, batch, seq, t, group):
    n_q = seq // t
    specs = [_q_spec(t, n_q, group), _q_spec(t, n_q, group), _kv_spec(seq, group),
             pl.BlockSpec((seq, LANES), lambda b, h, i: (b, 0)), _kv_spec(seq, group)]
    return _attention("mla", batch, seq, MLA_HEADS, t, group, (qn, qp, kn, kp, v), specs)


def _fox_attention(qkv, q_bias, k_bias, edge, batch, seq, t, group):
    n_q = seq // t
    n_hg = FOX_HEADS // group
    specs = [_q_spec(t, n_q, group, 0), _q_spec(t, n_q, group, 0), _kv_spec(seq, group, n_hg),
             _kv_spec(seq, group, 0), _kv_spec(seq, group, 2 * n_hg), pl.BlockSpec(memory_space=pltpu.SMEM)]
    return _attention("fox", batch, seq, FOX_HEADS, t, group, (qkv, q_bias, qkv, k_bias, qkv, edge), specs,
                      (pltpu.SMEM((group,), F32),))


def _moba_key_lanes(seq):
    n_kb = seq // MOBA_BLOCK
    kpos = np.arange(seq)
    lanes = np.zeros((seq, LANES), np.float32)
    lanes[kpos, kpos // MOBA_BLOCK] = 1.0
    lanes[:, n_kb:n_kb + N_SPLIT] = 1.0
    lanes[:, n_kb + N_SPLIT:n_kb + 2 * N_SPLIT] = (MOBA_BLOCK * (kpos // MOBA_BLOCK))[:, None]
    lanes[:, n_kb + 2 * N_SPLIT:n_kb + 3 * N_SPLIT] = (kpos % MOBA_BLOCK)[:, None]
    return jnp.asarray(lanes, dtype=BF)


def _moba_query_lanes(q, k_mean, slope, i, t):
    n_kb = k_mean.shape[0]
    gate = lax.dot_general(k_mean.astype(BF), q, NT_DIMS, preferred_element_type=F32)
    blk = lax.broadcasted_iota(I32, (n_kb, t), 0)
    beaten = jnp.zeros((n_kb, t), F32)
    for n in range(n_kb):
        g_n = gate[n:n + 1, :]
        wins = (g_n > gate) | ((g_n == gate) & (blk > n))
        beaten = beaten + jnp.where(wins, 1.0, 0.0) * jnp.where(i > n, 1.0, 0.0)
    attended = (blk == i) | ((blk < i) & (beaten < MOBA_TOPK))
    mask = jnp.concatenate([jnp.where(attended, 0.0, NEG), jnp.zeros((LANES - n_kb, t), F32)], axis=0)
    row = lax.broadcasted_iota(I32, (LANES, t), 0) - n_kb
    qpos = (i * t + lax.broadcasted_iota(I32, (1, t), 1)).astype(F32)
    slope_row = jnp.full((1, t), slope, F32)
    which = row % N_SPLIT
    lanes = jnp.where(row < 0, mask,
                      jnp.where(row < N_SPLIT, _split_piece(-slope_row * qpos, which),
                                jnp.where(row < 3 * N_SPLIT, _split_piece(slope_row, which), 0.0)))
    return lanes.T.astype(BF)


def _moba_attention(big, k_mean, slopes, batch, seq, col_q, col_k, col_v, group):
    t = MOBA_BLOCK
    n_q = seq // t
    specs = [_q_spec(t, n_q, group, col_q // group), _kv_spec(seq, group, col_k // group),
             pl.BlockSpec((seq, LANES), lambda b, h, i: (0, 0)),
             _kv_spec(seq, group, col_v // group),
             pl.BlockSpec((n_q, group * LANES), lambda b, h, i: (b, h)),
             pl.BlockSpec(memory_space=pltpu.SMEM)]
    scratch = (pltpu.VMEM((group, t, LANES), BF),)
    return _attention("moba", batch, seq, MOBA_HEADS, t, group,
                      (big, big, _moba_key_lanes(seq), big, k_mean, slopes), specs, scratch)


def _kmean_body(k_ref, o_ref):
    n_kb = o_ref.shape[0]
    for n in range(n_kb):
        blk = k_ref[n * MOBA_BLOCK:(n + 1) * MOBA_BLOCK, :].astype(F32)
        o_ref[n:n + 1, :] = jnp.mean(blk, axis=0, keepdims=True)


def _moba_kmean(big, batch, seq, col_block):
    n_kb = seq // MOBA_BLOCK
    width = MOBA_HEADS * MOBA_HEAD_DIM
    return pl.pallas_call(
        _kmean_body,
        grid=(batch,),
        in_specs=[pl.BlockSpec((seq, width), lambda b: (b, col_block))],
        out_specs=pl.BlockSpec((n_kb, width), lambda b: (b, 0)),
        out_shape=jax.ShapeDtypeStruct((batch * n_kb, width), F32),
        compiler_params=_params("parallel"),
        name="moba_kmean",
    )(big)


def _swa_body(sink_ref, q_ref, kp_ref, kc_ref, vp_ref, vc_ref, o_ref):
    i = pl.program_id(1)
    w = SWA_WINDOW
    half = SWA_HEAD_DIM
    lane = lax.broadcasted_iota(I32, (w, LANES), 1)
    r = lax.broadcasted_iota(I32, (w, w), 0)
    c = lax.broadcasted_iota(I32, (w, w), 1)
    from_prev = c > r
    dist = (r - c + jnp.where(from_prev, w, 0)).astype(F32)
    valid = jnp.logical_not(from_prev) | (i > 0)
    heads_per_kv = SWA_Q_HEADS // SWA_KV_HEADS

    def lo_hi(ref, kvh):
        own = jnp.where((lane < half) == (kvh == 0), ref[...].astype(F32), 0.0)
        other = pltpu.roll(own, half, axis=1)
        lo, hi = (own, other) if kvh == 0 else (other, own)
        return lo.astype(BF), hi.astype(BF)

    for kvh in range(SWA_KV_HEADS):
        kp, kc, vp, vc = (lo_hi(ref, kvh) for ref in (kp_ref, kc_ref, vp_ref, vc_ref))
        heads = range(kvh * heads_per_kv, (kvh + 1) * heads_per_kv)
        q = {h: q_ref[:, (h // 2) * LANES:(h // 2 + 1) * LANES] for h in heads}
        s = {h: jnp.where(from_prev,
                          lax.dot_general(q[h], kp[h % 2], NT_DIMS, preferred_element_type=F32),
                          lax.dot_general(q[h], kc[h % 2], NT_DIMS, preferred_element_type=F32)) for h in heads}
        p, inv = {}, {}
        for h in heads:
            slope = float(LOG2E * 2.0 ** (-8.0 * (h + 1) / SWA_Q_HEADS))
            sink = sink_ref[h] * LOG2E
            logits = jnp.where(valid, s[h] - slope * dist, NEG)
            m = jnp.maximum(jnp.max(logits, axis=1, keepdims=True), sink)
            p[h] = jnp.exp2(logits - m)
            inv[h] = 1.0 / (jnp.sum(p[h], axis=1, keepdims=True) + jnp.exp2(sink - m))
        for pair in range(kvh * heads_per_kv // 2, (kvh + 1) * heads_per_kv // 2):
            out = jnp.zeros((w, LANES), F32)
            for h in (2 * pair, 2 * pair + 1):
                pn = p[h] * inv[h]
                out = out + jnp.dot(jnp.where(from_prev, pn, 0.0).astype(BF), vp[h % 2], preferred_element_type=F32)
                out = out + jnp.dot(jnp.where(from_prev, 0.0, pn).astype(BF), vc[h % 2], preferred_element_type=F32)
            o_ref[:, pair * LANES:(pair + 1) * LANES] = out.astype(o_ref.dtype)


def _swa_attention(big, sinks, batch, seq, col_q, col_k, col_v):
    w = SWA_WINDOW
    n_q = seq // w
    width = SWA_Q_HEADS * SWA_HEAD_DIM
    prev = lambda col: pl.BlockSpec((w, LANES), lambda b, i: (b * n_q + jnp.maximum(i - 1, 0), col))
    cur = lambda col: pl.BlockSpec((w, LANES), lambda b, i: (b * n_q + i, col))
    return pl.pallas_call(
        _swa_body,
        grid=(batch, n_q),
        in_specs=[pl.BlockSpec(memory_space=pltpu.SMEM),
                  pl.BlockSpec((w, width), lambda b, i: (b * n_q + i, col_q)),
                  prev(col_k), cur(col_k), prev(col_v), cur(col_v)],
        out_specs=pl.BlockSpec((w, width), lambda b, i: (b * n_q + i, 0)),
        out_shape=jax.ShapeDtypeStruct((batch * seq, width), BF),
        compiler_params=_params("parallel", "parallel"),
        name="swa",
    )(sinks, big, big, big, big, big)


def _layer_norm(z, g, b):
    mu = jnp.mean(z, axis=-1, keepdims=True)
    zc = z - mu
    var = jnp.mean(zc * zc, axis=-1, keepdims=True)
    return zc * lax.rsqrt(var + LN_EPS) * g + b


def _outproj_body(a_ref, b_ref, wa_ref, wb_ref, h_ref, g_ref, beta_ref, o_ref):
    mix = jnp.dot(a_ref[...], wa_ref[...], preferred_element_type=F32)
    mix = mix + jnp.dot(b_ref[...], wb_ref[...], preferred_element_type=F32)
    o_ref[...] = _layer_norm(ALPHA * h_ref[...] + mix, g_ref[...], beta_ref[...])


def _outproj_ln(a, b, w_out, h, g, beta, tm):
    t, d = h.shape
    ka = a.shape[1]
    row = lambda i: (i, 0)
    return pl.pallas_call(
        _outproj_body,
        grid=(t // tm,),
        in_specs=[pl.BlockSpec((tm, ka), row), pl.BlockSpec((tm, ka), row),
                  pl.BlockSpec((ka, d), lambda i: (0, 0)), pl.BlockSpec((ka, d), lambda i: (1, 0)),
                  pl.BlockSpec((tm, d), row), pl.BlockSpec((1, d), lambda i: (0, 0)),
                  pl.BlockSpec((1, d), lambda i: (0, 0))],
        out_specs=pl.BlockSpec((tm, d), row),
        out_shape=jax.ShapeDtypeStruct((t, d), F32),
        compiler_params=_params("parallel"),
        name="outproj_ln",
    )(a, b, w_out, w_out, h, g, beta)


def _second_of_four(a, b, c, d):
    hi_ab, lo_ab = jnp.maximum(a, b), jnp.minimum(a, b)
    hi_cd, lo_cd = jnp.maximum(c, d), jnp.minimum(c, d)
    return jnp.maximum(jnp.maximum(lo_ab, lo_cd), jnp.minimum(hi_ab, hi_cd))


def _router_body(h_ref, wr_ref, bias_ref, e_ref, r_ref):
    tm = h_ref.shape[0]
    h = h_ref[...]
    h_hi = h.astype(BF)
    h_lo = (h - h_hi.astype(F32)).astype(BF)
    both = jnp.dot(h_hi, wr_ref[...], preferred_element_type=F32)
    logits = (both[:, :LANES] + both[:, LANES:]) + jnp.dot(h_lo, wr_ref[:, :LANES], preferred_element_type=F32)
    lt = logits.T
    aff = [jax.nn.sigmoid(lt[SUBLANES * j:SUBLANES * (j + 1), :]) for j in range(EXPERTS_PER_GROUP)]
    sel = [aff[j] + bias_ref[SUBLANES * j:SUBLANES * (j + 1), :] for j in range(EXPERTS_PER_GROUP)]
    top1 = jnp.maximum(jnp.maximum(sel[0], sel[1]), jnp.maximum(sel[2], sel[3]))
    score = top1 + _second_of_four(*sel)
    gid = lax.broadcasted_iota(I32, (N_GROUPS, tm), 0)
    best = jnp.min(jnp.where(score == jnp.max(score, axis=0, keepdims=True), gid, N_GROUPS),
                   axis=0, keepdims=True)
    in_grp = gid == best
    pick = lambda x: jnp.sum(jnp.where(in_grp, x, 0.0), axis=0, keepdims=True)
    s4 = [pick(x) for x in sel]
    a4 = [pick(x) for x in aff]

    def argmax4(vals):
        j, v = jnp.zeros((1, tm), I32), vals[0]
        for n in range(1, EXPERTS_PER_GROUP):
            better = vals[n] > v
            j, v = jnp.where(better, n, j), jnp.where(better, vals[n], v)
        return j

    j0 = argmax4(s4)
    j1 = argmax4([jnp.where(j0 == n, -jnp.inf, s4[n]) for n in range(EXPERTS_PER_GROUP)])
    take = lambda j: sum(jnp.where(j == n, a4[n], 0.0) for n in range(EXPERTS_PER_GROUP))
    w0, w1 = take(j0), take(j1)
    total = w0 + w1
    e0 = best * EXPERTS_PER_GROUP + j0
    e1 = best * EXPERTS_PER_GROUP + j1
    rid = lax.broadcasted_iota(I32, (SUBLANES, tm), 0)
    e_ref[...] = jnp.where(rid == 0, e0, jnp.where(rid == 1, e1, 0))
    rid = lax.broadcasted_iota(I32, (LANES, tm), 0)
    rows = jnp.where(rid == 0, w0 / total, jnp.where(rid == 1, w1 / total, 0.0))
    r_ref[...] = rows.T


def _router(h, wr_perm, bias_perm, tm):
    t, d = h.shape
    return pl.pallas_call(
        _router_body,
        grid=(t // tm,),
        in_specs=[pl.BlockSpec((tm, d), lambda i: (i, 0)), pl.BlockSpec((d, 2 * LANES), lambda i: (0, 0)),
                  pl.BlockSpec((N_EXPERTS, 1), lambda i: (0, 0))],
        out_specs=[pl.BlockSpec((SUBLANES, tm), lambda i: (0, i)), pl.BlockSpec((tm, LANES), lambda i: (i, 0))],
        out_shape=[jax.ShapeDtypeStruct((SUBLANES, t), I32), jax.ShapeDtypeStruct((t, LANES), F32)],
        compiler_params=_params("parallel"),
        name="router",
    )(h, wr_perm, bias_perm)


def _dispatch_plan(e01, tm):
    t = e01.shape[1]
    flat = e01.reshape(-1)
    onehot = (flat[:, None] == jnp.arange(N_EXPERTS, dtype=I32)[None, :]).astype(I32)
    cum = jnp.cumsum(onehot, axis=0)
    rank = jnp.sum(onehot * cum, axis=1) - 1
    counts = cum[-1]
    padded = ((counts + tm - 1) // tm) * tm
    starts = jnp.concatenate([jnp.zeros((1,), I32), jnp.cumsum(padded).astype(I32)])
    pos = jnp.sum(onehot * starts[None, :N_EXPERTS], axis=1) + rank
    return starts, starts[:N_EXPERTS] + counts, pos[:t], pos[t:]


def _dispatch_body(p0_ref, p1_ref, gs_ref, ge_ref, h_ref, xs_hbm, xbuf, zbuf, sem, zsem, *, tm, n_steps):
    i = pl.program_id(0)
    slot = i % 2
    n_tail = (xs_hbm.shape[0] - gs_ref[N_EXPERTS]) // tm

    def zero_copy(row0, size):
        return pltpu.make_async_copy(zbuf.at[pl.ds(0, size), :], xs_hbm.at[pl.ds(row0, size), :], zsem.at[0])

    def for_each_fill(fn):
        def group_pad(e, carry):
            end = gs_ref[e + 1]
            pad = end - ge_ref[e]
            covered = 0
            size = tm // 2
            while size >= SUBLANES:
                take = pad & size

                @pl.when(take != 0)
                def _():
                    fn(zero_copy(pl.multiple_of(end - covered - size, size), size))

                covered = covered + take
                size //= 2
            for k in range(SUBLANES - 1):
                @pl.when(k < (pad & (SUBLANES - 1)))
                def _():
                    fn(zero_copy(ge_ref[e] + k, 1))
            return carry

        def tail_tile(k, carry):
            fn(zero_copy(pl.multiple_of(gs_ref[N_EXPERTS] + k * tm, tm), tm))
            return carry

        lax.fori_loop(0, N_EXPERTS, group_pad, 0)
        lax.fori_loop(0, n_tail, tail_tile, 0)

    def wait_rows(sl):
        for _ in range(2):
            pltpu.make_async_copy(xbuf.at[sl], xs_hbm.at[pl.ds(0, tm), :], sem.at[sl]).wait()

    @pl.when(i == 0)
    def _():
        zbuf[...] = jnp.zeros(zbuf.shape, F32)
        for_each_fill(lambda cp: cp.start())

    @pl.when(i >= 2)
    def _():
        wait_rows(slot)

    xbuf[slot] = h_ref[...]
    base = i * tm
    for r in range(tm):
        src = xbuf.at[slot, pl.ds(r, 1), :]
        pltpu.make_async_copy(src, xs_hbm.at[pl.ds(p0_ref[base + r], 1), :], sem.at[slot]).start(priority=0)
        pltpu.make_async_copy(src, xs_hbm.at[pl.ds(p1_ref[base + r], 1), :], sem.at[slot]).start(priority=1)

    @pl.when(i == n_steps - 1)
    def _():
        wait_rows(slot)
        if n_steps > 1:
            wait_rows(1 - slot)
        for_each_fill(lambda cp: cp.wait())


def _dispatch(h, starts, ends, pos0, pos1, tm):
    t, d = h.shape
    n_steps = t // tm
    n_rows = 2 * t + N_EXPERTS * tm
    body = functools.partial(_dispatch_body, tm=tm, n_steps=n_steps)
    grid_spec = pltpu.PrefetchScalarGridSpec(
        num_scalar_prefetch=4,
        grid=(n_steps,),
        in_specs=[pl.BlockSpec((tm, d), lambda i, *_: (i, 0))],
        out_specs=pl.BlockSpec(memory_space=pl.ANY),
        scratch_shapes=[pltpu.VMEM((2, tm, d), F32), pltpu.VMEM((tm, d), F32),
                        pltpu.SemaphoreType.DMA((2,)), pltpu.SemaphoreType.DMA((1,))],
    )
    return pl.pallas_call(
        body,
        grid_spec=grid_spec,
        out_shape=jax.ShapeDtypeStruct((n_rows, d), F32),
        compiler_params=_params("arbitrary"),
        name="dispatch",
    )(pos0, pos1, starts, ends, h)


def _experts_body(gs_ref, xs_hbm, wgu_ref, wd_ref, ys_hbm, xbuf, ybuf, wgu_bf, wd_bf, xsem, osem, *, tm):
    e = pl.program_id(0)
    first = gs_ref[e] // tm
    n_tiles = gs_ref[e + 1] // tm - first
    total = gs_ref[N_EXPERTS] // tm

    def in_copy(g, slot):
        row0 = pl.multiple_of(g * tm, tm)
        return pltpu.make_async_copy(xs_hbm.at[pl.ds(row0, tm), :], xbuf.at[slot], xsem.at[slot])

    def out_copy(g, slot):
        row0 = pl.multiple_of(g * tm, tm)
        return pltpu.make_async_copy(ybuf.at[slot], ys_hbm.at[pl.ds(row0, tm), :], osem.at[slot])

    @pl.when(e == 0)
    def _():
        in_copy(0, 0).start()
        ybuf[...] = jnp.zeros(ybuf.shape, F32)
        out_copy(0, 0).start()
        out_copy(1, 1).start()

    @pl.when(n_tiles > 0)
    def _():
        wgu_bf[...] = wgu_ref[0, 0].astype(BF)
        wd_bf[...] = wd_ref[0, 0].astype(BF)

        def tile(k, carry):
            g = first + k
            slot = g % 2
            in_copy(g, slot).wait()
            out_copy(g, slot).wait()
            in_copy(jnp.minimum(g + 1, total - 1), 1 - slot).start()
            x = xbuf[slot].astype(BF)
            gu = jnp.dot(x, wgu_bf[...], preferred_element_type=F32)
            gate, up = gu[:, :EXPERT_FF], gu[:, EXPERT_FF:]
            hidden = (gate * jax.nn.sigmoid(gate) * up).astype(BF)
            ybuf[slot] = jnp.dot(hidden, wd_bf[...], preferred_element_type=F32)
            out_copy(g, slot).start()
            return carry

        lax.fori_loop(0, n_tiles, tile, 0)

    @pl.when(e == N_EXPERTS - 1)
    def _():
        in_copy(0, total % 2).wait()
        out_copy(0, 0).wait()
        out_copy(1, 1).wait()
        end = gs_ref[N_EXPERTS]
        n_tail = (ys_hbm.shape[0] - end) // tm
        ybuf[0] = jnp.zeros(ybuf.shape[1:], F32)

        def tail_copy(k):
            row0 = pl.multiple_of(end + k * tm, tm)
            return pltpu.make_async_copy(ybuf.at[0], ys_hbm.at[pl.ds(row0, tm), :], osem.at[0])

        def start_one(k, carry):
            tail_copy(k).start()
            return carry

        def wait_one(k, carry):
            tail_copy(k).wait()
            return carry

        lax.fori_loop(0, n_tail, start_one, 0)
        lax.fori_loop(0, n_tail, wait_one, 0)


def _experts(xs, starts, w_gate_up, w_down, layer, tm):
    n_rows, d = xs.shape
    body = functools.partial(_experts_body, tm=tm)
    grid_spec = pltpu.PrefetchScalarGridSpec(
        num_scalar_prefetch=1,
        grid=(N_EXPERTS,),
        in_specs=[pl.BlockSpec(memory_space=pl.ANY),
                  pl.BlockSpec((1, 1, d, 2 * EXPERT_FF), lambda e, gs: (layer, e, 0, 0)),
                  pl.BlockSpec((1, 1, EXPERT_FF, d), lambda e, gs: (layer, e, 0, 0))],
        out_specs=pl.BlockSpec(memory_space=pl.ANY),
        scratch_shapes=[pltpu.VMEM((2, tm, d), F32), pltpu.VMEM((2, tm, d), F32),
                        pltpu.VMEM((d, 2 * EXPERT_FF), BF), pltpu.VMEM((EXPERT_FF, d), BF),
                        pltpu.SemaphoreType.DMA((2,)), pltpu.SemaphoreType.DMA((2,))],
    )
    return pl.pallas_call(
        body,
        grid_spec=grid_spec,
        out_shape=jax.ShapeDtypeStruct((n_rows, d), F32),
        compiler_params=_params("arbitrary"),
        name="experts",
    )(starts, xs, w_gate_up, w_down)


def _combine_body(p0_ref, p1_ref, ys_hbm, h_ref, r_ref, g_ref, beta_ref, o_ref, buf0, buf1, sem, *, tc, n_steps):
    i = pl.program_id(0)
    slot = i % 2

    def start_row(base, r, sl):
        pltpu.make_async_copy(ys_hbm.at[pl.ds(p0_ref[base + r], 1), :], buf0.at[sl, pl.ds(r, 1), :],
                              sem.at[sl]).start(priority=0)
        pltpu.make_async_copy(ys_hbm.at[pl.ds(p1_ref[base + r], 1), :], buf1.at[sl, pl.ds(r, 1), :],
                              sem.at[sl]).start(priority=1)

    def wait_rows(sl):
        pltpu.make_async_copy(ys_hbm.at[pl.ds(0, tc), :], buf0.at[sl], sem.at[sl]).wait()
        pltpu.make_async_copy(ys_hbm.at[pl.ds(0, tc), :], buf1.at[sl], sem.at[sl]).wait()

    @pl.when(i == 0)
    def _():
        def one(r, carry):
            start_row(0, r, 0)
            return carry

        lax.fori_loop(0, tc, one, 0)

    wait_rows(slot)
    nxt = jnp.minimum(i + 1, n_steps - 1) * tc
    for r in range(tc):
        start_row(nxt, r, 1 - slot)
    gates = r_ref[...]
    y = gates[:, 0:1] * buf0[slot] + gates[:, 1:2] * buf1[slot]
    o_ref[...] = _layer_norm(ALPHA * h_ref[...] + y, g_ref[...], beta_ref[...])

    @pl.when(i == n_steps - 1)
    def _():
        wait_rows(1 - slot)


def _combine_ln(ys, pos0, pos1, h, gates, g, beta, tc):
    t, d = h.shape
    n_steps = t // tc
    body = functools.partial(_combine_body, tc=tc, n_steps=n_steps)
    row = lambda i, p0, p1: (i, 0)
    const = lambda i, p0, p1: (0, 0)
    grid_spec = pltpu.PrefetchScalarGridSpec(
        num_scalar_prefetch=2,
        grid=(n_steps,),
        in_specs=[pl.BlockSpec(memory_space=pl.ANY), pl.BlockSpec((tc, d), row), pl.BlockSpec((tc, LANES), row),
                  pl.BlockSpec((1, d), const), pl.BlockSpec((1, d), const)],
        out_specs=pl.BlockSpec((tc, d), row),
        scratch_shapes=[pltpu.VMEM((2, tc, d), F32), pltpu.VMEM((2, tc, d), F32), pltpu.SemaphoreType.DMA((2,))],
    )
    return pl.pallas_call(
        body,
        grid_spec=grid_spec,
        out_shape=jax.ShapeDtypeStruct((t, d), F32),
        compiler_params=_params("arbitrary"),
        name="combine_ln",
    )(pos0, pos1, ys, h, gates, g, beta)


def _moe_ln(h, wr_perm, bias_perm, w_gate_up, w_down, layer, g, beta, tm_router=512, tm_expert=256, tc=256):
    e01, gates = _router(h, wr_perm, bias_perm, tm_router)
    starts, ends, pos0, pos1 = _dispatch_plan(e01[:MOE_TOPK], tm_expert)
    xs = _dispatch(h, starts, ends, pos0, pos1, tm_expert)
    ys = _experts(xs, starts, w_gate_up, w_down, layer, tm_expert)
    return _combine_ln(ys, pos0, pos1, h, gates, g, beta, tc)


def _swap_halves(w):
    half = w.shape[-1] // 2
    return jnp.concatenate([w[..., half:], w[..., :half]], axis=-1)


def _even_weights(w_in, w_q_up, w_kv_up, forget_bias):
    d = w_in.shape[0]
    o_kv = MLA_Q_LORA
    o_kr = o_kv + MLA_KV_LORA
    o_fq = o_kr + MLA_ROPE
    hd = FOX_HEADS * FOX_HEAD_DIM
    o_fl = o_fq + 3 * hd
    k_r = w_in[:, o_kr:o_fq]
    f_l = jnp.pad(w_in[:, o_fl:], ((0, 0), (0, LANES - FOX_HEADS)))
    w_small = jnp.concatenate([w_in[:, :o_kr], k_r, _swap_halves(k_r), f_l], axis=1).astype(BF)
    w_fox = w_in[:, o_fq:o_fl].astype(BF)
    cs_fox = jnp.concatenate([jnp.full((1, hd), FOX_SCALE, F32), jnp.ones((1, 2 * hd), F32)], axis=1)
    wq = w_q_up.reshape(MLA_Q_LORA, MLA_HEADS, MLA_NOPE + MLA_ROPE)
    wqn = wq[:, :, :MLA_NOPE].reshape(MLA_Q_LORA, -1).astype(BF)
    pe = wq[:, :, MLA_NOPE:]
    wqp = jnp.concatenate([pe, _swap_halves(pe)], axis=-1).reshape(MLA_Q_LORA, -1).astype(BF)
    wkv = w_kv_up.reshape(MLA_KV_LORA, MLA_HEADS, MLA_NOPE + MLA_V)
    wk = wkv[:, :, :MLA_NOPE].reshape(MLA_KV_LORA, -1).astype(BF)
    wv = wkv[:, :, MLA_NOPE:].reshape(MLA_KV_LORA, -1).astype(BF)
    fb = jnp.pad(forget_bias.astype(F32), (0, LANES - FOX_HEADS)).reshape(1, LANES)
    return w_small, w_fox, cs_fox, wqn, wqp, wk, wv, fb


def _rope_table(seq):
    half = MLA_ROPE // 2
    inv_freq = ROPE_THETA ** (-jnp.arange(half, dtype=F32) / half)
    ang = jnp.arange(seq).astype(F32)[:, None] * inv_freq[None, :]
    cos, sin = jnp.cos(ang), jnp.sin(ang)
    return jnp.concatenate([cos, cos, -sin, sin], axis=1)


def _router_weights(w_router, router_bias):
    r = np.arange(N_EXPERTS)
    perm = (r % N_GROUPS) * EXPERTS_PER_GROUP + r // N_GROUPS
    wr = jnp.pad(w_router[:, perm], ((0, 0), (0, LANES - N_EXPERTS)))
    w_hi = wr.astype(BF)
    w_lo = (wr - w_hi.astype(F32)).astype(BF)
    return jnp.concatenate([w_hi, w_lo], axis=1), router_bias.astype(F32)[perm].reshape(N_EXPERTS, 1)


def _even_layer(h, batch, seq, w_in, q_norm, w_q_up, kv_norm, w_kv_up, forget_bias, w_out, g, beta):
    w_small, w_fox, cs_fox, wqn, wqp, wk, wv, fb = _even_weights(w_in, w_q_up, w_kv_up, forget_bias)
    small = _mm(h, w_small, jnp.ones((1, w_small.shape[1]), F32), F32, 512, 1024)
    fox_qkv = _mm(h, w_fox, cs_fox, BF, 512, 1536)
    qn, qp, kn, v, kp = _mla_prep(small, _rope_table(seq), q_norm.reshape(1, -1), kv_norm.reshape(1, -1),
                                  wqn, wqp, wk, wv, seq, 512)
    q_bias, k_bias, edge = _fox_cum(small, fb, batch, seq, 512)
    a = _mla_attention(qn, qp, kn, v, kp, batch, seq, 512, 4)
    bo = _fox_attention(fox_qkv, q_bias, k_bias, edge, batch, seq, 512, 4)
    return _outproj_ln(a, bo, w_out.astype(BF), h, g, beta, 512)


def _odd_layer(h, batch, seq, w_in, sinks, w_out, g, beta):
    n_sq = SWA_Q_HEADS * SWA_HEAD_DIM
    n_skv = SWA_KV_HEADS * SWA_HEAD_DIM
    n_m = MOBA_HEADS * MOBA_HEAD_DIM
    o_mq = n_sq + 2 * n_skv
    w_big = jnp.concatenate([w_in[:, :n_sq], w_in[:, o_mq:], w_in[:, n_sq:o_mq]], axis=1).astype(BF)
    cs = jnp.concatenate([jnp.full((1, n_sq), SWA_SCALE, F32), jnp.full((1, n_m), MOBA_SCALE, F32),
                          jnp.ones((1, 2 * n_m + 2 * n_skv), F32)], axis=1)
    big = _mm(h, w_big, cs, BF, 512, 2176)
    blocks = lambda cols: cols // LANES
    k_mean = _moba_kmean(big, batch, seq, 2)
    slopes = jnp.asarray(LOG2E * 2.0 ** (-8.0 * np.arange(1, MOBA_HEADS + 1) / MOBA_HEADS), dtype=F32)
    c = _swa_attention(big, sinks.astype(F32), batch, seq, 0, blocks(n_sq + 3 * n_m), blocks(n_sq + 3 * n_m) + 1)
    dd = _moba_attention(big, k_mean, slopes, batch, seq, blocks(n_sq), blocks(n_sq + n_m), blocks(n_sq + 2 * n_m),
                         8)
    return _outproj_ln(c, dd, w_out.astype(BF), h, g, beta, 512)


def kernel(x, w_router, router_bias, even_w_in, even_q_norm, even_w_q_up, even_kv_norm, even_w_kv_up,
           even_forget_bias, even_w_out, odd_w_in, odd_sinks, odd_w_out, ln_mix_g, ln_mix_b, ln_ffn_g, ln_ffn_b,
           w_gate_up, w_down):
    batch, seq, d = x.shape
    h = x.reshape(batch * seq, d)
    wr_perm, bias_perm = _router_weights(w_router, router_bias)
    row = lambda p, layer: p[layer].reshape(1, d)
    for layer in range(DEPTH):
        i = layer // 2
        if layer % 2 == 0:
            h = _even_layer(h, batch, seq, even_w_in[i], even_q_norm[i], even_w_q_up[i], even_kv_norm[i],
                            even_w_kv_up[i], even_forget_bias[i], even_w_out[i], row(ln_mix_g, layer),
                            row(ln_mix_b, layer))
        else:
            h = _odd_layer(h, batch, seq, odd_w_in[i], odd_sinks[i], odd_w_out[i], row(ln_mix_g, layer),
                           row(ln_mix_b, layer))
        h = _moe_ln(h, wr_perm, bias_perm, w_gate_up, w_down, layer, row(ln_ffn_g, layer), row(ln_ffn_b, layer))
    return h.reshape(batch, seq, d)
```

```python
import functools

import numpy as np
import jax
import jax.numpy as jnp
from jax import lax
from jax.experimental import pallas as pl
from jax.experimental.pallas import tpu as pltpu

BF = jnp.bfloat16
F32 = jnp.float32
I32 = jnp.int32

LANES = 128
SUBLANES = 8
VMEM_LIMIT = 56 * 1024 * 1024

D_MODEL = 2048
DEPTH = 2
LN_EPS = 1e-5
RMS_EPS = 1e-6
ALPHA = (2 * DEPTH) ** 0.25
MLA_HEADS, MLA_Q_LORA, MLA_KV_LORA, MLA_NOPE, MLA_ROPE, MLA_V = 8, 512, 256, 128, 64, 128
ROPE_THETA = 10000.0
FOX_HEADS, FOX_HEAD_DIM = 8, 128
SWA_Q_HEADS, SWA_KV_HEADS, SWA_HEAD_DIM, SWA_WINDOW = 16, 2, 64, 128
MOBA_HEADS, MOBA_HEAD_DIM, MOBA_BLOCK, MOBA_TOPK = 8, 128, 256, 3
N_EXPERTS, N_GROUPS, MOE_TOPK, EXPERT_FF = 32, 8, 2, 512
EXPERTS_PER_GROUP = N_EXPERTS // N_GROUPS

LOG2E = 1.4426950408889634
MLA_SCALE = (MLA_NOPE + MLA_ROPE) ** -0.5 * LOG2E
FOX_SCALE = FOX_HEAD_DIM ** -0.5 * LOG2E
MOBA_SCALE = MOBA_HEAD_DIM ** -0.5 * LOG2E
SWA_SCALE = SWA_HEAD_DIM ** -0.5 * LOG2E
NEG = -1e30

NT_DIMS = (((1,), (1,)), ((), ()))


def _params(*sem):
    return pltpu.CompilerParams(dimension_semantics=sem, vmem_limit_bytes=VMEM_LIMIT)


def _mm_body(x_ref, w_ref, cs_ref, o_ref):
    acc = jnp.dot(x_ref[...].astype(BF), w_ref[...], preferred_element_type=F32)
    o_ref[...] = (acc * cs_ref[...]).astype(o_ref.dtype)


def _mm(x, w, col_scale, out_dtype, tm, tn):
    m, k = x.shape
    n = w.shape[1]
    return pl.pallas_call(
        _mm_body,
        grid=(n // tn, m // tm),
        in_specs=[
            pl.BlockSpec((tm, k), lambda j, i: (i, 0)),
            pl.BlockSpec((k, tn), lambda j, i: (0, j)),
            pl.BlockSpec((1, tn), lambda j, i: (0, j)),
        ],
        out_specs=pl.BlockSpec((tm, tn), lambda j, i: (i, j)),
        out_shape=jax.ShapeDtypeStruct((m, n), out_dtype),
        compiler_params=_params("parallel", "parallel"),
        name="proj",
    )(x, w, col_scale)


def _rms(x, g):
    return x * lax.rsqrt(jnp.mean(x * x, axis=-1, keepdims=True) + RMS_EPS) * g


def _rope_pair(slab, table):
    r = slab * table
    return r + pltpu.roll(r, MLA_ROPE, axis=1)


def _mla_prep_body(cq_ref, ckv_ref, krp_ref, rope_ref, qg_ref, kvg_ref, wqn_ref, wqp_ref, wk_ref, wv_ref,
                   qn_ref, qp_ref, kn_ref, v_ref, kp_ref):
    table = rope_ref[...]
    cqn = _rms(cq_ref[...], qg_ref[...]).astype(BF)
    qn_ref[...] = (jnp.dot(cqn, wqn_ref[...], preferred_element_type=F32) * MLA_SCALE).astype(BF)
    qp = jnp.dot(cqn, wqp_ref[...], preferred_element_type=F32)
    for h in range(MLA_HEADS):
        sl = slice(h * LANES, (h + 1) * LANES)
        qp_ref[:, sl] = (_rope_pair(qp[:, sl], table) * MLA_SCALE).astype(BF)
    ckvn = _rms(ckv_ref[...], kvg_ref[...]).astype(BF)
    kn_ref[...] = jnp.dot(ckvn, wk_ref[...], preferred_element_type=F32).astype(BF)
    v_ref[...] = jnp.dot(ckvn, wv_ref[...], preferred_element_type=F32).astype(BF)
    kr = _rope_pair(krp_ref[...], table)
    lane = lax.broadcasted_iota(I32, kr.shape, 1)
    kp_ref[...] = jnp.where(lane < MLA_ROPE, kr, 0.0).astype(BF)


def _mla_prep(small, rope_table, q_norm, kv_norm, wqn, wqp, wk, wv, seq, tm):
    t = small.shape[0]
    n_s = seq // tm
    hd = MLA_HEADS * LANES
    row = lambda i: (i, 0)
    const = lambda i: (0, 0)
    out = lambda w: pl.BlockSpec((tm, w), row)
    return pl.pallas_call(
        _mla_prep_body,
        grid=(t // tm,),
        in_specs=[
            pl.BlockSpec((tm, MLA_Q_LORA), lambda i: (i, 0)),
            pl.BlockSpec((tm, MLA_KV_LORA), lambda i: (i, MLA_Q_LORA // MLA_KV_LORA)),
            pl.BlockSpec((tm, LANES), lambda i: (i, (MLA_Q_LORA + MLA_KV_LORA) // LANES)),
            pl.BlockSpec((tm, LANES), lambda i: (i % n_s, 0)),
            pl.BlockSpec((1, MLA_Q_LORA), const),
            pl.BlockSpec((1, MLA_KV_LORA), const),
            pl.BlockSpec((MLA_Q_LORA, hd), const),
            pl.BlockSpec((MLA_Q_LORA, hd), const),
            pl.BlockSpec((MLA_KV_LORA, hd), const),
            pl.BlockSpec((MLA_KV_LORA, hd), const),
        ],
        out_specs=[out(hd), out(hd), out(hd), out(hd), out(LANES)],
        out_shape=[jax.ShapeDtypeStruct((t, hd), BF)] * 4 + [jax.ShapeDtypeStruct((t, LANES), BF)],
        compiler_params=_params("parallel"),
        name="mla_prep",
    )(small, small, small, rope_table, q_norm, kv_norm, wqn, wqp, wk, wv)


N_SPLIT = 3


def _split_piece(x, which):
    hi = x.astype(BF).astype(F32)
    rest = x - hi
    mid = rest.astype(BF).astype(F32)
    return jnp.where(which == 0, hi, jnp.where(which == 1, mid, rest - mid))


def _fox_cum_body(fl_ref, fb_ref, spread_ref, qe_ref, ke_ref, edge_ref, carry_ref):
    @pl.when(pl.program_id(1) == 0)
    def _():
        carry_ref[...] = jnp.zeros_like(carry_ref)

    z = fl_ref[...] + fb_ref[...]
    log_f = jnp.minimum(z, 0.0) - jnp.log1p(jnp.exp(-jnp.abs(z)))
    tm = z.shape[0]
    r = lax.broadcasted_iota(I32, (tm, tm), 0)
    c = lax.broadcasted_iota(I32, (tm, tm), 1)
    tri = jnp.where(r >= c, 1.0, 0.0).astype(BF)
    parts = jnp.dot(tri, jnp.concatenate([_split_piece(log_f, n).astype(BF) for n in range(N_SPLIT)], axis=1),
                    preferred_element_type=F32)
    cum = (parts[:, :LANES] + parts[:, LANES:2 * LANES]) + parts[:, 2 * LANES:] + carry_ref[...]
    carry_ref[...] = cum[tm - 1:tm, :]
    cum = cum * LOG2E
    width = qe_ref.shape[1]
    pieces = jnp.concatenate([_split_piece(cum, n).astype(BF) for n in range(N_SPLIT)], axis=1)
    placed = jnp.dot(pieces, spread_ref[...], preferred_element_type=F32)
    li = lax.broadcasted_iota(I32, (tm, width), 1) % LANES
    qe_ref[...] = jnp.where((li >= N_SPLIT) & (li < 2 * N_SPLIT), 1.0, placed[:, :width]).astype(BF)
    ke_ref[...] = jnp.where(li < N_SPLIT, 1.0, placed[:, width:]).astype(BF)
    edge_ref[0, 0:1, :] = cum[0:1, :]
    edge_ref[0, 1:2, :] = cum[tm - 1:tm, :]


def _fox_spread():
    width = FOX_HEADS * LANES
    m = np.zeros((N_SPLIT * LANES, 2 * width), np.float32)
    for j in range(N_SPLIT):
        for h in range(FOX_HEADS):
            m[j * LANES + h, h * LANES + j] = 1.0
            m[j * LANES + h, width + h * LANES + N_SPLIT + j] = -1.0
    return jnp.asarray(m, dtype=BF)


def _fox_cum(small, forget_bias_row, batch, seq, tm):
    t = small.shape[0]
    n_s = seq // tm
    width = FOX_HEADS * LANES
    fl_block = (MLA_Q_LORA + MLA_KV_LORA + LANES) // LANES
    q_bias, k_bias, edge = pl.pallas_call(
        _fox_cum_body,
        grid=(batch, n_s),
        in_specs=[
            pl.BlockSpec((tm, LANES), lambda b, i: (b * n_s + i, fl_block)),
            pl.BlockSpec((1, LANES), lambda b, i: (0, 0)),
            pl.BlockSpec((N_SPLIT * LANES, 2 * width), lambda b, i: (0, 0)),
        ],
        out_specs=[
            pl.BlockSpec((tm, width), lambda b, i: (b * n_s + i, 0)),
            pl.BlockSpec((tm, width), lambda b, i: (b * n_s + i, 0)),
            pl.BlockSpec((1, 2, LANES), lambda b, i: (b * n_s + i, 0, 0)),
        ],
        out_shape=[jax.ShapeDtypeStruct((t, width), BF), jax.ShapeDtypeStruct((t, width), BF),
                   jax.ShapeDtypeStruct((batch * n_s, 2, LANES), F32)],
        scratch_shapes=[pltpu.VMEM((1, LANES), F32)],
        compiler_params=_params("parallel", "arbitrary"),
        name="fox_cum",
    )(small, forget_bias_row, _fox_spread())
    return q_bias, k_bias, edge[:, :, :FOX_HEADS].reshape(-1)


def _softmax_update(s_all, v_all, m_ref, l_ref, acc_ref):
    heads = range(len(s_all))
    m_prev = [m_ref[g] for g in heads]
    m_new = [jnp.maximum(m_prev[g], jnp.max(s_all[g], axis=1, keepdims=True)) for g in heads]
    reps = s_all[0].shape[1] // LANES
    p = [jnp.exp2(s_all[g] - jnp.concatenate([m_new[g]] * reps, axis=1)) for g in heads]
    pv = [jnp.dot(p[g].astype(BF), v_all[g], preferred_element_type=F32) for g in heads]
    for g in heads:
        alpha = jnp.exp2(m_prev[g] - m_new[g])
        l_ref[g] = alpha * l_ref[g] + jnp.sum(p[g], axis=1, keepdims=True)
        acc_ref[g] = alpha * acc_ref[g] + pv[g]
        m_ref[g] = m_new[g]


def _attn_body(*refs, mode, t, group):
    hg = pl.program_id(1)
    i = pl.program_id(2)
    heads = range(group)
    sl = lambda g: slice(g * LANES, (g + 1) * LANES)
    if mode == "moba":
        q_ref, k_ref, kx_ref, v_ref, km_ref, slope_ref, o_ref, m_ref, l_ref, acc_ref, qx_ref = refs
        for g in heads:
            qx_ref[g] = _moba_query_lanes(q_ref[:, sl(g)], km_ref[:, sl(g)], slope_ref[hg * group + g], i, t)
        q_extra = lambda g: qx_ref[g]
        k_extra = lambda g, ks: kx_ref[pl.ds(ks, t), :]
    elif mode == "mla":
        q_ref, qx_ref, k_ref, kx_ref, v_ref, o_ref, m_ref, l_ref, acc_ref = refs
        q_extra = lambda g: qx_ref[:, sl(g)]
        k_extra = lambda g, ks: kx_ref[pl.ds(ks, t), :]
    else:
        q_ref, qx_ref, k_ref, kx_ref, v_ref, edge_ref, o_ref, m_ref, l_ref, acc_ref, knorm_ref = refs
        q_extra = lambda g: qx_ref[:, sl(g)]
        k_extra = lambda g, ks: kx_ref[pl.ds(ks, t), sl(g)]
        first_tile = _fox_first_tile(q_ref, k_ref, edge_ref, knorm_ref, i, hg, t, group)

    def scores(g, ks):
        q = jnp.concatenate([q_ref[:, sl(g)], q_extra(g)], axis=1)
        k = jnp.concatenate([k_ref[pl.ds(ks, t), sl(g)], k_extra(g, ks)], axis=1)
        return lax.dot_general(q, k, NT_DIMS, preferred_element_type=F32)

    m_ref[...] = jnp.full(m_ref.shape, NEG, F32)
    l_ref[...] = jnp.zeros(l_ref.shape, F32)
    acc_ref[...] = jnp.zeros(acc_ref.shape, F32)

    ks = pl.multiple_of(i * t, t)
    r = lax.broadcasted_iota(I32, (t, t), 0)
    c = lax.broadcasted_iota(I32, (t, t), 1)
    values = lambda ks: [v_ref[pl.ds(ks, t), sl(g)] for g in heads]
    _softmax_update([jnp.where(c <= r, scores(g, ks), NEG) for g in heads], values(ks), m_ref, l_ref, acc_ref)

    def past_tile(j, carry):
        ks = pl.multiple_of(j * t, t)
        _softmax_update([scores(g, ks) for g in heads], values(ks), m_ref, l_ref, acc_ref)
        return carry

    lax.fori_loop(first_tile if mode == "fox" else 0, i, past_tile, 0)
    for g in heads:
        o_ref[:, sl(g)] = (acc_ref[g] / l_ref[g]).astype(o_ref.dtype)


FOX_SKIP_GAP = 160.0


def _fox_first_tile(q_ref, k_ref, edge_ref, knorm_ref, i, hg, t, group):
    b = pl.program_id(0)
    n_q = pl.num_programs(2)
    sl = lambda g: slice(g * LANES, (g + 1) * LANES)

    def max_row_norm(x):
        x = x.astype(F32)
        return jnp.sqrt(jnp.max(jnp.sum(x * x, axis=1, keepdims=True)))

    @pl.when(i == 0)
    def _():
        for g in range(group):
            knorm_ref[g] = max_row_norm(k_ref[:, sl(g)])

    def edge(tile, last, g):
        return edge_ref[((b * n_q + tile) * 2 + last) * FOX_HEADS + hg * group + g]

    limit = [-(2.0 * 1.001 * max_row_norm(q_ref[:, sl(g)]) * knorm_ref[g] + FOX_SKIP_GAP) for g in range(group)]
    first = [edge(i, 0, g) for g in range(group)]

    def scan(j, lo):
        needed = first[0] - edge(j, 1, 0) >= limit[0]
        for g in range(1, group):
            needed = needed | (first[g] - edge(j, 1, g) >= limit[g])
        return jnp.where(needed, jnp.minimum(lo, j), lo)

    return lax.fori_loop(0, i, scan, i)


def _attention(mode, batch, seq, heads, t, group, operands, in_specs, extra_scratch=()):
    n_q = seq // t
    body = functools.partial(_attn_body, mode=mode, t=t, group=group)
    return pl.pallas_call(
        body,
        grid=(batch, heads // group, n_q),
        in_specs=in_specs,
        out_specs=pl.BlockSpec((t, group * LANES), lambda b, h, i: (b * n_q + i, h)),
        out_shape=jax.ShapeDtypeStruct((batch * seq, heads * LANES), BF),
        scratch_shapes=[pltpu.VMEM((group, t, LANES), F32), pltpu.VMEM((group, t, LANES), F32),
                        pltpu.VMEM((group, t, LANES), F32), *extra_scratch],
        compiler_params=_params("parallel", "parallel", "arbitrary"),
        name="attn_" + mode,
    )(*operands)


def _q_spec(t, n_q, group, col0=0):
    return pl.BlockSpec((t, group * LANES), lambda b, h, i: (b * n_q + i, col0 + h))


def _kv_spec(seq, group, col0=0):
    return pl.BlockSpec((seq, group * LANES), lambda b, h, i: (b, col0 + h))


def _mla_attention(qn, qp, kn, v, kp, batch, seq, t, group):
    n_q = seq // t
    specs = [_q_spec(t, n_q, group), _q_spec(t, n_q, group), _kv_spec(seq, group),
             pl.BlockSpec((seq, LANES), lambda b, h, i: (b, 0)), _kv_spec(seq, group)]
    return _attention("mla", batch, seq, MLA_HEADS, t, group, (qn, qp, kn, kp, v), specs)


def _fox_attention(qkv, q_bias, k_bias, edge, batch, seq, t, group):
    n_q = seq // t
    n_hg = FOX_HEADS // group
    specs = [_q_spec(t, n_q, group, 0), _q_spec(t, n_q, group, 0), _kv_spec(seq, group, n_hg),
             _kv_spec(seq, group, 0), _kv_spec(seq, group, 2 * n_hg), pl.BlockSpec(memory_space=pltpu.SMEM)]
    return _attention("fox", batch, seq, FOX_HEADS, t, group, (qkv, q_bias, qkv, k_bias, qkv, edge), specs,
                      (pltpu.SMEM((group,), F32),))


def _moba_key_lanes(seq):
    n_kb = seq // MOBA_BLOCK
    kpos = np.arange(seq)
    lanes = np.zeros((seq, LANES), np.float32)
    lanes[kpos, kpos // MOBA_BLOCK] = 1.0
    lanes[:, n_kb:n_kb + N_SPLIT] = 1.0
    lanes[:, n_kb + N_SPLIT:n_kb + 2 * N_SPLIT] = (MOBA_BLOCK * (kpos // MOBA_BLOCK))[:, None]
    lanes[:, n_kb + 2 * N_SPLIT:n_kb + 3 * N_SPLIT] = (kpos % MOBA_BLOCK)[:, None]
    return jnp.asarray(lanes, dtype=BF)


def _moba_query_lanes(q, k_mean, slope, i, t):
    n_kb = k_mean.shape[0]
    gate = lax.dot_general(k_mean.astype(BF), q, NT_DIMS, preferred_element_type=F32)
    blk = lax.broadcasted_iota(I32, (n_kb, t), 0)
    beaten = jnp.zeros((n_kb, t), F32)
    for n in range(n_kb):
        g_n = gate[n:n + 1, :]
        wins = (g_n > gate) | ((g_n == gate) & (blk > n))
        beaten = beaten + jnp.where(wins, 1.0, 0.0) * jnp.where(i > n, 1.0, 0.0)
    attended = (blk == i) | ((blk < i) & (beaten < MOBA_TOPK))
    mask = jnp.concatenate([jnp.where(attended, 0.0, NEG), jnp.zeros((LANES - n_kb, t), F32)], axis=0)
    row = lax.broadcasted_iota(I32, (LANES, t), 0) - n_kb
    qpos = (i * t + lax.broadcasted_iota(I32, (1, t), 1)).astype(F32)
    slope_row = jnp.full((1, t), slope, F32)
    which = row % N_SPLIT
    lanes = jnp.where(row < 0, mask,
                      jnp.where(row < N_SPLIT, _split_piece(-slope_row * qpos, which),
                                jnp.where(row < 3 * N_SPLIT, _split_piece(slope_row, which), 0.0)))
    return lanes.T.astype(BF)


def _moba_attention(big, k_mean, slopes, batch, seq, col_q, col_k, col_v, group):
    t = MOBA_BLOCK
    n_q = seq // t
    specs = [_q_spec(t, n_q, group, col_q // group), _kv_spec(seq, group, col_k // group),
             pl.BlockSpec((seq, LANES), lambda b, h, i: (0, 0)),
             _kv_spec(seq, group, col_v // group),
             pl.BlockSpec((n_q, group * LANES), lambda b, h, i: (b, h)),
             pl.BlockSpec(memory_space=pltpu.SMEM)]
    scratch = (pltpu.VMEM((group, t, LANES), BF),)
    return _attention("moba", batch, seq, MOBA_HEADS, t, group,
                      (big, big, _moba_key_lanes(seq), big, k_mean, slopes), specs, scratch)


def _kmean_body(k_ref, o_ref):
    n_kb = o_ref.shape[0]
    for n in range(n_kb):
        blk = k_ref[n * MOBA_BLOCK:(n + 1) * MOBA_BLOCK, :].astype(F32)
        o_ref[n:n + 1, :] = jnp.mean(blk, axis=0, keepdims=True)


def _moba_kmean(big, batch, seq, col_block):
    n_kb = seq // MOBA_BLOCK
    width = MOBA_HEADS * MOBA_HEAD_DIM
    return pl.pallas_call(
        _kmean_body,
        grid=(batch,),
        in_specs=[pl.BlockSpec((seq, width), lambda b: (b, col_block))],
        out_specs=pl.BlockSpec((n_kb, width), lambda b: (b, 0)),
        out_shape=jax.ShapeDtypeStruct((batch * n_kb, width), F32),
        compiler_params=_params("parallel"),
        name="moba_kmean",
    )(big)


def _swa_body(sink_ref, q_ref, kp_ref, kc_ref, vp_ref, vc_ref, o_ref):
    i = pl.program_id(1)
    w = SWA_WINDOW
    half = SWA_HEAD_DIM
    lane = lax.broadcasted_iota(I32, (w, LANES), 1)
    r = lax.broadcasted_iota(I32, (w, w), 0)
    c = lax.broadcasted_iota(I32, (w, w), 1)
    from_prev = c > r
    dist = (r - c + jnp.where(from_prev, w, 0)).astype(F32)
    valid = jnp.logical_not(from_prev) | (i > 0)
    heads_per_kv = SWA_Q_HEADS // SWA_KV_HEADS

    def lo_hi(ref, kvh):
        own = jnp.where((lane < half) == (kvh == 0), ref[...].astype(F32), 0.0)
        other = pltpu.roll(own, half, axis=1)
        lo, hi = (own, other) if kvh == 0 else (other, own)
        return lo.astype(BF), hi.astype(BF)

    for kvh in range(SWA_KV_HEADS):
        kp, kc, vp, vc = (lo_hi(ref, kvh) for ref in (kp_ref, kc_ref, vp_ref, vc_ref))
        heads = range(kvh * heads_per_kv, (kvh + 1) * heads_per_kv)
        q = {h: q_ref[:, (h // 2) * LANES:(h // 2 + 1) * LANES] for h in heads}
        s = {h: jnp.where(from_prev,
                          lax.dot_general(q[h], kp[h % 2], NT_DIMS, preferred_element_type=F32),
                          lax.dot_general(q[h], kc[h % 2], NT_DIMS, preferred_element_type=F32)) for h in heads}
        p, inv = {}, {}
        for h in heads:
            slope = float(LOG2E * 2.0 ** (-8.0 * (h + 1) / SWA_Q_HEADS))
            sink = sink_ref[h] * LOG2E
            logits = jnp.where(valid, s[h] - slope * dist, NEG)
            m = jnp.maximum(jnp.max(logits, axis=1, keepdims=True), sink)
            p[h] = jnp.exp2(logits - m)
            inv[h] = 1.0 / (jnp.sum(p[h], axis=1, keepdims=True) + jnp.exp2(sink - m))
        for pair in range(kvh * heads_per_kv // 2, (kvh + 1) * heads_per_kv // 2):
            out = jnp.zeros((w, LANES), F32)
            for h in (2 * pair, 2 * pair + 1):
                pn = p[h] * inv[h]
                out = out + jnp.dot(jnp.where(from_prev, pn, 0.0).astype(BF), vp[h % 2], preferred_element_type=F32)
                out = out + jnp.dot(jnp.where(from_prev, 0.0, pn).astype(BF), vc[h % 2], preferred_element_type=F32)
            o_ref[:, pair * LANES:(pair + 1) * LANES] = out.astype(o_ref.dtype)


def _swa_attention(big, sinks, batch, seq, col_q, col_k, col_v):
    w = SWA_WINDOW
    n_q = seq // w
    width = SWA_Q_HEADS * SWA_HEAD_DIM
    prev = lambda col: pl.BlockSpec((w, LANES), lambda b, i: (b * n_q + jnp.maximum(i - 1, 0), col))
    cur = lambda col: pl.BlockSpec((w, LANES), lambda b, i: (b * n_q + i, col))
    return pl.pallas_call(
        _swa_body,
        grid=(batch, n_q),
        in_specs=[pl.BlockSpec(memory_space=pltpu.SMEM),
                  pl.BlockSpec((w, width), lambda b, i: (b * n_q + i, col_q)),
                  prev(col_k), cur(col_k), prev(col_v), cur(col_v)],
        out_specs=pl.BlockSpec((w, width), lambda b, i: (b * n_q + i, 0)),
        out_shape=jax.ShapeDtypeStruct((batch * seq, width), BF),
        compiler_params=_params("parallel", "parallel"),
        name="swa",
    )(sinks, big, big, big, big, big)


def _layer_norm(z, g, b):
    mu = jnp.mean(z, axis=-1, keepdims=True)
    zc = z - mu
    var = jnp.mean(zc * zc, axis=-1, keepdims=True)
    return zc * lax.rsqrt(var + LN_EPS) * g + b


def _outproj_body(a_ref, b_ref, wa_ref, wb_ref, h_ref, g_ref, beta_ref, o_ref):
    mix = jnp.dot(a_ref[...], wa_ref[...], preferred_element_type=F32)
    mix = mix + jnp.dot(b_ref[...], wb_ref[...], preferred_element_type=F32)
    o_ref[...] = _layer_norm(ALPHA * h_ref[...] + mix, g_ref[...], beta_ref[...])


def _outproj_ln(a, b, w_out, h, g, beta, tm):
    t, d = h.shape
    ka = a.shape[1]
    row = lambda i: (i, 0)
    return pl.pallas_call(
        _outproj_body,
        grid=(t // tm,),
        in_specs=[pl.BlockSpec((tm, ka), row), pl.BlockSpec((tm, ka), row),
                  pl.BlockSpec((ka, d), lambda i: (0, 0)), pl.BlockSpec((ka, d), lambda i: (1, 0)),
                  pl.BlockSpec((tm, d), row), pl.BlockSpec((1, d), lambda i: (0, 0)),
                  pl.BlockSpec((1, d), lambda i: (0, 0))],
        out_specs=pl.BlockSpec((tm, d), row),
        out_shape=jax.ShapeDtypeStruct((t, d), F32),
        compiler_params=_params("parallel"),
        name="outproj_ln",
    )(a, b, w_out, w_out, h, g, beta)


def _second_of_four(a, b, c, d):
    hi_ab, lo_ab = jnp.maximum(a, b), jnp.minimum(a, b)
    hi_cd, lo_cd = jnp.maximum(c, d), jnp.minimum(c, d)
    return jnp.maximum(jnp.maximum(lo_ab, lo_cd), jnp.minimum(hi_ab, hi_cd))


def _router_body(h_ref, wr_ref, bias_ref, e_ref, r_ref):
    tm = h_ref.shape[0]
    h = h_ref[...]
    h_hi = h.astype(BF)
    h_lo = (h - h_hi.astype(F32)).astype(BF)
    both = jnp.dot(h_hi, wr_ref[...], preferred_element_type=F32)
    logits = (both[:, :LANES] + both[:, LANES:]) + jnp.dot(h_lo, wr_ref[:, :LANES], preferred_element_type=F32)
    lt = logits.T
    aff = [jax.nn.sigmoid(lt[SUBLANES * j:SUBLANES * (j + 1), :]) for j in range(EXPERTS_PER_GROUP)]
    sel = [aff[j] + bias_ref[SUBLANES * j:SUBLANES * (j + 1), :] for j in range(EXPERTS_PER_GROUP)]
    top1 = jnp.maximum(jnp.maximum(sel[0], sel[1]), jnp.maximum(sel[2], sel[3]))
    score = top1 + _second_of_four(*sel)
    gid = lax.broadcasted_iota(I32, (N_GROUPS, tm), 0)
    best = jnp.min(jnp.where(score == jnp.max(score, axis=0, keepdims=True), gid, N_GROUPS),
                   axis=0, keepdims=True)
    in_grp = gid == best
    pick = lambda x: jnp.sum(jnp.where(in_grp, x, 0.0), axis=0, keepdims=True)
    s4 = [pick(x) for x in sel]
    a4 = [pick(x) for x in aff]

    def argmax4(vals):
        j, v = jnp.zeros((1, tm), I32), vals[0]
        for n in range(1, EXPERTS_PER_GROUP):
            better = vals[n] > v
            j, v = jnp.where(better, n, j), jnp.where(better, vals[n], v)
        return j

    j0 = argmax4(s4)
    j1 = argmax4([jnp.where(j0 == n, -jnp.inf, s4[n]) for n in range(EXPERTS_PER_GROUP)])
    take = lambda j: sum(jnp.where(j == n, a4[n], 0.0) for n in range(EXPERTS_PER_GROUP))
    w0, w1 = take(j0), take(j1)
    total = w0 + w1
    e0 = best * EXPERTS_PER_GROUP + j0
    e1 = best * EXPERTS_PER_GROUP + j1
    rid = lax.broadcasted_iota(I32, (SUBLANES, tm), 0)
    e_ref[...] = jnp.where(rid == 0, e0, jnp.where(rid == 1, e1, 0))
    rid = lax.broadcasted_iota(I32, (LANES, tm), 0)
    rows = jnp.where(rid == 0, w0 / total, jnp.where(rid == 1, w1 / total, 0.0))
    r_ref[...] = rows.T


def _router(h, wr_perm, bias_perm, tm):
    t, d = h.shape
    return pl.pallas_call(
        _router_body,
        grid=(t // tm,),
        in_specs=[pl.BlockSpec((tm, d), lambda i: (i, 0)), pl.BlockSpec((d, 2 * LANES), lambda i: (0, 0)),
                  pl.BlockSpec((N_EXPERTS, 1), lambda i: (0, 0))],
        out_specs=[pl.BlockSpec((SUBLANES, tm), lambda i: (0, i)), pl.BlockSpec((tm, LANES), lambda i: (i, 0))],
        out_shape=[jax.ShapeDtypeStruct((SUBLANES, t), I32), jax.ShapeDtypeStruct((t, LANES), F32)],
        compiler_params=_params("parallel"),
        name="router",
    )(h, wr_perm, bias_perm)


def _dispatch_plan(e01, tm):
    t = e01.shape[1]
    flat = e01.reshape(-1)
    onehot = (flat[:, None] == jnp.arange(N_EXPERTS, dtype=I32)[None, :]).astype(I32)
    cum = jnp.cumsum(onehot, axis=0)
    rank = jnp.sum(onehot * cum, axis=1) - 1
    counts = cum[-1]
    padded = ((counts + tm - 1) // tm) * tm
    starts = jnp.concatenate([jnp.zeros((1,), I32), jnp.cumsum(padded).astype(I32)])
    pos = jnp.sum(onehot * starts[None, :N_EXPERTS], axis=1) + rank
    return starts, starts[:N_EXPERTS] + counts, pos[:t], pos[t:]


def _dispatch_body(p0_ref, p1_ref, gs_ref, ge_ref, h_ref, xs_hbm, xbuf, zbuf, sem, zsem, *, tm, n_steps):
    i = pl.program_id(0)
    slot = i % 2
    n_tail = (xs_hbm.shape[0] - gs_ref[N_EXPERTS]) // tm

    def zero_copy(row0, size):
        return pltpu.make_async_copy(zbuf.at[pl.ds(0, size), :], xs_hbm.at[pl.ds(row0, size), :], zsem.at[0])

    def for_each_fill(fn):
        def group_pad(e, carry):
            end = gs_ref[e + 1]
            pad = end - ge_ref[e]
            covered = 0
            size = tm // 2
            while size >= SUBLANES:
                take = pad & size

                @pl.when(take != 0)
                def _():
                    fn(zero_copy(pl.multiple_of(end - covered - size, size), size))

                covered = covered + take
                size //= 2
            for k in range(SUBLANES - 1):
                @pl.when(k < (pad & (SUBLANES - 1)))
                def _():
                    fn(zero_copy(ge_ref[e] + k, 1))
            return carry

        def tail_tile(k, carry):
            fn(zero_copy(pl.multiple_of(gs_ref[N_EXPERTS] + k * tm, tm), tm))
            return carry

        lax.fori_loop(0, N_EXPERTS, group_pad, 0)
        lax.fori_loop(0, n_tail, tail_tile, 0)

    def wait_rows(sl):
        for _ in range(2):
            pltpu.make_async_copy(xbuf.at[sl], xs_hbm.at[pl.ds(0, tm), :], sem.at[sl]).wait()

    @pl.when(i == 0)
    def _():
        zbuf[...] = jnp.zeros(zbuf.shape, F32)
        for_each_fill(lambda cp: cp.start())

    @pl.when(i >= 2)
    def _():
        wait_rows(slot)

    xbuf[slot] = h_ref[...]
    base = i * tm
    for r in range(tm):
        src = xbuf.at[slot, pl.ds(r, 1), :]
        pltpu.make_async_copy(src, xs_hbm.at[pl.ds(p0_ref[base + r], 1), :], sem.at[slot]).start(priority=0)
        pltpu.make_async_copy(src, xs_hbm.at[pl.ds(p1_ref[base + r], 1), :], sem.at[slot]).start(priority=1)

    @pl.when(i == n_steps - 1)
    def _():
        wait_rows(slot)
        if n_steps > 1:
            wait_rows(1 - slot)
        for_each_fill(lambda cp: cp.wait())


def _dispatch(h, starts, ends, pos0, pos1, tm):
    t, d = h.shape
    n_steps = t // tm
    n_rows = 2 * t + N_EXPERTS * tm
    body = functools.partial(_dispatch_body, tm=tm, n_steps=n_steps)
    grid_spec = pltpu.PrefetchScalarGridSpec(
        num_scalar_prefetch=4,
        grid=(n_steps,),
        in_specs=[pl.BlockSpec((tm, d), lambda i, *_: (i, 0))],
        out_specs=pl.BlockSpec(memory_space=pl.ANY),
        scratch_shapes=[pltpu.VMEM((2, tm, d), F32), pltpu.VMEM((tm, d), F32),
                        pltpu.SemaphoreType.DMA((2,)), pltpu.SemaphoreType.DMA((1,))],
    )
    return pl.pallas_call(
        body,
        grid_spec=grid_spec,
        out_shape=jax.ShapeDtypeStruct((n_rows, d), F32),
        compiler_params=_params("arbitrary"),
        name="dispatch",
    )(pos0, pos1, starts, ends, h)


def _experts_body(gs_ref, xs_hbm, wgu_ref, wd_ref, ys_hbm, xbuf, ybuf, wgu_bf, wd_bf, xsem, osem, *, tm):
    e = pl.program_id(0)
    first = gs_ref[e] // tm
    n_tiles = gs_ref[e + 1] // tm - first
    total = gs_ref[N_EXPERTS] // tm

    def in_copy(g, slot):
        row0 = pl.multiple_of(g * tm, tm)
        return pltpu.make_async_copy(xs_hbm.at[pl.ds(row0, tm), :], xbuf.at[slot], xsem.at[slot])

    def out_copy(g, slot):
        row0 = pl.multiple_of(g * tm, tm)
        return pltpu.make_async_copy(ybuf.at[slot], ys_hbm.at[pl.ds(row0, tm), :], osem.at[slot])

    @pl.when(e == 0)
    def _():
        in_copy(0, 0).start()
        ybuf[...] = jnp.zeros(ybuf.shape, F32)
        out_copy(0, 0).start()
        out_copy(1, 1).start()

    @pl.when(n_tiles > 0)
    def _():
        wgu_bf[...] = wgu_ref[0, 0].astype(BF)
        wd_bf[...] = wd_ref[0, 0].astype(BF)

        def tile(k, carry):
            g = first + k
            slot = g % 2
            in_copy(g, slot).wait()
            out_copy(g, slot).wait()
            in_copy(jnp.minimum(g + 1, total - 1), 1 - slot).start(priority=1)
            x = xbuf[slot].astype(BF)
            gu = jnp.dot(x, wgu_bf[...], preferred_element_type=F32)
            gate, up = gu[:, :EXPERT_FF], gu[:, EXPERT_FF:]
            hidden = (gate * jax.nn.sigmoid(gate) * up).astype(BF)
            ybuf[slot] = jnp.dot(hidden, wd_bf[...], preferred_element_type=F32)
            out_copy(g, slot).start(priority=1)
            return carry

        lax.fori_loop(0, n_tiles, tile, 0)

    @pl.when(e == N_EXPERTS - 1)
    def _():
        in_copy(0, total % 2).wait()
        out_copy(0, 0).wait()
        out_copy(1, 1).wait()
        end = gs_ref[N_EXPERTS]
        n_tail = (ys_hbm.shape[0] - end) // tm
        ybuf[0] = jnp.zeros(ybuf.shape[1:], F32)

        def tail_copy(k):
            row0 = pl.multiple_of(end + k * tm, tm)
            return pltpu.make_async_copy(ybuf.at[0], ys_hbm.at[pl.ds(row0, tm), :], osem.at[0])

        def start_one(k, carry):
            tail_copy(k).start()
            return carry

        def wait_one(k, carry):
            tail_copy(k).wait()
            return carry

        lax.fori_loop(0, n_tail, start_one, 0)
        lax.fori_loop(0, n_tail, wait_one, 0)


def _experts(xs, starts, w_gate_up, w_down, layer, tm):
    n_rows, d = xs.shape
    body = functools.partial(_experts_body, tm=tm)
    grid_spec = pltpu.PrefetchScalarGridSpec(
        num_scalar_prefetch=1,
        grid=(N_EXPERTS,),
        in_specs=[pl.BlockSpec(memory_space=pl.ANY),
                  pl.BlockSpec((1, 1, d, 2 * EXPERT_FF), lambda e, gs: (layer, e, 0, 0)),
                  pl.BlockSpec((1, 1, EXPERT_FF, d), lambda e, gs: (layer, e, 0, 0))],
        out_specs=pl.BlockSpec(memory_space=pl.ANY),
        scratch_shapes=[pltpu.VMEM((2, tm, d), F32), pltpu.VMEM((2, tm, d), F32),
                        pltpu.VMEM((d, 2 * EXPERT_FF), BF), pltpu.VMEM((EXPERT_FF, d), BF),
                        pltpu.SemaphoreType.DMA((2,)), pltpu.SemaphoreType.DMA((2,))],
    )
    return pl.pallas_call(
        body,
        grid_spec=grid_spec,
        out_shape=jax.ShapeDtypeStruct((n_rows, d), F32),
        compiler_params=_params("arbitrary"),
        name="experts",
    )(starts, xs, w_gate_up, w_down)


def _combine_body(p0_ref, p1_ref, ys_hbm, h_ref, r_ref, g_ref, beta_ref, o_ref, buf0, buf1, sem, *, tc, n_steps):
    i = pl.program_id(0)
    slot = i % 2

    def start_row(base, r, sl):
        pltpu.make_async_copy(ys_hbm.at[pl.ds(p0_ref[base + r], 1), :], buf0.at[sl, pl.ds(r, 1), :],
                              sem.at[sl]).start(priority=0)
        pltpu.make_async_copy(ys_hbm.at[pl.ds(p1_ref[base + r], 1), :], buf1.at[sl, pl.ds(r, 1), :],
                              sem.at[sl]).start(priority=1)

    def wait_rows(sl):
        pltpu.make_async_copy(ys_hbm.at[pl.ds(0, tc), :], buf0.at[sl], sem.at[sl]).wait()
        pltpu.make_async_copy(ys_hbm.at[pl.ds(0, tc), :], buf1.at[sl], sem.at[sl]).wait()

    @pl.when(i == 0)
    def _():
        def one(r, carry):
            start_row(0, r, 0)
            return carry

        lax.fori_loop(0, tc, one, 0)

    wait_rows(slot)
    nxt = jnp.minimum(i + 1, n_steps - 1) * tc
    for r in range(tc):
        start_row(nxt, r, 1 - slot)
    gates = r_ref[...]
    y = gates[:, 0:1] * buf0[slot] + gates[:, 1:2] * buf1[slot]
    o_ref[...] = _layer_norm(ALPHA * h_ref[...] + y, g_ref[...], beta_ref[...])

    @pl.when(i == n_steps - 1)
    def _():
        wait_rows(1 - slot)


def _combine_ln(ys, pos0, pos1, h, gates, g, beta, tc):
    t, d = h.shape
    n_steps = t // tc
    body = functools.partial(_combine_body, tc=tc, n_steps=n_steps)
    row = lambda i, p0, p1: (i, 0)
    const = lambda i, p0, p1: (0, 0)
    grid_spec = pltpu.PrefetchScalarGridSpec(
        num_scalar_prefetch=2,
        grid=(n_steps,),
        in_specs=[pl.BlockSpec(memory_space=pl.ANY), pl.BlockSpec((tc, d), row), pl.BlockSpec((tc, LANES), row),
                  pl.BlockSpec((1, d), const), pl.BlockSpec((1, d), const)],
        out_specs=pl.BlockSpec((tc, d), row),
        scratch_shapes=[pltpu.VMEM((2, tc, d), F32), pltpu.VMEM((2, tc, d), F32), pltpu.SemaphoreType.DMA((2,))],
    )
    return pl.pallas_call(
        body,
        grid_spec=grid_spec,
        out_shape=jax.ShapeDtypeStruct((t, d), F32),
        compiler_params=_params("arbitrary"),
        name="combine_ln",
    )(pos0, pos1, ys, h, gates, g, beta)


def _moe_ln(h, wr_perm, bias_perm, w_gate_up, w_down, layer, g, beta, tm_router=512, tm_expert=256, tc=256):
    e01, gates = _router(h, wr_perm, bias_perm, tm_router)
    starts, ends, pos0, pos1 = _dispatch_plan(e01[:MOE_TOPK], tm_expert)
    xs = _dispatch(h, starts, ends, pos0, pos1, tm_expert)
    ys = _experts(xs, starts, w_gate_up, w_down, layer, tm_expert)
    return _combine_ln(ys, pos0, pos1, h, gates, g, beta, tc)


def _swap_halves(w):
    half = w.shape[-1] // 2
    return jnp.concatenate([w[..., half:], w[..., :half]], axis=-1)


def _even_weights(w_in, w_q_up, w_kv_up, forget_bias):
    d = w_in.shape[0]
    o_kv = MLA_Q_LORA
    o_kr = o_kv + MLA_KV_LORA
    o_fq = o_kr + MLA_ROPE
    hd = FOX_HEADS * FOX_HEAD_DIM
    o_fl = o_fq + 3 * hd
    k_r = w_in[:, o_kr:o_fq]
    f_l = jnp.pad(w_in[:, o_fl:], ((0, 0), (0, LANES - FOX_HEADS)))
    w_small = jnp.concatenate([w_in[:, :o_kr], k_r, _swap_halves(k_r), f_l], axis=1).astype(BF)
    w_fox = w_in[:, o_fq:o_fl].astype(BF)
    cs_fox = jnp.concatenate([jnp.full((1, hd), FOX_SCALE, F32), jnp.ones((1, 2 * hd), F32)], axis=1)
    wq = w_q_up.reshape(MLA_Q_LORA, MLA_HEADS, MLA_NOPE + MLA_ROPE)
    wqn = wq[:, :, :MLA_NOPE].reshape(MLA_Q_LORA, -1).astype(BF)
    pe = wq[:, :, MLA_NOPE:]
    wqp = jnp.concatenate([pe, _swap_halves(pe)], axis=-1).reshape(MLA_Q_LORA, -1).astype(BF)
    wkv = w_kv_up.reshape(MLA_KV_LORA, MLA_HEADS, MLA_NOPE + MLA_V)
    wk = wkv[:, :, :MLA_NOPE].reshape(MLA_KV_LORA, -1).astype(BF)
    wv = wkv[:, :, MLA_NOPE:].reshape(MLA_KV_LORA, -1).astype(BF)
    fb = jnp.pad(forget_bias.astype(F32), (0, LANES - FOX_HEADS)).reshape(1, LANES)
    return w_small, w_fox, cs_fox, wqn, wqp, wk, wv, fb


def _rope_table(seq):
    half = MLA_ROPE // 2
    inv_freq = ROPE_THETA ** (-jnp.arange(half, dtype=F32) / half)
    ang = jnp.arange(seq).astype(F32)[:, None] * inv_freq[None, :]
    cos, sin = jnp.cos(ang), jnp.sin(ang)
    return jnp.concatenate([cos, cos, -sin, sin], axis=1)


def _router_weights(w_router, router_bias):
    r = np.arange(N_EXPERTS)
    perm = (r % N_GROUPS) * EXPERTS_PER_GROUP + r // N_GROUPS
    wr = jnp.pad(w_router[:, perm], ((0, 0), (0, LANES - N_EXPERTS)))
    w_hi = wr.astype(BF)
    w_lo = (wr - w_hi.astype(F32)).astype(BF)
    return jnp.concatenate([w_hi, w_lo], axis=1), router_bias.astype(F32)[perm].reshape(N_EXPERTS, 1)


def _even_layer(h, batch, seq, w_in, q_norm, w_q_up, kv_norm, w_kv_up, forget_bias, w_out, g, beta):
    w_small, w_fox, cs_fox, wqn, wqp, wk, wv, fb = _even_weights(w_in, w_q_up, w_kv_up, forget_bias)
    small = _mm(h, w_small, jnp.ones((1, w_small.shape[1]), F32), F32, 512, 1024)
    fox_qkv = _mm(h, w_fox, cs_fox, BF, 512, 1536)
    qn, qp, kn, v, kp = _mla_prep(small, _rope_table(seq), q_norm.reshape(1, -1), kv_norm.reshape(1, -1),
                                  wqn, wqp, wk, wv, seq, 512)
    q_bias, k_bias, edge = _fox_cum(small, fb, batch, seq, 512)
    a = _mla_attention(qn, qp, kn, v, kp, batch, seq, 512, 4)
    bo = _fox_attention(fox_qkv, q_bias, k_bias, edge, batch, seq, 512, 4)
    return _outproj_ln(a, bo, w_out.astype(BF), h, g, beta, 512)


def _odd_layer(h, batch, seq, w_in, sinks, w_out, g, beta):
    n_sq = SWA_Q_HEADS * SWA_HEAD_DIM
    n_skv = SWA_KV_HEADS * SWA_HEAD_DIM
    n_m = MOBA_HEADS * MOBA_HEAD_DIM
    o_mq = n_sq + 2 * n_skv
    w_big = jnp.concatenate([w_in[:, :n_sq], w_in[:, o_mq:], w_in[:, n_sq:o_mq]], axis=1).astype(BF)
    cs = jnp.concatenate([jnp.full((1, n_sq), SWA_SCALE, F32), jnp.full((1, n_m), MOBA_SCALE, F32),
                          jnp.ones((1, 2 * n_m + 2 * n_skv), F32)], axis=1)
    big = _mm(h, w_big, cs, BF, 512, 2176)
    blocks = lambda cols: cols // LANES
    k_mean = _moba_kmean(big, batch, seq, 2)
    slopes = jnp.asarray(LOG2E * 2.0 ** (-8.0 * np.arange(1, MOBA_HEADS + 1) / MOBA_HEADS), dtype=F32)
    c = _swa_attention(big, sinks.astype(F32), batch, seq, 0, blocks(n_sq + 3 * n_m), blocks(n_sq + 3 * n_m) + 1)
    dd = _moba_attention(big, k_mean, slopes, batch, seq, blocks(n_sq), blocks(n_sq + n_m), blocks(n_sq + 2 * n_m),
                         8)
    return _outproj_ln(c, dd, w_out.astype(BF), h, g, beta, 512)


def kernel(x, w_router, router_bias, even_w_in, even_q_norm, even_w_q_up, even_kv_norm, even_w_kv_up,
           even_forget_bias, even_w_out, odd_w_in, odd_sinks, odd_w_out, ln_mix_g, ln_mix_b, ln_ffn_g, ln_ffn_b,
           w_gate_up, w_down):
    batch, seq, d = x.shape
    h = x.reshape(batch * seq, d)
    wr_perm, bias_perm = _router_weights(w_router, router_bias)
    row = lambda p, layer: p[layer].reshape(1, d)
    for layer in range(DEPTH):
        i = layer // 2
        if layer % 2 == 0:
            h = _even_layer(h, batch, seq, even_w_in[i], even_q_norm[i], even_w_q_up[i], even_kv_norm[i],
                            even_w_kv_up[i], even_forget_bias[i], even_w_out[i], row(ln_mix_g, layer),
                            row(ln_mix_b, layer))
        else:
            h = _odd_layer(h, batch, seq, odd_w_in[i], odd_sinks[i], odd_w_out[i], row(ln_mix_g, layer),
                           row(ln_mix_b, layer))
        h = _moe_ln(h, wr_perm, bias_perm, w_gate_up, w_down, layer, row(ln_ffn_g, layer), row(ln_ffn_b, layer))
    return h.reshape(batch, seq, d)
```

```python
import functools

import numpy as np
import jax
import jax.numpy as jnp
from jax import lax
from jax.experimental import pallas as pl
from jax.experimental.pallas import tpu as pltpu

BF = jnp.bfloat16
F32 = jnp.float32
I32 = jnp.int32

LANES = 128
SUBLANES = 8
VMEM_LIMIT = 56 * 1024 * 1024

D_MODEL = 2048
DEPTH = 2
LN_EPS = 1e-5
RMS_EPS = 1e-6
ALPHA = (2 * DEPTH) ** 0.25
MLA_HEADS, MLA_Q_LORA, MLA_KV_LORA, MLA_NOPE, MLA_ROPE, MLA_V = 8, 512, 256, 128, 64, 128
ROPE_THETA = 10000.0
FOX_HEADS, FOX_HEAD_DIM = 8, 128
SWA_Q_HEADS, SWA_KV_HEADS, SWA_HEAD_DIM, SWA_WINDOW = 16, 2, 64, 128
MOBA_HEADS, MOBA_HEAD_DIM, MOBA_BLOCK, MOBA_TOPK = 8, 128, 256, 3
N_EXPERTS, N_GROUPS, MOE_TOPK, EXPERT_FF = 32, 8, 2, 512
EXPERTS_PER_GROUP = N_EXPERTS // N_GROUPS

LOG2E = 1.4426950408889634
MLA_SCALE = (MLA_NOPE + MLA_ROPE) ** -0.5 * LOG2E
FOX_SCALE = FOX_HEAD_DIM ** -0.5 * LOG2E
MOBA_SCALE = MOBA_HEAD_DIM ** -0.5 * LOG2E
SWA_SCALE = SWA_HEAD_DIM ** -0.5 * LOG2E
NEG = -1e30

NT_DIMS = (((1,), (1,)), ((), ()))


def _params(*sem):
    return pltpu.CompilerParams(dimension_semantics=sem, vmem_limit_bytes=VMEM_LIMIT)


def _mm_body(x_ref, w_ref, cs_ref, o_ref):
    acc = jnp.dot(x_ref[...].astype(BF), w_ref[...], preferred_element_type=F32)
    o_ref[...] = (acc * cs_ref[...]).astype(o_ref.dtype)


def _mm(x, w, col_scale, out_dtype, tm, tn):
    m, k = x.shape
    n = w.shape[1]
    return pl.pallas_call(
        _mm_body,
        grid=(n // tn, m // tm),
        in_specs=[
            pl.BlockSpec((tm, k), lambda j, i: (i, 0)),
            pl.BlockSpec((k, tn), lambda j, i: (0, j)),
            pl.BlockSpec((1, tn), lambda j, i: (0, j)),
        ],
        out_specs=pl.BlockSpec((tm, tn), lambda j, i: (i, j)),
        out_shape=jax.ShapeDtypeStruct((m, n), out_dtype),
        compiler_params=_params("parallel", "parallel"),
        name="proj",
    )(x, w, col_scale)


def _rms(x, g):
    return x * lax.rsqrt(jnp.mean(x * x, axis=-1, keepdims=True) + RMS_EPS) * g


def _rope_pair(slab, table):
    r = slab * table
    return r + pltpu.roll(r, MLA_ROPE, axis=1)


def _mla_prep_body(x_ref, ws_ref, rope_ref, qg_ref, kvg_ref, wqn_ref, wqp_ref, wk_ref, wv_ref,
                   qn_ref, qp_ref, kn_ref, v_ref, kp_ref, fl_ref):
    small = jnp.dot(x_ref[...].astype(BF), ws_ref[...], preferred_element_type=F32)
    o_kv, o_kr, o_fl = MLA_Q_LORA, MLA_Q_LORA + MLA_KV_LORA, MLA_Q_LORA + MLA_KV_LORA + LANES
    fl_ref[...] = small[:, o_fl:]
    table = rope_ref[...]
    cqn = _rms(small[:, :o_kv], qg_ref[...]).astype(BF)
    qn_ref[...] = (jnp.dot(cqn, wqn_ref[...], preferred_element_type=F32) * MLA_SCALE).astype(BF)
    qp = jnp.dot(cqn, wqp_ref[...], preferred_element_type=F32)
    for h in range(MLA_HEADS):
        sl = slice(h * LANES, (h + 1) * LANES)
        qp_ref[:, sl] = (_rope_pair(qp[:, sl], table) * MLA_SCALE).astype(BF)
    ckvn = _rms(small[:, o_kv:o_kr], kvg_ref[...]).astype(BF)
    kn_ref[...] = jnp.dot(ckvn, wk_ref[...], preferred_element_type=F32).astype(BF)
    v_ref[...] = jnp.dot(ckvn, wv_ref[...], preferred_element_type=F32).astype(BF)
    kr = _rope_pair(small[:, o_kr:o_fl], table)
    lane = lax.broadcasted_iota(I32, kr.shape, 1)
    kp_ref[...] = jnp.where(lane < MLA_ROPE, kr, 0.0).astype(BF)


def _mla_prep(x, w_small, rope_table, q_norm, kv_norm, wqn, wqp, wk, wv, seq, tm):
    t, d = x.shape
    n_s = seq // tm
    hd = MLA_HEADS * LANES
    row = lambda i: (i, 0)
    const = lambda i: (0, 0)
    out = lambda w: pl.BlockSpec((tm, w), row)
    return pl.pallas_call(
        _mla_prep_body,
        grid=(t // tm,),
        in_specs=[
            pl.BlockSpec((tm, d), row),
            pl.BlockSpec(w_small.shape, const),
            pl.BlockSpec((tm, LANES), lambda i: (i % n_s, 0)),
            pl.BlockSpec((1, MLA_Q_LORA), const),
            pl.BlockSpec((1, MLA_KV_LORA), const),
            pl.BlockSpec((MLA_Q_LORA, hd), const),
            pl.BlockSpec((MLA_Q_LORA, hd), const),
            pl.BlockSpec((MLA_KV_LORA, hd), const),
            pl.BlockSpec((MLA_KV_LORA, hd), const),
        ],
        out_specs=[out(hd), out(hd), out(hd), out(hd), out(LANES), out(LANES)],
        out_shape=[jax.ShapeDtypeStruct((t, hd), BF)] * 4 + [jax.ShapeDtypeStruct((t, LANES), BF),
                                                             jax.ShapeDtypeStruct((t, LANES), F32)],
        compiler_params=_params("parallel"),
        name="mla_prep",
    )(x, w_small, rope_table, q_norm, kv_norm, wqn, wqp, wk, wv)


N_SPLIT = 3


def _split_piece(x, which):
    hi = x.astype(BF).astype(F32)
    rest = x - hi
    mid = rest.astype(BF).astype(F32)
    return jnp.where(which == 0, hi, jnp.where(which == 1, mid, rest - mid))


def _fox_cum_body(fl_ref, fb_ref, spread_ref, qe_ref, ke_ref, edge_ref, carry_ref):
    @pl.when(pl.program_id(1) == 0)
    def _():
        carry_ref[...] = jnp.zeros_like(carry_ref)

    z = fl_ref[...] + fb_ref[...]
    log_f = jnp.minimum(z, 0.0) - jnp.log1p(jnp.exp(-jnp.abs(z)))
    tm = z.shape[0]
    r = lax.broadcasted_iota(I32, (tm, tm), 0)
    c = lax.broadcasted_iota(I32, (tm, tm), 1)
    tri = jnp.where(r >= c, 1.0, 0.0).astype(BF)
    parts = jnp.dot(tri, jnp.concatenate([_split_piece(log_f, n).astype(BF) for n in range(N_SPLIT)], axis=1),
                    preferred_element_type=F32)
    cum = (parts[:, :LANES] + parts[:, LANES:2 * LANES]) + parts[:, 2 * LANES:] + carry_ref[...]
    carry_ref[...] = cum[tm - 1:tm, :]
    cum = cum * LOG2E
    width = qe_ref.shape[1]
    pieces = jnp.concatenate([_split_piece(cum, n).astype(BF) for n in range(N_SPLIT)], axis=1)
    placed = jnp.dot(pieces, spread_ref[...], preferred_element_type=F32)
    li = lax.broadcasted_iota(I32, (tm, width), 1) % LANES
    qe_ref[...] = jnp.where((li >= N_SPLIT) & (li < 2 * N_SPLIT), 1.0, placed[:, :width]).astype(BF)
    ke_ref[...] = jnp.where(li < N_SPLIT, 1.0, placed[:, width:]).astype(BF)
    edge_ref[0, 0:1, :] = cum[0:1, :]
    edge_ref[0, 1:2, :] = cum[tm - 1:tm, :]


def _fox_spread():
    width = FOX_HEADS * LANES
    m = np.zeros((N_SPLIT * LANES, 2 * width), np.float32)
    for j in range(N_SPLIT):
        for h in range(FOX_HEADS):
            m[j * LANES + h, h * LANES + j] = 1.0
            m[j * LANES + h, width + h * LANES + N_SPLIT + j] = -1.0
    return jnp.asarray(m, dtype=BF)


def _fox_cum(forget_logits, forget_bias_row, batch, seq, tm):
    t = forget_logits.shape[0]
    n_s = seq // tm
    width = FOX_HEADS * LANES
    q_bias, k_bias, edge = pl.pallas_call(
        _fox_cum_body,
        grid=(batch, n_s),
        in_specs=[
            pl.BlockSpec((tm, LANES), lambda b, i: (b * n_s + i, 0)),
            pl.BlockSpec((1, LANES), lambda b, i: (0, 0)),
            pl.BlockSpec((N_SPLIT * LANES, 2 * width), lambda b, i: (0, 0)),
        ],
        out_specs=[
            pl.BlockSpec((tm, width), lambda b, i: (b * n_s + i, 0)),
            pl.BlockSpec((tm, width), lambda b, i: (b * n_s + i, 0)),
            pl.BlockSpec((1, 2, LANES), lambda b, i: (b * n_s + i, 0, 0)),
        ],
        out_shape=[jax.ShapeDtypeStruct((t, width), BF), jax.ShapeDtypeStruct((t, width), BF),
                   jax.ShapeDtypeStruct((batch * n_s, 2, LANES), F32)],
        scratch_shapes=[pltpu.VMEM((1, LANES), F32)],
        compiler_params=_params("parallel", "arbitrary"),
        name="fox_cum",
    )(forget_logits, forget_bias_row, _fox_spread())
    return q_bias, k_bias, edge[:, :, :FOX_HEADS].reshape(-1)


def _softmax_update(s_all, v_all, m_ref, l_ref, acc_ref):
    heads = range(len(s_all))
    m_prev = [m_ref[g] for g in heads]
    m_new = [jnp.maximum(m_prev[g], jnp.max(s_all[g], axis=1, keepdims=True)) for g in heads]
    reps = s_all[0].shape[1] // LANES
    p = [jnp.exp2(s_all[g] - jnp.concatenate([m_new[g]] * reps, axis=1)) for g in heads]
    pv = [jnp.dot(p[g].astype(BF), v_all[g], preferred_element_type=F32) for g in heads]
    for g in heads:
        alpha = jnp.exp2(m_prev[g] - m_new[g])
        l_ref[g] = alpha * l_ref[g] + jnp.sum(p[g], axis=1, keepdims=True)
        acc_ref[g] = alpha * acc_ref[g] + pv[g]
        m_ref[g] = m_new[g]


def _attn_body(*refs, mode, t, group):
    hg = pl.program_id(1)
    i = pl.program_id(2)
    heads = range(group)
    sl = lambda g: slice(g * LANES, (g + 1) * LANES)
    if mode == "moba":
        q_ref, k_ref, kx_ref, v_ref, km_ref, slope_ref, o_ref, m_ref, l_ref, acc_ref, qx_ref = refs
        for g in heads:
            qx_ref[g] = _moba_query_lanes(q_ref[:, sl(g)], km_ref[:, sl(g)], slope_ref[hg * group + g], i, t)
        q_extra = lambda g: qx_ref[g]
        k_extra = lambda g, ks: kx_ref[pl.ds(ks, t), :]
    elif mode == "mla":
        q_ref, qx_ref, k_ref, kx_ref, v_ref, o_ref, m_ref, l_ref, acc_ref = refs
        q_extra = lambda g: qx_ref[:, sl(g)]
        k_extra = lambda g, ks: kx_ref[pl.ds(ks, t), :]
    else:
        q_ref, qx_ref, k_ref, kx_ref, v_ref, edge_ref, o_ref, m_ref, l_ref, acc_ref, knorm_ref = refs
        q_extra = lambda g: qx_ref[:, sl(g)]
        k_extra = lambda g, ks: kx_ref[pl.ds(ks, t), sl(g)]
        first_tile = _fox_first_tile(q_ref, k_ref, edge_ref, knorm_ref, i, hg, t, group)

    def scores(g, ks):
        q = jnp.concatenate([q_ref[:, sl(g)], q_extra(g)], axis=1)
        k = jnp.concatenate([k_ref[pl.ds(ks, t), sl(g)], k_extra(g, ks)], axis=1)
        return lax.dot_general(q, k, NT_DIMS, preferred_element_type=F32)

    m_ref[...] = jnp.full(m_ref.shape, NEG, F32)
    l_ref[...] = jnp.zeros(l_ref.shape, F32)
    acc_ref[...] = jnp.zeros(acc_ref.shape, F32)

    ks = pl.multiple_of(i * t, t)
    r = lax.broadcasted_iota(I32, (t, t), 0)
    c = lax.broadcasted_iota(I32, (t, t), 1)
    values = lambda ks: [v_ref[pl.ds(ks, t), sl(g)] for g in heads]
    _softmax_update([jnp.where(c <= r, scores(g, ks), NEG) for g in heads], values(ks), m_ref, l_ref, acc_ref)

    def past_tile(j, carry):
        ks = pl.multiple_of(j * t, t)
        _softmax_update([scores(g, ks) for g in heads], values(ks), m_ref, l_ref, acc_ref)
        return carry

    lax.fori_loop(first_tile if mode == "fox" else 0, i, past_tile, 0)
    for g in heads:
        o_ref[:, sl(g)] = (acc_ref[g] / l_ref[g]).astype(o_ref.dtype)


FOX_SKIP_GAP = 160.0


def _fox_first_tile(q_ref, k_ref, edge_ref, knorm_ref, i, hg, t, group):
    b = pl.program_id(0)
    n_q = pl.num_programs(2)
    sl = lambda g: slice(g * LANES, (g + 1) * LANES)

    def max_row_norm(x):
        x = x.astype(F32)
        return jnp.sqrt(jnp.max(jnp.sum(x * x, axis=1, keepdims=True)))

    @pl.when(i == 0)
    def _():
        for g in range(group):
            knorm_ref[g] = max_row_norm(k_ref[:, sl(g)])

    def edge(tile, last, g):
        return edge_ref[((b * n_q + tile) * 2 + last) * FOX_HEADS + hg * group + g]

    limit = [-(2.0 * 1.001 * max_row_norm(q_ref[:, sl(g)]) * knorm_ref[g] + FOX_SKIP_GAP) for g in range(group)]
    first = [edge(i, 0, g) for g in range(group)]

    def scan(j, lo):
        needed = first[0] - edge(j, 1, 0) >= limit[0]
        for g in range(1, group):
            needed = needed | (first[g] - edge(j, 1, g) >= limit[g])
        return jnp.where(needed, jnp.minimum(lo, j), lo)

    return lax.fori_loop(0, i, scan, i)


def _attention(mode, batch, seq, heads, t, group, operands, in_specs, extra_scratch=()):
    n_q = seq // t
    body = functools.partial(_attn_body, mode=mode, t=t, group=group)
    return pl.pallas_call(
        body,
        grid=(batch, heads // group, n_q),
        in_specs=in_specs,
        out_specs=pl.BlockSpec((t, group * LANES), lambda b, h, i: (b * n_q + i, h)),
        out_shape=jax.ShapeDtypeStruct((batch * seq, heads * LANES), BF),
        scratch_shapes=[pltpu.VMEM((group, t, LANES), F32), pltpu.VMEM((group, t, LANES), F32),
                        pltpu.VMEM((group, t, LANES), F32), *extra_scratch],
        compiler_params=_params("parallel", "parallel", "arbitrary"),
        name="attn_" + mode,
    )(*operands)


def _q_spec(t, n_q, group, col0=0):
    return pl.BlockSpec((t, group * LANES), lambda b, h, i: (b * n_q + i, col0 + h))


def _kv_spec(seq, group, col0=0):
    return pl.BlockSpec((seq, group * LANES), lambda b, h, i: (b, col0 + h))


def _mla_attention(qn, qp, kn, v, kp, batch, seq, t, group):
    n_q = seq // t
    specs = [_q_spec(t, n_q, group), _q_spec(t, n_q, group), _kv_spec(seq, group),
             pl.BlockSpec((seq, LANES), lambda b, h, i: (b, 0)), _kv_spec(seq, group)]
    return _attention("mla", batch, seq, MLA_HEADS, t, group, (qn, qp, kn, kp, v), specs)


def _fox_attention(qkv, q_bias, k_bias, edge, batch, seq, t, group):
    n_q = seq // t
    n_hg = FOX_HEADS // group
    specs = [_q_spec(t, n_q, group, 0), _q_spec(t, n_q, group, 0), _kv_spec(seq, group, n_hg),
             _kv_spec(seq, group, 0), _kv_spec(seq, group, 2 * n_hg), pl.BlockSpec(memory_space=pltpu.SMEM)]
    return _attention("fox", batch, seq, FOX_HEADS, t, group, (qkv, q_bias, qkv, k_bias, qkv, edge), specs,
                      (pltpu.SMEM((group,), F32),))


def _moba_key_lanes(seq):
    n_kb = seq // MOBA_BLOCK
    kpos = np.arange(seq)
    lanes = np.zeros((seq, LANES), np.float32)
    lanes[kpos, kpos // MOBA_BLOCK] = 1.0
    lanes[:, n_kb:n_kb + N_SPLIT] = 1.0
    lanes[:, n_kb + N_SPLIT:n_kb + 2 * N_SPLIT] = (MOBA_BLOCK * (kpos // MOBA_BLOCK))[:, None]
    lanes[:, n_kb + 2 * N_SPLIT:n_kb + 3 * N_SPLIT] = (kpos % MOBA_BLOCK)[:, None]
    return jnp.asarray(lanes, dtype=BF)


def _moba_query_lanes(q, k_mean, slope, i, t):
    n_kb = k_mean.shape[0]
    gate = lax.dot_general(k_mean.astype(BF), q, NT_DIMS, preferred_element_type=F32)
    blk = lax.broadcasted_iota(I32, (n_kb, t), 0)
    beaten = jnp.zeros((n_kb, t), F32)
    for n in range(n_kb):
        g_n = gate[n:n + 1, :]
        wins = (g_n > gate) | ((g_n == gate) & (blk > n))
        beaten = beaten + jnp.where(wins, 1.0, 0.0) * jnp.where(i > n, 1.0, 0.0)
    attended = (blk == i) | ((blk < i) & (beaten < MOBA_TOPK))
    mask = jnp.concatenate([jnp.where(attended, 0.0, NEG), jnp.zeros((LANES - n_kb, t), F32)], axis=0)
    row = lax.broadcasted_iota(I32, (LANES, t), 0) - n_kb
    qpos = (i * t + lax.broadcasted_iota(I32, (1, t), 1)).astype(F32)
    slope_row = jnp.full((1, t), slope, F32)
    which = row % N_SPLIT
    lanes = jnp.where(row < 0, mask,
                      jnp.where(row < N_SPLIT, _split_piece(-slope_row * qpos, which),
                                jnp.where(row < 3 * N_SPLIT, _split_piece(slope_row, which), 0.0)))
    return lanes.T.astype(BF)


def _moba_attention(big, k_mean, slopes, batch, seq, col_q, col_k, col_v, group):
    t = MOBA_BLOCK
    n_q = seq // t
    specs = [_q_spec(t, n_q, group, col_q // group), _kv_spec(seq, group, col_k // group),
             pl.BlockSpec((seq, LANES), lambda b, h, i: (0, 0)),
             _kv_spec(seq, group, col_v // group),
             pl.BlockSpec((n_q, group * LANES), lambda b, h, i: (b, h)),
             pl.BlockSpec(memory_space=pltpu.SMEM)]
    scratch = (pltpu.VMEM((group, t, LANES), BF),)
    return _attention("moba", batch, seq, MOBA_HEADS, t, group,
                      (big, big, _moba_key_lanes(seq), big, k_mean, slopes), specs, scratch)


def _kmean_body(k_ref, o_ref):
    n_kb = o_ref.shape[0]
    for n in range(n_kb):
        blk = k_ref[n * MOBA_BLOCK:(n + 1) * MOBA_BLOCK, :].astype(F32)
        o_ref[n:n + 1, :] = jnp.mean(blk, axis=0, keepdims=True)


def _moba_kmean(big, batch, seq, col_block):
    n_kb = seq // MOBA_BLOCK
    width = MOBA_HEADS * MOBA_HEAD_DIM
    return pl.pallas_call(
        _kmean_body,
        grid=(batch,),
        in_specs=[pl.BlockSpec((seq, width), lambda b: (b, col_block))],
        out_specs=pl.BlockSpec((n_kb, width), lambda b: (b, 0)),
        out_shape=jax.ShapeDtypeStruct((batch * n_kb, width), F32),
        compiler_params=_params("parallel"),
        name="moba_kmean",
    )(big)


def _swa_body(sink_ref, q_ref, kp_ref, kc_ref, vp_ref, vc_ref, o_ref):
    i = pl.program_id(1)
    w = SWA_WINDOW
    half = SWA_HEAD_DIM
    lane = lax.broadcasted_iota(I32, (w, LANES), 1)
    r = lax.broadcasted_iota(I32, (w, w), 0)
    c = lax.broadcasted_iota(I32, (w, w), 1)
    from_prev = c > r
    dist = (r - c + jnp.where(from_prev, w, 0)).astype(F32)
    valid = jnp.logical_not(from_prev) | (i > 0)
    heads_per_kv = SWA_Q_HEADS // SWA_KV_HEADS

    def lo_hi(ref, kvh):
        own = jnp.where((lane < half) == (kvh == 0), ref[...].astype(F32), 0.0)
        other = pltpu.roll(own, half, axis=1)
        lo, hi = (own, other) if kvh == 0 else (other, own)
        return lo.astype(BF), hi.astype(BF)

    for kvh in range(SWA_KV_HEADS):
        kp, kc, vp, vc = (lo_hi(ref, kvh) for ref in (kp_ref, kc_ref, vp_ref, vc_ref))
        heads = range(kvh * heads_per_kv, (kvh + 1) * heads_per_kv)
        q = {h: q_ref[:, (h // 2) * LANES:(h // 2 + 1) * LANES] for h in heads}
        s = {h: jnp.where(from_prev,
                          lax.dot_general(q[h], kp[h % 2], NT_DIMS, preferred_element_type=F32),
                          lax.dot_general(q[h], kc[h % 2], NT_DIMS, preferred_element_type=F32)) for h in heads}
        p, inv = {}, {}
        for h in heads:
            slope = float(LOG2E * 2.0 ** (-8.0 * (h + 1) / SWA_Q_HEADS))
            sink = sink_ref[h] * LOG2E
            logits = jnp.where(valid, s[h] - slope * dist, NEG)
            m = jnp.maximum(jnp.max(logits, axis=1, keepdims=True), sink)
            p[h] = jnp.exp2(logits - m)
            inv[h] = 1.0 / (jnp.sum(p[h], axis=1, keepdims=True) + jnp.exp2(sink - m))
        for pair in range(kvh * heads_per_kv // 2, (kvh + 1) * heads_per_kv // 2):
            out = jnp.zeros((w, LANES), F32)
            for h in (2 * pair, 2 * pair + 1):
                pn = p[h] * inv[h]
                out = out + jnp.dot(jnp.where(from_prev, pn, 0.0).astype(BF), vp[h % 2], preferred_element_type=F32)
                out = out + jnp.dot(jnp.where(from_prev, 0.0, pn).astype(BF), vc[h % 2], preferred_element_type=F32)
            o_ref[:, pair * LANES:(pair + 1) * LANES] = out.astype(o_ref.dtype)


def _swa_attention(big, sinks, batch, seq, col_q, col_k, col_v):
    w = SWA_WINDOW
    n_q = seq // w
    width = SWA_Q_HEADS * SWA_HEAD_DIM
    prev = lambda col: pl.BlockSpec((w, LANES), lambda b, i: (b * n_q + jnp.maximum(i - 1, 0), col))
    cur = lambda col: pl.BlockSpec((w, LANES), lambda b, i: (b * n_q + i, col))
    return pl.pallas_call(
        _swa_body,
        grid=(batch, n_q),
        in_specs=[pl.BlockSpec(memory_space=pltpu.SMEM),
                  pl.BlockSpec((w, width), lambda b, i: (b * n_q + i, col_q)),
                  prev(col_k), cur(col_k), prev(col_v), cur(col_v)],
        out_specs=pl.BlockSpec((w, width), lambda b, i: (b * n_q + i, 0)),
        out_shape=jax.ShapeDtypeStruct((batch * seq, width), BF),
        compiler_params=_params("parallel", "parallel"),
        name="swa",
    )(sinks, big, big, big, big, big)


def _layer_norm(z, g, b):
    mu = jnp.mean(z, axis=-1, keepdims=True)
    zc = z - mu
    var = jnp.mean(zc * zc, axis=-1, keepdims=True)
    return zc * lax.rsqrt(var + LN_EPS) * g + b


def _outproj_body(a_ref, b_ref, wa_ref, wb_ref, h_ref, g_ref, beta_ref, o_ref):
    mix = jnp.dot(a_ref[...], wa_ref[...], preferred_element_type=F32)
    mix = mix + jnp.dot(b_ref[...], wb_ref[...], preferred_element_type=F32)
    o_ref[...] = _layer_norm(ALPHA * h_ref[...] + mix, g_ref[...], beta_ref[...])


def _outproj_ln(a, b, w_out, h, g, beta, tm):
    t, d = h.shape
    ka = a.shape[1]
    row = lambda i: (i, 0)
    return pl.pallas_call(
        _outproj_body,
        grid=(t // tm,),
        in_specs=[pl.BlockSpec((tm, ka), row), pl.BlockSpec((tm, ka), row),
                  pl.BlockSpec((ka, d), lambda i: (0, 0)), pl.BlockSpec((ka, d), lambda i: (1, 0)),
                  pl.BlockSpec((tm, d), row), pl.BlockSpec((1, d), lambda i: (0, 0)),
                  pl.BlockSpec((1, d), lambda i: (0, 0))],
        out_specs=pl.BlockSpec((tm, d), row),
        out_shape=jax.ShapeDtypeStruct((t, d), F32),
        compiler_params=_params("parallel"),
        name="outproj_ln",
    )(a, b, w_out, w_out, h, g, beta)


def _second_of_four(a, b, c, d):
    hi_ab, lo_ab = jnp.maximum(a, b), jnp.minimum(a, b)
    hi_cd, lo_cd = jnp.maximum(c, d), jnp.minimum(c, d)
    return jnp.maximum(jnp.maximum(lo_ab, lo_cd), jnp.minimum(hi_ab, hi_cd))


def _router_body(h_ref, wr_ref, bias_ref, e_ref, r_ref):
    tm = h_ref.shape[0]
    h = h_ref[...]
    h_hi = h.astype(BF)
    h_lo = (h - h_hi.astype(F32)).astype(BF)
    both = jnp.dot(h_hi, wr_ref[...], preferred_element_type=F32)
    logits = (both[:, :LANES] + both[:, LANES:]) + jnp.dot(h_lo, wr_ref[:, :LANES], preferred_element_type=F32)
    lt = logits.T
    aff = [jax.nn.sigmoid(lt[SUBLANES * j:SUBLANES * (j + 1), :]) for j in range(EXPERTS_PER_GROUP)]
    sel = [aff[j] + bias_ref[SUBLANES * j:SUBLANES * (j + 1), :] for j in range(EXPERTS_PER_GROUP)]
    top1 = jnp.maximum(jnp.maximum(sel[0], sel[1]), jnp.maximum(sel[2], sel[3]))
    score = top1 + _second_of_four(*sel)
    gid = lax.broadcasted_iota(I32, (N_GROUPS, tm), 0)
    best = jnp.min(jnp.where(score == jnp.max(score, axis=0, keepdims=True), gid, N_GROUPS),
                   axis=0, keepdims=True)
    in_grp = gid == best
    pick = lambda x: jnp.sum(jnp.where(in_grp, x, 0.0), axis=0, keepdims=True)
    s4 = [pick(x) for x in sel]
    a4 = [pick(x) for x in aff]

    def argmax4(vals):
        j, v = jnp.zeros((1, tm), I32), vals[0]
        for n in range(1, EXPERTS_PER_GROUP):
            better = vals[n] > v
            j, v = jnp.where(better, n, j), jnp.where(better, vals[n], v)
        return j

    j0 = argmax4(s4)
    j1 = argmax4([jnp.where(j0 == n, -jnp.inf, s4[n]) for n in range(EXPERTS_PER_GROUP)])
    take = lambda j: sum(jnp.where(j == n, a4[n], 0.0) for n in range(EXPERTS_PER_GROUP))
    w0, w1 = take(j0), take(j1)
    total = w0 + w1
    e0 = best * EXPERTS_PER_GROUP + j0
    e1 = best * EXPERTS_PER_GROUP + j1
    rid = lax.broadcasted_iota(I32, (SUBLANES, tm), 0)
    e_ref[...] = jnp.where(rid == 0, e0, jnp.where(rid == 1, e1, 0))
    rid = lax.broadcasted_iota(I32, (LANES, tm), 0)
    rows = jnp.where(rid == 0, w0 / total, jnp.where(rid == 1, w1 / total, 0.0))
    r_ref[...] = rows.T


def _router(h, wr_perm, bias_perm, tm):
    t, d = h.shape
    return pl.pallas_call(
        _router_body,
        grid=(t // tm,),
        in_specs=[pl.BlockSpec((tm, d), lambda i: (i, 0)), pl.BlockSpec((d, 2 * LANES), lambda i: (0, 0)),
                  pl.BlockSpec((N_EXPERTS, 1), lambda i: (0, 0))],
        out_specs=[pl.BlockSpec((SUBLANES, tm), lambda i: (0, i)), pl.BlockSpec((tm, LANES), lambda i: (i, 0))],
        out_shape=[jax.ShapeDtypeStruct((SUBLANES, t), I32), jax.ShapeDtypeStruct((t, LANES), F32)],
        compiler_params=_params("parallel"),
        name="router",
    )(h, wr_perm, bias_perm)


def _dispatch_plan(e01, tm):
    t = e01.shape[1]
    flat = e01.reshape(-1)
    onehot = (flat[:, None] == jnp.arange(N_EXPERTS, dtype=I32)[None, :]).astype(I32)
    cum = jnp.cumsum(onehot, axis=0)
    rank = jnp.sum(onehot * cum, axis=1) - 1
    counts = cum[-1]
    padded = ((counts + tm - 1) // tm) * tm
    starts = jnp.concatenate([jnp.zeros((1,), I32), jnp.cumsum(padded).astype(I32)])
    pos = jnp.sum(onehot * starts[None, :N_EXPERTS], axis=1) + rank
    return starts, starts[:N_EXPERTS] + counts, pos[:t], pos[t:]


def _dispatch_body(p0_ref, p1_ref, gs_ref, ge_ref, h_ref, r_ref, xs_hbm, xbuf, zbuf, sem, zsem, *, tm, tt,
                   n_steps):
    i = pl.program_id(0)
    slot = i % 2
    n_tail = (xs_hbm.shape[0] - gs_ref[N_EXPERTS]) // tm

    def zero_copy(row0, size):
        return pltpu.make_async_copy(zbuf.at[pl.ds(0, size), :], xs_hbm.at[pl.ds(row0, size), :], zsem.at[0])

    def for_each_fill(fn):
        def group_pad(e, carry):
            end = gs_ref[e + 1]
            pad = end - ge_ref[e]
            covered = 0
            size = tm // 2
            while size >= SUBLANES:
                take = pad & size

                @pl.when(take != 0)
                def _():
                    fn(zero_copy(pl.multiple_of(end - covered - size, size), size))

                covered = covered + take
                size //= 2
            for k in range(SUBLANES - 1):
                @pl.when(k < (pad & (SUBLANES - 1)))
                def _():
                    fn(zero_copy(ge_ref[e] + k, 1))
            return carry

        def tail_tile(k, carry):
            fn(zero_copy(pl.multiple_of(gs_ref[N_EXPERTS] + k * tm, tm), tm))
            return carry

        lax.fori_loop(0, N_EXPERTS, group_pad, 0)
        lax.fori_loop(0, n_tail, tail_tile, 0)

    def wait_rows(sl):
        for _ in range(2):
            pltpu.make_async_copy(xbuf.at[sl, 0], xs_hbm.at[pl.ds(0, tt), :], sem.at[sl]).wait()

    @pl.when(i == 0)
    def _():
        zbuf[...] = jnp.zeros(zbuf.shape, F32)
        for_each_fill(lambda cp: cp.start())

    @pl.when(i >= 2)
    def _():
        wait_rows(slot)

    d = h_ref.shape[1]
    gates = r_ref[...]
    lane = lax.broadcasted_iota(I32, gates.shape, 1)
    for k in range(MOE_TOPK):
        xbuf[slot, k, :, :d] = h_ref[...]
    xbuf[slot, 0, :, d:] = jnp.where(lane == 0, gates, jnp.where(lane == 1, ALPHA, 0.0))
    xbuf[slot, 1, :, d:] = jnp.where(lane == 0, pltpu.roll(gates, LANES - 1, axis=1), 0.0)
    base = i * tt
    for r in range(tt):
        pltpu.make_async_copy(xbuf.at[slot, 0, pl.ds(r, 1), :], xs_hbm.at[pl.ds(p0_ref[base + r], 1), :],
                              sem.at[slot]).start(priority=0)
        pltpu.make_async_copy(xbuf.at[slot, 1, pl.ds(r, 1), :], xs_hbm.at[pl.ds(p1_ref[base + r], 1), :],
                              sem.at[slot]).start(priority=1)

    @pl.when(i == n_steps - 1)
    def _():
        wait_rows(slot)
        if n_steps > 1:
            wait_rows(1 - slot)
        for_each_fill(lambda cp: cp.wait())


def _dispatch(h, gates, starts, ends, pos0, pos1, tm, tt):
    t, d = h.shape
    dx = d + LANES
    n_steps = t // tt
    n_rows = 2 * t + N_EXPERTS * tm
    body = functools.partial(_dispatch_body, tm=tm, tt=tt, n_steps=n_steps)
    grid_spec = pltpu.PrefetchScalarGridSpec(
        num_scalar_prefetch=4,
        grid=(n_steps,),
        in_specs=[pl.BlockSpec((tt, d), lambda i, *_: (i, 0)), pl.BlockSpec((tt, LANES), lambda i, *_: (i, 0))],
        out_specs=pl.BlockSpec(memory_space=pl.ANY),
        scratch_shapes=[pltpu.VMEM((2, MOE_TOPK, tt, dx), F32), pltpu.VMEM((tm, dx), F32),
                        pltpu.SemaphoreType.DMA((2,)), pltpu.SemaphoreType.DMA((1,))],
    )
    return pl.pallas_call(
        body,
        grid_spec=grid_spec,
        out_shape=jax.ShapeDtypeStruct((n_rows, dx), F32),
        compiler_params=_params("arbitrary"),
        name="dispatch",
    )(pos0, pos1, starts, ends, h, gates)


def _experts_body(gs_ref, xs_hbm, wgu_ref, wd_ref, ys_hbm, xbuf, ybuf, wgu_bf, wd_bf, xsem, osem, *, tm):
    e = pl.program_id(0)
    first = gs_ref[e] // tm
    n_tiles = gs_ref[e + 1] // tm - first
    total = gs_ref[N_EXPERTS] // tm

    def in_copy(g, slot):
        row0 = pl.multiple_of(g * tm, tm)
        return pltpu.make_async_copy(xs_hbm.at[pl.ds(row0, tm), :], xbuf.at[slot], xsem.at[slot])

    def out_copy(g, slot):
        row0 = pl.multiple_of(g * tm, tm)
        return pltpu.make_async_copy(ybuf.at[slot], ys_hbm.at[pl.ds(row0, tm), :], osem.at[slot])

    @pl.when(e == 0)
    def _():
        in_copy(0, 0).start()
        ybuf[...] = jnp.zeros(ybuf.shape, F32)
        out_copy(0, 0).start()
        out_copy(1, 1).start()

    @pl.when(n_tiles > 0)
    def _():
        wgu_bf[...] = wgu_ref[0, 0].astype(BF)
        wd_bf[...] = wd_ref[0, 0].astype(BF)

        def tile(k, carry):
            g = first + k
            slot = g % 2
            in_copy(g, slot).wait()
            out_copy(g, slot).wait()
            in_copy(jnp.minimum(g + 1, total - 1), 1 - slot).start(priority=1)
            d = ybuf.shape[2]
            x = xbuf[slot, :, :d]
            row_gate = xbuf[slot, :, d:d + 1]
            row_residual = xbuf[slot, :, d + 1:d + 2]
            gu = jnp.dot(x.astype(BF), wgu_bf[...], preferred_element_type=F32)
            gate, up = gu[:, :EXPERT_FF], gu[:, EXPERT_FF:]
            hidden = (gate * jax.nn.sigmoid(gate) * up).astype(BF)
            y = jnp.dot(hidden, wd_bf[...], preferred_element_type=F32)
            ybuf[slot] = row_gate * y + row_residual * x
            out_copy(g, slot).start(priority=1)
            return carry

        lax.fori_loop(0, n_tiles, tile, 0)

    @pl.when(e == N_EXPERTS - 1)
    def _():
        in_copy(0, total % 2).wait()
        out_copy(0, 0).wait()
        out_copy(1, 1).wait()
        end = gs_ref[N_EXPERTS]
        n_tail = (ys_hbm.shape[0] - end) // tm
        ybuf[0] = jnp.zeros(ybuf.shape[1:], F32)

        def tail_copy(k):
            row0 = pl.multiple_of(end + k * tm, tm)
            return pltpu.make_async_copy(ybuf.at[0], ys_hbm.at[pl.ds(row0, tm), :], osem.at[0])

        def start_one(k, carry):
            tail_copy(k).start()
            return carry

        def wait_one(k, carry):
            tail_copy(k).wait()
            return carry

        lax.fori_loop(0, n_tail, start_one, 0)
        lax.fori_loop(0, n_tail, wait_one, 0)


def _experts(xs, starts, w_gate_up, w_down, layer, tm):
    n_rows, dx = xs.shape
    d = dx - LANES
    body = functools.partial(_experts_body, tm=tm)
    grid_spec = pltpu.PrefetchScalarGridSpec(
        num_scalar_prefetch=1,
        grid=(N_EXPERTS,),
        in_specs=[pl.BlockSpec(memory_space=pl.ANY),
                  pl.BlockSpec((1, 1, d, 2 * EXPERT_FF), lambda e, gs: (layer, e, 0, 0)),
                  pl.BlockSpec((1, 1, EXPERT_FF, d), lambda e, gs: (layer, e, 0, 0))],
        out_specs=pl.BlockSpec(memory_space=pl.ANY),
        scratch_shapes=[pltpu.VMEM((2, tm, dx), F32), pltpu.VMEM((2, tm, d), F32),
                        pltpu.VMEM((d, 2 * EXPERT_FF), BF), pltpu.VMEM((EXPERT_FF, d), BF),
                        pltpu.SemaphoreType.DMA((2,)), pltpu.SemaphoreType.DMA((2,))],
    )
    return pl.pallas_call(
        body,
        grid_spec=grid_spec,
        out_shape=jax.ShapeDtypeStruct((n_rows, d), F32),
        compiler_params=_params("arbitrary"),
        name="experts",
    )(starts, xs, w_gate_up, w_down)


def _combine_body(p0_ref, p1_ref, ys_hbm, g_ref, beta_ref, o_ref, buf0, buf1, sem, *, tc, n_steps):
    i = pl.program_id(0)
    slot = i % 2

    def start_row(base, r, sl):
        pltpu.make_async_copy(ys_hbm.at[pl.ds(p0_ref[base + r], 1), :], buf0.at[sl, pl.ds(r, 1), :],
                              sem.at[sl]).start(priority=0)
        pltpu.make_async_copy(ys_hbm.at[pl.ds(p1_ref[base + r], 1), :], buf1.at[sl, pl.ds(r, 1), :],
                              sem.at[sl]).start(priority=1)

    def wait_rows(sl):
        pltpu.make_async_copy(ys_hbm.at[pl.ds(0, tc), :], buf0.at[sl], sem.at[sl]).wait()
        pltpu.make_async_copy(ys_hbm.at[pl.ds(0, tc), :], buf1.at[sl], sem.at[sl]).wait()

    @pl.when(i == 0)
    def _():
        def one(r, carry):
            start_row(0, r, 0)
            return carry

        lax.fori_loop(0, tc, one, 0)

    wait_rows(slot)
    nxt = jnp.minimum(i + 1, n_steps - 1) * tc
    for r in range(tc):
        start_row(nxt, r, 1 - slot)
    o_ref[...] = _layer_norm(buf0[slot] + buf1[slot], g_ref[...], beta_ref[...])

    @pl.when(i == n_steps - 1)
    def _():
        wait_rows(1 - slot)


def _combine_ln(ys, pos0, pos1, g, beta, tc):
    t, d = pos0.shape[0], ys.shape[1]
    n_steps = t // tc
    body = functools.partial(_combine_body, tc=tc, n_steps=n_steps)
    row = lambda i, p0, p1: (i, 0)
    const = lambda i, p0, p1: (0, 0)
    grid_spec = pltpu.PrefetchScalarGridSpec(
        num_scalar_prefetch=2,
        grid=(n_steps,),
        in_specs=[pl.BlockSpec(memory_space=pl.ANY), pl.BlockSpec((1, d), const), pl.BlockSpec((1, d), const)],
        out_specs=pl.BlockSpec((tc, d), row),
        scratch_shapes=[pltpu.VMEM((2, tc, d), F32), pltpu.VMEM((2, tc, d), F32), pltpu.SemaphoreType.DMA((2,))],
    )
    return pl.pallas_call(
        body,
        grid_spec=grid_spec,
        out_shape=jax.ShapeDtypeStruct((t, d), F32),
        compiler_params=_params("arbitrary"),
        name="combine_ln",
    )(pos0, pos1, ys, g, beta)


def _moe_ln(h, wr_perm, bias_perm, w_gate_up, w_down, layer, g, beta, tm_router=512, tm_expert=128, tc=256):
    e01, gates = _router(h, wr_perm, bias_perm, tm_router)
    starts, ends, pos0, pos1 = _dispatch_plan(e01[:MOE_TOPK], tm_expert)
    xs = _dispatch(h, gates, starts, ends, pos0, pos1, tm_expert, tc)
    ys = _experts(xs, starts, w_gate_up, w_down, layer, tm_expert)
    return _combine_ln(ys, pos0, pos1, g, beta, tc)


def _swap_halves(w):
    half = w.shape[-1] // 2
    return jnp.concatenate([w[..., half:], w[..., :half]], axis=-1)


def _even_weights(w_in, w_q_up, w_kv_up, forget_bias):
    d = w_in.shape[0]
    o_kv = MLA_Q_LORA
    o_kr = o_kv + MLA_KV_LORA
    o_fq = o_kr + MLA_ROPE
    hd = FOX_HEADS * FOX_HEAD_DIM
    o_fl = o_fq + 3 * hd
    k_r = w_in[:, o_kr:o_fq]
    f_l = jnp.pad(w_in[:, o_fl:], ((0, 0), (0, LANES - FOX_HEADS)))
    w_small = jnp.concatenate([w_in[:, :o_kr], k_r, _swap_halves(k_r), f_l], axis=1).astype(BF)
    w_fox = w_in[:, o_fq:o_fl].astype(BF)
    cs_fox = jnp.concatenate([jnp.full((1, hd), FOX_SCALE, F32), jnp.ones((1, 2 * hd), F32)], axis=1)
    wq = w_q_up.reshape(MLA_Q_LORA, MLA_HEADS, MLA_NOPE + MLA_ROPE)
    wqn = wq[:, :, :MLA_NOPE].reshape(MLA_Q_LORA, -1).astype(BF)
    pe = wq[:, :, MLA_NOPE:]
    wqp = jnp.concatenate([pe, _swap_halves(pe)], axis=-1).reshape(MLA_Q_LORA, -1).astype(BF)
    wkv = w_kv_up.reshape(MLA_KV_LORA, MLA_HEADS, MLA_NOPE + MLA_V)
    wk = wkv[:, :, :MLA_NOPE].reshape(MLA_KV_LORA, -1).astype(BF)
    wv = wkv[:, :, MLA_NOPE:].reshape(MLA_KV_LORA, -1).astype(BF)
    fb = jnp.pad(forget_bias.astype(F32), (0, LANES - FOX_HEADS)).reshape(1, LANES)
    return w_small, w_fox, cs_fox, wqn, wqp, wk, wv, fb


def _rope_table(seq):
    half = MLA_ROPE // 2
    inv_freq = ROPE_THETA ** (-jnp.arange(half, dtype=F32) / half)
    ang = jnp.arange(seq).astype(F32)[:, None] * inv_freq[None, :]
    cos, sin = jnp.cos(ang), jnp.sin(ang)
    return jnp.concatenate([cos, cos, -sin, sin], axis=1)


def _router_weights(w_router, router_bias):
    r = np.arange(N_EXPERTS)
    perm = (r % N_GROUPS) * EXPERTS_PER_GROUP + r // N_GROUPS
    wr = jnp.pad(w_router[:, perm], ((0, 0), (0, LANES - N_EXPERTS)))
    w_hi = wr.astype(BF)
    w_lo = (wr - w_hi.astype(F32)).astype(BF)
    return jnp.concatenate([w_hi, w_lo], axis=1), router_bias.astype(F32)[perm].reshape(N_EXPERTS, 1)


def _even_layer(h, batch, seq, w_in, q_norm, w_q_up, kv_norm, w_kv_up, forget_bias, w_out, g, beta):
    w_small, w_fox, cs_fox, wqn, wqp, wk, wv, fb = _even_weights(w_in, w_q_up, w_kv_up, forget_bias)
    fox_qkv = _mm(h, w_fox, cs_fox, BF, 512, 1536)
    qn, qp, kn, v, kp, forget_logits = _mla_prep(h, w_small, _rope_table(seq), q_norm.reshape(1, -1),
                                                 kv_norm.reshape(1, -1), wqn, wqp, wk, wv, seq, 512)
    q_bias, k_bias, edge = _fox_cum(forget_logits, fb, batch, seq, 512)
    a = _mla_attention(qn, qp, kn, v, kp, batch, seq, 512, 4)
    bo = _fox_attention(fox_qkv, q_bias, k_bias, edge, batch, seq, 512, 4)
    return _outproj_ln(a, bo, w_out.astype(BF), h, g, beta, 512)


def _odd_layer(h, batch, seq, w_in, sinks, w_out, g, beta):
    n_sq = SWA_Q_HEADS * SWA_HEAD_DIM
    n_skv = SWA_KV_HEADS * SWA_HEAD_DIM
    n_m = MOBA_HEADS * MOBA_HEAD_DIM
    o_mq = n_sq + 2 * n_skv
    w_big = jnp.concatenate([w_in[:, :n_sq], w_in[:, o_mq:], w_in[:, n_sq:o_mq]], axis=1).astype(BF)
    cs = jnp.concatenate([jnp.full((1, n_sq), SWA_SCALE, F32), jnp.full((1, n_m), MOBA_SCALE, F32),
                          jnp.ones((1, 2 * n_m + 2 * n_skv), F32)], axis=1)
    big = _mm(h, w_big, cs, BF, 512, 2176)
    blocks = lambda cols: cols // LANES
    k_mean = _moba_kmean(big, batch, seq, 2)
    slopes = jnp.asarray(LOG2E * 2.0 ** (-8.0 * np.arange(1, MOBA_HEADS + 1) / MOBA_HEADS), dtype=F32)
    c = _swa_attention(big, sinks.astype(F32), batch, seq, 0, blocks(n_sq + 3 * n_m), blocks(n_sq + 3 * n_m) + 1)
    dd = _moba_attention(big, k_mean, slopes, batch, seq, blocks(n_sq), blocks(n_sq + n_m), blocks(n_sq + 2 * n_m),
                         8)
    return _outproj_ln(c, dd, w_out.astype(BF), h, g, beta, 512)


def kernel(x, w_router, router_bias, even_w_in, even_q_norm, even_w_q_up, even_kv_norm, even_w_kv_up,
           even_forget_bias, even_w_out, odd_w_in, odd_sinks, odd_w_out, ln_mix_g, ln_mix_b, ln_ffn_g, ln_ffn_b,
           w_gate_up, w_down):
    batch, seq, d = x.shape
    h = x.reshape(batch * seq, d)
    wr_perm, bias_perm = _router_weights(w_router, router_bias)
    row = lambda p, layer: p[layer].reshape(1, d)
    for layer in range(DEPTH):
        i = layer // 2
        if layer % 2 == 0:
            h = _even_layer(h, batch, seq, even_w_in[i], even_q_norm[i], even_w_q_up[i], even_kv_norm[i],
                            even_w_kv_up[i], even_forget_bias[i], even_w_out[i], row(ln_mix_g, layer),
                            row(ln_mix_b, layer))
        else:
            h = _odd_layer(h, batch, seq, odd_w_in[i], odd_sinks[i], odd_w_out[i], row(ln_mix_g, layer),
                           row(ln_mix_b, layer))
        h = _moe_ln(h, wr_perm, bias_perm, w_gate_up, w_down, layer, row(ln_ffn_g, layer), row(ln_ffn_b, layer))
    return h.reshape(batch, seq, d)
```

```python
import functools

import numpy as np
import jax
import jax.numpy as jnp
from jax import lax
from jax.experimental import pallas as pl
from jax.experimental.pallas import tpu as pltpu

BF = jnp.bfloat16
F32 = jnp.float32
I32 = jnp.int32

LANES = 128
SUBLANES = 8
VMEM_LIMIT = 56 * 1024 * 1024

D_MODEL = 2048
DEPTH = 2
LN_EPS = 1e-5
RMS_EPS = 1e-6
ALPHA = (2 * DEPTH) ** 0.25
MLA_HEADS, MLA_Q_LORA, MLA_KV_LORA, MLA_NOPE, MLA_ROPE, MLA_V = 8, 512, 256, 128, 64, 128
ROPE_THETA = 10000.0
FOX_HEADS, FOX_HEAD_DIM = 8, 128
SWA_Q_HEADS, SWA_KV_HEADS, SWA_HEAD_DIM, SWA_WINDOW = 16, 2, 64, 128
MOBA_HEADS, MOBA_HEAD_DIM, MOBA_BLOCK, MOBA_TOPK = 8, 128, 256, 3
N_EXPERTS, N_GROUPS, MOE_TOPK, EXPERT_FF = 32, 8, 2, 512
EXPERTS_PER_GROUP = N_EXPERTS // N_GROUPS

LOG2E = 1.4426950408889634
MLA_SCALE = (MLA_NOPE + MLA_ROPE) ** -0.5 * LOG2E
FOX_SCALE = FOX_HEAD_DIM ** -0.5 * LOG2E
MOBA_SCALE = MOBA_HEAD_DIM ** -0.5 * LOG2E
SWA_SCALE = SWA_HEAD_DIM ** -0.5 * LOG2E
NEG = -1e30

NT_DIMS = (((1,), (1,)), ((), ()))


def _params(*sem):
    return pltpu.CompilerParams(dimension_semantics=sem, vmem_limit_bytes=VMEM_LIMIT)


def _mm_body(x_ref, w_ref, cs_ref, o_ref):
    acc = jnp.dot(x_ref[...].astype(BF), w_ref[...], preferred_element_type=F32)
    o_ref[...] = (acc * cs_ref[...]).astype(o_ref.dtype)


def _mm(x, w, col_scale, out_dtype, tm, tn):
    m, k = x.shape
    n = w.shape[1]
    return pl.pallas_call(
        _mm_body,
        grid=(n // tn, m // tm),
        in_specs=[
            pl.BlockSpec((tm, k), lambda j, i: (i, 0)),
            pl.BlockSpec((k, tn), lambda j, i: (0, j)),
            pl.BlockSpec((1, tn), lambda j, i: (0, j)),
        ],
        out_specs=pl.BlockSpec((tm, tn), lambda j, i: (i, j)),
        out_shape=jax.ShapeDtypeStruct((m, n), out_dtype),
        compiler_params=_params("parallel", "parallel"),
        name="proj",
    )(x, w, col_scale)


def _rms(x, g):
    return x * lax.rsqrt(jnp.mean(x * x, axis=-1, keepdims=True) + RMS_EPS) * g


def _rope_pair(slab, table):
    r = slab * table
    return r + pltpu.roll(r, MLA_ROPE, axis=1)


def _mla_prep_body(x_ref, ws_ref, rope_ref, qg_ref, kvg_ref, wqn_ref, wqp_ref, wk_ref, wv_ref,
                   qn_ref, qp_ref, kn_ref, v_ref, kp_ref, fl_ref):
    small = jnp.dot(x_ref[...].astype(BF), ws_ref[...], preferred_element_type=F32)
    o_kv, o_kr, o_fl = MLA_Q_LORA, MLA_Q_LORA + MLA_KV_LORA, MLA_Q_LORA + MLA_KV_LORA + LANES
    fl_ref[...] = small[:, o_fl:]
    table = rope_ref[...]
    cqn = _rms(small[:, :o_kv], qg_ref[...]).astype(BF)
    qn_ref[...] = (jnp.dot(cqn, wqn_ref[...], preferred_element_type=F32) * MLA_SCALE).astype(BF)
    qp = jnp.dot(cqn, wqp_ref[...], preferred_element_type=F32)
    for h in range(MLA_HEADS):
        sl = slice(h * LANES, (h + 1) * LANES)
        qp_ref[:, sl] = (_rope_pair(qp[:, sl], table) * MLA_SCALE).astype(BF)
    ckvn = _rms(small[:, o_kv:o_kr], kvg_ref[...]).astype(BF)
    kn_ref[...] = jnp.dot(ckvn, wk_ref[...], preferred_element_type=F32).astype(BF)
    v_ref[...] = jnp.dot(ckvn, wv_ref[...], preferred_element_type=F32).astype(BF)
    kr = _rope_pair(small[:, o_kr:o_fl], table)
    lane = lax.broadcasted_iota(I32, kr.shape, 1)
    kp_ref[...] = jnp.where(lane < MLA_ROPE, kr, 0.0).astype(BF)


def _mla_prep(x, w_small, rope_table, q_norm, kv_norm, wqn, wqp, wk, wv, seq, tm):
    t, d = x.shape
    n_s = seq // tm
    hd = MLA_HEADS * LANES
    row = lambda i: (i, 0)
    const = lambda i: (0, 0)
    out = lambda w: pl.BlockSpec((tm, w), row)
    return pl.pallas_call(
        _mla_prep_body,
        grid=(t // tm,),
        in_specs=[
            pl.BlockSpec((tm, d), row),
            pl.BlockSpec(w_small.shape, const),
            pl.BlockSpec((tm, LANES), lambda i: (i % n_s, 0)),
            pl.BlockSpec((1, MLA_Q_LORA), const),
            pl.BlockSpec((1, MLA_KV_LORA), const),
            pl.BlockSpec((MLA_Q_LORA, hd), const),
            pl.BlockSpec((MLA_Q_LORA, hd), const),
            pl.BlockSpec((MLA_KV_LORA, hd), const),
            pl.BlockSpec((MLA_KV_LORA, hd), const),
        ],
        out_specs=[out(hd), out(hd), out(hd), out(hd), out(LANES), out(LANES)],
        out_shape=[jax.ShapeDtypeStruct((t, hd), BF)] * 4 + [jax.ShapeDtypeStruct((t, LANES), BF),
                                                             jax.ShapeDtypeStruct((t, LANES), F32)],
        compiler_params=_params("parallel"),
        name="mla_prep",
    )(x, w_small, rope_table, q_norm, kv_norm, wqn, wqp, wk, wv)


N_SPLIT = 3


def _split_piece(x, which):
    hi = x.astype(BF).astype(F32)
    rest = x - hi
    mid = rest.astype(BF).astype(F32)
    return jnp.where(which == 0, hi, jnp.where(which == 1, mid, rest - mid))


def _fox_cum_body(fl_ref, fb_ref, spread_ref, qe_ref, ke_ref, edge_ref, carry_ref):
    @pl.when(pl.program_id(1) == 0)
    def _():
        carry_ref[...] = jnp.zeros_like(carry_ref)

    z = fl_ref[...] + fb_ref[...]
    log_f = jnp.minimum(z, 0.0) - jnp.log1p(jnp.exp(-jnp.abs(z)))
    tm = z.shape[0]
    r = lax.broadcasted_iota(I32, (tm, tm), 0)
    c = lax.broadcasted_iota(I32, (tm, tm), 1)
    tri = jnp.where(r >= c, 1.0, 0.0).astype(BF)
    parts = jnp.dot(tri, jnp.concatenate([_split_piece(log_f, n).astype(BF) for n in range(N_SPLIT)], axis=1),
                    preferred_element_type=F32)
    cum = (parts[:, :LANES] + parts[:, LANES:2 * LANES]) + parts[:, 2 * LANES:] + carry_ref[...]
    carry_ref[...] = cum[tm - 1:tm, :]
    cum = cum * LOG2E
    width = qe_ref.shape[1]
    pieces = jnp.concatenate([_split_piece(cum, n).astype(BF) for n in range(N_SPLIT)], axis=1)
    placed = jnp.dot(pieces, spread_ref[...], preferred_element_type=F32)
    li = lax.broadcasted_iota(I32, (tm, width), 1) % LANES
    qe_ref[...] = jnp.where((li >= N_SPLIT) & (li < 2 * N_SPLIT), 1.0, placed[:, :width]).astype(BF)
    ke_ref[...] = jnp.where(li < N_SPLIT, 1.0, placed[:, width:]).astype(BF)
    edge_ref[0, 0:1, :] = cum[0:1, :]
    edge_ref[0, 1:2, :] = cum[tm - 1:tm, :]


def _fox_spread():
    width = FOX_HEADS * LANES
    m = np.zeros((N_SPLIT * LANES, 2 * width), np.float32)
    for j in range(N_SPLIT):
        for h in range(FOX_HEADS):
            m[j * LANES + h, h * LANES + j] = 1.0
            m[j * LANES + h, width + h * LANES + N_SPLIT + j] = -1.0
    return jnp.asarray(m, dtype=BF)


def _fox_cum(forget_logits, forget_bias_row, batch, seq, tm):
    t = forget_logits.shape[0]
    n_s = seq // tm
    width = FOX_HEADS * LANES
    q_bias, k_bias, edge = pl.pallas_call(
        _fox_cum_body,
        grid=(batch, n_s),
        in_specs=[
            pl.BlockSpec((tm, LANES), lambda b, i: (b * n_s + i, 0)),
            pl.BlockSpec((1, LANES), lambda b, i: (0, 0)),
            pl.BlockSpec((N_SPLIT * LANES, 2 * width), lambda b, i: (0, 0)),
        ],
        out_specs=[
            pl.BlockSpec((tm, width), lambda b, i: (b * n_s + i, 0)),
            pl.BlockSpec((tm, width), lambda b, i: (b * n_s + i, 0)),
            pl.BlockSpec((1, 2, LANES), lambda b, i: (b * n_s + i, 0, 0)),
        ],
        out_shape=[jax.ShapeDtypeStruct((t, width), BF), jax.ShapeDtypeStruct((t, width), BF),
                   jax.ShapeDtypeStruct((batch * n_s, 2, LANES), F32)],
        scratch_shapes=[pltpu.VMEM((1, LANES), F32)],
        compiler_params=_params("parallel", "arbitrary"),
        name="fox_cum",
    )(forget_logits, forget_bias_row, _fox_spread())
    return q_bias, k_bias, edge[:, :, :FOX_HEADS].reshape(-1)


def _softmax_update(s_all, v_all, m_ref, l_ref, acc_ref):
    heads = range(len(s_all))
    m_prev = [m_ref[g] for g in heads]
    m_new = [jnp.maximum(m_prev[g], jnp.max(s_all[g], axis=1, keepdims=True)) for g in heads]
    reps = s_all[0].shape[1] // LANES
    p = [jnp.exp2(s_all[g] - jnp.concatenate([m_new[g]] * reps, axis=1)) for g in heads]
    pv = [jnp.dot(p[g].astype(BF), v_all[g], preferred_element_type=F32) for g in heads]
    for g in heads:
        alpha = jnp.exp2(m_prev[g] - m_new[g])
        l_ref[g] = alpha * l_ref[g] + jnp.sum(p[g], axis=1, keepdims=True)
        acc_ref[g] = alpha * acc_ref[g] + pv[g]
        m_ref[g] = m_new[g]


def _attn_body(*refs, mode, t, group):
    hg = pl.program_id(1)
    i = pl.program_id(2)
    heads = range(group)
    sl = lambda g: slice(g * LANES, (g + 1) * LANES)
    if mode == "moba":
        q_ref, k_ref, kx_ref, v_ref, km_ref, slope_ref, o_ref, m_ref, l_ref, acc_ref, qx_ref = refs
        for g in heads:
            qx_ref[g] = _moba_query_lanes(q_ref[:, sl(g)], km_ref[:, sl(g)], slope_ref[hg * group + g], i, t)
        q_extra = lambda g: qx_ref[g]
        k_extra = lambda g, ks: kx_ref[pl.ds(ks, t), :]
    elif mode == "mla":
        q_ref, qx_ref, k_ref, kx_ref, v_ref, o_ref, m_ref, l_ref, acc_ref = refs
        q_extra = lambda g: qx_ref[:, sl(g)]
        k_extra = lambda g, ks: kx_ref[pl.ds(ks, t), :]
    else:
        q_ref, qx_ref, k_ref, kx_ref, v_ref, edge_ref, o_ref, m_ref, l_ref, acc_ref, knorm_ref = refs
        q_extra = lambda g: qx_ref[:, sl(g)]
        k_extra = lambda g, ks: kx_ref[pl.ds(ks, t), sl(g)]
        first_tile = _fox_first_tile(q_ref, k_ref, edge_ref, knorm_ref, i, hg, t, group)

    def scores(g, ks):
        q = jnp.concatenate([q_ref[:, sl(g)], q_extra(g)], axis=1)
        k = jnp.concatenate([k_ref[pl.ds(ks, t), sl(g)], k_extra(g, ks)], axis=1)
        return lax.dot_general(q, k, NT_DIMS, preferred_element_type=F32)

    m_ref[...] = jnp.full(m_ref.shape, NEG, F32)
    l_ref[...] = jnp.zeros(l_ref.shape, F32)
    acc_ref[...] = jnp.zeros(acc_ref.shape, F32)

    ks = pl.multiple_of(i * t, t)
    r = lax.broadcasted_iota(I32, (t, t), 0)
    c = lax.broadcasted_iota(I32, (t, t), 1)
    values = lambda ks: [v_ref[pl.ds(ks, t), sl(g)] for g in heads]
    _softmax_update([jnp.where(c <= r, scores(g, ks), NEG) for g in heads], values(ks), m_ref, l_ref, acc_ref)

    def past_tile(j, carry):
        ks = pl.multiple_of(j * t, t)
        _softmax_update([scores(g, ks) for g in heads], values(ks), m_ref, l_ref, acc_ref)
        return carry

    lax.fori_loop(first_tile if mode == "fox" else 0, i, past_tile, 0)
    for g in heads:
        o_ref[:, sl(g)] = (acc_ref[g] / l_ref[g]).astype(o_ref.dtype)


FOX_SKIP_GAP = 160.0


def _fox_first_tile(q_ref, k_ref, edge_ref, knorm_ref, i, hg, t, group):
    b = pl.program_id(0)
    n_q = pl.num_programs(2)
    sl = lambda g: slice(g * LANES, (g + 1) * LANES)

    def max_row_norm(x):
        x = x.astype(F32)
        return jnp.sqrt(jnp.max(jnp.sum(x * x, axis=1, keepdims=True)))

    @pl.when(i == 0)
    def _():
        for g in range(group):
            knorm_ref[g] = max_row_norm(k_ref[:, sl(g)])

    def edge(tile, last, g):
        return edge_ref[((b * n_q + tile) * 2 + last) * FOX_HEADS + hg * group + g]

    limit = [-(2.0 * 1.001 * max_row_norm(q_ref[:, sl(g)]) * knorm_ref[g] + FOX_SKIP_GAP) for g in range(group)]
    first = [edge(i, 0, g) for g in range(group)]

    def scan(j, lo):
        needed = first[0] - edge(j, 1, 0) >= limit[0]
        for g in range(1, group):
            needed = needed | (first[g] - edge(j, 1, g) >= limit[g])
        return jnp.where(needed, jnp.minimum(lo, j), lo)

    return lax.fori_loop(0, i, scan, i)


def _attention(mode, batch, seq, heads, t, group, operands, in_specs, extra_scratch=()):
    n_q = seq // t
    body = functools.partial(_attn_body, mode=mode, t=t, group=group)
    return pl.pallas_call(
        body,
        grid=(batch, heads // group, n_q),
        in_specs=in_specs,
        out_specs=pl.BlockSpec((t, group * LANES), lambda b, h, i: (b * n_q + i, h)),
        out_shape=jax.ShapeDtypeStruct((batch * seq, heads * LANES), BF),
        scratch_shapes=[pltpu.VMEM((group, t, LANES), F32), pltpu.VMEM((group, t, LANES), F32),
                        pltpu.VMEM((group, t, LANES), F32), *extra_scratch],
        compiler_params=_params("parallel", "parallel", "arbitrary"),
        name="attn_" + mode,
    )(*operands)


def _q_spec(t, n_q, group, col0=0):
    return pl.BlockSpec((t, group * LANES), lambda b, h, i: (b * n_q + i, col0 + h))


def _kv_spec(seq, group, col0=0):
    return pl.BlockSpec((seq, group * LANES), lambda b, h, i: (b, col0 + h))


def _mla_attention(qn, qp, kn, v, kp, batch, seq, t, group):
    n_q = seq // t
    specs = [_q_spec(t, n_q, group), _q_spec(t, n_q, group), _kv_spec(seq, group),
             pl.BlockSpec((seq, LANES), lambda b, h, i: (b, 0)), _kv_spec(seq, group)]
    return _attention("mla", batch, seq, MLA_HEADS, t, group, (qn, qp, kn, kp, v), specs)


def _fox_attention(qkv, q_bias, k_bias, edge, batch, seq, t, group):
    n_q = seq // t
    n_hg = FOX_HEADS // group
    specs = [_q_spec(t, n_q, group, 0), _q_spec(t, n_q, group, 0), _kv_spec(seq, group, n_hg),
             _kv_spec(seq, group, 0), _kv_spec(seq, group, 2 * n_hg), pl.BlockSpec(memory_space=pltpu.SMEM)]
    return _attention("fox", batch, seq, FOX_HEADS, t, group, (qkv, q_bias, qkv, k_bias, qkv, edge), specs,
                      (pltpu.SMEM((group,), F32),))


def _moba_key_lanes(seq):
    n_kb = seq // MOBA_BLOCK
    kpos = np.arange(seq)
    lanes = np.zeros((seq, LANES), np.float32)
    lanes[kpos, kpos // MOBA_BLOCK] = 1.0
    lanes[:, n_kb:n_kb + N_SPLIT] = 1.0
    lanes[:, n_kb + N_SPLIT:n_kb + 2 * N_SPLIT] = (MOBA_BLOCK * (kpos // MOBA_BLOCK))[:, None]
    lanes[:, n_kb + 2 * N_SPLIT:n_kb + 3 * N_SPLIT] = (kpos % MOBA_BLOCK)[:, None]
    return jnp.asarray(lanes, dtype=BF)


def _moba_query_lanes(q, k_mean, slope, i, t):
    n_kb = k_mean.shape[0]
    gate = lax.dot_general(k_mean.astype(BF), q, NT_DIMS, preferred_element_type=F32)
    blk = lax.broadcasted_iota(I32, (n_kb, t), 0)
    beaten = jnp.zeros((n_kb, t), F32)
    for n in range(n_kb):
        g_n = gate[n:n + 1, :]
        wins = (g_n > gate) | ((g_n == gate) & (blk > n))
        beaten = beaten + jnp.where(wins, 1.0, 0.0) * jnp.where(i > n, 1.0, 0.0)
    attended = (blk == i) | ((blk < i) & (beaten < MOBA_TOPK))
    mask = jnp.concatenate([jnp.where(attended, 0.0, NEG), jnp.zeros((LANES - n_kb, t), F32)], axis=0)
    row = lax.broadcasted_iota(I32, (LANES, t), 0) - n_kb
    qpos = (i * t + lax.broadcasted_iota(I32, (1, t), 1)).astype(F32)
    slope_row = jnp.full((1, t), slope, F32)
    which = row % N_SPLIT
    lanes = jnp.where(row < 0, mask,
                      jnp.where(row < N_SPLIT, _split_piece(-slope_row * qpos, which),
                                jnp.where(row < 3 * N_SPLIT, _split_piece(slope_row, which), 0.0)))
    return lanes.T.astype(BF)


def _moba_attention(big, k_mean, slopes, batch, seq, col_q, col_k, col_v, group):
    t = MOBA_BLOCK
    n_q = seq // t
    specs = [_q_spec(t, n_q, group, col_q // group), _kv_spec(seq, group, col_k // group),
             pl.BlockSpec((seq, LANES), lambda b, h, i: (0, 0)),
             _kv_spec(seq, group, col_v // group),
             pl.BlockSpec((n_q, group * LANES), lambda b, h, i: (b, h)),
             pl.BlockSpec(memory_space=pltpu.SMEM)]
    scratch = (pltpu.VMEM((group, t, LANES), BF),)
    return _attention("moba", batch, seq, MOBA_HEADS, t, group,
                      (big, big, _moba_key_lanes(seq), big, k_mean, slopes), specs, scratch)


def _kmean_body(k_ref, o_ref):
    n_kb = o_ref.shape[0]
    for n in range(n_kb):
        blk = k_ref[n * MOBA_BLOCK:(n + 1) * MOBA_BLOCK, :].astype(F32)
        o_ref[n:n + 1, :] = jnp.mean(blk, axis=0, keepdims=True)


def _moba_kmean(big, batch, seq, col_block):
    n_kb = seq // MOBA_BLOCK
    width = MOBA_HEADS * MOBA_HEAD_DIM
    return pl.pallas_call(
        _kmean_body,
        grid=(batch,),
        in_specs=[pl.BlockSpec((seq, width), lambda b: (b, col_block))],
        out_specs=pl.BlockSpec((n_kb, width), lambda b: (b, 0)),
        out_shape=jax.ShapeDtypeStruct((batch * n_kb, width), F32),
        compiler_params=_params("parallel"),
        name="moba_kmean",
    )(big)


def _swa_body(sink_ref, q_ref, kp_ref, kc_ref, vp_ref, vc_ref, o_ref):
    i = pl.program_id(1)
    w = SWA_WINDOW
    half = SWA_HEAD_DIM
    lane = lax.broadcasted_iota(I32, (w, LANES), 1)
    r = lax.broadcasted_iota(I32, (w, w), 0)
    c = lax.broadcasted_iota(I32, (w, w), 1)
    from_prev = c > r
    dist = (r - c + jnp.where(from_prev, w, 0)).astype(F32)
    valid = jnp.logical_not(from_prev) | (i > 0)
    heads_per_kv = SWA_Q_HEADS // SWA_KV_HEADS

    def lo_hi(ref, kvh):
        own = jnp.where((lane < half) == (kvh == 0), ref[...].astype(F32), 0.0)
        other = pltpu.roll(own, half, axis=1)
        lo, hi = (own, other) if kvh == 0 else (other, own)
        return lo.astype(BF), hi.astype(BF)

    for kvh in range(SWA_KV_HEADS):
        kp, kc, vp, vc = (lo_hi(ref, kvh) for ref in (kp_ref, kc_ref, vp_ref, vc_ref))
        heads = range(kvh * heads_per_kv, (kvh + 1) * heads_per_kv)
        q = {h: q_ref[:, (h // 2) * LANES:(h // 2 + 1) * LANES] for h in heads}
        s = {h: jnp.where(from_prev,
                          lax.dot_general(q[h], kp[h % 2], NT_DIMS, preferred_element_type=F32),
                          lax.dot_general(q[h], kc[h % 2], NT_DIMS, preferred_element_type=F32)) for h in heads}
        p, inv = {}, {}
        for h in heads:
            slope = float(LOG2E * 2.0 ** (-8.0 * (h + 1) / SWA_Q_HEADS))
            sink = sink_ref[h] * LOG2E
            logits = jnp.where(valid, s[h] - slope * dist, NEG)
            m = jnp.maximum(jnp.max(logits, axis=1, keepdims=True), sink)
            p[h] = jnp.exp2(logits - m)
            inv[h] = 1.0 / (jnp.sum(p[h], axis=1, keepdims=True) + jnp.exp2(sink - m))
        for pair in range(kvh * heads_per_kv // 2, (kvh + 1) * heads_per_kv // 2):
            out = jnp.zeros((w, LANES), F32)
            for h in (2 * pair, 2 * pair + 1):
                pn = p[h] * inv[h]
                out = out + jnp.dot(jnp.where(from_prev, pn, 0.0).astype(BF), vp[h % 2], preferred_element_type=F32)
                out = out + jnp.dot(jnp.where(from_prev, 0.0, pn).astype(BF), vc[h % 2], preferred_element_type=F32)
            o_ref[:, pair * LANES:(pair + 1) * LANES] = out.astype(o_ref.dtype)


def _swa_attention(big, sinks, batch, seq, col_q, col_k, col_v):
    w = SWA_WINDOW
    n_q = seq // w
    width = SWA_Q_HEADS * SWA_HEAD_DIM
    prev = lambda col: pl.BlockSpec((w, LANES), lambda b, i: (b * n_q + jnp.maximum(i - 1, 0), col))
    cur = lambda col: pl.BlockSpec((w, LANES), lambda b, i: (b * n_q + i, col))
    return pl.pallas_call(
        _swa_body,
        grid=(batch, n_q),
        in_specs=[pl.BlockSpec(memory_space=pltpu.SMEM),
                  pl.BlockSpec((w, width), lambda b, i: (b * n_q + i, col_q)),
                  prev(col_k), cur(col_k), prev(col_v), cur(col_v)],
        out_specs=pl.BlockSpec((w, width), lambda b, i: (b * n_q + i, 0)),
        out_shape=jax.ShapeDtypeStruct((batch * seq, width), BF),
        compiler_params=_params("parallel", "parallel"),
        name="swa",
    )(sinks, big, big, big, big, big)


def _layer_norm(z, g, b):
    mu = jnp.mean(z, axis=-1, keepdims=True)
    zc = z - mu
    var = jnp.mean(zc * zc, axis=-1, keepdims=True)
    return zc * lax.rsqrt(var + LN_EPS) * g + b


def _outproj_body(a_ref, b_ref, wa_ref, wb_ref, h_ref, g_ref, beta_ref, o_ref):
    mix = jnp.dot(a_ref[...], wa_ref[...], preferred_element_type=F32)
    mix = mix + jnp.dot(b_ref[...], wb_ref[...], preferred_element_type=F32)
    o_ref[...] = _layer_norm(ALPHA * h_ref[...] + mix, g_ref[...], beta_ref[...])


def _outproj_ln(a, b, w_out, h, g, beta, tm):
    t, d = h.shape
    ka = a.shape[1]
    row = lambda i: (i, 0)
    return pl.pallas_call(
        _outproj_body,
        grid=(t // tm,),
        in_specs=[pl.BlockSpec((tm, ka), row), pl.BlockSpec((tm, ka), row),
                  pl.BlockSpec((ka, d), lambda i: (0, 0)), pl.BlockSpec((ka, d), lambda i: (1, 0)),
                  pl.BlockSpec((tm, d), row), pl.BlockSpec((1, d), lambda i: (0, 0)),
                  pl.BlockSpec((1, d), lambda i: (0, 0))],
        out_specs=pl.BlockSpec((tm, d), row),
        out_shape=jax.ShapeDtypeStruct((t, d), F32),
        compiler_params=_params("parallel"),
        name="outproj_ln",
    )(a, b, w_out, w_out, h, g, beta)


def _second_of_four(a, b, c, d):
    hi_ab, lo_ab = jnp.maximum(a, b), jnp.minimum(a, b)
    hi_cd, lo_cd = jnp.maximum(c, d), jnp.minimum(c, d)
    return jnp.maximum(jnp.maximum(lo_ab, lo_cd), jnp.minimum(hi_ab, hi_cd))


def _router_body(h_ref, wr_ref, bias_ref, e_ref, r_ref):
    tm = h_ref.shape[0]
    h = h_ref[...]
    h_hi = h.astype(BF)
    h_lo = (h - h_hi.astype(F32)).astype(BF)
    both = jnp.dot(h_hi, wr_ref[...], preferred_element_type=F32)
    logits = (both[:, :LANES] + both[:, LANES:]) + jnp.dot(h_lo, wr_ref[:, :LANES], preferred_element_type=F32)
    lt = logits.T
    aff = [jax.nn.sigmoid(lt[SUBLANES * j:SUBLANES * (j + 1), :]) for j in range(EXPERTS_PER_GROUP)]
    sel = [aff[j] + bias_ref[SUBLANES * j:SUBLANES * (j + 1), :] for j in range(EXPERTS_PER_GROUP)]
    top1 = jnp.maximum(jnp.maximum(sel[0], sel[1]), jnp.maximum(sel[2], sel[3]))
    score = top1 + _second_of_four(*sel)
    gid = lax.broadcasted_iota(I32, (N_GROUPS, tm), 0)
    best = jnp.min(jnp.where(score == jnp.max(score, axis=0, keepdims=True), gid, N_GROUPS),
                   axis=0, keepdims=True)
    in_grp = gid == best
    pick = lambda x: jnp.sum(jnp.where(in_grp, x, 0.0), axis=0, keepdims=True)
    s4 = [pick(x) for x in sel]
    a4 = [pick(x) for x in aff]

    def argmax4(vals):
        j, v = jnp.zeros((1, tm), I32), vals[0]
        for n in range(1, EXPERTS_PER_GROUP):
            better = vals[n] > v
            j, v = jnp.where(better, n, j), jnp.where(better, vals[n], v)
        return j

    j0 = argmax4(s4)
    j1 = argmax4([jnp.where(j0 == n, -jnp.inf, s4[n]) for n in range(EXPERTS_PER_GROUP)])
    take = lambda j: sum(jnp.where(j == n, a4[n], 0.0) for n in range(EXPERTS_PER_GROUP))
    w0, w1 = take(j0), take(j1)
    total = w0 + w1
    e0 = best * EXPERTS_PER_GROUP + j0
    e1 = best * EXPERTS_PER_GROUP + j1
    rid = lax.broadcasted_iota(I32, (SUBLANES, tm), 0)
    e_ref[...] = jnp.where(rid == 0, e0, jnp.where(rid == 1, e1, 0))
    rid = lax.broadcasted_iota(I32, (LANES, tm), 0)
    rows = jnp.where(rid == 0, w0 / total, jnp.where(rid == 1, w1 / total, 0.0))
    r_ref[...] = rows.T


def _router(h, wr_perm, bias_perm, tm):
    t, d = h.shape
    return pl.pallas_call(
        _router_body,
        grid=(t // tm,),
        in_specs=[pl.BlockSpec((tm, d), lambda i: (i, 0)), pl.BlockSpec((d, 2 * LANES), lambda i: (0, 0)),
                  pl.BlockSpec((N_EXPERTS, 1), lambda i: (0, 0))],
        out_specs=[pl.BlockSpec((SUBLANES, tm), lambda i: (0, i)), pl.BlockSpec((tm, LANES), lambda i: (i, 0))],
        out_shape=[jax.ShapeDtypeStruct((SUBLANES, t), I32), jax.ShapeDtypeStruct((t, LANES), F32)],
        compiler_params=_params("parallel"),
        name="router",
    )(h, wr_perm, bias_perm)


def _dispatch_plan(e01, tm):
    t = e01.shape[1]
    flat = e01.reshape(-1)
    onehot = (flat[:, None] == jnp.arange(N_EXPERTS, dtype=I32)[None, :]).astype(I32)
    cum = jnp.cumsum(onehot, axis=0)
    rank = jnp.sum(onehot * cum, axis=1) - 1
    counts = cum[-1]
    padded = ((counts + tm - 1) // tm) * tm
    starts = jnp.concatenate([jnp.zeros((1,), I32), jnp.cumsum(padded).astype(I32)])
    pos = jnp.sum(onehot * starts[None, :N_EXPERTS], axis=1) + rank
    return starts, starts[:N_EXPERTS] + counts, pos[:t], pos[t:]


def _dispatch_body(p0_ref, p1_ref, gs_ref, ge_ref, h_ref, r_ref, xs_hbm, xbuf, zbuf, sem, zsem, *, tm, tt,
                   n_steps):
    i = pl.program_id(0)
    slot = i % 2
    n_tail = (xs_hbm.shape[0] - gs_ref[N_EXPERTS]) // tm

    def zero_copy(row0, size):
        return pltpu.make_async_copy(zbuf.at[pl.ds(0, size), :], xs_hbm.at[pl.ds(row0, size), :], zsem.at[0])

    def for_each_fill(fn):
        def group_pad(e, carry):
            end = gs_ref[e + 1]
            pad = end - ge_ref[e]
            covered = 0
            size = tm // 2
            while size >= SUBLANES:
                take = pad & size

                @pl.when(take != 0)
                def _():
                    fn(zero_copy(pl.multiple_of(end - covered - size, size), size))

                covered = covered + take
                size //= 2
            for k in range(SUBLANES - 1):
                @pl.when(k < (pad & (SUBLANES - 1)))
                def _():
                    fn(zero_copy(ge_ref[e] + k, 1))
            return carry

        def tail_tile(k, carry):
            fn(zero_copy(pl.multiple_of(gs_ref[N_EXPERTS] + k * tm, tm), tm))
            return carry

        lax.fori_loop(0, N_EXPERTS, group_pad, 0)
        lax.fori_loop(0, n_tail, tail_tile, 0)

    def wait_rows(sl):
        for _ in range(2):
            pltpu.make_async_copy(xbuf.at[sl, 0], xs_hbm.at[pl.ds(0, tt), :], sem.at[sl]).wait()

    @pl.when(i == 0)
    def _():
        zbuf[...] = jnp.zeros(zbuf.shape, F32)
        for_each_fill(lambda cp: cp.start())

    @pl.when(i >= 2)
    def _():
        wait_rows(slot)

    d = h_ref.shape[1]
    gates = r_ref[...]
    lane = lax.broadcasted_iota(I32, gates.shape, 1)
    for k in range(MOE_TOPK):
        xbuf[slot, k, :, :d] = h_ref[...]
    xbuf[slot, 0, :, d:] = jnp.where(lane == 0, gates, jnp.where(lane == 1, ALPHA, 0.0))
    xbuf[slot, 1, :, d:] = jnp.where(lane == 0, pltpu.roll(gates, LANES - 1, axis=1), 0.0)
    base = i * tt
    for r in range(tt):
        pltpu.make_async_copy(xbuf.at[slot, 0, pl.ds(r, 1), :], xs_hbm.at[pl.ds(p0_ref[base + r], 1), :],
                              sem.at[slot]).start(priority=0)
        pltpu.make_async_copy(xbuf.at[slot, 1, pl.ds(r, 1), :], xs_hbm.at[pl.ds(p1_ref[base + r], 1), :],
                              sem.at[slot]).start(priority=1)

    @pl.when(i == n_steps - 1)
    def _():
        wait_rows(slot)
        if n_steps > 1:
            wait_rows(1 - slot)
        for_each_fill(lambda cp: cp.wait())


def _dispatch(h, gates, starts, ends, pos0, pos1, tm, tt):
    t, d = h.shape
    dx = d + LANES
    n_steps = t // tt
    n_rows = 2 * t + N_EXPERTS * tm
    body = functools.partial(_dispatch_body, tm=tm, tt=tt, n_steps=n_steps)
    grid_spec = pltpu.PrefetchScalarGridSpec(
        num_scalar_prefetch=4,
        grid=(n_steps,),
        in_specs=[pl.BlockSpec((tt, d), lambda i, *_: (i, 0)), pl.BlockSpec((tt, LANES), lambda i, *_: (i, 0))],
        out_specs=pl.BlockSpec(memory_space=pl.ANY),
        scratch_shapes=[pltpu.VMEM((2, MOE_TOPK, tt, dx), F32), pltpu.VMEM((tm, dx), F32),
                        pltpu.SemaphoreType.DMA((2,)), pltpu.SemaphoreType.DMA((1,))],
    )
    return pl.pallas_call(
        body,
        grid_spec=grid_spec,
        out_shape=jax.ShapeDtypeStruct((n_rows, dx), F32),
        compiler_params=_params("arbitrary"),
        name="dispatch",
    )(pos0, pos1, starts, ends, h, gates)


def _experts_body(gs_ref, xs_hbm, wgu_ref, wd_ref, ys_hbm, xbuf, ybuf, wgu_bf, wd_bf, xsem, osem, *, tm):
    e = pl.program_id(0)
    first = gs_ref[e] // tm
    n_tiles = gs_ref[e + 1] // tm - first
    total = gs_ref[N_EXPERTS] // tm
    ring = xbuf.shape[0]

    def in_copy(g, slot):
        row0 = pl.multiple_of(g * tm, tm)
        return pltpu.make_async_copy(xs_hbm.at[pl.ds(row0, tm), :], xbuf.at[slot], xsem.at[slot])

    def out_copy(g, slot):
        row0 = pl.multiple_of(g * tm, tm)
        return pltpu.make_async_copy(ybuf.at[slot], ys_hbm.at[pl.ds(row0, tm), :], osem.at[slot])

    @pl.when(e == 0)
    def _():
        for g in range(ring - 1):
            in_copy(jnp.minimum(g, total - 1), g).start(priority=1)
        ybuf[...] = jnp.zeros(ybuf.shape, F32)
        out_copy(0, 0).start()
        out_copy(1, 1).start()

    @pl.when(n_tiles > 0)
    def _():
        wgu_bf[...] = wgu_ref[0, 0].astype(BF)
        wd_bf[...] = wd_ref[0, 0].astype(BF)

        def tile(k, carry):
            g = first + k
            slot = g % 2
            xslot = g % ring
            in_copy(g, xslot).wait()
            out_copy(g, slot).wait()
            in_copy(jnp.minimum(g + ring - 1, total - 1), (g + ring - 1) % ring).start(priority=1)
            d = ybuf.shape[2]
            x = xbuf[xslot, :, :d]
            row_gate = xbuf[xslot, :, d:d + 1]
            row_residual = xbuf[xslot, :, d + 1:d + 2]
            gu = jnp.dot(x.astype(BF), wgu_bf[...], preferred_element_type=F32)
            gate, up = gu[:, :EXPERT_FF], gu[:, EXPERT_FF:]
            hidden = (gate * jax.nn.sigmoid(gate) * up).astype(BF)
            y = jnp.dot(hidden, wd_bf[...], preferred_element_type=F32)
            ybuf[slot] = row_gate * y + row_residual * x
            out_copy(g, slot).start(priority=1)
            return carry

        lax.fori_loop(0, n_tiles, tile, 0)

    @pl.when(e == N_EXPERTS - 1)
    def _():
        for n in range(ring - 1):
            in_copy(0, (total + n) % ring).wait()
        out_copy(0, 0).wait()
        out_copy(1, 1).wait()
        end = gs_ref[N_EXPERTS]
        n_tail = (ys_hbm.shape[0] - end) // tm
        ybuf[0] = jnp.zeros(ybuf.shape[1:], F32)

        def tail_copy(k):
            row0 = pl.multiple_of(end + k * tm, tm)
            return pltpu.make_async_copy(ybuf.at[0], ys_hbm.at[pl.ds(row0, tm), :], osem.at[0])

        def start_one(k, carry):
            tail_copy(k).start()
            return carry

        def wait_one(k, carry):
            tail_copy(k).wait()
            return carry

        lax.fori_loop(0, n_tail, start_one, 0)
        lax.fori_loop(0, n_tail, wait_one, 0)


EXPERT_IN_RING = 4


def _experts(xs, starts, w_gate_up, w_down, layer, tm):
    n_rows, dx = xs.shape
    d = dx - LANES
    body = functools.partial(_experts_body, tm=tm)
    grid_spec = pltpu.PrefetchScalarGridSpec(
        num_scalar_prefetch=1,
        grid=(N_EXPERTS,),
        in_specs=[pl.BlockSpec(memory_space=pl.ANY),
                  pl.BlockSpec((1, 1, d, 2 * EXPERT_FF), lambda e, gs: (layer, e, 0, 0)),
                  pl.BlockSpec((1, 1, EXPERT_FF, d), lambda e, gs: (layer, e, 0, 0))],
        out_specs=pl.BlockSpec(memory_space=pl.ANY),
        scratch_shapes=[pltpu.VMEM((EXPERT_IN_RING, tm, dx), F32), pltpu.VMEM((2, tm, d), F32),
                        pltpu.VMEM((d, 2 * EXPERT_FF), BF), pltpu.VMEM((EXPERT_FF, d), BF),
                        pltpu.SemaphoreType.DMA((EXPERT_IN_RING,)), pltpu.SemaphoreType.DMA((2,))],
    )
    return pl.pallas_call(
        body,
        grid_spec=grid_spec,
        out_shape=jax.ShapeDtypeStruct((n_rows, d), F32),
        compiler_params=_params("arbitrary"),
        name="experts",
    )(starts, xs, w_gate_up, w_down)


def _combine_body(p0_ref, p1_ref, ys_hbm, g_ref, beta_ref, o_ref, buf0, buf1, sem, *, tc, n_steps):
    i = pl.program_id(0)
    slot = i % 2

    def start_row(base, r, sl):
        pltpu.make_async_copy(ys_hbm.at[pl.ds(p0_ref[base + r], 1), :], buf0.at[sl, pl.ds(r, 1), :],
                              sem.at[sl]).start(priority=0)
        pltpu.make_async_copy(ys_hbm.at[pl.ds(p1_ref[base + r], 1), :], buf1.at[sl, pl.ds(r, 1), :],
                              sem.at[sl]).start(priority=1)

    def wait_rows(sl):
        pltpu.make_async_copy(ys_hbm.at[pl.ds(0, tc), :], buf0.at[sl], sem.at[sl]).wait()
        pltpu.make_async_copy(ys_hbm.at[pl.ds(0, tc), :], buf1.at[sl], sem.at[sl]).wait()

    @pl.when(i == 0)
    def _():
        def one(r, carry):
            start_row(0, r, 0)
            return carry

        lax.fori_loop(0, tc, one, 0)

    wait_rows(slot)
    nxt = jnp.minimum(i + 1, n_steps - 1) * tc
    for r in range(tc):
        start_row(nxt, r, 1 - slot)
    o_ref[...] = _layer_norm(buf0[slot] + buf1[slot], g_ref[...], beta_ref[...])

    @pl.when(i == n_steps - 1)
    def _():
        wait_rows(1 - slot)


def _combine_ln(ys, pos0, pos1, g, beta, tc):
    t, d = pos0.shape[0], ys.shape[1]
    n_steps = t // tc
    body = functools.partial(_combine_body, tc=tc, n_steps=n_steps)
    row = lambda i, p0, p1: (i, 0)
    const = lambda i, p0, p1: (0, 0)
    grid_spec = pltpu.PrefetchScalarGridSpec(
        num_scalar_prefetch=2,
        grid=(n_steps,),
        in_specs=[pl.BlockSpec(memory_space=pl.ANY), pl.BlockSpec((1, d), const), pl.BlockSpec((1, d), const)],
        out_specs=pl.BlockSpec((tc, d), row),
        scratch_shapes=[pltpu.VMEM((2, tc, d), F32), pltpu.VMEM((2, tc, d), F32), pltpu.SemaphoreType.DMA((2,))],
    )
    return pl.pallas_call(
        body,
        grid_spec=grid_spec,
        out_shape=jax.ShapeDtypeStruct((t, d), F32),
        compiler_params=_params("arbitrary"),
        name="combine_ln",
    )(pos0, pos1, ys, g, beta)


def _moe_ln(h, wr_perm, bias_perm, w_gate_up, w_down, layer, g, beta, tm_router=512, tm_expert=256, tc=256):
    e01, gates = _router(h, wr_perm, bias_perm, tm_router)
    starts, ends, pos0, pos1 = _dispatch_plan(e01[:MOE_TOPK], tm_expert)
    xs = _dispatch(h, gates, starts, ends, pos0, pos1, tm_expert, tc)
    ys = _experts(xs, starts, w_gate_up, w_down, layer, tm_expert)
    return _combine_ln(ys, pos0, pos1, g, beta, tc)


def _swap_halves(w):
    half = w.shape[-1] // 2
    return jnp.concatenate([w[..., half:], w[..., :half]], axis=-1)


def _even_weights(w_in, w_q_up, w_kv_up, forget_bias):
    d = w_in.shape[0]
    o_kv = MLA_Q_LORA
    o_kr = o_kv + MLA_KV_LORA
    o_fq = o_kr + MLA_ROPE
    hd = FOX_HEADS * FOX_HEAD_DIM
    o_fl = o_fq + 3 * hd
    k_r = w_in[:, o_kr:o_fq]
    f_l = jnp.pad(w_in[:, o_fl:], ((0, 0), (0, LANES - FOX_HEADS)))
    w_small = jnp.concatenate([w_in[:, :o_kr], k_r, _swap_halves(k_r), f_l], axis=1).astype(BF)
    w_fox = w_in[:, o_fq:o_fl].astype(BF)
    cs_fox = jnp.concatenate([jnp.full((1, hd), FOX_SCALE, F32), jnp.ones((1, 2 * hd), F32)], axis=1)
    wq = w_q_up.reshape(MLA_Q_LORA, MLA_HEADS, MLA_NOPE + MLA_ROPE)
    wqn = wq[:, :, :MLA_NOPE].reshape(MLA_Q_LORA, -1).astype(BF)
    pe = wq[:, :, MLA_NOPE:]
    wqp = jnp.concatenate([pe, _swap_halves(pe)], axis=-1).reshape(MLA_Q_LORA, -1).astype(BF)
    wkv = w_kv_up.reshape(MLA_KV_LORA, MLA_HEADS, MLA_NOPE + MLA_V)
    wk = wkv[:, :, :MLA_NOPE].reshape(MLA_KV_LORA, -1).astype(BF)
    wv = wkv[:, :, MLA_NOPE:].reshape(MLA_KV_LORA, -1).astype(BF)
    fb = jnp.pad(forget_bias.astype(F32), (0, LANES - FOX_HEADS)).reshape(1, LANES)
    return w_small, w_fox, cs_fox, wqn, wqp, wk, wv, fb


def _rope_table(seq):
    half = MLA_ROPE // 2
    inv_freq = ROPE_THETA ** (-jnp.arange(half, dtype=F32) / half)
    ang = jnp.arange(seq).astype(F32)[:, None] * inv_freq[None, :]
    cos, sin = jnp.cos(ang), jnp.sin(ang)
    return jnp.concatenate([cos, cos, -sin, sin], axis=1)


def _router_weights(w_router, router_bias):
    r = np.arange(N_EXPERTS)
    perm = (r % N_GROUPS) * EXPERTS_PER_GROUP + r // N_GROUPS
    wr = jnp.pad(w_router[:, perm], ((0, 0), (0, LANES - N_EXPERTS)))
    w_hi = wr.astype(BF)
    w_lo = (wr - w_hi.astype(F32)).astype(BF)
    return jnp.concatenate([w_hi, w_lo], axis=1), router_bias.astype(F32)[perm].reshape(N_EXPERTS, 1)


def _even_layer(h, batch, seq, w_in, q_norm, w_q_up, kv_norm, w_kv_up, forget_bias, w_out, g, beta):
    w_small, w_fox, cs_fox, wqn, wqp, wk, wv, fb = _even_weights(w_in, w_q_up, w_kv_up, forget_bias)
    fox_qkv = _mm(h, w_fox, cs_fox, BF, 512, 1536)
    qn, qp, kn, v, kp, forget_logits = _mla_prep(h, w_small, _rope_table(seq), q_norm.reshape(1, -1),
                                                 kv_norm.reshape(1, -1), wqn, wqp, wk, wv, seq, 512)
    q_bias, k_bias, edge = _fox_cum(forget_logits, fb, batch, seq, 512)
    a = _mla_attention(qn, qp, kn, v, kp, batch, seq, 512, 4)
    bo = _fox_attention(fox_qkv, q_bias, k_bias, edge, batch, seq, 512, 4)
    return _outproj_ln(a, bo, w_out.astype(BF), h, g, beta, 512)


def _odd_layer(h, batch, seq, w_in, sinks, w_out, g, beta):
    n_sq = SWA_Q_HEADS * SWA_HEAD_DIM
    n_skv = SWA_KV_HEADS * SWA_HEAD_DIM
    n_m = MOBA_HEADS * MOBA_HEAD_DIM
    o_mq = n_sq + 2 * n_skv
    w_big = jnp.concatenate([w_in[:, :n_sq], w_in[:, o_mq:], w_in[:, n_sq:o_mq]], axis=1).astype(BF)
    cs = jnp.concatenate([jnp.full((1, n_sq), SWA_SCALE, F32), jnp.full((1, n_m), MOBA_SCALE, F32),
                          jnp.ones((1, 2 * n_m + 2 * n_skv), F32)], axis=1)
    big = _mm(h, w_big, cs, BF, 512, 2176)
    blocks = lambda cols: cols // LANES
    k_mean = _moba_kmean(big, batch, seq, 2)
    slopes = jnp.asarray(LOG2E * 2.0 ** (-8.0 * np.arange(1, MOBA_HEADS + 1) / MOBA_HEADS), dtype=F32)
    c = _swa_attention(big, sinks.astype(F32), batch, seq, 0, blocks(n_sq + 3 * n_m), blocks(n_sq + 3 * n_m) + 1)
    dd = _moba_attention(big, k_mean, slopes, batch, seq, blocks(n_sq), blocks(n_sq + n_m), blocks(n_sq + 2 * n_m),
                         8)
    return _outproj_ln(c, dd, w_out.astype(BF), h, g, beta, 512)


def kernel(x, w_router, router_bias, even_w_in, even_q_norm, even_w_q_up, even_kv_norm, even_w_kv_up,
           even_forget_bias, even_w_out, odd_w_in, odd_sinks, odd_w_out, ln_mix_g, ln_mix_b, ln_ffn_g, ln_ffn_b,
           w_gate_up, w_down):
    batch, seq, d = x.shape
    h = x.reshape(batch * seq, d)
    wr_perm, bias_perm = _router_weights(w_router, router_bias)
    row = lambda p, layer: p[layer].reshape(1, d)
    for layer in range(DEPTH):
        i = layer // 2
        if layer % 2 == 0:
            h = _even_layer(h, batch, seq, even_w_in[i], even_q_norm[i], even_w_q_up[i], even_kv_norm[i],
                            even_w_kv_up[i], even_forget_bias[i], even_w_out[i], row(ln_mix_g, layer),
                            row(ln_mix_b, layer))
        else:
            h = _odd_layer(h, batch, seq, odd_w_in[i], odd_sinks[i], odd_w_out[i], row(ln_mix_g, layer),
                           row(ln_mix_b, layer))
        h = _moe_ln(h, wr_perm, bias_perm, w_gate_up, w_down, layer, row(ln_ffn_g, layer), row(ln_ffn_b, layer))
    return h.reshape(batch, seq, d)
```

```python
import functools

import numpy as np
import jax
import jax.numpy as jnp
from jax import lax
from jax.experimental import pallas as pl
from jax.experimental.pallas import tpu as pltpu

BF = jnp.bfloat16
F32 = jnp.float32
I32 = jnp.int32

LANES = 128
SUBLANES = 8
VMEM_LIMIT = 56 * 1024 * 1024

D_MODEL = 2048
DEPTH = 2
LN_EPS = 1e-5
RMS_EPS = 1e-6
ALPHA = (2 * DEPTH) ** 0.25
MLA_HEADS, MLA_Q_LORA, MLA_KV_LORA, MLA_NOPE, MLA_ROPE, MLA_V = 8, 512, 256, 128, 64, 128
ROPE_THETA = 10000.0
FOX_HEADS, FOX_HEAD_DIM = 8, 128
SWA_Q_HEADS, SWA_KV_HEADS, SWA_HEAD_DIM, SWA_WINDOW = 16, 2, 64, 128
MOBA_HEADS, MOBA_HEAD_DIM, MOBA_BLOCK, MOBA_TOPK = 8, 128, 256, 3
N_EXPERTS, N_GROUPS, MOE_TOPK, EXPERT_FF = 32, 8, 2, 512
EXPERTS_PER_GROUP = N_EXPERTS // N_GROUPS

LOG2E = 1.4426950408889634
MLA_SCALE = (MLA_NOPE + MLA_ROPE) ** -0.5 * LOG2E
FOX_SCALE = FOX_HEAD_DIM ** -0.5 * LOG2E
MOBA_SCALE = MOBA_HEAD_DIM ** -0.5 * LOG2E
SWA_SCALE = SWA_HEAD_DIM ** -0.5 * LOG2E
NEG = -1e30

NT_DIMS = (((1,), (1,)), ((), ()))


def _params(*sem):
    return pltpu.CompilerParams(dimension_semantics=sem, vmem_limit_bytes=VMEM_LIMIT)


def _mm_body(x_ref, w_ref, cs_ref, o_ref):
    acc = jnp.dot(x_ref[...].astype(BF), w_ref[...], preferred_element_type=F32)
    o_ref[...] = (acc * cs_ref[...]).astype(o_ref.dtype)


def _mm(x, w, col_scale, out_dtype, tm, tn):
    m, k = x.shape
    n = w.shape[1]
    return pl.pallas_call(
        _mm_body,
        grid=(n // tn, m // tm),
        in_specs=[
            pl.BlockSpec((tm, k), lambda j, i: (i, 0)),
            pl.BlockSpec((k, tn), lambda j, i: (0, j)),
            pl.BlockSpec((1, tn), lambda j, i: (0, j)),
        ],
        out_specs=pl.BlockSpec((tm, tn), lambda j, i: (i, j)),
        out_shape=jax.ShapeDtypeStruct((m, n), out_dtype),
        compiler_params=_params("parallel", "parallel"),
        name="proj",
    )(x, w, col_scale)


def _rms(x, g):
    return x * lax.rsqrt(jnp.mean(x * x, axis=-1, keepdims=True) + RMS_EPS) * g


def _rope_pair(slab, table):
    r = slab * table
    return r + pltpu.roll(r, MLA_ROPE, axis=1)


def _mla_prep_body(x_ref, ws_ref, rope_ref, qg_ref, kvg_ref, wqn_ref, wqp_ref, wk_ref, wv_ref,
                   qn_ref, qp_ref, kn_ref, v_ref, kp_ref, fl_ref):
    small = jnp.dot(x_ref[...].astype(BF), ws_ref[...], preferred_element_type=F32)
    o_kv, o_kr, o_fl = MLA_Q_LORA, MLA_Q_LORA + MLA_KV_LORA, MLA_Q_LORA + MLA_KV_LORA + LANES
    fl_ref[...] = small[:, o_fl:]
    table = rope_ref[...]
    cqn = _rms(small[:, :o_kv], qg_ref[...]).astype(BF)
    qn_ref[...] = (jnp.dot(cqn, wqn_ref[...], preferred_element_type=F32) * MLA_SCALE).astype(BF)
    qp = jnp.dot(cqn, wqp_ref[...], preferred_element_type=F32)
    for h in range(MLA_HEADS):
        sl = slice(h * LANES, (h + 1) * LANES)
        qp_ref[:, sl] = (_rope_pair(qp[:, sl], table) * MLA_SCALE).astype(BF)
    ckvn = _rms(small[:, o_kv:o_kr], kvg_ref[...]).astype(BF)
    kn_ref[...] = jnp.dot(ckvn, wk_ref[...], preferred_element_type=F32).astype(BF)
    v_ref[...] = jnp.dot(ckvn, wv_ref[...], preferred_element_type=F32).astype(BF)
    kr = _rope_pair(small[:, o_kr:o_fl], table)
    lane = lax.broadcasted_iota(I32, kr.shape, 1)
    kp_ref[...] = jnp.where(lane < MLA_ROPE, kr, 0.0).astype(BF)


def _mla_prep(x, w_small, rope_table, q_norm, kv_norm, wqn, wqp, wk, wv, seq, tm):
    t, d = x.shape
    n_s = seq // tm
    hd = MLA_HEADS * LANES
    row = lambda i: (i, 0)
    const = lambda i: (0, 0)
    out = lambda w: pl.BlockSpec((tm, w), row)
    return pl.pallas_call(
        _mla_prep_body,
        grid=(t // tm,),
        in_specs=[
            pl.BlockSpec((tm, d), row),
            pl.BlockSpec(w_small.shape, const),
            pl.BlockSpec((tm, LANES), lambda i: (i % n_s, 0)),
            pl.BlockSpec((1, MLA_Q_LORA), const),
            pl.BlockSpec((1, MLA_KV_LORA), const),
            pl.BlockSpec((MLA_Q_LORA, hd), const),
            pl.BlockSpec((MLA_Q_LORA, hd), const),
            pl.BlockSpec((MLA_KV_LORA, hd), const),
            pl.BlockSpec((MLA_KV_LORA, hd), const),
        ],
        out_specs=[out(hd), out(hd), out(hd), out(hd), out(LANES), out(LANES)],
        out_shape=[jax.ShapeDtypeStruct((t, hd), BF)] * 4 + [jax.ShapeDtypeStruct((t, LANES), BF),
                                                             jax.ShapeDtypeStruct((t, LANES), F32)],
        compiler_params=_params("parallel"),
        name="mla_prep",
    )(x, w_small, rope_table, q_norm, kv_norm, wqn, wqp, wk, wv)


N_SPLIT = 3


def _split_piece(x, which):
    hi = x.astype(BF).astype(F32)
    rest = x - hi
    mid = rest.astype(BF).astype(F32)
    return jnp.where(which == 0, hi, jnp.where(which == 1, mid, rest - mid))


def _fox_cum_body(fl_ref, fb_ref, spread_ref, qe_ref, ke_ref, edge_ref, carry_ref):
    @pl.when(pl.program_id(1) == 0)
    def _():
        carry_ref[...] = jnp.zeros_like(carry_ref)

    z = fl_ref[...] + fb_ref[...]
    log_f = jnp.minimum(z, 0.0) - jnp.log1p(jnp.exp(-jnp.abs(z)))
    tm = z.shape[0]
    r = lax.broadcasted_iota(I32, (tm, tm), 0)
    c = lax.broadcasted_iota(I32, (tm, tm), 1)
    tri = jnp.where(r >= c, 1.0, 0.0).astype(BF)
    parts = jnp.dot(tri, jnp.concatenate([_split_piece(log_f, n).astype(BF) for n in range(N_SPLIT)], axis=1),
                    preferred_element_type=F32)
    cum = (parts[:, :LANES] + parts[:, LANES:2 * LANES]) + parts[:, 2 * LANES:] + carry_ref[...]
    carry_ref[...] = cum[tm - 1:tm, :]
    cum = cum * LOG2E
    width = qe_ref.shape[1]
    pieces = jnp.concatenate([_split_piece(cum, n).astype(BF) for n in range(N_SPLIT)], axis=1)
    placed = jnp.dot(pieces, spread_ref[...], preferred_element_type=F32)
    li = lax.broadcasted_iota(I32, (tm, width), 1) % LANES
    qe_ref[...] = jnp.where((li >= N_SPLIT) & (li < 2 * N_SPLIT), 1.0, placed[:, :width]).astype(BF)
    ke_ref[...] = jnp.where(li < N_SPLIT, 1.0, placed[:, width:]).astype(BF)
    edge_ref[0, 0:1, :] = cum[0:1, :]
    edge_ref[0, 1:2, :] = cum[tm - 1:tm, :]


def _fox_spread():
    width = FOX_HEADS * LANES
    m = np.zeros((N_SPLIT * LANES, 2 * width), np.float32)
    for j in range(N_SPLIT):
        for h in range(FOX_HEADS):
            m[j * LANES + h, h * LANES + j] = 1.0
            m[j * LANES + h, width + h * LANES + N_SPLIT + j] = -1.0
    return jnp.asarray(m, dtype=BF)


def _fox_cum(forget_logits, forget_bias_row, batch, seq, tm):
    t = forget_logits.shape[0]
    n_s = seq // tm
    width = FOX_HEADS * LANES
    q_bias, k_bias, edge = pl.pallas_call(
        _fox_cum_body,
        grid=(batch, n_s),
        in_specs=[
            pl.BlockSpec((tm, LANES), lambda b, i: (b * n_s + i, 0)),
            pl.BlockSpec((1, LANES), lambda b, i: (0, 0)),
            pl.BlockSpec((N_SPLIT * LANES, 2 * width), lambda b, i: (0, 0)),
        ],
        out_specs=[
            pl.BlockSpec((tm, width), lambda b, i: (b * n_s + i, 0)),
            pl.BlockSpec((tm, width), lambda b, i: (b * n_s + i, 0)),
            pl.BlockSpec((1, 2, LANES), lambda b, i: (b * n_s + i, 0, 0)),
        ],
        out_shape=[jax.ShapeDtypeStruct((t, width), BF), jax.ShapeDtypeStruct((t, width), BF),
                   jax.ShapeDtypeStruct((batch * n_s, 2, LANES), F32)],
        scratch_shapes=[pltpu.VMEM((1, LANES), F32)],
        compiler_params=_params("parallel", "arbitrary"),
        name="fox_cum",
    )(forget_logits, forget_bias_row, _fox_spread())
    return q_bias, k_bias, edge[:, :, :FOX_HEADS].reshape(-1)


def _softmax_update(s_all, v_all, m_ref, l_ref, acc_ref):
    heads = range(len(s_all))
    m_prev = [m_ref[g] for g in heads]
    m_new = [jnp.maximum(m_prev[g], jnp.max(s_all[g], axis=1, keepdims=True)) for g in heads]
    reps = s_all[0].shape[1] // LANES
    p = [jnp.exp2(s_all[g] - jnp.concatenate([m_new[g]] * reps, axis=1)) for g in heads]
    pv = [jnp.dot(p[g].astype(BF), v_all[g], preferred_element_type=F32) for g in heads]
    for g in heads:
        alpha = jnp.exp2(m_prev[g] - m_new[g])
        l_ref[g] = alpha * l_ref[g] + jnp.sum(p[g], axis=1, keepdims=True)
        acc_ref[g] = alpha * acc_ref[g] + pv[g]
        m_ref[g] = m_new[g]


def _attn_body(*refs, mode, t, group):
    hg = pl.program_id(1)
    i = pl.program_id(2)
    heads = range(group)
    sl = lambda g: slice(g * LANES, (g + 1) * LANES)
    if mode == "moba":
        q_ref, k_ref, kx_ref, v_ref, km_ref, slope_ref, o_ref, m_ref, l_ref, acc_ref, qx_ref = refs
        for g in heads:
            qx_ref[g] = _moba_query_lanes(q_ref[:, sl(g)], km_ref[:, sl(g)], slope_ref[hg * group + g], i, t)
        q_extra = lambda g: qx_ref[g]
        k_extra = lambda g, ks: kx_ref[pl.ds(ks, t), :]
    elif mode == "mla":
        q_ref, qx_ref, k_ref, kx_ref, v_ref, o_ref, m_ref, l_ref, acc_ref = refs
        q_extra = lambda g: qx_ref[:, sl(g)]
        k_extra = lambda g, ks: kx_ref[pl.ds(ks, t), :]
    else:
        q_ref, qx_ref, k_ref, kx_ref, v_ref, edge_ref, o_ref, m_ref, l_ref, acc_ref, knorm_ref = refs
        q_extra = lambda g: qx_ref[:, sl(g)]
        k_extra = lambda g, ks: kx_ref[pl.ds(ks, t), sl(g)]
        first_tile = _fox_first_tile(q_ref, k_ref, edge_ref, knorm_ref, i, hg, t, group)

    def scores(g, ks):
        q = jnp.concatenate([q_ref[:, sl(g)], q_extra(g)], axis=1)
        k = jnp.concatenate([k_ref[pl.ds(ks, t), sl(g)], k_extra(g, ks)], axis=1)
        return lax.dot_general(q, k, NT_DIMS, preferred_element_type=F32)

    m_ref[...] = jnp.full(m_ref.shape, NEG, F32)
    l_ref[...] = jnp.zeros(l_ref.shape, F32)
    acc_ref[...] = jnp.zeros(acc_ref.shape, F32)

    ks = pl.multiple_of(i * t, t)
    r = lax.broadcasted_iota(I32, (t, t), 0)
    c = lax.broadcasted_iota(I32, (t, t), 1)
    values = lambda ks: [v_ref[pl.ds(ks, t), sl(g)] for g in heads]
    _softmax_update([jnp.where(c <= r, scores(g, ks), NEG) for g in heads], values(ks), m_ref, l_ref, acc_ref)

    def past_tile(j, carry):
        ks = pl.multiple_of(j * t, t)
        _softmax_update([scores(g, ks) for g in heads], values(ks), m_ref, l_ref, acc_ref)
        return carry

    lax.fori_loop(first_tile if mode == "fox" else 0, i, past_tile, 0)
    for g in heads:
        o_ref[:, sl(g)] = (acc_ref[g] / l_ref[g]).astype(o_ref.dtype)


FOX_SKIP_GAP = 160.0


def _fox_first_tile(q_ref, k_ref, edge_ref, knorm_ref, i, hg, t, group):
    b = pl.program_id(0)
    n_q = pl.num_programs(2)
    sl = lambda g: slice(g * LANES, (g + 1) * LANES)

    def max_row_norm(x):
        x = x.astype(F32)
        return jnp.sqrt(jnp.max(jnp.sum(x * x, axis=1, keepdims=True)))

    @pl.when(i == 0)
    def _():
        for g in range(group):
            knorm_ref[g] = max_row_norm(k_ref[:, sl(g)])

    def edge(tile, last, g):
        return edge_ref[((b * n_q + tile) * 2 + last) * FOX_HEADS + hg * group + g]

    limit = [-(2.0 * 1.001 * max_row_norm(q_ref[:, sl(g)]) * knorm_ref[g] + FOX_SKIP_GAP) for g in range(group)]
    first = [edge(i, 0, g) for g in range(group)]

    def scan(j, lo):
        needed = first[0] - edge(j, 1, 0) >= limit[0]
        for g in range(1, group):
            needed = needed | (first[g] - edge(j, 1, g) >= limit[g])
        return jnp.where(needed, jnp.minimum(lo, j), lo)

    return lax.fori_loop(0, i, scan, i)


def _attention(mode, batch, seq, heads, t, group, operands, in_specs, extra_scratch=()):
    n_q = seq // t
    body = functools.partial(_attn_body, mode=mode, t=t, group=group)
    return pl.pallas_call(
        body,
        grid=(batch, heads // group, n_q),
        in_specs=in_specs,
        out_specs=pl.BlockSpec((t, group * LANES), lambda b, h, i: (b * n_q + i, h)),
        out_shape=jax.ShapeDtypeStruct((batch * seq, heads * LANES), BF),
        scratch_shapes=[pltpu.VMEM((group, t, LANES), F32), pltpu.VMEM((group, t, LANES), F32),
                        pltpu.VMEM((group, t, LANES), F32), *extra_scratch],
        compiler_params=_params("parallel", "parallel", "arbitrary"),
        name="attn_" + mode,
    )(*operands)


def _q_spec(t, n_q, group, col0=0):
    return pl.BlockSpec((t, group * LANES), lambda b, h, i: (b * n_q + i, col0 + h))


def _kv_spec(seq, group, col0=0):
    return pl.BlockSpec((seq, group * LANES), lambda b, h, i: (b, col0 + h))


def _mla_attention(qn, qp, kn, v, kp, batch, seq, t, group):
    n_q = seq // t
    specs = [_q_spec(t, n_q, group), _q_spec(t, n_q, group), _kv_spec(seq, group),
             pl.BlockSpec((seq, LANES), lambda b, h, i: (b, 0)), _kv_spec(seq, group)]
    return _attention("mla", batch, seq, MLA_HEADS, t, group, (qn, qp, kn, kp, v), specs)


def _fox_attention(qkv, q_bias, k_bias, edge, batch, seq, t, group):
    n_q = seq // t
    n_hg = FOX_HEADS // group
    specs = [_q_spec(t, n_q, group, 0), _q_spec(t, n_q, group, 0), _kv_spec(seq, group, n_hg),
             _kv_spec(seq, group, 0), _kv_spec(seq, group, 2 * n_hg), pl.BlockSpec(memory_space=pltpu.SMEM)]
    return _attention("fox", batch, seq, FOX_HEADS, t, group, (qkv, q_bias, qkv, k_bias, qkv, edge), specs,
                      (pltpu.SMEM((group,), F32),))


def _moba_key_lanes(seq):
    n_kb = seq // MOBA_BLOCK
    kpos = np.arange(seq)
    lanes = np.zeros((seq, LANES), np.float32)
    lanes[kpos, kpos // MOBA_BLOCK] = 1.0
    lanes[:, n_kb:n_kb + N_SPLIT] = 1.0
    lanes[:, n_kb + N_SPLIT:n_kb + 2 * N_SPLIT] = (MOBA_BLOCK * (kpos // MOBA_BLOCK))[:, None]
    lanes[:, n_kb + 2 * N_SPLIT:n_kb + 3 * N_SPLIT] = (kpos % MOBA_BLOCK)[:, None]
    return jnp.asarray(lanes, dtype=BF)


def _moba_query_lanes(q, k_mean, slope, i, t):
    n_kb = k_mean.shape[0]
    gate = lax.dot_general(k_mean.astype(BF), q, NT_DIMS, preferred_element_type=F32)
    blk = lax.broadcasted_iota(I32, (n_kb, t), 0)
    beaten = jnp.zeros((n_kb, t), F32)
    for n in range(n_kb):
        g_n = gate[n:n + 1, :]
        wins = (g_n > gate) | ((g_n == gate) & (blk > n))
        beaten = beaten + jnp.where(wins, 1.0, 0.0) * jnp.where(i > n, 1.0, 0.0)
    attended = (blk == i) | ((blk < i) & (beaten < MOBA_TOPK))
    mask = jnp.concatenate([jnp.where(attended, 0.0, NEG), jnp.zeros((LANES - n_kb, t), F32)], axis=0)
    row = lax.broadcasted_iota(I32, (LANES, t), 0) - n_kb
    qpos = (i * t + lax.broadcasted_iota(I32, (1, t), 1)).astype(F32)
    slope_row = jnp.full((1, t), slope, F32)
    which = row % N_SPLIT
    lanes = jnp.where(row < 0, mask,
                      jnp.where(row < N_SPLIT, _split_piece(-slope_row * qpos, which),
                                jnp.where(row < 3 * N_SPLIT, _split_piece(slope_row, which), 0.0)))
    return lanes.T.astype(BF)


def _moba_attention(big, k_mean, slopes, batch, seq, col_q, col_k, col_v, group):
    t = MOBA_BLOCK
    n_q = seq // t
    specs = [_q_spec(t, n_q, group, col_q // group), _kv_spec(seq, group, col_k // group),
             pl.BlockSpec((seq, LANES), lambda b, h, i: (0, 0)),
             _kv_spec(seq, group, col_v // group),
             pl.BlockSpec((n_q, group * LANES), lambda b, h, i: (b, h)),
             pl.BlockSpec(memory_space=pltpu.SMEM)]
    scratch = (pltpu.VMEM((group, t, LANES), BF),)
    return _attention("moba", batch, seq, MOBA_HEADS, t, group,
                      (big, big, _moba_key_lanes(seq), big, k_mean, slopes), specs, scratch)


def _kmean_body(k_ref, o_ref):
    n_kb = o_ref.shape[0]
    for n in range(n_kb):
        blk = k_ref[n * MOBA_BLOCK:(n + 1) * MOBA_BLOCK, :].astype(F32)
        o_ref[n:n + 1, :] = jnp.mean(blk, axis=0, keepdims=True)


def _moba_kmean(big, batch, seq, col_block):
    n_kb = seq // MOBA_BLOCK
    width = MOBA_HEADS * MOBA_HEAD_DIM
    return pl.pallas_call(
        _kmean_body,
        grid=(batch,),
        in_specs=[pl.BlockSpec((seq, width), lambda b: (b, col_block))],
        out_specs=pl.BlockSpec((n_kb, width), lambda b: (b, 0)),
        out_shape=jax.ShapeDtypeStruct((batch * n_kb, width), F32),
        compiler_params=_params("parallel"),
        name="moba_kmean",
    )(big)


def _swa_body(sink_ref, q_ref, kp_ref, kc_ref, vp_ref, vc_ref, o_ref):
    i = pl.program_id(1)
    w = SWA_WINDOW
    half = SWA_HEAD_DIM
    lane = lax.broadcasted_iota(I32, (w, LANES), 1)
    r = lax.broadcasted_iota(I32, (w, w), 0)
    c = lax.broadcasted_iota(I32, (w, w), 1)
    from_prev = c > r
    dist = (r - c + jnp.where(from_prev, w, 0)).astype(F32)
    valid = jnp.logical_not(from_prev) | (i > 0)
    heads_per_kv = SWA_Q_HEADS // SWA_KV_HEADS

    def lo_hi(ref, kvh):
        own = jnp.where((lane < half) == (kvh == 0), ref[...].astype(F32), 0.0)
        other = pltpu.roll(own, half, axis=1)
        lo, hi = (own, other) if kvh == 0 else (other, own)
        return lo.astype(BF), hi.astype(BF)

    for kvh in range(SWA_KV_HEADS):
        kp, kc, vp, vc = (lo_hi(ref, kvh) for ref in (kp_ref, kc_ref, vp_ref, vc_ref))
        heads = range(kvh * heads_per_kv, (kvh + 1) * heads_per_kv)
        q = {h: q_ref[:, (h // 2) * LANES:(h // 2 + 1) * LANES] for h in heads}
        s = {h: jnp.where(from_prev,
                          lax.dot_general(q[h], kp[h % 2], NT_DIMS, preferred_element_type=F32),
                          lax.dot_general(q[h], kc[h % 2], NT_DIMS, preferred_element_type=F32)) for h in heads}
        p, inv = {}, {}
        for h in heads:
            slope = float(LOG2E * 2.0 ** (-8.0 * (h + 1) / SWA_Q_HEADS))
            sink = sink_ref[h] * LOG2E
            logits = jnp.where(valid, s[h] - slope * dist, NEG)
            m = jnp.maximum(jnp.max(logits, axis=1, keepdims=True), sink)
            p[h] = jnp.exp2(logits - m)
            inv[h] = 1.0 / (jnp.sum(p[h], axis=1, keepdims=True) + jnp.exp2(sink - m))
        for pair in range(kvh * heads_per_kv // 2, (kvh + 1) * heads_per_kv // 2):
            out = jnp.zeros((w, LANES), F32)
            for h in (2 * pair, 2 * pair + 1):
                pn = p[h] * inv[h]
                out = out + jnp.dot(jnp.where(from_prev, pn, 0.0).astype(BF), vp[h % 2], preferred_element_type=F32)
                out = out + jnp.dot(jnp.where(from_prev, 0.0, pn).astype(BF), vc[h % 2], preferred_element_type=F32)
            o_ref[:, pair * LANES:(pair + 1) * LANES] = out.astype(o_ref.dtype)


def _swa_attention(big, sinks, batch, seq, col_q, col_k, col_v):
    w = SWA_WINDOW
    n_q = seq // w
    width = SWA_Q_HEADS * SWA_HEAD_DIM
    prev = lambda col: pl.BlockSpec((w, LANES), lambda b, i: (b * n_q + jnp.maximum(i - 1, 0), col))
    cur = lambda col: pl.BlockSpec((w, LANES), lambda b, i: (b * n_q + i, col))
    return pl.pallas_call(
        _swa_body,
        grid=(batch, n_q),
        in_specs=[pl.BlockSpec(memory_space=pltpu.SMEM),
                  pl.BlockSpec((w, width), lambda b, i: (b * n_q + i, col_q)),
                  prev(col_k), cur(col_k), prev(col_v), cur(col_v)],
        out_specs=pl.BlockSpec((w, width), lambda b, i: (b * n_q + i, 0)),
        out_shape=jax.ShapeDtypeStruct((batch * seq, width), BF),
        compiler_params=_params("parallel", "parallel"),
        name="swa",
    )(sinks, big, big, big, big, big)


def _layer_norm(z, g, b):
    mu = jnp.mean(z, axis=-1, keepdims=True)
    zc = z - mu
    var = jnp.mean(zc * zc, axis=-1, keepdims=True)
    return zc * lax.rsqrt(var + LN_EPS) * g + b


def _outproj_body(a_ref, b_ref, wa_ref, wb_ref, h_ref, g_ref, beta_ref, o_ref):
    mix = jnp.dot(a_ref[...], wa_ref[...], preferred_element_type=F32)
    mix = mix + jnp.dot(b_ref[...], wb_ref[...], preferred_element_type=F32)
    o_ref[...] = _layer_norm(ALPHA * h_ref[...] + mix, g_ref[...], beta_ref[...])


def _outproj_ln(a, b, w_out, h, g, beta, tm):
    t, d = h.shape
    ka = a.shape[1]
    row = lambda i: (i, 0)
    return pl.pallas_call(
        _outproj_body,
        grid=(t // tm,),
        in_specs=[pl.BlockSpec((tm, ka), row), pl.BlockSpec((tm, ka), row),
                  pl.BlockSpec((ka, d), lambda i: (0, 0)), pl.BlockSpec((ka, d), lambda i: (1, 0)),
                  pl.BlockSpec((tm, d), row), pl.BlockSpec((1, d), lambda i: (0, 0)),
                  pl.BlockSpec((1, d), lambda i: (0, 0))],
        out_specs=pl.BlockSpec((tm, d), row),
        out_shape=jax.ShapeDtypeStruct((t, d), F32),
        compiler_params=_params("parallel"),
        name="outproj_ln",
    )(a, b, w_out, w_out, h, g, beta)


def _second_of_four(a, b, c, d):
    hi_ab, lo_ab = jnp.maximum(a, b), jnp.minimum(a, b)
    hi_cd, lo_cd = jnp.maximum(c, d), jnp.minimum(c, d)
    return jnp.maximum(jnp.maximum(lo_ab, lo_cd), jnp.minimum(hi_ab, hi_cd))


def _router_body(h_ref, wr_ref, bias_ref, e_ref, r_ref):
    tm = h_ref.shape[0]
    h = h_ref[...]
    h_hi = h.astype(BF)
    h_lo = (h - h_hi.astype(F32)).astype(BF)
    both = jnp.dot(h_hi, wr_ref[...], preferred_element_type=F32)
    logits = (both[:, :LANES] + both[:, LANES:]) + jnp.dot(h_lo, wr_ref[:, :LANES], preferred_element_type=F32)
    lt = logits.T
    aff = [jax.nn.sigmoid(lt[SUBLANES * j:SUBLANES * (j + 1), :]) for j in range(EXPERTS_PER_GROUP)]
    sel = [aff[j] + bias_ref[SUBLANES * j:SUBLANES * (j + 1), :] for j in range(EXPERTS_PER_GROUP)]
    top1 = jnp.maximum(jnp.maximum(sel[0], sel[1]), jnp.maximum(sel[2], sel[3]))
    score = top1 + _second_of_four(*sel)
    gid = lax.broadcasted_iota(I32, (N_GROUPS, tm), 0)
    best = jnp.min(jnp.where(score == jnp.max(score, axis=0, keepdims=True), gid, N_GROUPS),
                   axis=0, keepdims=True)
    in_grp = gid == best
    pick = lambda x: jnp.sum(jnp.where(in_grp, x, 0.0), axis=0, keepdims=True)
    s4 = [pick(x) for x in sel]
    a4 = [pick(x) for x in aff]

    def argmax4(vals):
        j, v = jnp.zeros((1, tm), I32), vals[0]
        for n in range(1, EXPERTS_PER_GROUP):
            better = vals[n] > v
            j, v = jnp.where(better, n, j), jnp.where(better, vals[n], v)
        return j

    j0 = argmax4(s4)
    j1 = argmax4([jnp.where(j0 == n, -jnp.inf, s4[n]) for n in range(EXPERTS_PER_GROUP)])
    take = lambda j: sum(jnp.where(j == n, a4[n], 0.0) for n in range(EXPERTS_PER_GROUP))
    w0, w1 = take(j0), take(j1)
    total = w0 + w1
    e0 = best * EXPERTS_PER_GROUP + j0
    e1 = best * EXPERTS_PER_GROUP + j1
    rid = lax.broadcasted_iota(I32, (SUBLANES, tm), 0)
    e_ref[...] = jnp.where(rid == 0, e0, jnp.where(rid == 1, e1, 0))
    rid = lax.broadcasted_iota(I32, (LANES, tm), 0)
    rows = jnp.where(rid == 0, w0 / total, jnp.where(rid == 1, w1 / total, 0.0))
    r_ref[...] = rows.T


def _router(h, wr_perm, bias_perm, tm):
    t, d = h.shape
    return pl.pallas_call(
        _router_body,
        grid=(t // tm,),
        in_specs=[pl.BlockSpec((tm, d), lambda i: (i, 0)), pl.BlockSpec((d, 2 * LANES), lambda i: (0, 0)),
                  pl.BlockSpec((N_EXPERTS, 1), lambda i: (0, 0))],
        out_specs=[pl.BlockSpec((SUBLANES, tm), lambda i: (0, i)), pl.BlockSpec((tm, LANES), lambda i: (i, 0))],
        out_shape=[jax.ShapeDtypeStruct((SUBLANES, t), I32), jax.ShapeDtypeStruct((t, LANES), F32)],
        compiler_params=_params("parallel"),
        name="router",
    )(h, wr_perm, bias_perm)


def _dispatch_plan(e01, tm):
    t = e01.shape[1]
    flat = e01.reshape(-1)
    onehot = (flat[:, None] == jnp.arange(N_EXPERTS, dtype=I32)[None, :]).astype(I32)
    cum = jnp.cumsum(onehot, axis=0)
    rank = jnp.sum(onehot * cum, axis=1) - 1
    counts = cum[-1]
    padded = ((counts + tm - 1) // tm) * tm
    starts = jnp.concatenate([jnp.zeros((1,), I32), jnp.cumsum(padded).astype(I32)])
    pos = jnp.sum(onehot * starts[None, :N_EXPERTS], axis=1) + rank
    return starts, starts[:N_EXPERTS] + counts, pos[:t], pos[t:]


def _dispatch_body(p0_ref, p1_ref, gs_ref, ge_ref, h_ref, r_ref, xs_hbm, xbuf, zbuf, sem, zsem, *, tm, tt,
                   n_steps):
    i = pl.program_id(0)
    slot = i % 2
    n_tail = (xs_hbm.shape[0] - gs_ref[N_EXPERTS]) // tm

    def zero_copy(row0, size):
        return pltpu.make_async_copy(zbuf.at[pl.ds(0, size), :], xs_hbm.at[pl.ds(row0, size), :], zsem.at[0])

    def for_each_fill(fn):
        def group_pad(e, carry):
            end = gs_ref[e + 1]
            pad = end - ge_ref[e]
            covered = 0
            size = tm // 2
            while size >= SUBLANES:
                take = pad & size

                @pl.when(take != 0)
                def _():
                    fn(zero_copy(pl.multiple_of(end - covered - size, size), size))

                covered = covered + take
                size //= 2
            for k in range(SUBLANES - 1):
                @pl.when(k < (pad & (SUBLANES - 1)))
                def _():
                    fn(zero_copy(ge_ref[e] + k, 1))
            return carry

        def tail_tile(k, carry):
            fn(zero_copy(pl.multiple_of(gs_ref[N_EXPERTS] + k * tm, tm), tm))
            return carry

        lax.fori_loop(0, N_EXPERTS, group_pad, 0)
        lax.fori_loop(0, n_tail, tail_tile, 0)

    def wait_rows(sl):
        for _ in range(2):
            pltpu.make_async_copy(xbuf.at[sl, 0], xs_hbm.at[pl.ds(0, tt), :], sem.at[sl]).wait()

    @pl.when(i == 0)
    def _():
        zbuf[...] = jnp.zeros(zbuf.shape, F32)
        for_each_fill(lambda cp: cp.start())

    @pl.when(i >= 2)
    def _():
        wait_rows(slot)

    d = h_ref.shape[1]
    gates = r_ref[...]
    lane = lax.broadcasted_iota(I32, gates.shape, 1)
    for k in range(MOE_TOPK):
        xbuf[slot, k, :, :d] = h_ref[...]
    xbuf[slot, 0, :, d:] = jnp.where(lane == 0, gates, jnp.where(lane == 1, ALPHA, 0.0))
    xbuf[slot, 1, :, d:] = jnp.where(lane == 0, pltpu.roll(gates, LANES - 1, axis=1), 0.0)
    base = i * tt
    for r in range(tt):
        pltpu.make_async_copy(xbuf.at[slot, 0, pl.ds(r, 1), :], xs_hbm.at[pl.ds(p0_ref[base + r], 1), :],
                              sem.at[slot]).start(priority=0)
        pltpu.make_async_copy(xbuf.at[slot, 1, pl.ds(r, 1), :], xs_hbm.at[pl.ds(p1_ref[base + r], 1), :],
                              sem.at[slot]).start(priority=1)

    @pl.when(i == n_steps - 1)
    def _():
        wait_rows(slot)
        if n_steps > 1:
            wait_rows(1 - slot)
        for_each_fill(lambda cp: cp.wait())


def _dispatch(h, gates, starts, ends, pos0, pos1, tm, tt):
    t, d = h.shape
    dx = d + LANES
    n_steps = t // tt
    n_rows = 2 * t + N_EXPERTS * tm
    body = functools.partial(_dispatch_body, tm=tm, tt=tt, n_steps=n_steps)
    grid_spec = pltpu.PrefetchScalarGridSpec(
        num_scalar_prefetch=4,
        grid=(n_steps,),
        in_specs=[pl.BlockSpec((tt, d), lambda i, *_: (i, 0)), pl.BlockSpec((tt, LANES), lambda i, *_: (i, 0))],
        out_specs=pl.BlockSpec(memory_space=pl.ANY),
        scratch_shapes=[pltpu.VMEM((2, MOE_TOPK, tt, dx), F32), pltpu.VMEM((tm, dx), F32),
                        pltpu.SemaphoreType.DMA((2,)), pltpu.SemaphoreType.DMA((1,))],
    )
    return pl.pallas_call(
        body,
        grid_spec=grid_spec,
        out_shape=jax.ShapeDtypeStruct((n_rows, dx), F32),
        compiler_params=_params("arbitrary"),
        name="dispatch",
    )(pos0, pos1, starts, ends, h, gates)


def _experts_body(gs_ref, xs_hbm, wgu_ref, wd_ref, ys_hbm, xbuf, ybuf, wgu_bf, wd_bf, xsem, osem, *, tm):
    e = pl.program_id(0)
    first = gs_ref[e] // tm
    n_tiles = gs_ref[e + 1] // tm - first
    total = gs_ref[N_EXPERTS] // tm
    ring = xbuf.shape[0]

    def in_copy(g, slot):
        row0 = pl.multiple_of(g * tm, tm)
        return pltpu.make_async_copy(xs_hbm.at[pl.ds(row0, tm), :], xbuf.at[slot], xsem.at[slot])

    def out_copy(g, slot):
        row0 = pl.multiple_of(g * tm, tm)
        return pltpu.make_async_copy(ybuf.at[slot], ys_hbm.at[pl.ds(row0, tm), :], osem.at[slot])

    @pl.when(e == 0)
    def _():
        for g in range(ring - 1):
            in_copy(jnp.minimum(g, total - 1), g).start(priority=1)
        ybuf[...] = jnp.zeros(ybuf.shape, F32)
        out_copy(0, 0).start()
        out_copy(1, 1).start()

    @pl.when(n_tiles > 0)
    def _():
        wgu_bf[...] = wgu_ref[0, 0].astype(BF)
        wd_bf[...] = wd_ref[0, 0].astype(BF)

        def tile(k, carry):
            g = first + k
            slot = g % 2
            xslot = g % ring
            in_copy(g, xslot).wait()
            out_copy(g, slot).wait()
            in_copy(jnp.minimum(g + ring - 1, total - 1), (g + ring - 1) % ring).start(priority=1)
            d = ybuf.shape[2]
            x = xbuf[xslot, :, :d]
            row_gate = xbuf[xslot, :, d:d + 1]
            row_residual = xbuf[xslot, :, d + 1:d + 2]
            gu = jnp.dot(x.astype(BF), wgu_bf[...], preferred_element_type=F32)
            gate, up = gu[:, :EXPERT_FF], gu[:, EXPERT_FF:]
            hidden = (gate * jax.nn.sigmoid(gate) * up).astype(BF)
            y = jnp.dot(hidden, wd_bf[...], preferred_element_type=F32)
            ybuf[slot] = row_gate * y + row_residual * x
            out_copy(g, slot).start(priority=1)
            return carry

        lax.fori_loop(0, n_tiles, tile, 0)

    @pl.when(e == N_EXPERTS - 1)
    def _():
        for n in range(ring - 1):
            in_copy(0, (total + n) % ring).wait()
        out_copy(0, 0).wait()
        out_copy(1, 1).wait()
        end = gs_ref[N_EXPERTS]
        n_tail = (ys_hbm.shape[0] - end) // tm
        ybuf[0] = jnp.zeros(ybuf.shape[1:], F32)

        def tail_copy(k):
            row0 = pl.multiple_of(end + k * tm, tm)
            return pltpu.make_async_copy(ybuf.at[0], ys_hbm.at[pl.ds(row0, tm), :], osem.at[0])

        def start_one(k, carry):
            tail_copy(k).start()
            return carry

        def wait_one(k, carry):
            tail_copy(k).wait()
            return carry

        lax.fori_loop(0, n_tail, start_one, 0)
        lax.fori_loop(0, n_tail, wait_one, 0)


EXPERT_IN_RING = 6


def _experts(xs, starts, w_gate_up, w_down, layer, tm):
    n_rows, dx = xs.shape
    d = dx - LANES
    body = functools.partial(_experts_body, tm=tm)
    grid_spec = pltpu.PrefetchScalarGridSpec(
        num_scalar_prefetch=1,
        grid=(N_EXPERTS,),
        in_specs=[pl.BlockSpec(memory_space=pl.ANY),
                  pl.BlockSpec((1, 1, d, 2 * EXPERT_FF), lambda e, gs: (layer, e, 0, 0)),
                  pl.BlockSpec((1, 1, EXPERT_FF, d), lambda e, gs: (layer, e, 0, 0))],
        out_specs=pl.BlockSpec(memory_space=pl.ANY),
        scratch_shapes=[pltpu.VMEM((EXPERT_IN_RING, tm, dx), F32), pltpu.VMEM((2, tm, d), F32),
                        pltpu.VMEM((d, 2 * EXPERT_FF), BF), pltpu.VMEM((EXPERT_FF, d), BF),
                        pltpu.SemaphoreType.DMA((EXPERT_IN_RING,)), pltpu.SemaphoreType.DMA((2,))],
    )
    return pl.pallas_call(
        body,
        grid_spec=grid_spec,
        out_shape=jax.ShapeDtypeStruct((n_rows, d), F32),
        compiler_params=_params("arbitrary"),
        name="experts",
    )(starts, xs, w_gate_up, w_down)


def _combine_body(p0_ref, p1_ref, ys_hbm, g_ref, beta_ref, o_ref, buf0, buf1, sem, *, tc, n_steps):
    i = pl.program_id(0)
    ring = buf0.shape[0]
    slot = i % ring

    def start_row(base, r, sl):
        pltpu.make_async_copy(ys_hbm.at[pl.ds(p0_ref[base + r], 1), :], buf0.at[sl, pl.ds(r, 1), :],
                              sem.at[sl]).start(priority=0)
        pltpu.make_async_copy(ys_hbm.at[pl.ds(p1_ref[base + r], 1), :], buf1.at[sl, pl.ds(r, 1), :],
                              sem.at[sl]).start(priority=1)

    def wait_rows(sl):
        pltpu.make_async_copy(ys_hbm.at[pl.ds(0, tc), :], buf0.at[sl], sem.at[sl]).wait()
        pltpu.make_async_copy(ys_hbm.at[pl.ds(0, tc), :], buf1.at[sl], sem.at[sl]).wait()

    @pl.when(i == 0)
    def _():
        for step in range(ring - 1):
            def one(r, carry):
                start_row(min(step, n_steps - 1) * tc, r, step)
                return carry

            lax.fori_loop(0, tc, one, 0)

    wait_rows(slot)
    nxt = jnp.minimum(i + ring - 1, n_steps - 1) * tc
    for r in range(tc):
        start_row(nxt, r, (i + ring - 1) % ring)
    o_ref[...] = _layer_norm(buf0[slot] + buf1[slot], g_ref[...], beta_ref[...])

    @pl.when(i == n_steps - 1)
    def _():
        for n in range(1, ring):
            wait_rows((i + n) % ring)


COMBINE_RING = 3


def _combine_ln(ys, pos0, pos1, g, beta, tc):
    t, d = pos0.shape[0], ys.shape[1]
    n_steps = t // tc
    body = functools.partial(_combine_body, tc=tc, n_steps=n_steps)
    row = lambda i, p0, p1: (i, 0)
    const = lambda i, p0, p1: (0, 0)
    grid_spec = pltpu.PrefetchScalarGridSpec(
        num_scalar_prefetch=2,
        grid=(n_steps,),
        in_specs=[pl.BlockSpec(memory_space=pl.ANY), pl.BlockSpec((1, d), const), pl.BlockSpec((1, d), const)],
        out_specs=pl.BlockSpec((tc, d), row),
        scratch_shapes=[pltpu.VMEM((COMBINE_RING, tc, d), F32), pltpu.VMEM((COMBINE_RING, tc, d), F32),
                        pltpu.SemaphoreType.DMA((COMBINE_RING,))],
    )
    return pl.pallas_call(
        body,
        grid_spec=grid_spec,
        out_shape=jax.ShapeDtypeStruct((t, d), F32),
        compiler_params=_params("arbitrary"),
        name="combine_ln",
    )(pos0, pos1, ys, g, beta)


def _moe_ln(h, wr_perm, bias_perm, w_gate_up, w_down, layer, g, beta, tm_router=512, tm_expert=256, tc=256):
    e01, gates = _router(h, wr_perm, bias_perm, tm_router)
    starts, ends, pos0, pos1 = _dispatch_plan(e01[:MOE_TOPK], tm_expert)
    xs = _dispatch(h, gates, starts, ends, pos0, pos1, tm_expert, tc)
    ys = _experts(xs, starts, w_gate_up, w_down, layer, tm_expert)
    return _combine_ln(ys, pos0, pos1, g, beta, tc)


def _swap_halves(w):
    half = w.shape[-1] // 2
    return jnp.concatenate([w[..., half:], w[..., :half]], axis=-1)


def _even_weights(w_in, w_q_up, w_kv_up, forget_bias):
    d = w_in.shape[0]
    o_kv = MLA_Q_LORA
    o_kr = o_kv + MLA_KV_LORA
    o_fq = o_kr + MLA_ROPE
    hd = FOX_HEADS * FOX_HEAD_DIM
    o_fl = o_fq + 3 * hd
    k_r = w_in[:, o_kr:o_fq]
    f_l = jnp.pad(w_in[:, o_fl:], ((0, 0), (0, LANES - FOX_HEADS)))
    w_small = jnp.concatenate([w_in[:, :o_kr], k_r, _swap_halves(k_r), f_l], axis=1).astype(BF)
    w_fox = w_in[:, o_fq:o_fl].astype(BF)
    cs_fox = jnp.concatenate([jnp.full((1, hd), FOX_SCALE, F32), jnp.ones((1, 2 * hd), F32)], axis=1)
    wq = w_q_up.reshape(MLA_Q_LORA, MLA_HEADS, MLA_NOPE + MLA_ROPE)
    wqn = wq[:, :, :MLA_NOPE].reshape(MLA_Q_LORA, -1).astype(BF)
    pe = wq[:, :, MLA_NOPE:]
    wqp = jnp.concatenate([pe, _swap_halves(pe)], axis=-1).reshape(MLA_Q_LORA, -1).astype(BF)
    wkv = w_kv_up.reshape(MLA_KV_LORA, MLA_HEADS, MLA_NOPE + MLA_V)
    wk = wkv[:, :, :MLA_NOPE].reshape(MLA_KV_LORA, -1).astype(BF)
    wv = wkv[:, :, MLA_NOPE:].reshape(MLA_KV_LORA, -1).astype(BF)
    fb = jnp.pad(forget_bias.astype(F32), (0, LANES - FOX_HEADS)).reshape(1, LANES)
    return w_small, w_fox, cs_fox, wqn, wqp, wk, wv, fb


def _rope_table(seq):
    half = MLA_ROPE // 2
    inv_freq = ROPE_THETA ** (-jnp.arange(half, dtype=F32) / half)
    ang = jnp.arange(seq).astype(F32)[:, None] * inv_freq[None, :]
    cos, sin = jnp.cos(ang), jnp.sin(ang)
    return jnp.concatenate([cos, cos, -sin, sin], axis=1)


def _router_weights(w_router, router_bias):
    r = np.arange(N_EXPERTS)
    perm = (r % N_GROUPS) * EXPERTS_PER_GROUP + r // N_GROUPS
    wr = jnp.pad(w_router[:, perm], ((0, 0), (0, LANES - N_EXPERTS)))
    w_hi = wr.astype(BF)
    w_lo = (wr - w_hi.astype(F32)).astype(BF)
    return jnp.concatenate([w_hi, w_lo], axis=1), router_bias.astype(F32)[perm].reshape(N_EXPERTS, 1)


def _even_layer(h, batch, seq, w_in, q_norm, w_q_up, kv_norm, w_kv_up, forget_bias, w_out, g, beta):
    w_small, w_fox, cs_fox, wqn, wqp, wk, wv, fb = _even_weights(w_in, w_q_up, w_kv_up, forget_bias)
    fox_qkv = _mm(h, w_fox, cs_fox, BF, 512, 1536)
    qn, qp, kn, v, kp, forget_logits = _mla_prep(h, w_small, _rope_table(seq), q_norm.reshape(1, -1),
                                                 kv_norm.reshape(1, -1), wqn, wqp, wk, wv, seq, 512)
    q_bias, k_bias, edge = _fox_cum(forget_logits, fb, batch, seq, 512)
    a = _mla_attention(qn, qp, kn, v, kp, batch, seq, 512, 4)
    bo = _fox_attention(fox_qkv, q_bias, k_bias, edge, batch, seq, 512, 4)
    return _outproj_ln(a, bo, w_out.astype(BF), h, g, beta, 512)


def _odd_layer(h, batch, seq, w_in, sinks, w_out, g, beta):
    n_sq = SWA_Q_HEADS * SWA_HEAD_DIM
    n_skv = SWA_KV_HEADS * SWA_HEAD_DIM
    n_m = MOBA_HEADS * MOBA_HEAD_DIM
    o_mq = n_sq + 2 * n_skv
    w_big = jnp.concatenate([w_in[:, :n_sq], w_in[:, o_mq:], w_in[:, n_sq:o_mq]], axis=1).astype(BF)
    cs = jnp.concatenate([jnp.full((1, n_sq), SWA_SCALE, F32), jnp.full((1, n_m), MOBA_SCALE, F32),
                          jnp.ones((1, 2 * n_m + 2 * n_skv), F32)], axis=1)
    big = _mm(h, w_big, cs, BF, 512, 2176)
    blocks = lambda cols: cols // LANES
    k_mean = _moba_kmean(big, batch, seq, 2)
    slopes = jnp.asarray(LOG2E * 2.0 ** (-8.0 * np.arange(1, MOBA_HEADS + 1) / MOBA_HEADS), dtype=F32)
    c = _swa_attention(big, sinks.astype(F32), batch, seq, 0, blocks(n_sq + 3 * n_m), blocks(n_sq + 3 * n_m) + 1)
    dd = _moba_attention(big, k_mean, slopes, batch, seq, blocks(n_sq), blocks(n_sq + n_m), blocks(n_sq + 2 * n_m),
                         8)
    return _outproj_ln(c, dd, w_out.astype(BF), h, g, beta, 512)


def kernel(x, w_router, router_bias, even_w_in, even_q_norm, even_w_q_up, even_kv_norm, even_w_kv_up,
           even_forget_bias, even_w_out, odd_w_in, odd_sinks, odd_w_out, ln_mix_g, ln_mix_b, ln_ffn_g, ln_ffn_b,
           w_gate_up, w_down):
    batch, seq, d = x.shape
    h = x.reshape(batch * seq, d)
    wr_perm, bias_perm = _router_weights(w_router, router_bias)
    row = lambda p, layer: p[layer].reshape(1, d)
    for layer in range(DEPTH):
        i = layer // 2
        if layer % 2 == 0:
            h = _even_layer(h, batch, seq, even_w_in[i], even_q_norm[i], even_w_q_up[i], even_kv_norm[i],
                            even_w_kv_up[i], even_forget_bias[i], even_w_out[i], row(ln_mix_g, layer),
                            row(ln_mix_b, layer))
        else:
            h = _odd_layer(h, batch, seq, odd_w_in[i], odd_sinks[i], odd_w_out[i], row(ln_mix_g, layer),
                           row(ln_mix_b, layer))
        h = _moe_ln(h, wr_perm, bias_perm, w_gate_up, w_down, layer, row(ln_ffn_g, layer), row(ln_ffn_b, layer))
    return h.reshape(batch, seq, d)
```

```python
import functools

import numpy as np
import jax
import jax.numpy as jnp
from jax import lax
from jax.experimental import pallas as pl
from jax.experimental.pallas import tpu as pltpu

BF = jnp.bfloat16
F32 = jnp.float32
I32 = jnp.int32

LANES = 128
SUBLANES = 8
VMEM_LIMIT = 56 * 1024 * 1024

D_MODEL = 2048
DEPTH = 2
LN_EPS = 1e-5
RMS_EPS = 1e-6
ALPHA = (2 * DEPTH) ** 0.25
MLA_HEADS, MLA_Q_LORA, MLA_KV_LORA, MLA_NOPE, MLA_ROPE, MLA_V = 8, 512, 256, 128, 64, 128
ROPE_THETA = 10000.0
FOX_HEADS, FOX_HEAD_DIM = 8, 128
SWA_Q_HEADS, SWA_KV_HEADS, SWA_HEAD_DIM, SWA_WINDOW = 16, 2, 64, 128
MOBA_HEADS, MOBA_HEAD_DIM, MOBA_BLOCK, MOBA_TOPK = 8, 128, 256, 3
N_EXPERTS, N_GROUPS, MOE_TOPK, EXPERT_FF = 32, 8, 2, 512
EXPERTS_PER_GROUP = N_EXPERTS // N_GROUPS

LOG2E = 1.4426950408889634
MLA_SCALE = (MLA_NOPE + MLA_ROPE) ** -0.5 * LOG2E
FOX_SCALE = FOX_HEAD_DIM ** -0.5 * LOG2E
MOBA_SCALE = MOBA_HEAD_DIM ** -0.5 * LOG2E
SWA_SCALE = SWA_HEAD_DIM ** -0.5 * LOG2E
NEG = -1e30

NT_DIMS = (((1,), (1,)), ((), ()))


def _params(*sem):
    return pltpu.CompilerParams(dimension_semantics=sem, vmem_limit_bytes=VMEM_LIMIT)


def _mm_body(x_ref, w_ref, cs_ref, o_ref):
    acc = jnp.dot(x_ref[...].astype(BF), w_ref[...], preferred_element_type=F32)
    o_ref[...] = (acc * cs_ref[...]).astype(o_ref.dtype)


def _mm(x, w, col_scale, out_dtype, tm, tn):
    m, k = x.shape
    n = w.shape[1]
    return pl.pallas_call(
        _mm_body,
        grid=(n // tn, m // tm),
        in_specs=[
            pl.BlockSpec((tm, k), lambda j, i: (i, 0)),
            pl.BlockSpec((k, tn), lambda j, i: (0, j)),
            pl.BlockSpec((1, tn), lambda j, i: (0, j)),
        ],
        out_specs=pl.BlockSpec((tm, tn), lambda j, i: (i, j)),
        out_shape=jax.ShapeDtypeStruct((m, n), out_dtype),
        compiler_params=_params("parallel", "parallel"),
        name="proj",
    )(x, w, col_scale)


def _rms(x, g):
    return x * lax.rsqrt(jnp.mean(x * x, axis=-1, keepdims=True) + RMS_EPS) * g


def _rope_pair(slab, table):
    r = slab * table
    return r + pltpu.roll(r, MLA_ROPE, axis=1)


def _mla_prep_body(x_ref, ws_ref, rope_ref, qg_ref, kvg_ref, wqn_ref, wqp_ref, wk_ref, wv_ref,
                   qn_ref, qp_ref, kn_ref, v_ref, kp_ref, fl_ref):
    small = jnp.dot(x_ref[...].astype(BF), ws_ref[...], preferred_element_type=F32)
    o_kv, o_kr, o_fl = MLA_Q_LORA, MLA_Q_LORA + MLA_KV_LORA, MLA_Q_LORA + MLA_KV_LORA + LANES
    fl_ref[...] = small[:, o_fl:]
    table = rope_ref[...]
    cqn = _rms(small[:, :o_kv], qg_ref[...]).astype(BF)
    qn_ref[...] = (jnp.dot(cqn, wqn_ref[...], preferred_element_type=F32) * MLA_SCALE).astype(BF)
    qp = jnp.dot(cqn, wqp_ref[...], preferred_element_type=F32)
    for h in range(MLA_HEADS):
        sl = slice(h * LANES, (h + 1) * LANES)
        qp_ref[:, sl] = (_rope_pair(qp[:, sl], table) * MLA_SCALE).astype(BF)
    ckvn = _rms(small[:, o_kv:o_kr], kvg_ref[...]).astype(BF)
    kn_ref[...] = jnp.dot(ckvn, wk_ref[...], preferred_element_type=F32).astype(BF)
    v_ref[...] = jnp.dot(ckvn, wv_ref[...], preferred_element_type=F32).astype(BF)
    kr = _rope_pair(small[:, o_kr:o_fl], table)
    lane = lax.broadcasted_iota(I32, kr.shape, 1)
    kp_ref[...] = jnp.where(lane < MLA_ROPE, kr, 0.0).astype(BF)


def _mla_prep(x, w_small, rope_table, q_norm, kv_norm, wqn, wqp, wk, wv, seq, tm):
    t, d = x.shape
    n_s = seq // tm
    hd = MLA_HEADS * LANES
    row = lambda i: (i, 0)
    const = lambda i: (0, 0)
    out = lambda w: pl.BlockSpec((tm, w), row)
    return pl.pallas_call(
        _mla_prep_body,
        grid=(t // tm,),
        in_specs=[
            pl.BlockSpec((tm, d), row),
            pl.BlockSpec(w_small.shape, const),
            pl.BlockSpec((tm, LANES), lambda i: (i % n_s, 0)),
            pl.BlockSpec((1, MLA_Q_LORA), const),
            pl.BlockSpec((1, MLA_KV_LORA), const),
            pl.BlockSpec((MLA_Q_LORA, hd), const),
            pl.BlockSpec((MLA_Q_LORA, hd), const),
            pl.BlockSpec((MLA_KV_LORA, hd), const),
            pl.BlockSpec((MLA_KV_LORA, hd), const),
        ],
        out_specs=[out(hd), out(hd), out(hd), out(hd), out(LANES), out(LANES)],
        out_shape=[jax.ShapeDtypeStruct((t, hd), BF)] * 4 + [jax.ShapeDtypeStruct((t, LANES), BF),
                                                             jax.ShapeDtypeStruct((t, LANES), F32)],
        compiler_params=_params("parallel"),
        name="mla_prep",
    )(x, w_small, rope_table, q_norm, kv_norm, wqn, wqp, wk, wv)


N_SPLIT = 3


def _split_piece(x, which):
    hi = x.astype(BF).astype(F32)
    rest = x - hi
    mid = rest.astype(BF).astype(F32)
    return jnp.where(which == 0, hi, jnp.where(which == 1, mid, rest - mid))


def _fox_cum_body(fl_ref, fb_ref, spread_ref, qe_ref, ke_ref, edge_ref, carry_ref):
    @pl.when(pl.program_id(1) == 0)
    def _():
        carry_ref[...] = jnp.zeros_like(carry_ref)

    z = fl_ref[...] + fb_ref[...]
    log_f = jnp.minimum(z, 0.0) - jnp.log1p(jnp.exp(-jnp.abs(z)))
    tm = z.shape[0]
    r = lax.broadcasted_iota(I32, (tm, tm), 0)
    c = lax.broadcasted_iota(I32, (tm, tm), 1)
    tri = jnp.where(r >= c, 1.0, 0.0).astype(BF)
    parts = jnp.dot(tri, jnp.concatenate([_split_piece(log_f, n).astype(BF) for n in range(N_SPLIT)], axis=1),
                    preferred_element_type=F32)
    cum = (parts[:, :LANES] + parts[:, LANES:2 * LANES]) + parts[:, 2 * LANES:] + carry_ref[...]
    carry_ref[...] = cum[tm - 1:tm, :]
    cum = cum * LOG2E
    width = qe_ref.shape[1]
    pieces = jnp.concatenate([_split_piece(cum, n).astype(BF) for n in range(N_SPLIT)], axis=1)
    placed = jnp.dot(pieces, spread_ref[...], preferred_element_type=F32)
    li = lax.broadcasted_iota(I32, (tm, width), 1) % LANES
    qe_ref[...] = jnp.where((li >= N_SPLIT) & (li < 2 * N_SPLIT), 1.0, placed[:, :width]).astype(BF)
    ke_ref[...] = jnp.where(li < N_SPLIT, 1.0, placed[:, width:]).astype(BF)
    edge_ref[0, 0:1, :] = cum[0:1, :]
    edge_ref[0, 1:2, :] = cum[tm - 1:tm, :]


def _fox_spread():
    width = FOX_HEADS * LANES
    m = np.zeros((N_SPLIT * LANES, 2 * width), np.float32)
    for j in range(N_SPLIT):
        for h in range(FOX_HEADS):
            m[j * LANES + h, h * LANES + j] = 1.0
            m[j * LANES + h, width + h * LANES + N_SPLIT + j] = -1.0
    return jnp.asarray(m, dtype=BF)


def _fox_cum(forget_logits, forget_bias_row, batch, seq, tm):
    t = forget_logits.shape[0]
    n_s = seq // tm
    width = FOX_HEADS * LANES
    q_bias, k_bias, edge = pl.pallas_call(
        _fox_cum_body,
        grid=(batch, n_s),
        in_specs=[
            pl.BlockSpec((tm, LANES), lambda b, i: (b * n_s + i, 0)),
            pl.BlockSpec((1, LANES), lambda b, i: (0, 0)),
            pl.BlockSpec((N_SPLIT * LANES, 2 * width), lambda b, i: (0, 0)),
        ],
        out_specs=[
            pl.BlockSpec((tm, width), lambda b, i: (b * n_s + i, 0)),
            pl.BlockSpec((tm, width), lambda b, i: (b * n_s + i, 0)),
            pl.BlockSpec((1, 2, LANES), lambda b, i: (b * n_s + i, 0, 0)),
        ],
        out_shape=[jax.ShapeDtypeStruct((t, width), BF), jax.ShapeDtypeStruct((t, width), BF),
                   jax.ShapeDtypeStruct((batch * n_s, 2, LANES), F32)],
        scratch_shapes=[pltpu.VMEM((1, LANES), F32)],
        compiler_params=_params("parallel", "arbitrary"),
        name="fox_cum",
    )(forget_logits, forget_bias_row, _fox_spread())
    return q_bias, k_bias, edge[:, :, :FOX_HEADS].reshape(-1)


def _softmax_update(s_all, v_all, m_ref, l_ref, acc_ref):
    heads = range(len(s_all))
    m_prev = [m_ref[g] for g in heads]
    m_new = [jnp.maximum(m_prev[g], jnp.max(s_all[g], axis=1, keepdims=True)) for g in heads]
    reps = s_all[0].shape[1] // LANES
    p = [jnp.exp2(s_all[g] - jnp.concatenate([m_new[g]] * reps, axis=1)) for g in heads]
    pv = [jnp.dot(p[g].astype(BF), v_all[g], preferred_element_type=F32) for g in heads]
    for g in heads:
        alpha = jnp.exp2(m_prev[g] - m_new[g])
        l_ref[g] = alpha * l_ref[g] + jnp.sum(p[g], axis=1, keepdims=True)
        acc_ref[g] = alpha * acc_ref[g] + pv[g]
        m_ref[g] = m_new[g]


def _attn_body(*refs, mode, t, group):
    hg = pl.program_id(1)
    i = pl.program_id(2)
    heads = range(group)
    sl = lambda g: slice(g * LANES, (g + 1) * LANES)
    if mode == "moba":
        q_ref, k_ref, kx_ref, v_ref, km_ref, slope_ref, o_ref, m_ref, l_ref, acc_ref, qx_ref = refs
        for g in heads:
            qx_ref[g] = _moba_query_lanes(q_ref[:, sl(g)], km_ref[:, sl(g)], slope_ref[hg * group + g], i, t)
        q_extra = lambda g: qx_ref[g]
        k_extra = lambda g, ks: kx_ref[pl.ds(ks, t), :]
    elif mode == "mla":
        q_ref, qx_ref, k_ref, kx_ref, v_ref, o_ref, m_ref, l_ref, acc_ref = refs
        q_extra = lambda g: qx_ref[:, sl(g)]
        k_extra = lambda g, ks: kx_ref[pl.ds(ks, t), :]
    else:
        q_ref, qx_ref, k_ref, kx_ref, v_ref, edge_ref, o_ref, m_ref, l_ref, acc_ref, knorm_ref = refs
        q_extra = lambda g: qx_ref[:, sl(g)]
        k_extra = lambda g, ks: kx_ref[pl.ds(ks, t), sl(g)]
        first_tile = _fox_first_tile(q_ref, k_ref, edge_ref, knorm_ref, i, hg, t, group)

    def scores(g, ks):
        q = jnp.concatenate([q_ref[:, sl(g)], q_extra(g)], axis=1)
        k = jnp.concatenate([k_ref[pl.ds(ks, t), sl(g)], k_extra(g, ks)], axis=1)
        return lax.dot_general(q, k, NT_DIMS, preferred_element_type=F32)

    m_ref[...] = jnp.full(m_ref.shape, NEG, F32)
    l_ref[...] = jnp.zeros(l_ref.shape, F32)
    acc_ref[...] = jnp.zeros(acc_ref.shape, F32)

    ks = pl.multiple_of(i * t, t)
    r = lax.broadcasted_iota(I32, (t, t), 0)
    c = lax.broadcasted_iota(I32, (t, t), 1)
    values = lambda ks: [v_ref[pl.ds(ks, t), sl(g)] for g in heads]
    _softmax_update([jnp.where(c <= r, scores(g, ks), NEG) for g in heads], values(ks), m_ref, l_ref, acc_ref)

    def past_tile(j, carry):
        ks = pl.multiple_of(j * t, t)
        _softmax_update([scores(g, ks) for g in heads], values(ks), m_ref, l_ref, acc_ref)
        return carry

    lax.fori_loop(first_tile if mode == "fox" else 0, i, past_tile, 0)
    for g in heads:
        o_ref[:, sl(g)] = (acc_ref[g] / l_ref[g]).astype(o_ref.dtype)


FOX_SKIP_GAP = 160.0


def _fox_first_tile(q_ref, k_ref, edge_ref, knorm_ref, i, hg, t, group):
    b = pl.program_id(0)
    n_q = pl.num_programs(2)
    sl = lambda g: slice(g * LANES, (g + 1) * LANES)

    def max_row_norm(x):
        x = x.astype(F32)
        return jnp.sqrt(jnp.max(jnp.sum(x * x, axis=1, keepdims=True)))

    @pl.when(i == 0)
    def _():
        for g in range(group):
            knorm_ref[g] = max_row_norm(k_ref[:, sl(g)])

    def edge(tile, last, g):
        return edge_ref[((b * n_q + tile) * 2 + last) * FOX_HEADS + hg * group + g]

    limit = [-(2.0 * 1.001 * max_row_norm(q_ref[:, sl(g)]) * knorm_ref[g] + FOX_SKIP_GAP) for g in range(group)]
    first = [edge(i, 0, g) for g in range(group)]

    def scan(j, lo):
        needed = first[0] - edge(j, 1, 0) >= limit[0]
        for g in range(1, group):
            needed = needed | (first[g] - edge(j, 1, g) >= limit[g])
        return jnp.where(needed, jnp.minimum(lo, j), lo)

    return lax.fori_loop(0, i, scan, i)


def _attention(mode, batch, seq, heads, t, group, operands, in_specs, extra_scratch=()):
    n_q = seq // t
    body = functools.partial(_attn_body, mode=mode, t=t, group=group)
    return pl.pallas_call(
        body,
        grid=(batch, heads // group, n_q),
        in_specs=in_specs,
        out_specs=pl.BlockSpec((t, group * LANES), lambda b, h, i: (b * n_q + i, h)),
        out_shape=jax.ShapeDtypeStruct((batch * seq, heads * LANES), BF),
        scratch_shapes=[pltpu.VMEM((group, t, LANES), F32), pltpu.VMEM((group, t, LANES), F32),
                        pltpu.VMEM((group, t, LANES), F32), *extra_scratch],
        compiler_params=_params("parallel", "parallel", "arbitrary"),
        name="attn_" + mode,
    )(*operands)


def _q_spec(t, n_q, group, col0=0):
    return pl.BlockSpec((t, group * LANES), lambda b, h, i: (b * n_q + i, col0 + h))


def _kv_spec(seq, group, col0=0):
    return pl.BlockSpec((seq, group * LANES), lambda b, h, i: (b, col0 + h))


def _mla_attention(qn, qp, kn, v, kp, batch, seq, t, group):
    n_q = seq // t
    specs = [_q_spec(t, n_q, group), _q_spec(t, n_q, group), _kv_spec(seq, group),
             pl.BlockSpec((seq, LANES), lambda b, h, i: (b, 0)), _kv_spec(seq, group)]
    return _attention("mla", batch, seq, MLA_HEADS, t, group, (qn, qp, kn, kp, v), specs)


def _fox_attention(qkv, q_bias, k_bias, edge, batch, seq, t, group):
    n_q = seq // t
    n_hg = FOX_HEADS // group
    specs = [_q_spec(t, n_q, group, 0), _q_spec(t, n_q, group, 0), _kv_spec(seq, group, n_hg),
             _kv_spec(seq, group, 0), _kv_spec(seq, group, 2 * n_hg), pl.BlockSpec(memory_space=pltpu.SMEM)]
    return _attention("fox", batch, seq, FOX_HEADS, t, group, (qkv, q_bias, qkv, k_bias, qkv, edge), specs,
                      (pltpu.SMEM((group,), F32),))


def _moba_key_lanes(seq):
    n_kb = seq // MOBA_BLOCK
    kpos = np.arange(seq)
    lanes = np.zeros((seq, LANES), np.float32)
    lanes[kpos, kpos // MOBA_BLOCK] = 1.0
    lanes[:, n_kb:n_kb + N_SPLIT] = 1.0
    lanes[:, n_kb + N_SPLIT:n_kb + 2 * N_SPLIT] = (MOBA_BLOCK * (kpos // MOBA_BLOCK))[:, None]
    lanes[:, n_kb + 2 * N_SPLIT:n_kb + 3 * N_SPLIT] = (kpos % MOBA_BLOCK)[:, None]
    return jnp.asarray(lanes, dtype=BF)


def _moba_query_lanes(q, k_mean, slope, i, t):
    n_kb = k_mean.shape[0]
    gate = lax.dot_general(k_mean.astype(BF), q, NT_DIMS, preferred_element_type=F32)
    blk = lax.broadcasted_iota(I32, (n_kb, t), 0)
    qpos_i = i * t + lax.broadcasted_iota(I32, (1, t), 1)
    own = qpos_i // MOBA_BLOCK
    beaten = jnp.zeros((n_kb, t), F32)
    for n in range(n_kb):
        g_n = gate[n:n + 1, :]
        wins = (g_n > gate) | ((g_n == gate) & (blk > n))
        beaten = beaten + jnp.where(wins & (own > n), 1.0, 0.0)
    attended = (blk == own) | ((blk < own) & (beaten < MOBA_TOPK))
    mask = jnp.concatenate([jnp.where(attended, 0.0, NEG), jnp.zeros((LANES - n_kb, t), F32)], axis=0)
    row = lax.broadcasted_iota(I32, (LANES, t), 0) - n_kb
    qpos = qpos_i.astype(F32)
    slope_row = jnp.full((1, t), slope, F32)
    which = row % N_SPLIT
    lanes = jnp.where(row < 0, mask,
                      jnp.where(row < N_SPLIT, _split_piece(-slope_row * qpos, which),
                                jnp.where(row < 3 * N_SPLIT, _split_piece(slope_row, which), 0.0)))
    return lanes.T.astype(BF)


def _moba_attention(big, k_mean, slopes, batch, seq, col_q, col_k, col_v, t, group):
    n_q = seq // t
    specs = [_q_spec(t, n_q, group, col_q // group), _kv_spec(seq, group, col_k // group),
             pl.BlockSpec((seq, LANES), lambda b, h, i: (0, 0)),
             _kv_spec(seq, group, col_v // group),
             pl.BlockSpec((seq // MOBA_BLOCK, group * LANES), lambda b, h, i: (b, h)),
             pl.BlockSpec(memory_space=pltpu.SMEM)]
    scratch = (pltpu.VMEM((group, t, LANES), BF),)
    return _attention("moba", batch, seq, MOBA_HEADS, t, group,
                      (big, big, _moba_key_lanes(seq), big, k_mean, slopes), specs, scratch)


def _kmean_body(k_ref, o_ref):
    n_kb = o_ref.shape[0]
    for n in range(n_kb):
        blk = k_ref[n * MOBA_BLOCK:(n + 1) * MOBA_BLOCK, :].astype(F32)
        o_ref[n:n + 1, :] = jnp.mean(blk, axis=0, keepdims=True)


def _moba_kmean(big, batch, seq, col_block):
    n_kb = seq // MOBA_BLOCK
    width = MOBA_HEADS * MOBA_HEAD_DIM
    return pl.pallas_call(
        _kmean_body,
        grid=(batch,),
        in_specs=[pl.BlockSpec((seq, width), lambda b: (b, col_block))],
        out_specs=pl.BlockSpec((n_kb, width), lambda b: (b, 0)),
        out_shape=jax.ShapeDtypeStruct((batch * n_kb, width), F32),
        compiler_params=_params("parallel"),
        name="moba_kmean",
    )(big)


def _swa_body(sink_ref, q_ref, kp_ref, kc_ref, vp_ref, vc_ref, o_ref):
    i = pl.program_id(1)
    w = SWA_WINDOW
    half = SWA_HEAD_DIM
    lane = lax.broadcasted_iota(I32, (w, LANES), 1)
    r = lax.broadcasted_iota(I32, (w, w), 0)
    c = lax.broadcasted_iota(I32, (w, w), 1)
    from_prev = c > r
    dist = (r - c + jnp.where(from_prev, w, 0)).astype(F32)
    valid = jnp.logical_not(from_prev) | (i > 0)
    heads_per_kv = SWA_Q_HEADS // SWA_KV_HEADS

    def lo_hi(ref, kvh):
        own = jnp.where((lane < half) == (kvh == 0), ref[...].astype(F32), 0.0)
        other = pltpu.roll(own, half, axis=1)
        lo, hi = (own, other) if kvh == 0 else (other, own)
        return lo.astype(BF), hi.astype(BF)

    for kvh in range(SWA_KV_HEADS):
        kp, kc, vp, vc = (lo_hi(ref, kvh) for ref in (kp_ref, kc_ref, vp_ref, vc_ref))
        heads = range(kvh * heads_per_kv, (kvh + 1) * heads_per_kv)
        q = {h: q_ref[:, (h // 2) * LANES:(h // 2 + 1) * LANES] for h in heads}
        s = {h: jnp.where(from_prev,
                          lax.dot_general(q[h], kp[h % 2], NT_DIMS, preferred_element_type=F32),
                          lax.dot_general(q[h], kc[h % 2], NT_DIMS, preferred_element_type=F32)) for h in heads}
        p, inv = {}, {}
        for h in heads:
            slope = float(LOG2E * 2.0 ** (-8.0 * (h + 1) / SWA_Q_HEADS))
            sink = sink_ref[h] * LOG2E
            logits = jnp.where(valid, s[h] - slope * dist, NEG)
            m = jnp.maximum(jnp.max(logits, axis=1, keepdims=True), sink)
            p[h] = jnp.exp2(logits - m)
            inv[h] = 1.0 / (jnp.sum(p[h], axis=1, keepdims=True) + jnp.exp2(sink - m))
        for pair in range(kvh * heads_per_kv // 2, (kvh + 1) * heads_per_kv // 2):
            out = jnp.zeros((w, LANES), F32)
            for h in (2 * pair, 2 * pair + 1):
                pn = p[h] * inv[h]
                out = out + jnp.dot(jnp.where(from_prev, pn, 0.0).astype(BF), vp[h % 2], preferred_element_type=F32)
                out = out + jnp.dot(jnp.where(from_prev, 0.0, pn).astype(BF), vc[h % 2], preferred_element_type=F32)
            o_ref[:, pair * LANES:(pair + 1) * LANES] = out.astype(o_ref.dtype)


def _swa_attention(big, sinks, batch, seq, col_q, col_k, col_v):
    w = SWA_WINDOW
    n_q = seq // w
    width = SWA_Q_HEADS * SWA_HEAD_DIM
    prev = lambda col: pl.BlockSpec((w, LANES), lambda b, i: (b * n_q + jnp.maximum(i - 1, 0), col))
    cur = lambda col: pl.BlockSpec((w, LANES), lambda b, i: (b * n_q + i, col))
    return pl.pallas_call(
        _swa_body,
        grid=(batch, n_q),
        in_specs=[pl.BlockSpec(memory_space=pltpu.SMEM),
                  pl.BlockSpec((w, width), lambda b, i: (b * n_q + i, col_q)),
                  prev(col_k), cur(col_k), prev(col_v), cur(col_v)],
        out_specs=pl.BlockSpec((w, width), lambda b, i: (b * n_q + i, 0)),
        out_shape=jax.ShapeDtypeStruct((batch * seq, width), BF),
        compiler_params=_params("parallel", "parallel"),
        name="swa",
    )(sinks, big, big, big, big, big)


def _layer_norm(z, g, b):
    mu = jnp.mean(z, axis=-1, keepdims=True)
    zc = z - mu
    var = jnp.mean(zc * zc, axis=-1, keepdims=True)
    return zc * lax.rsqrt(var + LN_EPS) * g + b


def _outproj_body(a_ref, b_ref, wa_ref, wb_ref, h_ref, g_ref, beta_ref, o_ref):
    mix = jnp.dot(a_ref[...], wa_ref[...], preferred_element_type=F32)
    mix = mix + jnp.dot(b_ref[...], wb_ref[...], preferred_element_type=F32)
    o_ref[...] = _layer_norm(ALPHA * h_ref[...] + mix, g_ref[...], beta_ref[...])


def _outproj_ln(a, b, w_out, h, g, beta, tm):
    t, d = h.shape
    ka = a.shape[1]
    row = lambda i: (i, 0)
    return pl.pallas_call(
        _outproj_body,
        grid=(t // tm,),
        in_specs=[pl.BlockSpec((tm, ka), row), pl.BlockSpec((tm, ka), row),
                  pl.BlockSpec((ka, d), lambda i: (0, 0)), pl.BlockSpec((ka, d), lambda i: (1, 0)),
                  pl.BlockSpec((tm, d), row), pl.BlockSpec((1, d), lambda i: (0, 0)),
                  pl.BlockSpec((1, d), lambda i: (0, 0))],
        out_specs=pl.BlockSpec((tm, d), row),
        out_shape=jax.ShapeDtypeStruct((t, d), F32),
        compiler_params=_params("parallel"),
        name="outproj_ln",
    )(a, b, w_out, w_out, h, g, beta)


def _second_of_four(a, b, c, d):
    hi_ab, lo_ab = jnp.maximum(a, b), jnp.minimum(a, b)
    hi_cd, lo_cd = jnp.maximum(c, d), jnp.minimum(c, d)
    return jnp.maximum(jnp.maximum(lo_ab, lo_cd), jnp.minimum(hi_ab, hi_cd))


def _router_body(h_ref, wr_ref, bias_ref, e_ref, r_ref):
    tm = h_ref.shape[0]
    h = h_ref[...]
    h_hi = h.astype(BF)
    h_lo = (h - h_hi.astype(F32)).astype(BF)
    both = jnp.dot(h_hi, wr_ref[...], preferred_element_type=F32)
    logits = (both[:, :LANES] + both[:, LANES:]) + jnp.dot(h_lo, wr_ref[:, :LANES], preferred_element_type=F32)
    lt = logits.T
    aff = [jax.nn.sigmoid(lt[SUBLANES * j:SUBLANES * (j + 1), :]) for j in range(EXPERTS_PER_GROUP)]
    sel = [aff[j] + bias_ref[SUBLANES * j:SUBLANES * (j + 1), :] for j in range(EXPERTS_PER_GROUP)]
    top1 = jnp.maximum(jnp.maximum(sel[0], sel[1]), jnp.maximum(sel[2], sel[3]))
    score = top1 + _second_of_four(*sel)
    gid = lax.broadcasted_iota(I32, (N_GROUPS, tm), 0)
    best = jnp.min(jnp.where(score == jnp.max(score, axis=0, keepdims=True), gid, N_GROUPS),
                   axis=0, keepdims=True)
    in_grp = gid == best
    pick = lambda x: jnp.sum(jnp.where(in_grp, x, 0.0), axis=0, keepdims=True)
    s4 = [pick(x) for x in sel]
    a4 = [pick(x) for x in aff]

    def argmax4(vals):
        j, v = jnp.zeros((1, tm), I32), vals[0]
        for n in range(1, EXPERTS_PER_GROUP):
            better = vals[n] > v
            j, v = jnp.where(better, n, j), jnp.where(better, vals[n], v)
        return j

    j0 = argmax4(s4)
    j1 = argmax4([jnp.where(j0 == n, -jnp.inf, s4[n]) for n in range(EXPERTS_PER_GROUP)])
    take = lambda j: sum(jnp.where(j == n, a4[n], 0.0) for n in range(EXPERTS_PER_GROUP))
    w0, w1 = take(j0), take(j1)
    total = w0 + w1
    e0 = best * EXPERTS_PER_GROUP + j0
    e1 = best * EXPERTS_PER_GROUP + j1
    rid = lax.broadcasted_iota(I32, (SUBLANES, tm), 0)
    e_ref[...] = jnp.where(rid == 0, e0, jnp.where(rid == 1, e1, 0))
    rid = lax.broadcasted_iota(I32, (LANES, tm), 0)
    rows = jnp.where(rid == 0, w0 / total, jnp.where(rid == 1, w1 / total, 0.0))
    r_ref[...] = rows.T


def _router(h, wr_perm, bias_perm, tm):
    t, d = h.shape
    return pl.pallas_call(
        _router_body,
        grid=(t // tm,),
        in_specs=[pl.BlockSpec((tm, d), lambda i: (i, 0)), pl.BlockSpec((d, 2 * LANES), lambda i: (0, 0)),
                  pl.BlockSpec((N_EXPERTS, 1), lambda i: (0, 0))],
        out_specs=[pl.BlockSpec((SUBLANES, tm), lambda i: (0, i)), pl.BlockSpec((tm, LANES), lambda i: (i, 0))],
        out_shape=[jax.ShapeDtypeStruct((SUBLANES, t), I32), jax.ShapeDtypeStruct((t, LANES), F32)],
        compiler_params=_params("parallel"),
        name="router",
    )(h, wr_perm, bias_perm)


def _dispatch_plan(e01, tm):
    t = e01.shape[1]
    flat = e01.reshape(-1)
    onehot = (flat[:, None] == jnp.arange(N_EXPERTS, dtype=I32)[None, :]).astype(I32)
    cum = jnp.cumsum(onehot, axis=0)
    rank = jnp.sum(onehot * cum, axis=1) - 1
    counts = cum[-1]
    padded = ((counts + tm - 1) // tm) * tm
    starts = jnp.concatenate([jnp.zeros((1,), I32), jnp.cumsum(padded).astype(I32)])
    pos = jnp.sum(onehot * starts[None, :N_EXPERTS], axis=1) + rank
    return starts, starts[:N_EXPERTS] + counts, pos[:t], pos[t:]


def _dispatch_body(p0_ref, p1_ref, gs_ref, ge_ref, h_ref, r_ref, xs_hbm, xbuf, zbuf, sem, zsem, *, tm, tt,
                   n_steps):
    i = pl.program_id(0)
    slot = i % 2
    n_tail = (xs_hbm.shape[0] - gs_ref[N_EXPERTS]) // tm

    def zero_copy(row0, size):
        return pltpu.make_async_copy(zbuf.at[pl.ds(0, size), :], xs_hbm.at[pl.ds(row0, size), :], zsem.at[0])

    def for_each_fill(fn):
        def group_pad(e, carry):
            end = gs_ref[e + 1]
            pad = end - ge_ref[e]
            covered = 0
            size = tm // 2
            while size >= SUBLANES:
                take = pad & size

                @pl.when(take != 0)
                def _():
                    fn(zero_copy(pl.multiple_of(end - covered - size, size), size))

                covered = covered + take
                size //= 2
            for k in range(SUBLANES - 1):
                @pl.when(k < (pad & (SUBLANES - 1)))
                def _():
                    fn(zero_copy(ge_ref[e] + k, 1))
            return carry

        def tail_tile(k, carry):
            fn(zero_copy(pl.multiple_of(gs_ref[N_EXPERTS] + k * tm, tm), tm))
            return carry

        lax.fori_loop(0, N_EXPERTS, group_pad, 0)
        lax.fori_loop(0, n_tail, tail_tile, 0)

    def wait_rows(sl):
        for _ in range(2):
            pltpu.make_async_copy(xbuf.at[sl, 0], xs_hbm.at[pl.ds(0, tt), :], sem.at[sl]).wait()

    @pl.when(i == 0)
    def _():
        zbuf[...] = jnp.zeros(zbuf.shape, F32)
        for_each_fill(lambda cp: cp.start())

    @pl.when(i >= 2)
    def _():
        wait_rows(slot)

    d = h_ref.shape[1]
    gates = r_ref[...]
    lane = lax.broadcasted_iota(I32, gates.shape, 1)
    for k in range(MOE_TOPK):
        xbuf[slot, k, :, :d] = h_ref[...]
    xbuf[slot, 0, :, d:] = jnp.where(lane == 0, gates, jnp.where(lane == 1, ALPHA, 0.0))
    xbuf[slot, 1, :, d:] = jnp.where(lane == 0, pltpu.roll(gates, LANES - 1, axis=1), 0.0)
    base = i * tt
    for r in range(tt):
        pltpu.make_async_copy(xbuf.at[slot, 0, pl.ds(r, 1), :], xs_hbm.at[pl.ds(p0_ref[base + r], 1), :],
                              sem.at[slot]).start(priority=0)
        pltpu.make_async_copy(xbuf.at[slot, 1, pl.ds(r, 1), :], xs_hbm.at[pl.ds(p1_ref[base + r], 1), :],
                              sem.at[slot]).start(priority=1)

    @pl.when(i == n_steps - 1)
    def _():
        wait_rows(slot)
        if n_steps > 1:
            wait_rows(1 - slot)
        for_each_fill(lambda cp: cp.wait())


def _dispatch(h, gates, starts, ends, pos0, pos1, tm, tt):
    t, d = h.shape
    dx = d + LANES
    n_steps = t // tt
    n_rows = 2 * t + N_EXPERTS * tm
    body = functools.partial(_dispatch_body, tm=tm, tt=tt, n_steps=n_steps)
    grid_spec = pltpu.PrefetchScalarGridSpec(
        num_scalar_prefetch=4,
        grid=(n_steps,),
        in_specs=[pl.BlockSpec((tt, d), lambda i, *_: (i, 0)), pl.BlockSpec((tt, LANES), lambda i, *_: (i, 0))],
        out_specs=pl.BlockSpec(memory_space=pl.ANY),
        scratch_shapes=[pltpu.VMEM((2, MOE_TOPK, tt, dx), F32), pltpu.VMEM((tm, dx), F32),
                        pltpu.SemaphoreType.DMA((2,)), pltpu.SemaphoreType.DMA((1,))],
    )
    return pl.pallas_call(
        body,
        grid_spec=grid_spec,
        out_shape=jax.ShapeDtypeStruct((n_rows, dx), F32),
        compiler_params=_params("arbitrary"),
        name="dispatch",
    )(pos0, pos1, starts, ends, h, gates)


def _experts_body(gs_ref, xs_hbm, wgu_ref, wd_ref, ys_hbm, xbuf, ybuf, wgu_bf, wd_bf, xsem, osem, *, tm):
    e = pl.program_id(0)
    first = gs_ref[e] // tm
    n_tiles = gs_ref[e + 1] // tm - first
    total = gs_ref[N_EXPERTS] // tm
    ring = xbuf.shape[0]

    def in_copy(g, slot):
        row0 = pl.multiple_of(g * tm, tm)
        return pltpu.make_async_copy(xs_hbm.at[pl.ds(row0, tm), :], xbuf.at[slot], xsem.at[slot])

    def out_copy(g, slot):
        row0 = pl.multiple_of(g * tm, tm)
        return pltpu.make_async_copy(ybuf.at[slot], ys_hbm.at[pl.ds(row0, tm), :], osem.at[slot])

    @pl.when(e == 0)
    def _():
        for g in range(ring - 1):
            in_copy(jnp.minimum(g, total - 1), g).start(priority=1)
        ybuf[...] = jnp.zeros(ybuf.shape, F32)
        out_copy(0, 0).start()
        out_copy(1, 1).start()

    @pl.when(n_tiles > 0)
    def _():
        wgu_bf[...] = wgu_ref[0, 0].astype(BF)
        wd_bf[...] = wd_ref[0, 0].astype(BF)

        def tile(k, carry):
            g = first + k
            slot = g % 2
            xslot = g % ring
            in_copy(g, xslot).wait()
            out_copy(g, slot).wait()
            in_copy(jnp.minimum(g + ring - 1, total - 1), (g + ring - 1) % ring).start(priority=1)
            d = ybuf.shape[2]
            x = xbuf[xslot, :, :d]
            row_gate = xbuf[xslot, :, d:d + 1]
            row_residual = xbuf[xslot, :, d + 1:d + 2]
            gu = jnp.dot(x.astype(BF), wgu_bf[...], preferred_element_type=F32)
            gate, up = gu[:, :EXPERT_FF], gu[:, EXPERT_FF:]
            hidden = (gate * jax.nn.sigmoid(gate) * up).astype(BF)
            y = jnp.dot(hidden, wd_bf[...], preferred_element_type=F32)
            ybuf[slot] = row_gate * y + row_residual * x
            out_copy(g, slot).start(priority=1)
            return carry

        lax.fori_loop(0, n_tiles, tile, 0)

    @pl.when(e == N_EXPERTS - 1)
    def _():
        for n in range(ring - 1):
            in_copy(0, (total + n) % ring).wait()
        out_copy(0, 0).wait()
        out_copy(1, 1).wait()
        end = gs_ref[N_EXPERTS]
        n_tail = (ys_hbm.shape[0] - end) // tm
        ybuf[0] = jnp.zeros(ybuf.shape[1:], F32)

        def tail_copy(k):
            row0 = pl.multiple_of(end + k * tm, tm)
            return pltpu.make_async_copy(ybuf.at[0], ys_hbm.at[pl.ds(row0, tm), :], osem.at[0])

        def start_one(k, carry):
            tail_copy(k).start()
            return carry

        def wait_one(k, carry):
            tail_copy(k).wait()
            return carry

        lax.fori_loop(0, n_tail, start_one, 0)
        lax.fori_loop(0, n_tail, wait_one, 0)


EXPERT_IN_RING = 6


def _experts(xs, starts, w_gate_up, w_down, layer, tm):
    n_rows, dx = xs.shape
    d = dx - LANES
    body = functools.partial(_experts_body, tm=tm)
    grid_spec = pltpu.PrefetchScalarGridSpec(
        num_scalar_prefetch=1,
        grid=(N_EXPERTS,),
        in_specs=[pl.BlockSpec(memory_space=pl.ANY),
                  pl.BlockSpec((1, 1, d, 2 * EXPERT_FF), lambda e, gs: (layer, e, 0, 0)),
                  pl.BlockSpec((1, 1, EXPERT_FF, d), lambda e, gs: (layer, e, 0, 0))],
        out_specs=pl.BlockSpec(memory_space=pl.ANY),
        scratch_shapes=[pltpu.VMEM((EXPERT_IN_RING, tm, dx), F32), pltpu.VMEM((2, tm, d), F32),
                        pltpu.VMEM((d, 2 * EXPERT_FF), BF), pltpu.VMEM((EXPERT_FF, d), BF),
                        pltpu.SemaphoreType.DMA((EXPERT_IN_RING,)), pltpu.SemaphoreType.DMA((2,))],
    )
    return pl.pallas_call(
        body,
        grid_spec=grid_spec,
        out_shape=jax.ShapeDtypeStruct((n_rows, d), F32),
        compiler_params=_params("arbitrary"),
        name="experts",
    )(starts, xs, w_gate_up, w_down)


def _combine_body(p0_ref, p1_ref, ys_hbm, g_ref, beta_ref, o_ref, buf0, buf1, sem, *, tc, n_steps):
    i = pl.program_id(0)
    ring = buf0.shape[0]
    slot = i % ring

    def start_row(base, r, sl):
        pltpu.make_async_copy(ys_hbm.at[pl.ds(p0_ref[base + r], 1), :], buf0.at[sl, pl.ds(r, 1), :],
                              sem.at[sl]).start(priority=0)
        pltpu.make_async_copy(ys_hbm.at[pl.ds(p1_ref[base + r], 1), :], buf1.at[sl, pl.ds(r, 1), :],
                              sem.at[sl]).start(priority=1)

    def wait_rows(sl):
        pltpu.make_async_copy(ys_hbm.at[pl.ds(0, tc), :], buf0.at[sl], sem.at[sl]).wait()
        pltpu.make_async_copy(ys_hbm.at[pl.ds(0, tc), :], buf1.at[sl], sem.at[sl]).wait()

    @pl.when(i == 0)
    def _():
        for step in range(ring - 1):
            def one(r, carry):
                start_row(min(step, n_steps - 1) * tc, r, step)
                return carry

            lax.fori_loop(0, tc, one, 0)

    wait_rows(slot)
    nxt = jnp.minimum(i + ring - 1, n_steps - 1) * tc
    for r in range(tc):
        start_row(nxt, r, (i + ring - 1) % ring)
    o_ref[...] = _layer_norm(buf0[slot] + buf1[slot], g_ref[...], beta_ref[...])

    @pl.when(i == n_steps - 1)
    def _():
        for n in range(1, ring):
            wait_rows((i + n) % ring)


COMBINE_RING = 3


def _combine_ln(ys, pos0, pos1, g, beta, tc):
    t, d = pos0.shape[0], ys.shape[1]
    n_steps = t // tc
    body = functools.partial(_combine_body, tc=tc, n_steps=n_steps)
    row = lambda i, p0, p1: (i, 0)
    const = lambda i, p0, p1: (0, 0)
    grid_spec = pltpu.PrefetchScalarGridSpec(
        num_scalar_prefetch=2,
        grid=(n_steps,),
        in_specs=[pl.BlockSpec(memory_space=pl.ANY), pl.BlockSpec((1, d), const), pl.BlockSpec((1, d), const)],
        out_specs=pl.BlockSpec((tc, d), row),
        scratch_shapes=[pltpu.VMEM((COMBINE_RING, tc, d), F32), pltpu.VMEM((COMBINE_RING, tc, d), F32),
                        pltpu.SemaphoreType.DMA((COMBINE_RING,))],
    )
    return pl.pallas_call(
        body,
        grid_spec=grid_spec,
        out_shape=jax.ShapeDtypeStruct((t, d), F32),
        compiler_params=_params("arbitrary"),
        name="combine_ln",
    )(pos0, pos1, ys, g, beta)


def _moe_ln(h, wr_perm, bias_perm, w_gate_up, w_down, layer, g, beta, tm_router=512, tm_expert=256, tc=256):
    e01, gates = _router(h, wr_perm, bias_perm, tm_router)
    starts, ends, pos0, pos1 = _dispatch_plan(e01[:MOE_TOPK], tm_expert)
    xs = _dispatch(h, gates, starts, ends, pos0, pos1, tm_expert, tc)
    ys = _experts(xs, starts, w_gate_up, w_down, layer, tm_expert)
    return _combine_ln(ys, pos0, pos1, g, beta, tc)


def _swap_halves(w):
    half = w.shape[-1] // 2
    return jnp.concatenate([w[..., half:], w[..., :half]], axis=-1)


def _even_weights(w_in, w_q_up, w_kv_up, forget_bias):
    d = w_in.shape[0]
    o_kv = MLA_Q_LORA
    o_kr = o_kv + MLA_KV_LORA
    o_fq = o_kr + MLA_ROPE
    hd = FOX_HEADS * FOX_HEAD_DIM
    o_fl = o_fq + 3 * hd
    k_r = w_in[:, o_kr:o_fq]
    f_l = jnp.pad(w_in[:, o_fl:], ((0, 0), (0, LANES - FOX_HEADS)))
    w_small = jnp.concatenate([w_in[:, :o_kr], k_r, _swap_halves(k_r), f_l], axis=1).astype(BF)
    w_fox = w_in[:, o_fq:o_fl].astype(BF)
    cs_fox = jnp.concatenate([jnp.full((1, hd), FOX_SCALE, F32), jnp.ones((1, 2 * hd), F32)], axis=1)
    wq = w_q_up.reshape(MLA_Q_LORA, MLA_HEADS, MLA_NOPE + MLA_ROPE)
    wqn = wq[:, :, :MLA_NOPE].reshape(MLA_Q_LORA, -1).astype(BF)
    pe = wq[:, :, MLA_NOPE:]
    wqp = jnp.concatenate([pe, _swap_halves(pe)], axis=-1).reshape(MLA_Q_LORA, -1).astype(BF)
    wkv = w_kv_up.reshape(MLA_KV_LORA, MLA_HEADS, MLA_NOPE + MLA_V)
    wk = wkv[:, :, :MLA_NOPE].reshape(MLA_KV_LORA, -1).astype(BF)
    wv = wkv[:, :, MLA_NOPE:].reshape(MLA_KV_LORA, -1).astype(BF)
    fb = jnp.pad(forget_bias.astype(F32), (0, LANES - FOX_HEADS)).reshape(1, LANES)
    return w_small, w_fox, cs_fox, wqn, wqp, wk, wv, fb


def _rope_table(seq):
    half = MLA_ROPE // 2
    inv_freq = ROPE_THETA ** (-jnp.arange(half, dtype=F32) / half)
    ang = jnp.arange(seq).astype(F32)[:, None] * inv_freq[None, :]
    cos, sin = jnp.cos(ang), jnp.sin(ang)
    return jnp.concatenate([cos, cos, -sin, sin], axis=1)


def _router_weights(w_router, router_bias):
    r = np.arange(N_EXPERTS)
    perm = (r % N_GROUPS) * EXPERTS_PER_GROUP + r // N_GROUPS
    wr = jnp.pad(w_router[:, perm], ((0, 0), (0, LANES - N_EXPERTS)))
    w_hi = wr.astype(BF)
    w_lo = (wr - w_hi.astype(F32)).astype(BF)
    return jnp.concatenate([w_hi, w_lo], axis=1), router_bias.astype(F32)[perm].reshape(N_EXPERTS, 1)


def _even_layer(h, batch, seq, w_in, q_norm, w_q_up, kv_norm, w_kv_up, forget_bias, w_out, g, beta):
    w_small, w_fox, cs_fox, wqn, wqp, wk, wv, fb = _even_weights(w_in, w_q_up, w_kv_up, forget_bias)
    fox_qkv = _mm(h, w_fox, cs_fox, BF, 512, 1536)
    qn, qp, kn, v, kp, forget_logits = _mla_prep(h, w_small, _rope_table(seq), q_norm.reshape(1, -1),
                                                 kv_norm.reshape(1, -1), wqn, wqp, wk, wv, seq, 512)
    q_bias, k_bias, edge = _fox_cum(forget_logits, fb, batch, seq, 512)
    a = _mla_attention(qn, qp, kn, v, kp, batch, seq, 512, 4)
    bo = _fox_attention(fox_qkv, q_bias, k_bias, edge, batch, seq, 512, 4)
    return _outproj_ln(a, bo, w_out.astype(BF), h, g, beta, 512)


def _odd_layer(h, batch, seq, w_in, sinks, w_out, g, beta):
    n_sq = SWA_Q_HEADS * SWA_HEAD_DIM
    n_skv = SWA_KV_HEADS * SWA_HEAD_DIM
    n_m = MOBA_HEADS * MOBA_HEAD_DIM
    o_mq = n_sq + 2 * n_skv
    w_big = jnp.concatenate([w_in[:, :n_sq], w_in[:, o_mq:], w_in[:, n_sq:o_mq]], axis=1).astype(BF)
    cs = jnp.concatenate([jnp.full((1, n_sq), SWA_SCALE, F32), jnp.full((1, n_m), MOBA_SCALE, F32),
                          jnp.ones((1, 2 * n_m + 2 * n_skv), F32)], axis=1)
    big = _mm(h, w_big, cs, BF, 512, 2176)
    blocks = lambda cols: cols // LANES
    k_mean = _moba_kmean(big, batch, seq, 2)
    slopes = jnp.asarray(LOG2E * 2.0 ** (-8.0 * np.arange(1, MOBA_HEADS + 1) / MOBA_HEADS), dtype=F32)
    c = _swa_attention(big, sinks.astype(F32), batch, seq, 0, blocks(n_sq + 3 * n_m), blocks(n_sq + 3 * n_m) + 1)
    dd = _moba_attention(big, k_mean, slopes, batch, seq, blocks(n_sq), blocks(n_sq + n_m), blocks(n_sq + 2 * n_m),
                         512, 4)
    return _outproj_ln(c, dd, w_out.astype(BF), h, g, beta, 512)


def kernel(x, w_router, router_bias, even_w_in, even_q_norm, even_w_q_up, even_kv_norm, even_w_kv_up,
           even_forget_bias, even_w_out, odd_w_in, odd_sinks, odd_w_out, ln_mix_g, ln_mix_b, ln_ffn_g, ln_ffn_b,
           w_gate_up, w_down):
    batch, seq, d = x.shape
    h = x.reshape(batch * seq, d)
    wr_perm, bias_perm = _router_weights(w_router, router_bias)
    row = lambda p, layer: p[layer].reshape(1, d)
    for layer in range(DEPTH):
        i = layer // 2
        if layer % 2 == 0:
            h = _even_layer(h, batch, seq, even_w_in[i], even_q_norm[i], even_w_q_up[i], even_kv_norm[i],
                            even_w_kv_up[i], even_forget_bias[i], even_w_out[i], row(ln_mix_g, layer),
                            row(ln_mix_b, layer))
        else:
            h = _odd_layer(h, batch, seq, odd_w_in[i], odd_sinks[i], odd_w_out[i], row(ln_mix_g, layer),
                           row(ln_mix_b, layer))
        h = _moe_ln(h, wr_perm, bias_perm, w_gate_up, w_down, layer, row(ln_ffn_g, layer), row(ln_ffn_b, layer))
    return h.reshape(batch, seq, d)
```

```python
import functools

import numpy as np
import jax
import jax.numpy as jnp
from jax import lax
from jax.experimental import pallas as pl
from jax.experimental.pallas import tpu as pltpu

BF = jnp.bfloat16
F32 = jnp.float32
I32 = jnp.int32

LANES = 128
SUBLANES = 8
VMEM_LIMIT = 56 * 1024 * 1024

D_MODEL = 2048
DEPTH = 2
LN_EPS = 1e-5
RMS_EPS = 1e-6
ALPHA = (2 * DEPTH) ** 0.25
MLA_HEADS, MLA_Q_LORA, MLA_KV_LORA, MLA_NOPE, MLA_ROPE, MLA_V = 8, 512, 256, 128, 64, 128
ROPE_THETA = 10000.0
FOX_HEADS, FOX_HEAD_DIM = 8, 128
SWA_Q_HEADS, SWA_KV_HEADS, SWA_HEAD_DIM, SWA_WINDOW = 16, 2, 64, 128
MOBA_HEADS, MOBA_HEAD_DIM, MOBA_BLOCK, MOBA_TOPK = 8, 128, 256, 3
N_EXPERTS, N_GROUPS, MOE_TOPK, EXPERT_FF = 32, 8, 2, 512
EXPERTS_PER_GROUP = N_EXPERTS // N_GROUPS

LOG2E = 1.4426950408889634
MLA_SCALE = (MLA_NOPE + MLA_ROPE) ** -0.5 * LOG2E
FOX_SCALE = FOX_HEAD_DIM ** -0.5 * LOG2E
MOBA_SCALE = MOBA_HEAD_DIM ** -0.5 * LOG2E
SWA_SCALE = SWA_HEAD_DIM ** -0.5 * LOG2E
NEG = -1e30

NT_DIMS = (((1,), (1,)), ((), ()))


def _params(*sem):
    return pltpu.CompilerParams(dimension_semantics=sem, vmem_limit_bytes=VMEM_LIMIT)


def _mm_body(x_ref, w_ref, cs_ref, o_ref):
    acc = jnp.dot(x_ref[...].astype(BF), w_ref[...], preferred_element_type=F32)
    o_ref[...] = (acc * cs_ref[...]).astype(o_ref.dtype)


def _mm(x, w, col_scale, out_dtype, tm, tn):
    m, k = x.shape
    n = w.shape[1]
    return pl.pallas_call(
        _mm_body,
        grid=(n // tn, m // tm),
        in_specs=[
            pl.BlockSpec((tm, k), lambda j, i: (i, 0)),
            pl.BlockSpec((k, tn), lambda j, i: (0, j)),
            pl.BlockSpec((1, tn), lambda j, i: (0, j)),
        ],
        out_specs=pl.BlockSpec((tm, tn), lambda j, i: (i, j)),
        out_shape=jax.ShapeDtypeStruct((m, n), out_dtype),
        compiler_params=_params("parallel", "parallel"),
        name="proj",
    )(x, w, col_scale)


def _rms(x, g):
    return x * lax.rsqrt(jnp.mean(x * x, axis=-1, keepdims=True) + RMS_EPS) * g


def _rope_pair(slab, table):
    r = slab * table
    return r + pltpu.roll(r, MLA_ROPE, axis=1)


def _mla_prep_body(x_ref, ws_ref, rope_ref, qg_ref, kvg_ref, wqn_ref, wqp_ref, wk_ref, wv_ref,
                   qn_ref, qp_ref, kn_ref, v_ref, kp_ref, fl_ref):
    small = jnp.dot(x_ref[...].astype(BF), ws_ref[...], preferred_element_type=F32)
    o_kv, o_kr, o_fl = MLA_Q_LORA, MLA_Q_LORA + MLA_KV_LORA, MLA_Q_LORA + MLA_KV_LORA + LANES
    fl_ref[...] = small[:, o_fl:]
    table = rope_ref[...]
    cqn = _rms(small[:, :o_kv], qg_ref[...]).astype(BF)
    qn_ref[...] = (jnp.dot(cqn, wqn_ref[...], preferred_element_type=F32) * MLA_SCALE).astype(BF)
    qp = jnp.dot(cqn, wqp_ref[...], preferred_element_type=F32)
    for h in range(MLA_HEADS):
        sl = slice(h * LANES, (h + 1) * LANES)
        qp_ref[:, sl] = (_rope_pair(qp[:, sl], table) * MLA_SCALE).astype(BF)
    ckvn = _rms(small[:, o_kv:o_kr], kvg_ref[...]).astype(BF)
    kn_ref[...] = jnp.dot(ckvn, wk_ref[...], preferred_element_type=F32).astype(BF)
    v_ref[...] = jnp.dot(ckvn, wv_ref[...], preferred_element_type=F32).astype(BF)
    kr = _rope_pair(small[:, o_kr:o_fl], table)
    lane = lax.broadcasted_iota(I32, kr.shape, 1)
    kp_ref[...] = jnp.where(lane < MLA_ROPE, kr, 0.0).astype(BF)


def _mla_prep(x, w_small, rope_table, q_norm, kv_norm, wqn, wqp, wk, wv, seq, tm):
    t, d = x.shape
    n_s = seq // tm
    hd = MLA_HEADS * LANES
    row = lambda i: (i, 0)
    const = lambda i: (0, 0)
    out = lambda w: pl.BlockSpec((tm, w), row)
    return pl.pallas_call(
        _mla_prep_body,
        grid=(t // tm,),
        in_specs=[
            pl.BlockSpec((tm, d), row),
            pl.BlockSpec(w_small.shape, const),
            pl.BlockSpec((tm, LANES), lambda i: (i % n_s, 0)),
            pl.BlockSpec((1, MLA_Q_LORA), const),
            pl.BlockSpec((1, MLA_KV_LORA), const),
            pl.BlockSpec((MLA_Q_LORA, hd), const),
            pl.BlockSpec((MLA_Q_LORA, hd), const),
            pl.BlockSpec((MLA_KV_LORA, hd), const),
            pl.BlockSpec((MLA_KV_LORA, hd), const),
        ],
        out_specs=[out(hd), out(hd), out(hd), out(hd), out(LANES), out(LANES)],
        out_shape=[jax.ShapeDtypeStruct((t, hd), BF)] * 4 + [jax.ShapeDtypeStruct((t, LANES), BF),
                                                             jax.ShapeDtypeStruct((t, LANES), F32)],
        compiler_params=_params("parallel"),
        name="mla_prep",
    )(x, w_small, rope_table, q_norm, kv_norm, wqn, wqp, wk, wv)


N_SPLIT = 3


def _split_piece(x, which):
    hi = x.astype(BF).astype(F32)
    rest = x - hi
    mid = rest.astype(BF).astype(F32)
    return jnp.where(which == 0, hi, jnp.where(which == 1, mid, rest - mid))


def _fox_cum_body(fl_ref, fb_ref, spread_ref, qe_ref, ke_ref, edge_ref, carry_ref):
    @pl.when(pl.program_id(1) == 0)
    def _():
        carry_ref[...] = jnp.zeros_like(carry_ref)

    z = fl_ref[...] + fb_ref[...]
    log_f = jnp.minimum(z, 0.0) - jnp.log1p(jnp.exp(-jnp.abs(z)))
    tm = z.shape[0]
    r = lax.broadcasted_iota(I32, (tm, tm), 0)
    c = lax.broadcasted_iota(I32, (tm, tm), 1)
    tri = jnp.where(r >= c, 1.0, 0.0).astype(BF)
    parts = jnp.dot(tri, jnp.concatenate([_split_piece(log_f, n).astype(BF) for n in range(N_SPLIT)], axis=1),
                    preferred_element_type=F32)
    cum = (parts[:, :LANES] + parts[:, LANES:2 * LANES]) + parts[:, 2 * LANES:] + carry_ref[...]
    carry_ref[...] = cum[tm - 1:tm, :]
    cum = cum * LOG2E
    width = qe_ref.shape[1]
    pieces = jnp.concatenate([_split_piece(cum, n).astype(BF) for n in range(N_SPLIT)], axis=1)
    placed = jnp.dot(pieces, spread_ref[...], preferred_element_type=F32)
    li = lax.broadcasted_iota(I32, (tm, width), 1) % LANES
    qe_ref[...] = jnp.where((li >= N_SPLIT) & (li < 2 * N_SPLIT), 1.0, placed[:, :width]).astype(BF)
    ke_ref[...] = jnp.where(li < N_SPLIT, 1.0, placed[:, width:]).astype(BF)
    edge_ref[0, 0:1, :] = cum[0:1, :]
    edge_ref[0, 1:2, :] = cum[tm - 1:tm, :]


def _fox_spread():
    width = FOX_HEADS * LANES
    m = np.zeros((N_SPLIT * LANES, 2 * width), np.float32)
    for j in range(N_SPLIT):
        for h in range(FOX_HEADS):
            m[j * LANES + h, h * LANES + j] = 1.0
            m[j * LANES + h, width + h * LANES + N_SPLIT + j] = -1.0
    return jnp.asarray(m, dtype=BF)


def _fox_cum(forget_logits, forget_bias_row, batch, seq, tm):
    t = forget_logits.shape[0]
    n_s = seq // tm
    width = FOX_HEADS * LANES
    q_bias, k_bias, edge = pl.pallas_call(
        _fox_cum_body,
        grid=(batch, n_s),
        in_specs=[
            pl.BlockSpec((tm, LANES), lambda b, i: (b * n_s + i, 0)),
            pl.BlockSpec((1, LANES), lambda b, i: (0, 0)),
            pl.BlockSpec((N_SPLIT * LANES, 2 * width), lambda b, i: (0, 0)),
        ],
        out_specs=[
            pl.BlockSpec((tm, width), lambda b, i: (b * n_s + i, 0)),
            pl.BlockSpec((tm, width), lambda b, i: (b * n_s + i, 0)),
            pl.BlockSpec((1, 2, LANES), lambda b, i: (b * n_s + i, 0, 0)),
        ],
        out_shape=[jax.ShapeDtypeStruct((t, width), BF), jax.ShapeDtypeStruct((t, width), BF),
                   jax.ShapeDtypeStruct((batch * n_s, 2, LANES), F32)],
        scratch_shapes=[pltpu.VMEM((1, LANES), F32)],
        compiler_params=_params("parallel", "arbitrary"),
        name="fox_cum",
    )(forget_logits, forget_bias_row, _fox_spread())
    return q_bias, k_bias, edge[:, :, :FOX_HEADS].reshape(-1)


def _softmax_update(s_all, v_all, m_ref, l_ref, acc_ref, rows=slice(None)):
    heads = range(len(s_all))
    m_prev = [m_ref[g, rows] for g in heads]
    m_new = [jnp.maximum(m_prev[g], jnp.max(s_all[g], axis=1, keepdims=True)) for g in heads]
    reps = s_all[0].shape[1] // LANES
    p = [jnp.exp2(s_all[g] - jnp.concatenate([m_new[g]] * reps, axis=1)) for g in heads]
    pv = [jnp.dot(p[g].astype(BF), v_all[g], preferred_element_type=F32) for g in heads]
    for g in heads:
        alpha = jnp.exp2(m_prev[g] - m_new[g])
        l_ref[g, rows] = alpha * l_ref[g, rows] + jnp.sum(p[g], axis=1, keepdims=True)
        acc_ref[g, rows] = alpha * acc_ref[g, rows] + pv[g]
        m_ref[g, rows] = m_new[g]


def _attn_body(*refs, mode, t, group):
    hg = pl.program_id(1)
    i = pl.program_id(2)
    heads = range(group)
    sl = lambda g: slice(g * LANES, (g + 1) * LANES)
    if mode == "moba":
        q_ref, k_ref, kx_ref, v_ref, km_ref, slope_ref, o_ref, m_ref, l_ref, acc_ref, qx_ref = refs
        for g in heads:
            qx_ref[g] = _moba_query_lanes(q_ref[:, sl(g)], km_ref[:, sl(g)], slope_ref[hg * group + g], i, t)
        q_extra = lambda g, rows: qx_ref[g, rows]
        k_extra = lambda g, ks, nk: kx_ref[pl.ds(ks, nk), :]
    elif mode == "mla":
        q_ref, qx_ref, k_ref, kx_ref, v_ref, o_ref, m_ref, l_ref, acc_ref = refs
        q_extra = lambda g, rows: qx_ref[rows, sl(g)]
        k_extra = lambda g, ks, nk: kx_ref[pl.ds(ks, nk), :]
    else:
        q_ref, qx_ref, k_ref, kx_ref, v_ref, edge_ref, o_ref, m_ref, l_ref, acc_ref, knorm_ref = refs
        q_extra = lambda g, rows: qx_ref[rows, sl(g)]
        k_extra = lambda g, ks, nk: kx_ref[pl.ds(ks, nk), sl(g)]
        first_tile = _fox_first_tile(q_ref, k_ref, edge_ref, knorm_ref, i, hg, t, group)

    def scores(g, ks, nk=t, rows=slice(None)):
        q = jnp.concatenate([q_ref[rows, sl(g)], q_extra(g, rows)], axis=1)
        k = jnp.concatenate([k_ref[pl.ds(ks, nk), sl(g)], k_extra(g, ks, nk)], axis=1)
        return lax.dot_general(q, k, NT_DIMS, preferred_element_type=F32)

    m_ref[...] = jnp.full(m_ref.shape, NEG, F32)
    l_ref[...] = jnp.zeros(l_ref.shape, F32)
    acc_ref[...] = jnp.zeros(acc_ref.shape, F32)

    values = lambda ks, nk=t: [v_ref[pl.ds(ks, nk), sl(g)] for g in heads]
    ks = pl.multiple_of(i * t, t)
    half = t // 2
    for rows, nk in ((slice(0, half), half), (slice(half, t), t)):
        r = lax.broadcasted_iota(I32, (half, nk), 0) + rows.start
        c = lax.broadcasted_iota(I32, (half, nk), 1)
        _softmax_update([jnp.where(c <= r, scores(g, ks, nk, rows), NEG) for g in heads], values(ks, nk),
                        m_ref, l_ref, acc_ref, rows)

    def past_tile(j, carry):
        ks = pl.multiple_of(j * t, t)
        _softmax_update([scores(g, ks) for g in heads], values(ks), m_ref, l_ref, acc_ref)
        return carry

    lax.fori_loop(first_tile if mode == "fox" else 0, i, past_tile, 0)
    for g in heads:
        o_ref[:, sl(g)] = (acc_ref[g] / l_ref[g]).astype(o_ref.dtype)


FOX_SKIP_GAP = 160.0


def _fox_first_tile(q_ref, k_ref, edge_ref, knorm_ref, i, hg, t, group):
    b = pl.program_id(0)
    n_q = pl.num_programs(2)
    sl = lambda g: slice(g * LANES, (g + 1) * LANES)

    def max_row_norm(x):
        x = x.astype(F32)
        return jnp.sqrt(jnp.max(jnp.sum(x * x, axis=1, keepdims=True)))

    @pl.when(i == 0)
    def _():
        for g in range(group):
            knorm_ref[g] = max_row_norm(k_ref[:, sl(g)])

    def edge(tile, last, g):
        return edge_ref[((b * n_q + tile) * 2 + last) * FOX_HEADS + hg * group + g]

    limit = [-(2.0 * 1.001 * max_row_norm(q_ref[:, sl(g)]) * knorm_ref[g] + FOX_SKIP_GAP) for g in range(group)]
    first = [edge(i, 0, g) for g in range(group)]

    def scan(j, lo):
        needed = first[0] - edge(j, 1, 0) >= limit[0]
        for g in range(1, group):
            needed = needed | (first[g] - edge(j, 1, g) >= limit[g])
        return jnp.where(needed, jnp.minimum(lo, j), lo)

    return lax.fori_loop(0, i, scan, i)


def _attention(mode, batch, seq, heads, t, group, operands, in_specs, extra_scratch=()):
    n_q = seq // t
    body = functools.partial(_attn_body, mode=mode, t=t, group=group)
    return pl.pallas_call(
        body,
        grid=(batch, heads // group, n_q),
        in_specs=in_specs,
        out_specs=pl.BlockSpec((t, group * LANES), lambda b, h, i: (b * n_q + i, h)),
        out_shape=jax.ShapeDtypeStruct((batch * seq, heads * LANES), BF),
        scratch_shapes=[pltpu.VMEM((group, t, LANES), F32), pltpu.VMEM((group, t, LANES), F32),
                        pltpu.VMEM((group, t, LANES), F32), *extra_scratch],
        compiler_params=_params("parallel", "parallel", "arbitrary"),
        name="attn_" + mode,
    )(*operands)


def _q_spec(t, n_q, group, col0=0):
    return pl.BlockSpec((t, group * LANES), lambda b, h, i: (b * n_q + i, col0 + h))


def _kv_spec(seq, group, col0=0):
    return pl.BlockSpec((seq, group * LANES), lambda b, h, i: (b, col0 + h))


def _mla_attention(qn, qp, kn, v, kp, batch, seq, t, group):
    n_q = seq // t
    specs = [_q_spec(t, n_q, group), _q_spec(t, n_q, group), _kv_spec(seq, group),
             pl.BlockSpec((seq, LANES), lambda b, h, i: (b, 0)), _kv_spec(seq, group)]
    return _attention("mla", batch, seq, MLA_HEADS, t, group, (qn, qp, kn, kp, v), specs)


def _fox_attention(qkv, q_bias, k_bias, edge, batch, seq, t, group):
    n_q = seq // t
    n_hg = FOX_HEADS // group
    specs = [_q_spec(t, n_q, group, 0), _q_spec(t, n_q, group, 0), _kv_spec(seq, group, n_hg),
             _kv_spec(seq, group, 0), _kv_spec(seq, group, 2 * n_hg), pl.BlockSpec(memory_space=pltpu.SMEM)]
    return _attention("fox", batch, seq, FOX_HEADS, t, group, (qkv, q_bias, qkv, k_bias, qkv, edge), specs,
                      (pltpu.SMEM((group,), F32),))


def _moba_key_lanes(seq):
    n_kb = seq // MOBA_BLOCK
    kpos = np.arange(seq)
    lanes = np.zeros((seq, LANES), np.float32)
    lanes[kpos, kpos // MOBA_BLOCK] = 1.0
    lanes[:, n_kb:n_kb + N_SPLIT] = 1.0
    lanes[:, n_kb + N_SPLIT:n_kb + 2 * N_SPLIT] = (MOBA_BLOCK * (kpos // MOBA_BLOCK))[:, None]
    lanes[:, n_kb + 2 * N_SPLIT:n_kb + 3 * N_SPLIT] = (kpos % MOBA_BLOCK)[:, None]
    return jnp.asarray(lanes, dtype=BF)


def _moba_query_lanes(q, k_mean, slope, i, t):
    n_kb = k_mean.shape[0]
    gate = lax.dot_general(k_mean.astype(BF), q, NT_DIMS, preferred_element_type=F32)
    blk = lax.broadcasted_iota(I32, (n_kb, t), 0)
    qpos_i = i * t + lax.broadcasted_iota(I32, (1, t), 1)
    own = qpos_i // MOBA_BLOCK
    beaten = jnp.zeros((n_kb, t), F32)
    for n in range(n_kb):
        g_n = gate[n:n + 1, :]
        wins = (g_n > gate) | ((g_n == gate) & (blk > n))
        beaten = beaten + jnp.where(wins & (own > n), 1.0, 0.0)
    attended = (blk == own) | ((blk < own) & (beaten < MOBA_TOPK))
    mask = jnp.concatenate([jnp.where(attended, 0.0, NEG), jnp.zeros((LANES - n_kb, t), F32)], axis=0)
    row = lax.broadcasted_iota(I32, (LANES, t), 0) - n_kb
    qpos = qpos_i.astype(F32)
    slope_row = jnp.full((1, t), slope, F32)
    which = row % N_SPLIT
    lanes = jnp.where(row < 0, mask,
                      jnp.where(row < N_SPLIT, _split_piece(-slope_row * qpos, which),
                                jnp.where(row < 3 * N_SPLIT, _split_piece(slope_row, which), 0.0)))
    return lanes.T.astype(BF)


def _moba_attention(big, k_mean, slopes, batch, seq, col_q, col_k, col_v, t, group):
    n_q = seq // t
    specs = [_q_spec(t, n_q, group, col_q // group), _kv_spec(seq, group, col_k // group),
             pl.BlockSpec((seq, LANES), lambda b, h, i: (0, 0)),
             _kv_spec(seq, group, col_v // group),
             pl.BlockSpec((seq // MOBA_BLOCK, group * LANES), lambda b, h, i: (b, h)),
             pl.BlockSpec(memory_space=pltpu.SMEM)]
    scratch = (pltpu.VMEM((group, t, LANES), BF),)
    return _attention("moba", batch, seq, MOBA_HEADS, t, group,
                      (big, big, _moba_key_lanes(seq), big, k_mean, slopes), specs, scratch)


def _kmean_body(k_ref, o_ref):
    n_kb = o_ref.shape[0]
    for n in range(n_kb):
        blk = k_ref[n * MOBA_BLOCK:(n + 1) * MOBA_BLOCK, :].astype(F32)
        o_ref[n:n + 1, :] = jnp.mean(blk, axis=0, keepdims=True)


def _moba_kmean(big, batch, seq, col_block):
    n_kb = seq // MOBA_BLOCK
    width = MOBA_HEADS * MOBA_HEAD_DIM
    return pl.pallas_call(
        _kmean_body,
        grid=(batch,),
        in_specs=[pl.BlockSpec((seq, width), lambda b: (b, col_block))],
        out_specs=pl.BlockSpec((n_kb, width), lambda b: (b, 0)),
        out_shape=jax.ShapeDtypeStruct((batch * n_kb, width), F32),
        compiler_params=_params("parallel"),
        name="moba_kmean",
    )(big)


def _swa_body(sink_ref, q_ref, kp_ref, kc_ref, vp_ref, vc_ref, o_ref):
    i = pl.program_id(1)
    w = SWA_WINDOW
    half = SWA_HEAD_DIM
    lane = lax.broadcasted_iota(I32, (w, LANES), 1)
    r = lax.broadcasted_iota(I32, (w, w), 0)
    c = lax.broadcasted_iota(I32, (w, w), 1)
    from_prev = c > r
    dist = (r - c + jnp.where(from_prev, w, 0)).astype(F32)
    valid = jnp.logical_not(from_prev) | (i > 0)
    heads_per_kv = SWA_Q_HEADS // SWA_KV_HEADS

    def lo_hi(ref, kvh):
        own = jnp.where((lane < half) == (kvh == 0), ref[...].astype(F32), 0.0)
        other = pltpu.roll(own, half, axis=1)
        lo, hi = (own, other) if kvh == 0 else (other, own)
        return lo.astype(BF), hi.astype(BF)

    for kvh in range(SWA_KV_HEADS):
        kp, kc, vp, vc = (lo_hi(ref, kvh) for ref in (kp_ref, kc_ref, vp_ref, vc_ref))
        heads = range(kvh * heads_per_kv, (kvh + 1) * heads_per_kv)
        q = {h: q_ref[:, (h // 2) * LANES:(h // 2 + 1) * LANES] for h in heads}
        s = {h: jnp.where(from_prev,
                          lax.dot_general(q[h], kp[h % 2], NT_DIMS, preferred_element_type=F32),
                          lax.dot_general(q[h], kc[h % 2], NT_DIMS, preferred_element_type=F32)) for h in heads}
        p, inv = {}, {}
        for h in heads:
            slope = float(LOG2E * 2.0 ** (-8.0 * (h + 1) / SWA_Q_HEADS))
            sink = sink_ref[h] * LOG2E
            logits = jnp.where(valid, s[h] - slope * dist, NEG)
            m = jnp.maximum(jnp.max(logits, axis=1, keepdims=True), sink)
            p[h] = jnp.exp2(logits - m)
            inv[h] = 1.0 / (jnp.sum(p[h], axis=1, keepdims=True) + jnp.exp2(sink - m))
        for pair in range(kvh * heads_per_kv // 2, (kvh + 1) * heads_per_kv // 2):
            out = jnp.zeros((w, LANES), F32)
            for h in (2 * pair, 2 * pair + 1):
                pn = p[h] * inv[h]
                out = out + jnp.dot(jnp.where(from_prev, pn, 0.0).astype(BF), vp[h % 2], preferred_element_type=F32)
                out = out + jnp.dot(jnp.where(from_prev, 0.0, pn).astype(BF), vc[h % 2], preferred_element_type=F32)
            o_ref[:, pair * LANES:(pair + 1) * LANES] = out.astype(o_ref.dtype)


def _swa_attention(big, sinks, batch, seq, col_q, col_k, col_v):
    w = SWA_WINDOW
    n_q = seq // w
    width = SWA_Q_HEADS * SWA_HEAD_DIM
    prev = lambda col: pl.BlockSpec((w, LANES), lambda b, i: (b * n_q + jnp.maximum(i - 1, 0), col))
    cur = lambda col: pl.BlockSpec((w, LANES), lambda b, i: (b * n_q + i, col))
    return pl.pallas_call(
        _swa_body,
        grid=(batch, n_q),
        in_specs=[pl.BlockSpec(memory_space=pltpu.SMEM),
                  pl.BlockSpec((w, width), lambda b, i: (b * n_q + i, col_q)),
                  prev(col_k), cur(col_k), prev(col_v), cur(col_v)],
        out_specs=pl.BlockSpec((w, width), lambda b, i: (b * n_q + i, 0)),
        out_shape=jax.ShapeDtypeStruct((batch * seq, width), BF),
        compiler_params=_params("parallel", "parallel"),
        name="swa",
    )(sinks, big, big, big, big, big)


def _layer_norm(z, g, b):
    mu = jnp.mean(z, axis=-1, keepdims=True)
    zc = z - mu
    var = jnp.mean(zc * zc, axis=-1, keepdims=True)
    return zc * lax.rsqrt(var + LN_EPS) * g + b


def _outproj_body(a_ref, b_ref, wa_ref, wb_ref, h_ref, g_ref, beta_ref, o_ref):
    mix = jnp.dot(a_ref[...], wa_ref[...], preferred_element_type=F32)
    mix = mix + jnp.dot(b_ref[...], wb_ref[...], preferred_element_type=F32)
    o_ref[...] = _layer_norm(ALPHA * h_ref[...] + mix, g_ref[...], beta_ref[...])


def _outproj_ln(a, b, w_out, h, g, beta, tm):
    t, d = h.shape
    ka = a.shape[1]
    row = lambda i: (i, 0)
    return pl.pallas_call(
        _outproj_body,
        grid=(t // tm,),
        in_specs=[pl.BlockSpec((tm, ka), row), pl.BlockSpec((tm, ka), row),
                  pl.BlockSpec((ka, d), lambda i: (0, 0)), pl.BlockSpec((ka, d), lambda i: (1, 0)),
                  pl.BlockSpec((tm, d), row), pl.BlockSpec((1, d), lambda i: (0, 0)),
                  pl.BlockSpec((1, d), lambda i: (0, 0))],
        out_specs=pl.BlockSpec((tm, d), row),
        out_shape=jax.ShapeDtypeStruct((t, d), F32),
        compiler_params=_params("parallel"),
        name="outproj_ln",
    )(a, b, w_out, w_out, h, g, beta)


def _second_of_four(a, b, c, d):
    hi_ab, lo_ab = jnp.maximum(a, b), jnp.minimum(a, b)
    hi_cd, lo_cd = jnp.maximum(c, d), jnp.minimum(c, d)
    return jnp.maximum(jnp.maximum(lo_ab, lo_cd), jnp.minimum(hi_ab, hi_cd))


def _router_body(h_ref, wr_ref, bias_ref, e_ref, r_ref):
    tm = h_ref.shape[0]
    h = h_ref[...]
    h_hi = h.astype(BF)
    h_lo = (h - h_hi.astype(F32)).astype(BF)
    both = jnp.dot(h_hi, wr_ref[...], preferred_element_type=F32)
    logits = (both[:, :LANES] + both[:, LANES:]) + jnp.dot(h_lo, wr_ref[:, :LANES], preferred_element_type=F32)
    lt = logits.T
    aff = [jax.nn.sigmoid(lt[SUBLANES * j:SUBLANES * (j + 1), :]) for j in range(EXPERTS_PER_GROUP)]
    sel = [aff[j] + bias_ref[SUBLANES * j:SUBLANES * (j + 1), :] for j in range(EXPERTS_PER_GROUP)]
    top1 = jnp.maximum(jnp.maximum(sel[0], sel[1]), jnp.maximum(sel[2], sel[3]))
    score = top1 + _second_of_four(*sel)
    gid = lax.broadcasted_iota(I32, (N_GROUPS, tm), 0)
    best = jnp.min(jnp.where(score == jnp.max(score, axis=0, keepdims=True), gid, N_GROUPS),
                   axis=0, keepdims=True)
    in_grp = gid == best
    pick = lambda x: jnp.sum(jnp.where(in_grp, x, 0.0), axis=0, keepdims=True)
    s4 = [pick(x) for x in sel]
    a4 = [pick(x) for x in aff]

    def argmax4(vals):
        j, v = jnp.zeros((1, tm), I32), vals[0]
        for n in range(1, EXPERTS_PER_GROUP):
            better = vals[n] > v
            j, v = jnp.where(better, n, j), jnp.where(better, vals[n], v)
        return j

    j0 = argmax4(s4)
    j1 = argmax4([jnp.where(j0 == n, -jnp.inf, s4[n]) for n in range(EXPERTS_PER_GROUP)])
    take = lambda j: sum(jnp.where(j == n, a4[n], 0.0) for n in range(EXPERTS_PER_GROUP))
    w0, w1 = take(j0), take(j1)
    total = w0 + w1
    e0 = best * EXPERTS_PER_GROUP + j0
    e1 = best * EXPERTS_PER_GROUP + j1
    rid = lax.broadcasted_iota(I32, (SUBLANES, tm), 0)
    e_ref[...] = jnp.where(rid == 0, e0, jnp.where(rid == 1, e1, 0))
    rid = lax.broadcasted_iota(I32, (LANES, tm), 0)
    rows = jnp.where(rid == 0, w0 / total, jnp.where(rid == 1, w1 / total, 0.0))
    r_ref[...] = rows.T


def _router(h, wr_perm, bias_perm, tm):
    t, d = h.shape
    return pl.pallas_call(
        _router_body,
        grid=(t // tm,),
        in_specs=[pl.BlockSpec((tm, d), lambda i: (i, 0)), pl.BlockSpec((d, 2 * LANES), lambda i: (0, 0)),
                  pl.BlockSpec((N_EXPERTS, 1), lambda i: (0, 0))],
        out_specs=[pl.BlockSpec((SUBLANES, tm), lambda i: (0, i)), pl.BlockSpec((tm, LANES), lambda i: (i, 0))],
        out_shape=[jax.ShapeDtypeStruct((SUBLANES, t), I32), jax.ShapeDtypeStruct((t, LANES), F32)],
        compiler_params=_params("parallel"),
        name="router",
    )(h, wr_perm, bias_perm)


def _dispatch_plan(e01, tm):
    t = e01.shape[1]
    flat = e01.reshape(-1)
    onehot = (flat[:, None] == jnp.arange(N_EXPERTS, dtype=I32)[None, :]).astype(I32)
    cum = jnp.cumsum(onehot, axis=0)
    rank = jnp.sum(onehot * cum, axis=1) - 1
    counts = cum[-1]
    padded = ((counts + tm - 1) // tm) * tm
    starts = jnp.concatenate([jnp.zeros((1,), I32), jnp.cumsum(padded).astype(I32)])
    pos = jnp.sum(onehot * starts[None, :N_EXPERTS], axis=1) + rank
    return starts, starts[:N_EXPERTS] + counts, pos[:t], pos[t:]


def _dispatch_body(p0_ref, p1_ref, gs_ref, ge_ref, h_ref, r_ref, xs_hbm, xbuf, zbuf, sem, zsem, *, tm, tt,
                   n_steps):
    i = pl.program_id(0)
    slot = i % 2
    n_tail = (xs_hbm.shape[0] - gs_ref[N_EXPERTS]) // tm

    def zero_copy(row0, size):
        return pltpu.make_async_copy(zbuf.at[pl.ds(0, size), :], xs_hbm.at[pl.ds(row0, size), :], zsem.at[0])

    def for_each_fill(fn):
        def group_pad(e, carry):
            end = gs_ref[e + 1]
            pad = end - ge_ref[e]
            covered = 0
            size = tm // 2
            while size >= SUBLANES:
                take = pad & size

                @pl.when(take != 0)
                def _():
                    fn(zero_copy(pl.multiple_of(end - covered - size, size), size))

                covered = covered + take
                size //= 2
            for k in range(SUBLANES - 1):
                @pl.when(k < (pad & (SUBLANES - 1)))
                def _():
                    fn(zero_copy(ge_ref[e] + k, 1))
            return carry

        def tail_tile(k, carry):
            fn(zero_copy(pl.multiple_of(gs_ref[N_EXPERTS] + k * tm, tm), tm))
            return carry

        lax.fori_loop(0, N_EXPERTS, group_pad, 0)
        lax.fori_loop(0, n_tail, tail_tile, 0)

    def wait_rows(sl):
        for _ in range(2):
            pltpu.make_async_copy(xbuf.at[sl, 0], xs_hbm.at[pl.ds(0, tt), :], sem.at[sl]).wait()

    @pl.when(i == 0)
    def _():
        zbuf[...] = jnp.zeros(zbuf.shape, F32)
        for_each_fill(lambda cp: cp.start())

    @pl.when(i >= 2)
    def _():
        wait_rows(slot)

    d = h_ref.shape[1]
    gates = r_ref[...]
    lane = lax.broadcasted_iota(I32, gates.shape, 1)
    for k in range(MOE_TOPK):
        xbuf[slot, k, :, :d] = h_ref[...]
    xbuf[slot, 0, :, d:] = jnp.where(lane == 0, gates, jnp.where(lane == 1, ALPHA, 0.0))
    xbuf[slot, 1, :, d:] = jnp.where(lane == 0, pltpu.roll(gates, LANES - 1, axis=1), 0.0)
    base = i * tt
    for r in range(tt):
        pltpu.make_async_copy(xbuf.at[slot, 0, pl.ds(r, 1), :], xs_hbm.at[pl.ds(p0_ref[base + r], 1), :],
                              sem.at[slot]).start(priority=0)
        pltpu.make_async_copy(xbuf.at[slot, 1, pl.ds(r, 1), :], xs_hbm.at[pl.ds(p1_ref[base + r], 1), :],
                              sem.at[slot]).start(priority=1)

    @pl.when(i == n_steps - 1)
    def _():
        wait_rows(slot)
        if n_steps > 1:
            wait_rows(1 - slot)
        for_each_fill(lambda cp: cp.wait())


def _dispatch(h, gates, starts, ends, pos0, pos1, tm, tt):
    t, d = h.shape
    dx = d + LANES
    n_steps = t // tt
    n_rows = 2 * t + N_EXPERTS * tm
    body = functools.partial(_dispatch_body, tm=tm, tt=tt, n_steps=n_steps)
    grid_spec = pltpu.PrefetchScalarGridSpec(
        num_scalar_prefetch=4,
        grid=(n_steps,),
        in_specs=[pl.BlockSpec((tt, d), lambda i, *_: (i, 0)), pl.BlockSpec((tt, LANES), lambda i, *_: (i, 0))],
        out_specs=pl.BlockSpec(memory_space=pl.ANY),
        scratch_shapes=[pltpu.VMEM((2, MOE_TOPK, tt, dx), F32), pltpu.VMEM((tm, dx), F32),
                        pltpu.SemaphoreType.DMA((2,)), pltpu.SemaphoreType.DMA((1,))],
    )
    return pl.pallas_call(
        body,
        grid_spec=grid_spec,
        out_shape=jax.ShapeDtypeStruct((n_rows, dx), F32),
        compiler_params=_params("arbitrary"),
        name="dispatch",
    )(pos0, pos1, starts, ends, h, gates)


def _experts_body(gs_ref, xs_hbm, wgu_ref, wd_ref, ys_hbm, xbuf, ybuf, wgu_bf, wd_bf, xsem, osem, *, tm):
    e = pl.program_id(0)
    first = gs_ref[e] // tm
    n_tiles = gs_ref[e + 1] // tm - first
    total = gs_ref[N_EXPERTS] // tm
    ring = xbuf.shape[0]

    def in_copy(g, slot):
        row0 = pl.multiple_of(g * tm, tm)
        return pltpu.make_async_copy(xs_hbm.at[pl.ds(row0, tm), :], xbuf.at[slot], xsem.at[slot])

    def out_copy(g, slot):
        row0 = pl.multiple_of(g * tm, tm)
        return pltpu.make_async_copy(ybuf.at[slot], ys_hbm.at[pl.ds(row0, tm), :], osem.at[slot])

    @pl.when(e == 0)
    def _():
        for g in range(ring - 1):
            in_copy(jnp.minimum(g, total - 1), g).start(priority=1)
        ybuf[...] = jnp.zeros(ybuf.shape, F32)
        out_copy(0, 0).start()
        out_copy(1, 1).start()

    @pl.when(n_tiles > 0)
    def _():
        wgu_bf[...] = wgu_ref[0, 0].astype(BF)
        wd_bf[...] = wd_ref[0, 0].astype(BF)

        def tile(k, carry):
            g = first + k
            slot = g % 2
            xslot = g % ring
            in_copy(g, xslot).wait()
            out_copy(g, slot).wait()
            in_copy(jnp.minimum(g + ring - 1, total - 1), (g + ring - 1) % ring).start(priority=1)
            d = ybuf.shape[2]
            x = xbuf[xslot, :, :d]
            row_gate = xbuf[xslot, :, d:d + 1]
            row_residual = xbuf[xslot, :, d + 1:d + 2]
            gu = jnp.dot(x.astype(BF), wgu_bf[...], preferred_element_type=F32)
            gate, up = gu[:, :EXPERT_FF], gu[:, EXPERT_FF:]
            hidden = (gate * jax.nn.sigmoid(gate) * up).astype(BF)
            y = jnp.dot(hidden, wd_bf[...], preferred_element_type=F32)
            ybuf[slot] = row_gate * y + row_residual * x
            out_copy(g, slot).start(priority=1)
            return carry

        lax.fori_loop(0, n_tiles, tile, 0)

    @pl.when(e == N_EXPERTS - 1)
    def _():
        for n in range(ring - 1):
            in_copy(0, (total + n) % ring).wait()
        out_copy(0, 0).wait()
        out_copy(1, 1).wait()
        end = gs_ref[N_EXPERTS]
        n_tail = (ys_hbm.shape[0] - end) // tm
        ybuf[0] = jnp.zeros(ybuf.shape[1:], F32)

        def tail_copy(k):
            row0 = pl.multiple_of(end + k * tm, tm)
            return pltpu.make_async_copy(ybuf.at[0], ys_hbm.at[pl.ds(row0, tm), :], osem.at[0])

        def start_one(k, carry):
            tail_copy(k).start()
            return carry

        def wait_one(k, carry):
            tail_copy(k).wait()
            return carry

        lax.fori_loop(0, n_tail, start_one, 0)
        lax.fori_loop(0, n_tail, wait_one, 0)


EXPERT_IN_RING = 6


def _experts(xs, starts, w_gate_up, w_down, layer, tm):
    n_rows, dx = xs.shape
    d = dx - LANES
    body = functools.partial(_experts_body, tm=tm)
    grid_spec = pltpu.PrefetchScalarGridSpec(
        num_scalar_prefetch=1,
        grid=(N_EXPERTS,),
        in_specs=[pl.BlockSpec(memory_space=pl.ANY),
                  pl.BlockSpec((1, 1, d, 2 * EXPERT_FF), lambda e, gs: (layer, e, 0, 0)),
                  pl.BlockSpec((1, 1, EXPERT_FF, d), lambda e, gs: (layer, e, 0, 0))],
        out_specs=pl.BlockSpec(memory_space=pl.ANY),
        scratch_shapes=[pltpu.VMEM((EXPERT_IN_RING, tm, dx), F32), pltpu.VMEM((2, tm, d), F32),
                        pltpu.VMEM((d, 2 * EXPERT_FF), BF), pltpu.VMEM((EXPERT_FF, d), BF),
                        pltpu.SemaphoreType.DMA((EXPERT_IN_RING,)), pltpu.SemaphoreType.DMA((2,))],
    )
    return pl.pallas_call(
        body,
        grid_spec=grid_spec,
        out_shape=jax.ShapeDtypeStruct((n_rows, d), F32),
        compiler_params=_params("arbitrary"),
        name="experts",
    )(starts, xs, w_gate_up, w_down)


def _combine_body(p0_ref, p1_ref, ys_hbm, g_ref, beta_ref, o_ref, buf0, buf1, sem, *, tc, n_steps):
    i = pl.program_id(0)
    ring = buf0.shape[0]
    slot = i % ring

    def start_row(base, r, sl):
        pltpu.make_async_copy(ys_hbm.at[pl.ds(p0_ref[base + r], 1), :], buf0.at[sl, pl.ds(r, 1), :],
                              sem.at[sl]).start(priority=0)
        pltpu.make_async_copy(ys_hbm.at[pl.ds(p1_ref[base + r], 1), :], buf1.at[sl, pl.ds(r, 1), :],
                              sem.at[sl]).start(priority=1)

    def wait_rows(sl):
        pltpu.make_async_copy(ys_hbm.at[pl.ds(0, tc), :], buf0.at[sl], sem.at[sl]).wait()
        pltpu.make_async_copy(ys_hbm.at[pl.ds(0, tc), :], buf1.at[sl], sem.at[sl]).wait()

    @pl.when(i == 0)
    def _():
        for step in range(ring - 1):
            def one(r, carry):
                start_row(min(step, n_steps - 1) * tc, r, step)
                return carry

            lax.fori_loop(0, tc, one, 0)

    wait_rows(slot)
    nxt = jnp.minimum(i + ring - 1, n_steps - 1) * tc
    for r in range(tc):
        start_row(nxt, r, (i + ring - 1) % ring)
    o_ref[...] = _layer_norm(buf0[slot] + buf1[slot], g_ref[...], beta_ref[...])

    @pl.when(i == n_steps - 1)
    def _():
        for n in range(1, ring):
            wait_rows((i + n) % ring)


COMBINE_RING = 3


def _combine_ln(ys, pos0, pos1, g, beta, tc):
    t, d = pos0.shape[0], ys.shape[1]
    n_steps = t // tc
    body = functools.partial(_combine_body, tc=tc, n_steps=n_steps)
    row = lambda i, p0, p1: (i, 0)
    const = lambda i, p0, p1: (0, 0)
    grid_spec = pltpu.PrefetchScalarGridSpec(
        num_scalar_prefetch=2,
        grid=(n_steps,),
        in_specs=[pl.BlockSpec(memory_space=pl.ANY), pl.BlockSpec((1, d), const), pl.BlockSpec((1, d), const)],
        out_specs=pl.BlockSpec((tc, d), row),
        scratch_shapes=[pltpu.VMEM((COMBINE_RING, tc, d), F32), pltpu.VMEM((COMBINE_RING, tc, d), F32),
                        pltpu.SemaphoreType.DMA((COMBINE_RING,))],
    )
    return pl.pallas_call(
        body,
        grid_spec=grid_spec,
        out_shape=jax.ShapeDtypeStruct((t, d), F32),
        compiler_params=_params("arbitrary"),
        name="combine_ln",
    )(pos0, pos1, ys, g, beta)


def _moe_ln(h, wr_perm, bias_perm, w_gate_up, w_down, layer, g, beta, tm_router=512, tm_expert=256, tc=256):
    e01, gates = _router(h, wr_perm, bias_perm, tm_router)
    starts, ends, pos0, pos1 = _dispatch_plan(e01[:MOE_TOPK], tm_expert)
    xs = _dispatch(h, gates, starts, ends, pos0, pos1, tm_expert, tc)
    ys = _experts(xs, starts, w_gate_up, w_down, layer, tm_expert)
    return _combine_ln(ys, pos0, pos1, g, beta, tc)


def _swap_halves(w):
    half = w.shape[-1] // 2
    return jnp.concatenate([w[..., half:], w[..., :half]], axis=-1)


def _even_weights(w_in, w_q_up, w_kv_up, forget_bias):
    d = w_in.shape[0]
    o_kv = MLA_Q_LORA
    o_kr = o_kv + MLA_KV_LORA
    o_fq = o_kr + MLA_ROPE
    hd = FOX_HEADS * FOX_HEAD_DIM
    o_fl = o_fq + 3 * hd
    k_r = w_in[:, o_kr:o_fq]
    f_l = jnp.pad(w_in[:, o_fl:], ((0, 0), (0, LANES - FOX_HEADS)))
    w_small = jnp.concatenate([w_in[:, :o_kr], k_r, _swap_halves(k_r), f_l], axis=1).astype(BF)
    w_fox = w_in[:, o_fq:o_fl].astype(BF)
    cs_fox = jnp.concatenate([jnp.full((1, hd), FOX_SCALE, F32), jnp.ones((1, 2 * hd), F32)], axis=1)
    wq = w_q_up.reshape(MLA_Q_LORA, MLA_HEADS, MLA_NOPE + MLA_ROPE)
    wqn = wq[:, :, :MLA_NOPE].reshape(MLA_Q_LORA, -1).astype(BF)
    pe = wq[:, :, MLA_NOPE:]
    wqp = jnp.concatenate([pe, _swap_halves(pe)], axis=-1).reshape(MLA_Q_LORA, -1).astype(BF)
    wkv = w_kv_up.reshape(MLA_KV_LORA, MLA_HEADS, MLA_NOPE + MLA_V)
    wk = wkv[:, :, :MLA_NOPE].reshape(MLA_KV_LORA, -1).astype(BF)
    wv = wkv[:, :, MLA_NOPE:].reshape(MLA_KV_LORA, -1).astype(BF)
    fb = jnp.pad(forget_bias.astype(F32), (0, LANES - FOX_HEADS)).reshape(1, LANES)
    return w_small, w_fox, cs_fox, wqn, wqp, wk, wv, fb


def _rope_table(seq):
    half = MLA_ROPE // 2
    inv_freq = ROPE_THETA ** (-jnp.arange(half, dtype=F32) / half)
    ang = jnp.arange(seq).astype(F32)[:, None] * inv_freq[None, :]
    cos, sin = jnp.cos(ang), jnp.sin(ang)
    return jnp.concatenate([cos, cos, -sin, sin], axis=1)


def _router_weights(w_router, router_bias):
    r = np.arange(N_EXPERTS)
    perm = (r % N_GROUPS) * EXPERTS_PER_GROUP + r // N_GROUPS
    wr = jnp.pad(w_router[:, perm], ((0, 0), (0, LANES - N_EXPERTS)))
    w_hi = wr.astype(BF)
    w_lo = (wr - w_hi.astype(F32)).astype(BF)
    return jnp.concatenate([w_hi, w_lo], axis=1), router_bias.astype(F32)[perm].reshape(N_EXPERTS, 1)


def _even_layer(h, batch, seq, w_in, q_norm, w_q_up, kv_norm, w_kv_up, forget_bias, w_out, g, beta):
    w_small, w_fox, cs_fox, wqn, wqp, wk, wv, fb = _even_weights(w_in, w_q_up, w_kv_up, forget_bias)
    fox_qkv = _mm(h, w_fox, cs_fox, BF, 512, 1536)
    qn, qp, kn, v, kp, forget_logits = _mla_prep(h, w_small, _rope_table(seq), q_norm.reshape(1, -1),
                                                 kv_norm.reshape(1, -1), wqn, wqp, wk, wv, seq, 512)
    q_bias, k_bias, edge = _fox_cum(forget_logits, fb, batch, seq, 512)
    a = _mla_attention(qn, qp, kn, v, kp, batch, seq, 512, 4)
    bo = _fox_attention(fox_qkv, q_bias, k_bias, edge, batch, seq, 512, 4)
    return _outproj_ln(a, bo, w_out.astype(BF), h, g, beta, 512)


def _odd_layer(h, batch, seq, w_in, sinks, w_out, g, beta):
    n_sq = SWA_Q_HEADS * SWA_HEAD_DIM
    n_skv = SWA_KV_HEADS * SWA_HEAD_DIM
    n_m = MOBA_HEADS * MOBA_HEAD_DIM
    o_mq = n_sq + 2 * n_skv
    w_big = jnp.concatenate([w_in[:, :n_sq], w_in[:, o_mq:], w_in[:, n_sq:o_mq]], axis=1).astype(BF)
    cs = jnp.concatenate([jnp.full((1, n_sq), SWA_SCALE, F32), jnp.full((1, n_m), MOBA_SCALE, F32),
                          jnp.ones((1, 2 * n_m + 2 * n_skv), F32)], axis=1)
    big = _mm(h, w_big, cs, BF, 512, 2176)
    blocks = lambda cols: cols // LANES
    k_mean = _moba_kmean(big, batch, seq, 2)
    slopes = jnp.asarray(LOG2E * 2.0 ** (-8.0 * np.arange(1, MOBA_HEADS + 1) / MOBA_HEADS), dtype=F32)
    c = _swa_attention(big, sinks.astype(F32), batch, seq, 0, blocks(n_sq + 3 * n_m), blocks(n_sq + 3 * n_m) + 1)
    dd = _moba_attention(big, k_mean, slopes, batch, seq, blocks(n_sq), blocks(n_sq + n_m), blocks(n_sq + 2 * n_m),
                         512, 4)
    return _outproj_ln(c, dd, w_out.astype(BF), h, g, beta, 512)


def kernel(x, w_router, router_bias, even_w_in, even_q_norm, even_w_q_up, even_kv_norm, even_w_kv_up,
           even_forget_bias, even_w_out, odd_w_in, odd_sinks, odd_w_out, ln_mix_g, ln_mix_b, ln_ffn_g, ln_ffn_b,
           w_gate_up, w_down):
    batch, seq, d = x.shape
    h = x.reshape(batch * seq, d)
    wr_perm, bias_perm = _router_weights(w_router, router_bias)
    row = lambda p, layer: p[layer].reshape(1, d)
    for layer in range(DEPTH):
        i = layer // 2
        if layer % 2 == 0:
            h = _even_layer(h, batch, seq, even_w_in[i], even_q_norm[i], even_w_q_up[i], even_kv_norm[i],
                            even_w_kv_up[i], even_forget_bias[i], even_w_out[i], row(ln_mix_g, layer),
                            row(ln_mix_b, layer))
        else:
            h = _odd_layer(h, batch, seq, odd_w_in[i], odd_sinks[i], odd_w_out[i], row(ln_mix_g, layer),
                           row(ln_mix_b, layer))
        h = _moe_ln(h, wr_perm, bias_perm, w_gate_up, w_down, layer, row(ln_ffn_g, layer), row(ln_ffn_b, layer))
    return h.reshape(batch, seq, d)
```

```python
import functools

import numpy as np
import jax
import jax.numpy as jnp
from jax import lax
from jax.experimental import pallas as pl
from jax.experimental.pallas import tpu as pltpu

BF = jnp.bfloat16
F32 = jnp.float32
I32 = jnp.int32

LANES = 128
SUBLANES = 8
VMEM_LIMIT = 56 * 1024 * 1024

ROW_TILE = 512
HEAD_GROUP = 4

DEPTH = 2
LN_EPS = 1e-5
RMS_EPS = 1e-6
ALPHA = (2 * DEPTH) ** 0.25
MLA_HEADS, MLA_Q_LORA, MLA_KV_LORA, MLA_NOPE, MLA_ROPE, MLA_V = 8, 512, 256, 128, 64, 128
ROPE_THETA = 10000.0
FOX_HEADS, FOX_HEAD_DIM = 8, 128
SWA_Q_HEADS, SWA_KV_HEADS, SWA_HEAD_DIM, SWA_WINDOW = 16, 2, 64, 128
MOBA_HEADS, MOBA_HEAD_DIM, MOBA_BLOCK, MOBA_TOPK = 8, 128, 256, 3
N_EXPERTS, N_GROUPS, MOE_TOPK, EXPERT_FF = 32, 8, 2, 512
EXPERTS_PER_GROUP = N_EXPERTS // N_GROUPS

LOG2E = 1.4426950408889634
MLA_SCALE = (MLA_NOPE + MLA_ROPE) ** -0.5 * LOG2E
FOX_SCALE = FOX_HEAD_DIM ** -0.5 * LOG2E
MOBA_SCALE = MOBA_HEAD_DIM ** -0.5 * LOG2E
SWA_SCALE = SWA_HEAD_DIM ** -0.5 * LOG2E
NEG = -1e30

NT_DIMS = (((1,), (1,)), ((), ()))


def _params(*sem):
    return pltpu.CompilerParams(dimension_semantics=sem, vmem_limit_bytes=VMEM_LIMIT)


def _mm_body(x_ref, w_ref, cs_ref, o_ref):
    acc = jnp.dot(x_ref[...].astype(BF), w_ref[...], preferred_element_type=F32)
    o_ref[...] = (acc * cs_ref[...]).astype(o_ref.dtype)


def _mm(x, w, col_scale, out_dtype, tm, tn):
    m, k = x.shape
    n = w.shape[1]
    return pl.pallas_call(
        _mm_body,
        grid=(n // tn, m // tm),
        in_specs=[
            pl.BlockSpec((tm, k), lambda j, i: (i, 0)),
            pl.BlockSpec((k, tn), lambda j, i: (0, j)),
            pl.BlockSpec((1, tn), lambda j, i: (0, j)),
        ],
        out_specs=pl.BlockSpec((tm, tn), lambda j, i: (i, j)),
        out_shape=jax.ShapeDtypeStruct((m, n), out_dtype),
        compiler_params=_params("parallel", "parallel"),
        name="proj",
    )(x, w, col_scale)


def _rms(x, g):
    return x * lax.rsqrt(jnp.mean(x * x, axis=-1, keepdims=True) + RMS_EPS) * g


def _rope_pair(slab, table):
    r = slab * table
    return r + pltpu.roll(r, MLA_ROPE, axis=1)


def _mla_prep_body(x_ref, ws_ref, rope_ref, qg_ref, kvg_ref, wqn_ref, wqp_ref, wk_ref, wv_ref,
                   qn_ref, qp_ref, kn_ref, v_ref, kp_ref, fl_ref):
    small = jnp.dot(x_ref[...].astype(BF), ws_ref[...], preferred_element_type=F32)
    o_kv, o_kr, o_fl = MLA_Q_LORA, MLA_Q_LORA + MLA_KV_LORA, MLA_Q_LORA + MLA_KV_LORA + LANES
    fl_ref[...] = small[:, o_fl:]
    table = rope_ref[...]
    cqn = _rms(small[:, :o_kv], qg_ref[...]).astype(BF)
    qn_ref[...] = (jnp.dot(cqn, wqn_ref[...], preferred_element_type=F32) * MLA_SCALE).astype(BF)
    qp = jnp.dot(cqn, wqp_ref[...], preferred_element_type=F32)
    for h in range(MLA_HEADS):
        sl = slice(h * LANES, (h + 1) * LANES)
        qp_ref[:, sl] = (_rope_pair(qp[:, sl], table) * MLA_SCALE).astype(BF)
    ckvn = _rms(small[:, o_kv:o_kr], kvg_ref[...]).astype(BF)
    kn_ref[...] = jnp.dot(ckvn, wk_ref[...], preferred_element_type=F32).astype(BF)
    v_ref[...] = jnp.dot(ckvn, wv_ref[...], preferred_element_type=F32).astype(BF)
    kr = _rope_pair(small[:, o_kr:o_fl], table)
    lane = lax.broadcasted_iota(I32, kr.shape, 1)
    kp_ref[...] = jnp.where(lane < MLA_ROPE, kr, 0.0).astype(BF)


def _mla_prep(x, w_small, rope_table, q_norm, kv_norm, wqn, wqp, wk, wv, seq, tm):
    t, d = x.shape
    n_s = seq // tm
    hd = MLA_HEADS * LANES
    row = lambda i: (i, 0)
    const = lambda i: (0, 0)
    out = lambda w: pl.BlockSpec((tm, w), row)
    return pl.pallas_call(
        _mla_prep_body,
        grid=(t // tm,),
        in_specs=[
            pl.BlockSpec((tm, d), row),
            pl.BlockSpec(w_small.shape, const),
            pl.BlockSpec((tm, LANES), lambda i: (i % n_s, 0)),
            pl.BlockSpec((1, MLA_Q_LORA), const),
            pl.BlockSpec((1, MLA_KV_LORA), const),
            pl.BlockSpec((MLA_Q_LORA, hd), const),
            pl.BlockSpec((MLA_Q_LORA, hd), const),
            pl.BlockSpec((MLA_KV_LORA, hd), const),
            pl.BlockSpec((MLA_KV_LORA, hd), const),
        ],
        out_specs=[out(hd), out(hd), out(hd), out(hd), out(LANES), out(LANES)],
        out_shape=[jax.ShapeDtypeStruct((t, hd), BF)] * 4 + [jax.ShapeDtypeStruct((t, LANES), BF),
                                                             jax.ShapeDtypeStruct((t, LANES), F32)],
        compiler_params=_params("parallel"),
        name="mla_prep",
    )(x, w_small, rope_table, q_norm, kv_norm, wqn, wqp, wk, wv)


N_SPLIT = 3


def _split_piece(x, which):
    hi = x.astype(BF).astype(F32)
    rest = x - hi
    mid = rest.astype(BF).astype(F32)
    return jnp.where(which == 0, hi, jnp.where(which == 1, mid, rest - mid))


def _fox_cum_body(fl_ref, fb_ref, spread_ref, qe_ref, ke_ref, edge_ref, carry_ref):
    @pl.when(pl.program_id(1) == 0)
    def _():
        carry_ref[...] = jnp.zeros_like(carry_ref)

    z = fl_ref[...] + fb_ref[...]
    log_f = jnp.minimum(z, 0.0) - jnp.log1p(jnp.exp(-jnp.abs(z)))
    tm = z.shape[0]
    r = lax.broadcasted_iota(I32, (tm, tm), 0)
    c = lax.broadcasted_iota(I32, (tm, tm), 1)
    tri = jnp.where(r >= c, 1.0, 0.0).astype(BF)
    parts = jnp.dot(tri, jnp.concatenate([_split_piece(log_f, n).astype(BF) for n in range(N_SPLIT)], axis=1),
                    preferred_element_type=F32)
    cum = (parts[:, :LANES] + parts[:, LANES:2 * LANES]) + parts[:, 2 * LANES:] + carry_ref[...]
    carry_ref[...] = cum[tm - 1:tm, :]
    cum = cum * LOG2E
    width = qe_ref.shape[1]
    pieces = jnp.concatenate([_split_piece(cum, n).astype(BF) for n in range(N_SPLIT)], axis=1)
    placed = jnp.dot(pieces, spread_ref[...], preferred_element_type=F32)
    li = lax.broadcasted_iota(I32, (tm, width), 1) % LANES
    qe_ref[...] = jnp.where((li >= N_SPLIT) & (li < 2 * N_SPLIT), 1.0, placed[:, :width]).astype(BF)
    ke_ref[...] = jnp.where(li < N_SPLIT, 1.0, placed[:, width:]).astype(BF)
    edge_ref[0, 0:1, :] = cum[0:1, :]
    edge_ref[0, 1:2, :] = cum[tm - 1:tm, :]


def _fox_spread():
    width = FOX_HEADS * LANES
    m = np.zeros((N_SPLIT * LANES, 2 * width), np.float32)
    for j in range(N_SPLIT):
        for h in range(FOX_HEADS):
            m[j * LANES + h, h * LANES + j] = 1.0
            m[j * LANES + h, width + h * LANES + N_SPLIT + j] = -1.0
    return jnp.asarray(m, dtype=BF)


def _fox_cum(forget_logits, forget_bias_row, batch, seq, tm):
    t = forget_logits.shape[0]
    n_s = seq // tm
    width = FOX_HEADS * LANES
    q_bias, k_bias, edge = pl.pallas_call(
        _fox_cum_body,
        grid=(batch, n_s),
        in_specs=[
            pl.BlockSpec((tm, LANES), lambda b, i: (b * n_s + i, 0)),
            pl.BlockSpec((1, LANES), lambda b, i: (0, 0)),
            pl.BlockSpec((N_SPLIT * LANES, 2 * width), lambda b, i: (0, 0)),
        ],
        out_specs=[
            pl.BlockSpec((tm, width), lambda b, i: (b * n_s + i, 0)),
            pl.BlockSpec((tm, width), lambda b, i: (b * n_s + i, 0)),
            pl.BlockSpec((1, 2, LANES), lambda b, i: (b * n_s + i, 0, 0)),
        ],
        out_shape=[jax.ShapeDtypeStruct((t, width), BF), jax.ShapeDtypeStruct((t, width), BF),
                   jax.ShapeDtypeStruct((batch * n_s, 2, LANES), F32)],
        scratch_shapes=[pltpu.VMEM((1, LANES), F32)],
        compiler_params=_params("parallel", "arbitrary"),
        name="fox_cum",
    )(forget_logits, forget_bias_row, _fox_spread())
    return q_bias, k_bias, edge[:, :, :FOX_HEADS].reshape(-1)


def _softmax_update(s_all, v_all, m_ref, l_ref, acc_ref, rows=slice(None)):
    heads = range(len(s_all))
    m_prev = [m_ref[g, rows] for g in heads]
    m_new = [jnp.maximum(m_prev[g], jnp.max(s_all[g], axis=1, keepdims=True)) for g in heads]
    reps = s_all[0].shape[1] // LANES
    p = [jnp.exp2(s_all[g] - jnp.concatenate([m_new[g]] * reps, axis=1)) for g in heads]
    pv = [jnp.dot(p[g].astype(BF), v_all[g], preferred_element_type=F32) for g in heads]
    for g in heads:
        alpha = jnp.exp2(m_prev[g] - m_new[g])
        l_ref[g, rows] = alpha * l_ref[g, rows] + jnp.sum(p[g], axis=1, keepdims=True)
        acc_ref[g, rows] = alpha * acc_ref[g, rows] + pv[g]
        m_ref[g, rows] = m_new[g]


def _attn_body(*refs, mode, t, group):
    hg = pl.program_id(1)
    i = pl.program_id(2)
    heads = range(group)
    sl = lambda g: slice(g * LANES, (g + 1) * LANES)
    if mode == "moba":
        q_ref, k_ref, kx_ref, v_ref, km_ref, slope_ref, o_ref, m_ref, l_ref, acc_ref, qx_ref = refs
        for g in heads:
            qx_ref[g] = _moba_query_lanes(q_ref[:, sl(g)], km_ref[:, sl(g)], slope_ref[hg * group + g], i, t)
        q_extra = lambda g, rows: qx_ref[g, rows]
        k_extra = lambda g, ks, nk: kx_ref[pl.ds(ks, nk), :]
    elif mode == "mla":
        q_ref, qx_ref, k_ref, kx_ref, v_ref, o_ref, m_ref, l_ref, acc_ref = refs
        q_extra = lambda g, rows: qx_ref[rows, sl(g)]
        k_extra = lambda g, ks, nk: kx_ref[pl.ds(ks, nk), :]
    else:
        q_ref, qx_ref, k_ref, kx_ref, v_ref, edge_ref, o_ref, m_ref, l_ref, acc_ref, knorm_ref = refs
        q_extra = lambda g, rows: qx_ref[rows, sl(g)]
        k_extra = lambda g, ks, nk: kx_ref[pl.ds(ks, nk), sl(g)]
        first_tile = _fox_first_tile(q_ref, k_ref, edge_ref, knorm_ref, i, hg, t, group)

    def scores(g, ks, nk=t, rows=slice(None)):
        q = jnp.concatenate([q_ref[rows, sl(g)], q_extra(g, rows)], axis=1)
        k = jnp.concatenate([k_ref[pl.ds(ks, nk), sl(g)], k_extra(g, ks, nk)], axis=1)
        return lax.dot_general(q, k, NT_DIMS, preferred_element_type=F32)

    m_ref[...] = jnp.full(m_ref.shape, NEG, F32)
    l_ref[...] = jnp.zeros(l_ref.shape, F32)
    acc_ref[...] = jnp.zeros(acc_ref.shape, F32)

    values = lambda ks, nk=t: [v_ref[pl.ds(ks, nk), sl(g)] for g in heads]
    ks = pl.multiple_of(i * t, t)
    half = t // 2
    for rows, nk in ((slice(0, half), half), (slice(half, t), t)):
        r = lax.broadcasted_iota(I32, (half, nk), 0) + rows.start
        c = lax.broadcasted_iota(I32, (half, nk), 1)
        _softmax_update([jnp.where(c <= r, scores(g, ks, nk, rows), NEG) for g in heads], values(ks, nk),
                        m_ref, l_ref, acc_ref, rows)

    def past_tile(j, carry):
        ks = pl.multiple_of(j * t, t)
        _softmax_update([scores(g, ks) for g in heads], values(ks), m_ref, l_ref, acc_ref)
        return carry

    lax.fori_loop(first_tile if mode == "fox" else 0, i, past_tile, 0)
    for g in heads:
        o_ref[:, sl(g)] = (acc_ref[g] / l_ref[g]).astype(o_ref.dtype)


FOX_SKIP_GAP = 160.0


def _fox_first_tile(q_ref, k_ref, edge_ref, knorm_ref, i, hg, t, group):
    b = pl.program_id(0)
    n_q = pl.num_programs(2)
    sl = lambda g: slice(g * LANES, (g + 1) * LANES)

    def max_row_norm(x):
        x = x.astype(F32)
        return jnp.sqrt(jnp.max(jnp.sum(x * x, axis=1, keepdims=True)))

    @pl.when(i == 0)
    def _():
        for g in range(group):
            knorm_ref[g] = max_row_norm(k_ref[:, sl(g)])

    def edge(tile, last, g):
        return edge_ref[((b * n_q + tile) * 2 + last) * FOX_HEADS + hg * group + g]

    limit = [-(2.0 * 1.001 * max_row_norm(q_ref[:, sl(g)]) * knorm_ref[g] + FOX_SKIP_GAP) for g in range(group)]
    first = [edge(i, 0, g) for g in range(group)]

    def scan(j, lo):
        needed = first[0] - edge(j, 1, 0) >= limit[0]
        for g in range(1, group):
            needed = needed | (first[g] - edge(j, 1, g) >= limit[g])
        return jnp.where(needed, jnp.minimum(lo, j), lo)

    return lax.fori_loop(0, i, scan, i)


def _attention(mode, batch, seq, heads, t, group, operands, in_specs, extra_scratch=()):
    n_q = seq // t
    body = functools.partial(_attn_body, mode=mode, t=t, group=group)
    return pl.pallas_call(
        body,
        grid=(batch, heads // group, n_q),
        in_specs=in_specs,
        out_specs=pl.BlockSpec((t, group * LANES), lambda b, h, i: (b * n_q + i, h)),
        out_shape=jax.ShapeDtypeStruct((batch * seq, heads * LANES), BF),
        scratch_shapes=[pltpu.VMEM((group, t, LANES), F32), pltpu.VMEM((group, t, LANES), F32),
                        pltpu.VMEM((group, t, LANES), F32), *extra_scratch],
        compiler_params=_params("parallel", "parallel", "arbitrary"),
        name="attn_" + mode,
    )(*operands)


def _q_spec(t, n_q, group, col0=0):
    return pl.BlockSpec((t, group * LANES), lambda b, h, i: (b * n_q + i, col0 + h))


def _kv_spec(seq, group, col0=0):
    return pl.BlockSpec((seq, group * LANES), lambda b, h, i: (b, col0 + h))


def _mla_attention(qn, qp, kn, v, kp, batch, seq, t, group):
    n_q = seq // t
    specs = [_q_spec(t, n_q, group), _q_spec(t, n_q, group), _kv_spec(seq, group),
             pl.BlockSpec((seq, LANES), lambda b, h, i: (b, 0)), _kv_spec(seq, group)]
    return _attention("mla", batch, seq, MLA_HEADS, t, group, (qn, qp, kn, kp, v), specs)


def _fox_attention(qkv, q_bias, k_bias, edge, batch, seq, t, group):
    n_q = seq // t
    n_hg = FOX_HEADS // group
    specs = [_q_spec(t, n_q, group, 0), _q_spec(t, n_q, group, 0), _kv_spec(seq, group, n_hg),
             _kv_spec(seq, group, 0), _kv_spec(seq, group, 2 * n_hg), pl.BlockSpec(memory_space=pltpu.SMEM)]
    return _attention("fox", batch, seq, FOX_HEADS, t, group, (qkv, q_bias, qkv, k_bias, qkv, edge), specs,
                      (pltpu.SMEM((group,), F32),))


def _moba_key_lanes(seq):
    n_kb = seq // MOBA_BLOCK
    kpos = np.arange(seq)
    lanes = np.zeros((seq, LANES), np.float32)
    lanes[kpos, kpos // MOBA_BLOCK] = 1.0
    lanes[:, n_kb:n_kb + N_SPLIT] = 1.0
    lanes[:, n_kb + N_SPLIT:n_kb + 2 * N_SPLIT] = (MOBA_BLOCK * (kpos // MOBA_BLOCK))[:, None]
    lanes[:, n_kb + 2 * N_SPLIT:n_kb + 3 * N_SPLIT] = (kpos % MOBA_BLOCK)[:, None]
    return jnp.asarray(lanes, dtype=BF)


def _moba_query_lanes(q, k_mean, slope, i, t):
    n_kb = k_mean.shape[0]
    gate = lax.dot_general(k_mean.astype(BF), q, NT_DIMS, preferred_element_type=F32)
    blk = lax.broadcasted_iota(I32, (n_kb, t), 0)
    qpos_i = i * t + lax.broadcasted_iota(I32, (1, t), 1)
    own = qpos_i // MOBA_BLOCK
    beaten = jnp.zeros((n_kb, t), F32)
    for n in range(n_kb):
        g_n = gate[n:n + 1, :]
        wins = (g_n > gate) | ((g_n == gate) & (blk > n))
        beaten = beaten + jnp.where(wins & (own > n), 1.0, 0.0)
    attended = (blk == own) | ((blk < own) & (beaten < MOBA_TOPK))
    mask = jnp.concatenate([jnp.where(attended, 0.0, NEG), jnp.zeros((LANES - n_kb, t), F32)], axis=0)
    row = lax.broadcasted_iota(I32, (LANES, t), 0) - n_kb
    qpos = qpos_i.astype(F32)
    slope_row = jnp.full((1, t), slope, F32)
    which = row % N_SPLIT
    lanes = jnp.where(row < 0, mask,
                      jnp.where(row < N_SPLIT, _split_piece(-slope_row * qpos, which),
                                jnp.where(row < 3 * N_SPLIT, _split_piece(slope_row, which), 0.0)))
    return lanes.T.astype(BF)


def _moba_attention(big, k_mean, slopes, batch, seq, col_q, col_k, col_v, t, group):
    n_q = seq // t
    specs = [_q_spec(t, n_q, group, col_q // group), _kv_spec(seq, group, col_k // group),
             pl.BlockSpec((seq, LANES), lambda b, h, i: (0, 0)),
             _kv_spec(seq, group, col_v // group),
             pl.BlockSpec((seq // MOBA_BLOCK, group * LANES), lambda b, h, i: (b, h)),
             pl.BlockSpec(memory_space=pltpu.SMEM)]
    scratch = (pltpu.VMEM((group, t, LANES), BF),)
    return _attention("moba", batch, seq, MOBA_HEADS, t, group,
                      (big, big, _moba_key_lanes(seq), big, k_mean, slopes), specs, scratch)


def _kmean_body(k_ref, o_ref):
    n_kb = o_ref.shape[0]
    for n in range(n_kb):
        blk = k_ref[n * MOBA_BLOCK:(n + 1) * MOBA_BLOCK, :].astype(F32)
        o_ref[n:n + 1, :] = jnp.mean(blk, axis=0, keepdims=True)


def _moba_kmean(big, batch, seq, col_block):
    n_kb = seq // MOBA_BLOCK
    width = MOBA_HEADS * MOBA_HEAD_DIM
    return pl.pallas_call(
        _kmean_body,
        grid=(batch,),
        in_specs=[pl.BlockSpec((seq, width), lambda b: (b, col_block))],
        out_specs=pl.BlockSpec((n_kb, width), lambda b: (b, 0)),
        out_shape=jax.ShapeDtypeStruct((batch * n_kb, width), F32),
        compiler_params=_params("parallel"),
        name="moba_kmean",
    )(big)


def _swa_body(sink_ref, q_ref, kp_ref, kc_ref, vp_ref, vc_ref, o_ref):
    i = pl.program_id(1)
    w = SWA_WINDOW
    half = SWA_HEAD_DIM
    lane = lax.broadcasted_iota(I32, (w, LANES), 1)
    r = lax.broadcasted_iota(I32, (w, w), 0)
    c = lax.broadcasted_iota(I32, (w, w), 1)
    from_prev = c > r
    dist = (r - c + jnp.where(from_prev, w, 0)).astype(F32)
    valid = jnp.logical_not(from_prev) | (i > 0)
    heads_per_kv = SWA_Q_HEADS // SWA_KV_HEADS

    def lo_hi(ref, kvh):
        own = jnp.where((lane < half) == (kvh == 0), ref[...].astype(F32), 0.0)
        other = pltpu.roll(own, half, axis=1)
        lo, hi = (own, other) if kvh == 0 else (other, own)
        return lo.astype(BF), hi.astype(BF)

    for kvh in range(SWA_KV_HEADS):
        kp, kc, vp, vc = (lo_hi(ref, kvh) for ref in (kp_ref, kc_ref, vp_ref, vc_ref))
        heads = range(kvh * heads_per_kv, (kvh + 1) * heads_per_kv)
        q = {h: q_ref[:, (h // 2) * LANES:(h // 2 + 1) * LANES] for h in heads}
        s = {h: jnp.where(from_prev,
                          lax.dot_general(q[h], kp[h % 2], NT_DIMS, preferred_element_type=F32),
                          lax.dot_general(q[h], kc[h % 2], NT_DIMS, preferred_element_type=F32)) for h in heads}
        p, inv = {}, {}
        for h in heads:
            slope = float(LOG2E * 2.0 ** (-8.0 * (h + 1) / SWA_Q_HEADS))
            sink = sink_ref[h] * LOG2E
            logits = jnp.where(valid, s[h] - slope * dist, NEG)
            m = jnp.maximum(jnp.max(logits, axis=1, keepdims=True), sink)
            p[h] = jnp.exp2(logits - m)
            inv[h] = 1.0 / (jnp.sum(p[h], axis=1, keepdims=True) + jnp.exp2(sink - m))
        for pair in range(kvh * heads_per_kv // 2, (kvh + 1) * heads_per_kv // 2):
            out = jnp.zeros((w, LANES), F32)
            for h in (2 * pair, 2 * pair + 1):
                pn = p[h] * inv[h]
                out = out + jnp.dot(jnp.where(from_prev, pn, 0.0).astype(BF), vp[h % 2], preferred_element_type=F32)
                out = out + jnp.dot(jnp.where(from_prev, 0.0, pn).astype(BF), vc[h % 2], preferred_element_type=F32)
            o_ref[:, pair * LANES:(pair + 1) * LANES] = out.astype(o_ref.dtype)


def _swa_attention(big, sinks, batch, seq, col_q, col_k, col_v):
    w = SWA_WINDOW
    n_q = seq // w
    width = SWA_Q_HEADS * SWA_HEAD_DIM
    prev = lambda col: pl.BlockSpec((w, LANES), lambda b, i: (b * n_q + jnp.maximum(i - 1, 0), col))
    cur = lambda col: pl.BlockSpec((w, LANES), lambda b, i: (b * n_q + i, col))
    return pl.pallas_call(
        _swa_body,
        grid=(batch, n_q),
        in_specs=[pl.BlockSpec(memory_space=pltpu.SMEM),
                  pl.BlockSpec((w, width), lambda b, i: (b * n_q + i, col_q)),
                  prev(col_k), cur(col_k), prev(col_v), cur(col_v)],
        out_specs=pl.BlockSpec((w, width), lambda b, i: (b * n_q + i, 0)),
        out_shape=jax.ShapeDtypeStruct((batch * seq, width), BF),
        compiler_params=_params("parallel", "parallel"),
        name="swa",
    )(sinks, big, big, big, big, big)


def _layer_norm(z, g, b):
    mu = jnp.mean(z, axis=-1, keepdims=True)
    zc = z - mu
    var = jnp.mean(zc * zc, axis=-1, keepdims=True)
    return zc * lax.rsqrt(var + LN_EPS) * g + b


def _outproj_body(a_ref, b_ref, wa_ref, wb_ref, h_ref, g_ref, beta_ref, o_ref):
    half = a_ref.shape[0] // 2
    for rows in (slice(0, half), slice(half, 2 * half)):
        mix = jnp.dot(a_ref[rows, :], wa_ref[...], preferred_element_type=F32)
        mix = mix + jnp.dot(b_ref[rows, :], wb_ref[...], preferred_element_type=F32)
        o_ref[rows, :] = _layer_norm(ALPHA * h_ref[rows, :] + mix, g_ref[...], beta_ref[...])


def _outproj_ln(a, b, w_out, h, g, beta, tm):
    t, d = h.shape
    ka = a.shape[1]
    row = lambda i: (i, 0)
    return pl.pallas_call(
        _outproj_body,
        grid=(t // tm,),
        in_specs=[pl.BlockSpec((tm, ka), row), pl.BlockSpec((tm, ka), row),
                  pl.BlockSpec((ka, d), lambda i: (0, 0)), pl.BlockSpec((ka, d), lambda i: (1, 0)),
                  pl.BlockSpec((tm, d), row), pl.BlockSpec((1, d), lambda i: (0, 0)),
                  pl.BlockSpec((1, d), lambda i: (0, 0))],
        out_specs=pl.BlockSpec((tm, d), row),
        out_shape=jax.ShapeDtypeStruct((t, d), F32),
        compiler_params=_params("parallel"),
        name="outproj_ln",
    )(a, b, w_out, w_out, h, g, beta)


def _second_of_four(a, b, c, d):
    hi_ab, lo_ab = jnp.maximum(a, b), jnp.minimum(a, b)
    hi_cd, lo_cd = jnp.maximum(c, d), jnp.minimum(c, d)
    return jnp.maximum(jnp.maximum(lo_ab, lo_cd), jnp.minimum(hi_ab, hi_cd))


def _router_body(h_ref, wr_ref, bias_ref, e_ref, r_ref):
    tm = h_ref.shape[0]
    h = h_ref[...]
    h_hi = h.astype(BF)
    h_lo = (h - h_hi.astype(F32)).astype(BF)
    both = jnp.dot(h_hi, wr_ref[...], preferred_element_type=F32)
    logits = (both[:, :LANES] + both[:, LANES:]) + jnp.dot(h_lo, wr_ref[:, :LANES], preferred_element_type=F32)
    lt = logits.T
    aff = [jax.nn.sigmoid(lt[SUBLANES * j:SUBLANES * (j + 1), :]) for j in range(EXPERTS_PER_GROUP)]
    sel = [aff[j] + bias_ref[SUBLANES * j:SUBLANES * (j + 1), :] for j in range(EXPERTS_PER_GROUP)]
    top1 = jnp.maximum(jnp.maximum(sel[0], sel[1]), jnp.maximum(sel[2], sel[3]))
    score = top1 + _second_of_four(*sel)
    gid = lax.broadcasted_iota(I32, (N_GROUPS, tm), 0)
    best = jnp.min(jnp.where(score == jnp.max(score, axis=0, keepdims=True), gid, N_GROUPS),
                   axis=0, keepdims=True)
    in_grp = gid == best
    pick = lambda x: jnp.sum(jnp.where(in_grp, x, 0.0), axis=0, keepdims=True)
    s4 = [pick(x) for x in sel]
    a4 = [pick(x) for x in aff]

    def argmax4(vals):
        j, v = jnp.zeros((1, tm), I32), vals[0]
        for n in range(1, EXPERTS_PER_GROUP):
            better = vals[n] > v
            j, v = jnp.where(better, n, j), jnp.where(better, vals[n], v)
        return j

    j0 = argmax4(s4)
    j1 = argmax4([jnp.where(j0 == n, -jnp.inf, s4[n]) for n in range(EXPERTS_PER_GROUP)])
    take = lambda j: sum(jnp.where(j == n, a4[n], 0.0) for n in range(EXPERTS_PER_GROUP))
    w0, w1 = take(j0), take(j1)
    total = w0 + w1
    e0 = best * EXPERTS_PER_GROUP + j0
    e1 = best * EXPERTS_PER_GROUP + j1
    rid = lax.broadcasted_iota(I32, (SUBLANES, tm), 0)
    e_ref[...] = jnp.where(rid == 0, e0, jnp.where(rid == 1, e1, 0))
    rid = lax.broadcasted_iota(I32, (LANES, tm), 0)
    rows = jnp.where(rid == 0, w0 / total, jnp.where(rid == 1, w1 / total, 0.0))
    r_ref[...] = rows.T


def _router(h, wr_perm, bias_perm, tm):
    t, d = h.shape
    return pl.pallas_call(
        _router_body,
        grid=(t // tm,),
        in_specs=[pl.BlockSpec((tm, d), lambda i: (i, 0)), pl.BlockSpec((d, 2 * LANES), lambda i: (0, 0)),
                  pl.BlockSpec((N_EXPERTS, 1), lambda i: (0, 0))],
        out_specs=[pl.BlockSpec((SUBLANES, tm), lambda i: (0, i)), pl.BlockSpec((tm, LANES), lambda i: (i, 0))],
        out_shape=[jax.ShapeDtypeStruct((SUBLANES, t), I32), jax.ShapeDtypeStruct((t, LANES), F32)],
        compiler_params=_params("parallel"),
        name="router",
    )(h, wr_perm, bias_perm)


def _dispatch_plan(e01, tm):
    t = e01.shape[1]
    flat = e01.reshape(-1)
    onehot = (flat[:, None] == jnp.arange(N_EXPERTS, dtype=I32)[None, :]).astype(I32)
    cum = jnp.cumsum(onehot, axis=0)
    rank = jnp.sum(onehot * cum, axis=1) - 1
    counts = cum[-1]
    padded = ((counts + tm - 1) // tm) * tm
    starts = jnp.concatenate([jnp.zeros((1,), I32), jnp.cumsum(padded).astype(I32)])
    pos = jnp.sum(onehot * starts[None, :N_EXPERTS], axis=1) + rank
    return starts, starts[:N_EXPERTS] + counts, pos[:t], pos[t:]


def _dispatch_body(p0_ref, p1_ref, gs_ref, ge_ref, h_ref, r_ref, xs_hbm, xbuf, zbuf, sem, zsem, *, tm, tt,
                   n_steps):
    i = pl.program_id(0)
    slot = i % 2
    n_tail = (xs_hbm.shape[0] - gs_ref[N_EXPERTS]) // tm

    def zero_copy(row0, size):
        return pltpu.make_async_copy(zbuf.at[pl.ds(0, size), :], xs_hbm.at[pl.ds(row0, size), :], zsem.at[0])

    def for_each_fill(fn):
        def group_pad(e, carry):
            end = gs_ref[e + 1]
            pad = end - ge_ref[e]
            covered = 0
            size = tm // 2
            while size >= SUBLANES:
                take = pad & size

                @pl.when(take != 0)
                def _():
                    fn(zero_copy(pl.multiple_of(end - covered - size, size), size))

                covered = covered + take
                size //= 2
            for k in range(SUBLANES - 1):
                @pl.when(k < (pad & (SUBLANES - 1)))
                def _():
                    fn(zero_copy(ge_ref[e] + k, 1))
            return carry

        def tail_tile(k, carry):
            fn(zero_copy(pl.multiple_of(gs_ref[N_EXPERTS] + k * tm, tm), tm))
            return carry

        lax.fori_loop(0, N_EXPERTS, group_pad, 0)
        lax.fori_loop(0, n_tail, tail_tile, 0)

    def wait_rows(sl):
        for _ in range(2):
            pltpu.make_async_copy(xbuf.at[sl, 0], xs_hbm.at[pl.ds(0, tt), :], sem.at[sl]).wait()

    @pl.when(i == 0)
    def _():
        zbuf[...] = jnp.zeros(zbuf.shape, F32)
        for_each_fill(lambda cp: cp.start())

    @pl.when(i >= 2)
    def _():
        wait_rows(slot)

    d = h_ref.shape[1]
    gates = r_ref[...]
    lane = lax.broadcasted_iota(I32, gates.shape, 1)
    for k in range(MOE_TOPK):
        xbuf[slot, k, :, :d] = h_ref[...]
    xbuf[slot, 0, :, d:] = jnp.where(lane == 0, gates, jnp.where(lane == 1, ALPHA, 0.0))
    xbuf[slot, 1, :, d:] = jnp.where(lane == 0, pltpu.roll(gates, LANES - 1, axis=1), 0.0)
    base = i * tt
    for r in range(tt):
        pltpu.make_async_copy(xbuf.at[slot, 0, pl.ds(r, 1), :], xs_hbm.at[pl.ds(p0_ref[base + r], 1), :],
                              sem.at[slot]).start(priority=0)
        pltpu.make_async_copy(xbuf.at[slot, 1, pl.ds(r, 1), :], xs_hbm.at[pl.ds(p1_ref[base + r], 1), :],
                              sem.at[slot]).start(priority=1)

    @pl.when(i == n_steps - 1)
    def _():
        wait_rows(slot)
        if n_steps > 1:
            wait_rows(1 - slot)
        for_each_fill(lambda cp: cp.wait())


def _dispatch(h, gates, starts, ends, pos0, pos1, tm, tt):
    t, d = h.shape
    dx = d + LANES
    n_steps = t // tt
    n_rows = 2 * t + N_EXPERTS * tm
    body = functools.partial(_dispatch_body, tm=tm, tt=tt, n_steps=n_steps)
    grid_spec = pltpu.PrefetchScalarGridSpec(
        num_scalar_prefetch=4,
        grid=(n_steps,),
        in_specs=[pl.BlockSpec((tt, d), lambda i, *_: (i, 0)), pl.BlockSpec((tt, LANES), lambda i, *_: (i, 0))],
        out_specs=pl.BlockSpec(memory_space=pl.ANY),
        scratch_shapes=[pltpu.VMEM((2, MOE_TOPK, tt, dx), F32), pltpu.VMEM((tm, dx), F32),
                        pltpu.SemaphoreType.DMA((2,)), pltpu.SemaphoreType.DMA((1,))],
    )
    return pl.pallas_call(
        body,
        grid_spec=grid_spec,
        out_shape=jax.ShapeDtypeStruct((n_rows, dx), F32),
        compiler_params=_params("arbitrary"),
        name="dispatch",
    )(pos0, pos1, starts, ends, h, gates)


def _experts_body(gs_ref, xs_hbm, wgu_ref, wd_ref, ys_hbm, xbuf, ybuf, xsem, osem, *, tm):
    e = pl.program_id(0)
    first = gs_ref[e] // tm
    n_tiles = gs_ref[e + 1] // tm - first
    total = gs_ref[N_EXPERTS] // tm
    ring = xbuf.shape[0]

    def in_copy(g, slot):
        row0 = pl.multiple_of(g * tm, tm)
        return pltpu.make_async_copy(xs_hbm.at[pl.ds(row0, tm), :], xbuf.at[slot], xsem.at[slot])

    def out_copy(g, slot):
        row0 = pl.multiple_of(g * tm, tm)
        return pltpu.make_async_copy(ybuf.at[slot], ys_hbm.at[pl.ds(row0, tm), :], osem.at[slot])

    @pl.when(e == 0)
    def _():
        for g in range(ring - 1):
            in_copy(jnp.minimum(g, total - 1), g).start(priority=1)
        ybuf[...] = jnp.zeros(ybuf.shape, F32)
        out_copy(0, 0).start()
        out_copy(1, 1).start()

    @pl.when(n_tiles > 0)
    def _():
        def tile(k, carry):
            g = first + k
            slot = g % 2
            xslot = g % ring
            in_copy(g, xslot).wait()
            out_copy(g, slot).wait()
            in_copy(jnp.minimum(g + ring - 1, total - 1), (g + ring - 1) % ring).start(priority=1)
            d = ybuf.shape[2]
            x = xbuf[xslot, :, :d]
            row_gate = xbuf[xslot, :, d:d + 1]
            row_residual = xbuf[xslot, :, d + 1:d + 2]
            gu = jnp.dot(x.astype(BF), wgu_ref[0, 0].astype(BF), preferred_element_type=F32)
            gate, up = gu[:, :EXPERT_FF], gu[:, EXPERT_FF:]
            hidden = (gate * jax.nn.sigmoid(gate) * up).astype(BF)
            y = jnp.dot(hidden, wd_ref[0, 0].astype(BF), preferred_element_type=F32)
            ybuf[slot] = row_gate * y + row_residual * x
            out_copy(g, slot).start(priority=1)
            return carry

        lax.fori_loop(0, n_tiles, tile, 0)

    @pl.when(e == N_EXPERTS - 1)
    def _():
        for n in range(ring - 1):
            in_copy(0, (total + n) % ring).wait()
        out_copy(0, 0).wait()
        out_copy(1, 1).wait()
        end = gs_ref[N_EXPERTS]
        n_tail = (ys_hbm.shape[0] - end) // tm
        ybuf[0] = jnp.zeros(ybuf.shape[1:], F32)

        def tail_copy(k):
            row0 = pl.multiple_of(end + k * tm, tm)
            return pltpu.make_async_copy(ybuf.at[0], ys_hbm.at[pl.ds(row0, tm), :], osem.at[0])

        def start_one(k, carry):
            tail_copy(k).start()
            return carry

        def wait_one(k, carry):
            tail_copy(k).wait()
            return carry

        lax.fori_loop(0, n_tail, start_one, 0)
        lax.fori_loop(0, n_tail, wait_one, 0)


EXPERT_IN_RING = 8


def _experts(xs, starts, w_gate_up, w_down, layer, tm):
    n_rows, dx = xs.shape
    d = dx - LANES
    body = functools.partial(_experts_body, tm=tm)
    grid_spec = pltpu.PrefetchScalarGridSpec(
        num_scalar_prefetch=1,
        grid=(N_EXPERTS,),
        in_specs=[pl.BlockSpec(memory_space=pl.ANY),
                  pl.BlockSpec((1, 1, d, 2 * EXPERT_FF), lambda e, gs: (layer, e, 0, 0)),
                  pl.BlockSpec((1, 1, EXPERT_FF, d), lambda e, gs: (layer, e, 0, 0))],
        out_specs=pl.BlockSpec(memory_space=pl.ANY),
        scratch_shapes=[pltpu.VMEM((EXPERT_IN_RING, tm, dx), F32), pltpu.VMEM((2, tm, d), F32),
                        pltpu.SemaphoreType.DMA((EXPERT_IN_RING,)), pltpu.SemaphoreType.DMA((2,))],
    )
    return pl.pallas_call(
        body,
        grid_spec=grid_spec,
        out_shape=jax.ShapeDtypeStruct((n_rows, d), F32),
        compiler_params=_params("arbitrary"),
        name="experts",
    )(starts, xs, w_gate_up, w_down)


def _combine_body(p0_ref, p1_ref, ys_hbm, g_ref, beta_ref, o_ref, buf0, buf1, sem, *, tc, n_steps):
    i = pl.program_id(0)
    ring = buf0.shape[0]

    def start_row(base, r, sl):
        pltpu.make_async_copy(ys_hbm.at[pl.ds(p0_ref[base + r], 1), :], buf0.at[sl, pl.ds(r, 1), :],
                              sem.at[sl]).start(priority=0)
        pltpu.make_async_copy(ys_hbm.at[pl.ds(p1_ref[base + r], 1), :], buf1.at[sl, pl.ds(r, 1), :],
                              sem.at[sl]).start(priority=1)

    def wait_rows(sl):
        pltpu.make_async_copy(ys_hbm.at[pl.ds(0, tc), :], buf0.at[sl], sem.at[sl]).wait()
        pltpu.make_async_copy(ys_hbm.at[pl.ds(0, tc), :], buf1.at[sl], sem.at[sl]).wait()

    @pl.when(i == 0)
    def _():
        for sl in range(ring):
            def one(r, carry):
                start_row(sl * tc, r, sl)
                return carry

            lax.fori_loop(0, tc, one, 0)

    nxt = jnp.minimum(i + 1, n_steps - 1) * ring * tc
    for sl in range(ring):
        wait_rows(sl)
        o_ref[sl * tc:(sl + 1) * tc, :] = _layer_norm(buf0[sl] + buf1[sl], g_ref[...], beta_ref[...])
        for r in range(tc):
            start_row(nxt + sl * tc, r, sl)

    @pl.when(i == n_steps - 1)
    def _():
        for sl in range(ring):
            wait_rows(sl)


COMBINE_RING = 4


def _combine_ln(ys, pos0, pos1, g, beta, tc):
    t, d = pos0.shape[0], ys.shape[1]
    n_steps = t // (COMBINE_RING * tc)
    body = functools.partial(_combine_body, tc=tc, n_steps=n_steps)
    row = lambda i, p0, p1: (i, 0)
    const = lambda i, p0, p1: (0, 0)
    grid_spec = pltpu.PrefetchScalarGridSpec(
        num_scalar_prefetch=2,
        grid=(n_steps,),
        in_specs=[pl.BlockSpec(memory_space=pl.ANY), pl.BlockSpec((1, d), const), pl.BlockSpec((1, d), const)],
        out_specs=pl.BlockSpec((COMBINE_RING * tc, d), row),
        scratch_shapes=[pltpu.VMEM((COMBINE_RING, tc, d), F32), pltpu.VMEM((COMBINE_RING, tc, d), F32),
                        pltpu.SemaphoreType.DMA((COMBINE_RING,))],
    )
    return pl.pallas_call(
        body,
        grid_spec=grid_spec,
        out_shape=jax.ShapeDtypeStruct((t, d), F32),
        compiler_params=_params("arbitrary"),
        name="combine_ln",
    )(pos0, pos1, ys, g, beta)


def _moe_ln(h, wr_perm, bias_perm, w_gate_up, w_down, layer, g, beta, tm_router=512, tm_expert=256, tc=256):
    e01, gates = _router(h, wr_perm, bias_perm, tm_router)
    starts, ends, pos0, pos1 = _dispatch_plan(e01[:MOE_TOPK], tm_expert)
    xs = _dispatch(h, gates, starts, ends, pos0, pos1, tm_expert, tc)
    ys = _experts(xs, starts, w_gate_up, w_down, layer, tm_expert)
    return _combine_ln(ys, pos0, pos1, g, beta, tc)


def _swap_halves(w):
    half = w.shape[-1] // 2
    return jnp.concatenate([w[..., half:], w[..., :half]], axis=-1)


def _even_weights(w_in, w_q_up, w_kv_up, forget_bias):
    d = w_in.shape[0]
    o_kv = MLA_Q_LORA
    o_kr = o_kv + MLA_KV_LORA
    o_fq = o_kr + MLA_ROPE
    hd = FOX_HEADS * FOX_HEAD_DIM
    o_fl = o_fq + 3 * hd
    k_r = w_in[:, o_kr:o_fq]
    f_l = jnp.pad(w_in[:, o_fl:], ((0, 0), (0, LANES - FOX_HEADS)))
    w_small = jnp.concatenate([w_in[:, :o_kr], k_r, _swap_halves(k_r), f_l], axis=1).astype(BF)
    w_fox = w_in[:, o_fq:o_fl].astype(BF)
    cs_fox = jnp.concatenate([jnp.full((1, hd), FOX_SCALE, F32), jnp.ones((1, 2 * hd), F32)], axis=1)
    wq = w_q_up.reshape(MLA_Q_LORA, MLA_HEADS, MLA_NOPE + MLA_ROPE)
    wqn = wq[:, :, :MLA_NOPE].reshape(MLA_Q_LORA, -1).astype(BF)
    pe = wq[:, :, MLA_NOPE:]
    wqp = jnp.concatenate([pe, _swap_halves(pe)], axis=-1).reshape(MLA_Q_LORA, -1).astype(BF)
    wkv = w_kv_up.reshape(MLA_KV_LORA, MLA_HEADS, MLA_NOPE + MLA_V)
    wk = wkv[:, :, :MLA_NOPE].reshape(MLA_KV_LORA, -1).astype(BF)
    wv = wkv[:, :, MLA_NOPE:].reshape(MLA_KV_LORA, -1).astype(BF)
    fb = jnp.pad(forget_bias.astype(F32), (0, LANES - FOX_HEADS)).reshape(1, LANES)
    return w_small, w_fox, cs_fox, wqn, wqp, wk, wv, fb


def _rope_table(seq):
    half = MLA_ROPE // 2
    inv_freq = ROPE_THETA ** (-np.arange(half, dtype=np.float64) / half)
    ang = np.arange(seq, dtype=np.float64)[:, None] * inv_freq[None, :]
    cos, sin = np.cos(ang), np.sin(ang)
    return jnp.asarray(np.concatenate([cos, cos, -sin, sin], axis=1), dtype=F32)


def _router_weights(w_router, router_bias):
    r = np.arange(N_EXPERTS)
    perm = (r % N_GROUPS) * EXPERTS_PER_GROUP + r // N_GROUPS
    wr = jnp.pad(w_router[:, perm], ((0, 0), (0, LANES - N_EXPERTS)))
    w_hi = wr.astype(BF)
    w_lo = (wr - w_hi.astype(F32)).astype(BF)
    return jnp.concatenate([w_hi, w_lo], axis=1), router_bias.astype(F32)[perm].reshape(N_EXPERTS, 1)


def _even_layer(h, batch, seq, w_in, q_norm, w_q_up, kv_norm, w_kv_up, forget_bias, w_out, g, beta):
    w_small, w_fox, cs_fox, wqn, wqp, wk, wv, fb = _even_weights(w_in, w_q_up, w_kv_up, forget_bias)
    fox_qkv = _mm(h, w_fox, cs_fox, BF, ROW_TILE, w_fox.shape[1] // 2)
    qn, qp, kn, v, kp, forget_logits = _mla_prep(h, w_small, _rope_table(seq), q_norm.reshape(1, -1),
                                                 kv_norm.reshape(1, -1), wqn, wqp, wk, wv, seq, ROW_TILE)
    q_bias, k_bias, edge = _fox_cum(forget_logits, fb, batch, seq, ROW_TILE)
    a = _mla_attention(qn, qp, kn, v, kp, batch, seq, ROW_TILE, HEAD_GROUP)
    bo = _fox_attention(fox_qkv, q_bias, k_bias, edge, batch, seq, ROW_TILE, HEAD_GROUP)
    return _outproj_ln(a, bo, w_out.astype(BF), h, g, beta, ROW_TILE)


def _odd_layer(h, batch, seq, w_in, sinks, w_out, g, beta):
    n_sq = SWA_Q_HEADS * SWA_HEAD_DIM
    n_skv = SWA_KV_HEADS * SWA_HEAD_DIM
    n_m = MOBA_HEADS * MOBA_HEAD_DIM
    o_mq = n_sq + 2 * n_skv
    w_big = jnp.concatenate([w_in[:, :n_sq], w_in[:, o_mq:], w_in[:, n_sq:o_mq]], axis=1).astype(BF)
    cs = jnp.concatenate([jnp.full((1, n_sq), SWA_SCALE, F32), jnp.full((1, n_m), MOBA_SCALE, F32),
                          jnp.ones((1, 2 * n_m + 2 * n_skv), F32)], axis=1)
    big = _mm(h, w_big, cs, BF, ROW_TILE, w_big.shape[1] // 2)
    blocks = lambda cols: cols // LANES
    k_mean = _moba_kmean(big, batch, seq, (n_sq + n_m) // n_m)
    slopes = jnp.asarray(LOG2E * 2.0 ** (-8.0 * np.arange(1, MOBA_HEADS + 1) / MOBA_HEADS), dtype=F32)
    c = _swa_attention(big, sinks.astype(F32), batch, seq, 0, blocks(n_sq + 3 * n_m), blocks(n_sq + 3 * n_m) + 1)
    dd = _moba_attention(big, k_mean, slopes, batch, seq, blocks(n_sq), blocks(n_sq + n_m), blocks(n_sq + 2 * n_m),
                         ROW_TILE, HEAD_GROUP)
    return _outproj_ln(c, dd, w_out.astype(BF), h, g, beta, ROW_TILE)


def kernel(x, w_router, router_bias, even_w_in, even_q_norm, even_w_q_up, even_kv_norm, even_w_kv_up,
           even_forget_bias, even_w_out, odd_w_in, odd_sinks, odd_w_out, ln_mix_g, ln_mix_b, ln_ffn_g, ln_ffn_b,
           w_gate_up, w_down):
    batch, seq, d = x.shape
    h = x.reshape(batch * seq, d)
    wr_perm, bias_perm = _router_weights(w_router, router_bias)
    row = lambda p, layer: p[layer].reshape(1, d)
    for layer in range(DEPTH):
        i = layer // 2
        if layer % 2 == 0:
            h = _even_layer(h, batch, seq, even_w_in[i], even_q_norm[i], even_w_q_up[i], even_kv_norm[i],
                            even_w_kv_up[i], even_forget_bias[i], even_w_out[i], row(ln_mix_g, layer),
                            row(ln_mix_b, layer))
        else:
            h = _odd_layer(h, batch, seq, odd_w_in[i], odd_sinks[i], odd_w_out[i], row(ln_mix_g, layer),
                           row(ln_mix_b, layer))
        h = _moe_ln(h, wr_perm, bias_perm, w_gate_up, w_down, layer, row(ln_ffn_g, layer), row(ln_ffn_b, layer))
    return h.reshape(batch, seq, d)
```

```python
import functools

import numpy as np
import jax
import jax.numpy as jnp
from jax import lax
from jax.experimental import pallas as pl
from jax.experimental.pallas import tpu as pltpu

BF = jnp.bfloat16
F32 = jnp.float32
I32 = jnp.int32

LANES = 128
SUBLANES = 8
VMEM_LIMIT = 56 * 1024 * 1024

ROW_TILE = 512
HEAD_GROUP = 4

DEPTH = 2
LN_EPS = 1e-5
RMS_EPS = 1e-6
ALPHA = (2 * DEPTH) ** 0.25
MLA_HEADS, MLA_Q_LORA, MLA_KV_LORA, MLA_NOPE, MLA_ROPE, MLA_V = 8, 512, 256, 128, 64, 128
ROPE_THETA = 10000.0
FOX_HEADS, FOX_HEAD_DIM = 8, 128
SWA_Q_HEADS, SWA_KV_HEADS, SWA_HEAD_DIM, SWA_WINDOW = 16, 2, 64, 128
MOBA_HEADS, MOBA_HEAD_DIM, MOBA_BLOCK, MOBA_TOPK = 8, 128, 256, 3
N_EXPERTS, N_GROUPS, MOE_TOPK, EXPERT_FF = 32, 8, 2, 512
EXPERTS_PER_GROUP = N_EXPERTS // N_GROUPS

LOG2E = 1.4426950408889634
MLA_SCALE = (MLA_NOPE + MLA_ROPE) ** -0.5 * LOG2E
FOX_SCALE = FOX_HEAD_DIM ** -0.5 * LOG2E
MOBA_SCALE = MOBA_HEAD_DIM ** -0.5 * LOG2E
SWA_SCALE = SWA_HEAD_DIM ** -0.5 * LOG2E
NEG = -1e30

NT_DIMS = (((1,), (1,)), ((), ()))


def _params(*sem):
    return pltpu.CompilerParams(dimension_semantics=sem, vmem_limit_bytes=VMEM_LIMIT)


def _mm_body(x_ref, w_ref, cs_ref, o_ref):
    acc = jnp.dot(x_ref[...].astype(BF), w_ref[...], preferred_element_type=F32)
    o_ref[...] = (acc * cs_ref[...]).astype(o_ref.dtype)


def _mm(x, w, col_scale, out_dtype, tm, tn):
    m, k = x.shape
    n = w.shape[1]
    return pl.pallas_call(
        _mm_body,
        grid=(n // tn, m // tm),
        in_specs=[
            pl.BlockSpec((tm, k), lambda j, i: (i, 0)),
            pl.BlockSpec((k, tn), lambda j, i: (0, j)),
            pl.BlockSpec((1, tn), lambda j, i: (0, j)),
        ],
        out_specs=pl.BlockSpec((tm, tn), lambda j, i: (i, j)),
        out_shape=jax.ShapeDtypeStruct((m, n), out_dtype),
        compiler_params=_params("parallel", "parallel"),
        name="proj",
    )(x, w, col_scale)


def _rms(x, g):
    return x * lax.rsqrt(jnp.mean(x * x, axis=-1, keepdims=True) + RMS_EPS) * g


def _rope_pair(slab, table):
    r = slab * table
    return r + pltpu.roll(r, MLA_ROPE, axis=1)


def _mla_prep_body(x_ref, ws_ref, rope_ref, qg_ref, kvg_ref, wqn_ref, wqp_ref, wk_ref, wv_ref,
                   qn_ref, qp_ref, kn_ref, v_ref, kp_ref, fl_ref):
    small = jnp.dot(x_ref[...].astype(BF), ws_ref[...], preferred_element_type=F32)
    o_kv, o_kr, o_fl = MLA_Q_LORA, MLA_Q_LORA + MLA_KV_LORA, MLA_Q_LORA + MLA_KV_LORA + LANES
    fl_ref[...] = small[:, o_fl:]
    table = rope_ref[...]
    cqn = _rms(small[:, :o_kv], qg_ref[...]).astype(BF)
    qn_ref[...] = (jnp.dot(cqn, wqn_ref[...], preferred_element_type=F32) * MLA_SCALE).astype(BF)
    qp = jnp.dot(cqn, wqp_ref[...], preferred_element_type=F32)
    for h in range(MLA_HEADS):
        sl = slice(h * LANES, (h + 1) * LANES)
        qp_ref[:, sl] = (_rope_pair(qp[:, sl], table) * MLA_SCALE).astype(BF)
    ckvn = _rms(small[:, o_kv:o_kr], kvg_ref[...]).astype(BF)
    kn_ref[...] = jnp.dot(ckvn, wk_ref[...], preferred_element_type=F32).astype(BF)
    v_ref[...] = jnp.dot(ckvn, wv_ref[...], preferred_element_type=F32).astype(BF)
    kr = _rope_pair(small[:, o_kr:o_fl], table)
    lane = lax.broadcasted_iota(I32, kr.shape, 1)
    kp_ref[...] = jnp.where(lane < MLA_ROPE, kr, 0.0).astype(BF)


def _mla_prep(x, w_small, rope_table, q_norm, kv_norm, wqn, wqp, wk, wv, seq, tm):
    t, d = x.shape
    n_s = seq // tm
    hd = MLA_HEADS * LANES
    row = lambda i: (i, 0)
    const = lambda i: (0, 0)
    out = lambda w: pl.BlockSpec((tm, w), row)
    return pl.pallas_call(
        _mla_prep_body,
        grid=(t // tm,),
        in_specs=[
            pl.BlockSpec((tm, d), row),
            pl.BlockSpec(w_small.shape, const),
            pl.BlockSpec((tm, LANES), lambda i: (i % n_s, 0)),
            pl.BlockSpec((1, MLA_Q_LORA), const),
            pl.BlockSpec((1, MLA_KV_LORA), const),
            pl.BlockSpec((MLA_Q_LORA, hd), const),
            pl.BlockSpec((MLA_Q_LORA, hd), const),
            pl.BlockSpec((MLA_KV_LORA, hd), const),
            pl.BlockSpec((MLA_KV_LORA, hd), const),
        ],
        out_specs=[out(hd), out(hd), out(hd), out(hd), out(LANES), out(LANES)],
        out_shape=[jax.ShapeDtypeStruct((t, hd), BF)] * 4 + [jax.ShapeDtypeStruct((t, LANES), BF),
                                                             jax.ShapeDtypeStruct((t, LANES), F32)],
        compiler_params=_params("parallel"),
        name="mla_prep",
    )(x, w_small, rope_table, q_norm, kv_norm, wqn, wqp, wk, wv)


N_SPLIT = 3


def _split_piece(x, which):
    hi = x.astype(BF).astype(F32)
    rest = x - hi
    mid = rest.astype(BF).astype(F32)
    return jnp.where(which == 0, hi, jnp.where(which == 1, mid, rest - mid))


def _fox_cum_body(fl_ref, fb_ref, spread_ref, qe_ref, ke_ref, edge_ref, carry_ref):
    @pl.when(pl.program_id(1) == 0)
    def _():
        carry_ref[...] = jnp.zeros_like(carry_ref)

    z = fl_ref[...] + fb_ref[...]
    log_f = jnp.minimum(z, 0.0) - jnp.log1p(jnp.exp(-jnp.abs(z)))
    tm = z.shape[0]
    r = lax.broadcasted_iota(I32, (tm, tm), 0)
    c = lax.broadcasted_iota(I32, (tm, tm), 1)
    tri = jnp.where(r >= c, 1.0, 0.0).astype(BF)
    parts = jnp.dot(tri, jnp.concatenate([_split_piece(log_f, n).astype(BF) for n in range(N_SPLIT)], axis=1),
                    preferred_element_type=F32)
    cum = (parts[:, :LANES] + parts[:, LANES:2 * LANES]) + parts[:, 2 * LANES:] + carry_ref[...]
    carry_ref[...] = cum[tm - 1:tm, :]
    cum = cum * LOG2E
    width = qe_ref.shape[1]
    pieces = jnp.concatenate([_split_piece(cum, n).astype(BF) for n in range(N_SPLIT)], axis=1)
    placed = jnp.dot(pieces, spread_ref[...], preferred_element_type=F32)
    li = lax.broadcasted_iota(I32, (tm, width), 1) % LANES
    qe_ref[...] = jnp.where((li >= N_SPLIT) & (li < 2 * N_SPLIT), 1.0, placed[:, :width]).astype(BF)
    ke_ref[...] = jnp.where(li < N_SPLIT, 1.0, placed[:, width:]).astype(BF)
    edge_ref[0, 0:1, :] = cum[0:1, :]
    edge_ref[0, 1:2, :] = cum[tm - 1:tm, :]


def _fox_spread():
    width = FOX_HEADS * LANES
    m = np.zeros((N_SPLIT * LANES, 2 * width), np.float32)
    for j in range(N_SPLIT):
        for h in range(FOX_HEADS):
            m[j * LANES + h, h * LANES + j] = 1.0
            m[j * LANES + h, width + h * LANES + N_SPLIT + j] = -1.0
    return jnp.asarray(m, dtype=BF)


def _fox_cum(forget_logits, forget_bias_row, batch, seq, tm):
    t = forget_logits.shape[0]
    n_s = seq // tm
    width = FOX_HEADS * LANES
    q_bias, k_bias, edge = pl.pallas_call(
        _fox_cum_body,
        grid=(batch, n_s),
        in_specs=[
            pl.BlockSpec((tm, LANES), lambda b, i: (b * n_s + i, 0)),
            pl.BlockSpec((1, LANES), lambda b, i: (0, 0)),
            pl.BlockSpec((N_SPLIT * LANES, 2 * width), lambda b, i: (0, 0)),
        ],
        out_specs=[
            pl.BlockSpec((tm, width), lambda b, i: (b * n_s + i, 0)),
            pl.BlockSpec((tm, width), lambda b, i: (b * n_s + i, 0)),
            pl.BlockSpec((1, 2, LANES), lambda b, i: (b * n_s + i, 0, 0)),
        ],
        out_shape=[jax.ShapeDtypeStruct((t, width), BF), jax.ShapeDtypeStruct((t, width), BF),
                   jax.ShapeDtypeStruct((batch * n_s, 2, LANES), F32)],
        scratch_shapes=[pltpu.VMEM((1, LANES), F32)],
        compiler_params=_params("parallel", "arbitrary"),
        name="fox_cum",
    )(forget_logits, forget_bias_row, _fox_spread())
    return q_bias, k_bias, edge[:, :, :FOX_HEADS].reshape(-1)


def _softmax_update(s_all, v_all, m_ref, l_ref, acc_ref, rows=slice(None)):
    heads = range(len(s_all))
    m_prev = [m_ref[g, rows] for g in heads]
    m_new = [jnp.maximum(m_prev[g], jnp.max(s_all[g], axis=1, keepdims=True)) for g in heads]
    reps = s_all[0].shape[1] // LANES
    p = [jnp.exp2(s_all[g] - jnp.concatenate([m_new[g]] * reps, axis=1)) for g in heads]
    pv = [jnp.dot(p[g].astype(BF), v_all[g], preferred_element_type=F32) for g in heads]
    for g in heads:
        alpha = jnp.exp2(m_prev[g] - m_new[g])
        l_ref[g, rows] = alpha * l_ref[g, rows] + jnp.sum(p[g], axis=1, keepdims=True)
        acc_ref[g, rows] = alpha * acc_ref[g, rows] + pv[g]
        m_ref[g, rows] = m_new[g]


def _attn_body(*refs, mode, t, group):
    hg = pl.program_id(1)
    i = pl.program_id(2)
    heads = range(group)
    sl = lambda g: slice(g * LANES, (g + 1) * LANES)
    if mode == "moba":
        q_ref, k_ref, kx_ref, v_ref, km_ref, slope_ref, o_ref, m_ref, l_ref, acc_ref, qx_ref = refs
        for g in heads:
            qx_ref[g] = _moba_query_lanes(q_ref[:, sl(g)], km_ref[:, sl(g)], slope_ref[hg * group + g], i, t)
        q_extra = lambda g, rows: qx_ref[g, rows]
        k_extra = lambda g, ks, nk: kx_ref[pl.ds(ks, nk), :]
    elif mode == "mla":
        q_ref, qx_ref, k_ref, kx_ref, v_ref, o_ref, m_ref, l_ref, acc_ref = refs
        q_extra = lambda g, rows: qx_ref[rows, sl(g)]
        k_extra = lambda g, ks, nk: kx_ref[pl.ds(ks, nk), :]
    else:
        q_ref, qx_ref, k_ref, kx_ref, v_ref, edge_ref, o_ref, m_ref, l_ref, acc_ref, knorm_ref = refs
        q_extra = lambda g, rows: qx_ref[rows, sl(g)]
        k_extra = lambda g, ks, nk: kx_ref[pl.ds(ks, nk), sl(g)]
        first_tile = _fox_first_tile(q_ref, k_ref, edge_ref, knorm_ref, i, hg, t, group)

    def scores(g, ks, nk=t, rows=slice(None)):
        q = jnp.concatenate([q_ref[rows, sl(g)], q_extra(g, rows)], axis=1)
        k = jnp.concatenate([k_ref[pl.ds(ks, nk), sl(g)], k_extra(g, ks, nk)], axis=1)
        return lax.dot_general(q, k, NT_DIMS, preferred_element_type=F32)

    m_ref[...] = jnp.full(m_ref.shape, NEG, F32)
    l_ref[...] = jnp.zeros(l_ref.shape, F32)
    acc_ref[...] = jnp.zeros(acc_ref.shape, F32)

    values = lambda ks, nk=t: [v_ref[pl.ds(ks, nk), sl(g)] for g in heads]
    ks = pl.multiple_of(i * t, t)
    half = t // 2
    for rows, nk in ((slice(0, half), half), (slice(half, t), t)):
        r = lax.broadcasted_iota(I32, (half, nk), 0) + rows.start
        c = lax.broadcasted_iota(I32, (half, nk), 1)
        _softmax_update([jnp.where(c <= r, scores(g, ks, nk, rows), NEG) for g in heads], values(ks, nk),
                        m_ref, l_ref, acc_ref, rows)

    def past_tile(j, carry):
        ks = pl.multiple_of(j * t, t)
        _softmax_update([scores(g, ks) for g in heads], values(ks), m_ref, l_ref, acc_ref)
        return carry

    lax.fori_loop(first_tile if mode == "fox" else 0, i, past_tile, 0)
    for g in heads:
        o_ref[:, sl(g)] = (acc_ref[g] / l_ref[g]).astype(o_ref.dtype)


FOX_SKIP_GAP = 160.0


def _fox_first_tile(q_ref, k_ref, edge_ref, knorm_ref, i, hg, t, group):
    b = pl.program_id(0)
    n_q = pl.num_programs(2)
    sl = lambda g: slice(g * LANES, (g + 1) * LANES)

    def max_row_norm(x):
        x = x.astype(F32)
        return jnp.sqrt(jnp.max(jnp.sum(x * x, axis=1, keepdims=True)))

    @pl.when(i == 0)
    def _():
        for g in range(group):
            knorm_ref[g] = max_row_norm(k_ref[:, sl(g)])

    def edge(tile, last, g):
        return edge_ref[((b * n_q + tile) * 2 + last) * FOX_HEADS + hg * group + g]

    limit = [-(2.0 * 1.001 * max_row_norm(q_ref[:, sl(g)]) * knorm_ref[g] + FOX_SKIP_GAP) for g in range(group)]
    first = [edge(i, 0, g) for g in range(group)]

    def scan(j, lo):
        needed = first[0] - edge(j, 1, 0) >= limit[0]
        for g in range(1, group):
            needed = needed | (first[g] - edge(j, 1, g) >= limit[g])
        return jnp.where(needed, jnp.minimum(lo, j), lo)

    return lax.fori_loop(0, i, scan, i)


def _attention(mode, batch, seq, heads, t, group, operands, in_specs, extra_scratch=()):
    n_q = seq // t
    body = functools.partial(_attn_body, mode=mode, t=t, group=group)
    return pl.pallas_call(
        body,
        grid=(batch, heads // group, n_q),
        in_specs=in_specs,
        out_specs=pl.BlockSpec((t, group * LANES), lambda b, h, i: (b * n_q + i, h)),
        out_shape=jax.ShapeDtypeStruct((batch * seq, heads * LANES), BF),
        scratch_shapes=[pltpu.VMEM((group, t, LANES), F32), pltpu.VMEM((group, t, LANES), F32),
                        pltpu.VMEM((group, t, LANES), F32), *extra_scratch],
        compiler_params=_params("parallel", "parallel", "arbitrary"),
        name="attn_" + mode,
    )(*operands)


def _q_spec(t, n_q, group, col0=0):
    return pl.BlockSpec((t, group * LANES), lambda b, h, i: (b * n_q + i, col0 + h))


def _kv_spec(seq, group, col0=0):
    return pl.BlockSpec((seq, group * LANES), lambda b, h, i: (b, col0 + h))


def _mla_attention(qn, qp, kn, v, kp, batch, seq, t, group):
    n_q = seq // t
    specs = [_q_spec(t, n_q, group), _q_spec(t, n_q, group), _kv_spec(seq, group),
             pl.BlockSpec((seq, LANES), lambda b, h, i: (b, 0)), _kv_spec(seq, group)]
    return _attention("mla", batch, seq, MLA_HEADS, t, group, (qn, qp, kn, kp, v), specs)


def _fox_attention(qkv, q_bias, k_bias, edge, batch, seq, t, group):
    n_q = seq // t
    n_hg = FOX_HEADS // group
    specs = [_q_spec(t, n_q, group, 0), _q_spec(t, n_q, group, 0), _kv_spec(seq, group, n_hg),
             _kv_spec(seq, group, 0), _kv_spec(seq, group, 2 * n_hg), pl.BlockSpec(memory_space=pltpu.SMEM)]
    return _attention("fox", batch, seq, FOX_HEADS, t, group, (qkv, q_bias, qkv, k_bias, qkv, edge), specs,
                      (pltpu.SMEM((group,), F32),))


def _moba_key_lanes(seq):
    n_kb = seq // MOBA_BLOCK
    kpos = np.arange(seq)
    lanes = np.zeros((seq, LANES), np.float32)
    lanes[kpos, kpos // MOBA_BLOCK] = 1.0
    lanes[:, n_kb:n_kb + N_SPLIT] = 1.0
    lanes[:, n_kb + N_SPLIT:n_kb + 2 * N_SPLIT] = (MOBA_BLOCK * (kpos // MOBA_BLOCK))[:, None]
    lanes[:, n_kb + 2 * N_SPLIT:n_kb + 3 * N_SPLIT] = (kpos % MOBA_BLOCK)[:, None]
    return jnp.asarray(lanes, dtype=BF)


def _moba_query_lanes(q, k_mean, slope, i, t):
    n_kb = k_mean.shape[0]
    gate = lax.dot_general(k_mean.astype(BF), q, NT_DIMS, preferred_element_type=F32)
    blk = lax.broadcasted_iota(I32, (n_kb, t), 0)
    qpos_i = i * t + lax.broadcasted_iota(I32, (1, t), 1)
    own = qpos_i // MOBA_BLOCK
    beaten = jnp.zeros((n_kb, t), F32)
    for n in range(n_kb):
        g_n = gate[n:n + 1, :]
        wins = (g_n > gate) | ((g_n == gate) & (blk > n))
        beaten = beaten + jnp.where(wins & (own > n), 1.0, 0.0)
    attended = (blk == own) | ((blk < own) & (beaten < MOBA_TOPK))
    mask = jnp.concatenate([jnp.where(attended, 0.0, NEG), jnp.zeros((LANES - n_kb, t), F32)], axis=0)
    row = lax.broadcasted_iota(I32, (LANES, t), 0) - n_kb
    qpos = qpos_i.astype(F32)
    slope_row = jnp.full((1, t), slope, F32)
    which = row % N_SPLIT
    lanes = jnp.where(row < 0, mask,
                      jnp.where(row < N_SPLIT, _split_piece(-slope_row * qpos, which),
                                jnp.where(row < 3 * N_SPLIT, _split_piece(slope_row, which), 0.0)))
    return lanes.T.astype(BF)


def _moba_attention(big, k_mean, slopes, batch, seq, col_q, col_k, col_v, t, group):
    n_q = seq // t
    specs = [_q_spec(t, n_q, group, col_q // group), _kv_spec(seq, group, col_k // group),
             pl.BlockSpec((seq, LANES), lambda b, h, i: (0, 0)),
             _kv_spec(seq, group, col_v // group),
             pl.BlockSpec((seq // MOBA_BLOCK, group * LANES), lambda b, h, i: (b, h)),
             pl.BlockSpec(memory_space=pltpu.SMEM)]
    scratch = (pltpu.VMEM((group, t, LANES), BF),)
    return _attention("moba", batch, seq, MOBA_HEADS, t, group,
                      (big, big, _moba_key_lanes(seq), big, k_mean, slopes), specs, scratch)


def _kmean_body(k_ref, o_ref):
    n_kb = o_ref.shape[0]
    for n in range(n_kb):
        blk = k_ref[n * MOBA_BLOCK:(n + 1) * MOBA_BLOCK, :].astype(F32)
        o_ref[n:n + 1, :] = jnp.mean(blk, axis=0, keepdims=True)


def _moba_kmean(big, batch, seq, col_block):
    n_kb = seq // MOBA_BLOCK
    width = MOBA_HEADS * MOBA_HEAD_DIM
    return pl.pallas_call(
        _kmean_body,
        grid=(batch,),
        in_specs=[pl.BlockSpec((seq, width), lambda b: (b, col_block))],
        out_specs=pl.BlockSpec((n_kb, width), lambda b: (b, 0)),
        out_shape=jax.ShapeDtypeStruct((batch * n_kb, width), F32),
        compiler_params=_params("parallel"),
        name="moba_kmean",
    )(big)


def _swa_body(sink_ref, q_ref, kp_ref, kc_ref, vp_ref, vc_ref, o_ref):
    i = pl.program_id(1)
    w = SWA_WINDOW
    half = SWA_HEAD_DIM
    lane = lax.broadcasted_iota(I32, (w, LANES), 1)
    r = lax.broadcasted_iota(I32, (w, w), 0)
    c = lax.broadcasted_iota(I32, (w, w), 1)
    from_prev = c > r
    dist = (r - c + jnp.where(from_prev, w, 0)).astype(F32)
    valid = jnp.logical_not(from_prev) | (i > 0)
    heads_per_kv = SWA_Q_HEADS // SWA_KV_HEADS

    def lo_hi(ref, kvh):
        own = jnp.where((lane < half) == (kvh == 0), ref[...].astype(F32), 0.0)
        other = pltpu.roll(own, half, axis=1)
        lo, hi = (own, other) if kvh == 0 else (other, own)
        return lo.astype(BF), hi.astype(BF)

    for kvh in range(SWA_KV_HEADS):
        kp, kc, vp, vc = (lo_hi(ref, kvh) for ref in (kp_ref, kc_ref, vp_ref, vc_ref))
        heads = range(kvh * heads_per_kv, (kvh + 1) * heads_per_kv)
        q = {h: q_ref[:, (h // 2) * LANES:(h // 2 + 1) * LANES] for h in heads}
        s = {h: jnp.where(from_prev,
                          lax.dot_general(q[h], kp[h % 2], NT_DIMS, preferred_element_type=F32),
                          lax.dot_general(q[h], kc[h % 2], NT_DIMS, preferred_element_type=F32)) for h in heads}
        p, inv = {}, {}
        for h in heads:
            slope = float(LOG2E * 2.0 ** (-8.0 * (h + 1) / SWA_Q_HEADS))
            sink = sink_ref[h] * LOG2E
            logits = jnp.where(valid, s[h] - slope * dist, NEG)
            m = jnp.maximum(jnp.max(logits, axis=1, keepdims=True), sink)
            p[h] = jnp.exp2(logits - m)
            inv[h] = 1.0 / (jnp.sum(p[h], axis=1, keepdims=True) + jnp.exp2(sink - m))
        for pair in range(kvh * heads_per_kv // 2, (kvh + 1) * heads_per_kv // 2):
            out = jnp.zeros((w, LANES), F32)
            for h in (2 * pair, 2 * pair + 1):
                pn = p[h] * inv[h]
                out = out + jnp.dot(jnp.where(from_prev, pn, 0.0).astype(BF), vp[h % 2], preferred_element_type=F32)
                out = out + jnp.dot(jnp.where(from_prev, 0.0, pn).astype(BF), vc[h % 2], preferred_element_type=F32)
            o_ref[:, pair * LANES:(pair + 1) * LANES] = out.astype(o_ref.dtype)


def _swa_attention(big, sinks, batch, seq, col_q, col_k, col_v):
    w = SWA_WINDOW
    n_q = seq // w
    width = SWA_Q_HEADS * SWA_HEAD_DIM
    prev = lambda col: pl.BlockSpec((w, LANES), lambda b, i: (b * n_q + jnp.maximum(i - 1, 0), col))
    cur = lambda col: pl.BlockSpec((w, LANES), lambda b, i: (b * n_q + i, col))
    return pl.pallas_call(
        _swa_body,
        grid=(batch, n_q),
        in_specs=[pl.BlockSpec(memory_space=pltpu.SMEM),
                  pl.BlockSpec((w, width), lambda b, i: (b * n_q + i, col_q)),
                  prev(col_k), cur(col_k), prev(col_v), cur(col_v)],
        out_specs=pl.BlockSpec((w, width), lambda b, i: (b * n_q + i, 0)),
        out_shape=jax.ShapeDtypeStruct((batch * seq, width), BF),
        compiler_params=_params("parallel", "parallel"),
        name="swa",
    )(sinks, big, big, big, big, big)


def _layer_norm(z, g, b):
    mu = jnp.mean(z, axis=-1, keepdims=True)
    zc = z - mu
    var = jnp.mean(zc * zc, axis=-1, keepdims=True)
    return zc * lax.rsqrt(var + LN_EPS) * g + b


def _outproj_body(a_ref, b_ref, wa_ref, wb_ref, h_ref, g_ref, beta_ref, o_ref):
    half = a_ref.shape[0] // 2
    for rows in (slice(0, half), slice(half, 2 * half)):
        mix = jnp.dot(a_ref[rows, :], wa_ref[...], preferred_element_type=F32)
        mix = mix + jnp.dot(b_ref[rows, :], wb_ref[...], preferred_element_type=F32)
        o_ref[rows, :] = _layer_norm(ALPHA * h_ref[rows, :] + mix, g_ref[...], beta_ref[...])


def _outproj_ln(a, b, w_out, h, g, beta, tm):
    t, d = h.shape
    ka = a.shape[1]
    row = lambda i: (i, 0)
    return pl.pallas_call(
        _outproj_body,
        grid=(t // tm,),
        in_specs=[pl.BlockSpec((tm, ka), row), pl.BlockSpec((tm, ka), row),
                  pl.BlockSpec((ka, d), lambda i: (0, 0)), pl.BlockSpec((ka, d), lambda i: (1, 0)),
                  pl.BlockSpec((tm, d), row), pl.BlockSpec((1, d), lambda i: (0, 0)),
                  pl.BlockSpec((1, d), lambda i: (0, 0))],
        out_specs=pl.BlockSpec((tm, d), row),
        out_shape=jax.ShapeDtypeStruct((t, d), F32),
        compiler_params=_params("parallel"),
        name="outproj_ln",
    )(a, b, w_out, w_out, h, g, beta)


def _second_of_four(a, b, c, d):
    hi_ab, lo_ab = jnp.maximum(a, b), jnp.minimum(a, b)
    hi_cd, lo_cd = jnp.maximum(c, d), jnp.minimum(c, d)
    return jnp.maximum(jnp.maximum(lo_ab, lo_cd), jnp.minimum(hi_ab, hi_cd))


def _router_body(h_ref, wr_ref, bias_ref, e_ref, r_ref):
    tm = h_ref.shape[0]
    h = h_ref[...]
    h_hi = h.astype(BF)
    h_lo = (h - h_hi.astype(F32)).astype(BF)
    both = jnp.dot(h_hi, wr_ref[...], preferred_element_type=F32)
    logits = (both[:, :LANES] + both[:, LANES:]) + jnp.dot(h_lo, wr_ref[:, :LANES], preferred_element_type=F32)
    lt = logits.T
    aff = [jax.nn.sigmoid(lt[SUBLANES * j:SUBLANES * (j + 1), :]) for j in range(EXPERTS_PER_GROUP)]
    sel = [aff[j] + bias_ref[SUBLANES * j:SUBLANES * (j + 1), :] for j in range(EXPERTS_PER_GROUP)]
    top1 = jnp.maximum(jnp.maximum(sel[0], sel[1]), jnp.maximum(sel[2], sel[3]))
    score = top1 + _second_of_four(*sel)
    gid = lax.broadcasted_iota(I32, (N_GROUPS, tm), 0)
    best = jnp.min(jnp.where(score == jnp.max(score, axis=0, keepdims=True), gid, N_GROUPS),
                   axis=0, keepdims=True)
    in_grp = gid == best
    pick = lambda x: jnp.sum(jnp.where(in_grp, x, 0.0), axis=0, keepdims=True)
    s4 = [pick(x) for x in sel]
    a4 = [pick(x) for x in aff]

    def argmax4(vals):
        j, v = jnp.zeros((1, tm), I32), vals[0]
        for n in range(1, EXPERTS_PER_GROUP):
            better = vals[n] > v
            j, v = jnp.where(better, n, j), jnp.where(better, vals[n], v)
        return j

    j0 = argmax4(s4)
    j1 = argmax4([jnp.where(j0 == n, -jnp.inf, s4[n]) for n in range(EXPERTS_PER_GROUP)])
    take = lambda j: sum(jnp.where(j == n, a4[n], 0.0) for n in range(EXPERTS_PER_GROUP))
    w0, w1 = take(j0), take(j1)
    total = w0 + w1
    e0 = best * EXPERTS_PER_GROUP + j0
    e1 = best * EXPERTS_PER_GROUP + j1
    rid = lax.broadcasted_iota(I32, (SUBLANES, tm), 0)
    e_ref[...] = jnp.where(rid == 0, e0, jnp.where(rid == 1, e1, 0))
    rid = lax.broadcasted_iota(I32, (LANES, tm), 0)
    rows = jnp.where(rid == 0, w0 / total, jnp.where(rid == 1, w1 / total, 0.0))
    r_ref[...] = rows.T


def _router(h, wr_perm, bias_perm, tm):
    t, d = h.shape
    return pl.pallas_call(
        _router_body,
        grid=(t // tm,),
        in_specs=[pl.BlockSpec((tm, d), lambda i: (i, 0)), pl.BlockSpec((d, 2 * LANES), lambda i: (0, 0)),
                  pl.BlockSpec((N_EXPERTS, 1), lambda i: (0, 0))],
        out_specs=[pl.BlockSpec((SUBLANES, tm), lambda i: (0, i)), pl.BlockSpec((tm, LANES), lambda i: (i, 0))],
        out_shape=[jax.ShapeDtypeStruct((SUBLANES, t), I32), jax.ShapeDtypeStruct((t, LANES), F32)],
        compiler_params=_params("parallel"),
        name="router",
    )(h, wr_perm, bias_perm)


def _dispatch_plan(e01, tm):
    t = e01.shape[1]
    flat = e01.reshape(-1)
    onehot = (flat[:, None] == jnp.arange(N_EXPERTS, dtype=I32)[None, :]).astype(I32)
    cum = jnp.cumsum(onehot, axis=0)
    rank = jnp.sum(onehot * cum, axis=1) - 1
    counts = cum[-1]
    padded = ((counts + tm - 1) // tm) * tm
    starts = jnp.concatenate([jnp.zeros((1,), I32), jnp.cumsum(padded).astype(I32)])
    pos = jnp.sum(onehot * starts[None, :N_EXPERTS], axis=1) + rank
    return starts, starts[:N_EXPERTS] + counts, pos[:t], pos[t:]


def _dispatch_body(p0_ref, p1_ref, gs_ref, ge_ref, h_ref, r_ref, xs_hbm, xbuf, zbuf, sem, zsem, *, tm, tt,
                   n_steps):
    i = pl.program_id(0)
    slot = i % 2
    n_tail = (xs_hbm.shape[0] - gs_ref[N_EXPERTS]) // tm

    def zero_copy(row0, size):
        return pltpu.make_async_copy(zbuf.at[pl.ds(0, size), :], xs_hbm.at[pl.ds(row0, size), :], zsem.at[0])

    def for_each_fill(fn):
        def group_pad(e, carry):
            end = gs_ref[e + 1]
            pad = end - ge_ref[e]
            covered = 0
            size = tm // 2
            while size >= SUBLANES:
                take = pad & size

                @pl.when(take != 0)
                def _():
                    fn(zero_copy(pl.multiple_of(end - covered - size, size), size))

                covered = covered + take
                size //= 2
            for k in range(SUBLANES - 1):
                @pl.when(k < (pad & (SUBLANES - 1)))
                def _():
                    fn(zero_copy(ge_ref[e] + k, 1))
            return carry

        def tail_tile(k, carry):
            fn(zero_copy(pl.multiple_of(gs_ref[N_EXPERTS] + k * tm, tm), tm))
            return carry

        lax.fori_loop(0, N_EXPERTS, group_pad, 0)
        lax.fori_loop(0, n_tail, tail_tile, 0)

    def wait_rows(sl):
        for _ in range(2):
            pltpu.make_async_copy(xbuf.at[sl, 0], xs_hbm.at[pl.ds(0, tt), :], sem.at[sl]).wait()

    @pl.when(i == 0)
    def _():
        zbuf[...] = jnp.zeros(zbuf.shape, F32)
        for_each_fill(lambda cp: cp.start())

    @pl.when(i >= 2)
    def _():
        wait_rows(slot)

    d = h_ref.shape[1]
    gates = r_ref[...]
    lane = lax.broadcasted_iota(I32, gates.shape, 1)
    for k in range(MOE_TOPK):
        xbuf[slot, k, :, :d] = h_ref[...]
    xbuf[slot, 0, :, d:] = jnp.where(lane == 0, gates, jnp.where(lane == 1, ALPHA, 0.0))
    xbuf[slot, 1, :, d:] = jnp.where(lane == 0, pltpu.roll(gates, LANES - 1, axis=1), 0.0)
    base = i * tt
    for r in range(tt):
        pltpu.make_async_copy(xbuf.at[slot, 0, pl.ds(r, 1), :], xs_hbm.at[pl.ds(p0_ref[base + r], 1), :],
                              sem.at[slot]).start(priority=0)
        pltpu.make_async_copy(xbuf.at[slot, 1, pl.ds(r, 1), :], xs_hbm.at[pl.ds(p1_ref[base + r], 1), :],
                              sem.at[slot]).start(priority=1)

    @pl.when(i == n_steps - 1)
    def _():
        wait_rows(slot)
        if n_steps > 1:
            wait_rows(1 - slot)
        for_each_fill(lambda cp: cp.wait())


def _dispatch(h, gates, starts, ends, pos0, pos1, tm, tt):
    t, d = h.shape
    dx = d + LANES
    n_steps = t // tt
    n_rows = 2 * t + N_EXPERTS * tm
    body = functools.partial(_dispatch_body, tm=tm, tt=tt, n_steps=n_steps)
    grid_spec = pltpu.PrefetchScalarGridSpec(
        num_scalar_prefetch=4,
        grid=(n_steps,),
        in_specs=[pl.BlockSpec((tt, d), lambda i, *_: (i, 0)), pl.BlockSpec((tt, LANES), lambda i, *_: (i, 0))],
        out_specs=pl.BlockSpec(memory_space=pl.ANY),
        scratch_shapes=[pltpu.VMEM((2, MOE_TOPK, tt, dx), F32), pltpu.VMEM((tm, dx), F32),
                        pltpu.SemaphoreType.DMA((2,)), pltpu.SemaphoreType.DMA((1,))],
    )
    return pl.pallas_call(
        body,
        grid_spec=grid_spec,
        out_shape=jax.ShapeDtypeStruct((n_rows, dx), F32),
        compiler_params=_params("arbitrary"),
        name="dispatch",
    )(pos0, pos1, starts, ends, h, gates)


def _experts_body(gs_ref, xs_hbm, wgu_ref, wd_ref, ys_hbm, xbuf, ybuf, xsem, osem, *, tm):
    e = pl.program_id(0)
    first = gs_ref[e] // tm
    n_tiles = gs_ref[e + 1] // tm - first
    total = gs_ref[N_EXPERTS] // tm
    ring = xbuf.shape[0]

    def in_copy(g, slot):
        row0 = pl.multiple_of(g * tm, tm)
        return pltpu.make_async_copy(xs_hbm.at[pl.ds(row0, tm), :], xbuf.at[slot], xsem.at[slot])

    def out_copy(g, slot):
        row0 = pl.multiple_of(g * tm, tm)
        return pltpu.make_async_copy(ybuf.at[slot], ys_hbm.at[pl.ds(row0, tm), :], osem.at[slot])

    @pl.when(e == 0)
    def _():
        for g in range(ring - 1):
            in_copy(jnp.minimum(g, total - 1), g).start(priority=1)
        ybuf[...] = jnp.zeros(ybuf.shape, F32)
        out_copy(0, 0).start()
        out_copy(1, 1).start()

    @pl.when(n_tiles > 0)
    def _():
        def tile(k, carry):
            g = first + k
            slot = g % 2
            xslot = g % ring
            in_copy(g, xslot).wait()
            out_copy(g, slot).wait()
            in_copy(jnp.minimum(g + ring - 1, total - 1), (g + ring - 1) % ring).start(priority=1)
            d = ybuf.shape[2]
            x = xbuf[xslot, :, :d]
            row_gate = xbuf[xslot, :, d:d + 1]
            row_residual = xbuf[xslot, :, d + 1:d + 2]
            gu = jnp.dot(x.astype(BF), wgu_ref[0, 0].astype(BF), preferred_element_type=F32)
            gate, up = gu[:, :EXPERT_FF], gu[:, EXPERT_FF:]
            hidden = (gate * jax.nn.sigmoid(gate) * up).astype(BF)
            y = jnp.dot(hidden, wd_ref[0, 0].astype(BF), preferred_element_type=F32)
            ybuf[slot] = row_gate * y + row_residual * x
            out_copy(g, slot).start(priority=1)
            return carry

        lax.fori_loop(0, n_tiles, tile, 0)

    @pl.when(e == N_EXPERTS - 1)
    def _():
        for n in range(ring - 1):
            in_copy(0, (total + n) % ring).wait()
        out_copy(0, 0).wait()
        out_copy(1, 1).wait()
        end = gs_ref[N_EXPERTS]
        n_tail = (ys_hbm.shape[0] - end) // tm
        ybuf[0] = jnp.zeros(ybuf.shape[1:], F32)

        def tail_copy(k):
            row0 = pl.multiple_of(end + k * tm, tm)
            return pltpu.make_async_copy(ybuf.at[0], ys_hbm.at[pl.ds(row0, tm), :], osem.at[0])

        def start_one(k, carry):
            tail_copy(k).start()
            return carry

        def wait_one(k, carry):
            tail_copy(k).wait()
            return carry

        lax.fori_loop(0, n_tail, start_one, 0)
        lax.fori_loop(0, n_tail, wait_one, 0)


EXPERT_IN_RING = 12


def _experts(xs, starts, w_gate_up, w_down, layer, tm):
    n_rows, dx = xs.shape
    d = dx - LANES
    body = functools.partial(_experts_body, tm=tm)
    grid_spec = pltpu.PrefetchScalarGridSpec(
        num_scalar_prefetch=1,
        grid=(N_EXPERTS,),
        in_specs=[pl.BlockSpec(memory_space=pl.ANY),
                  pl.BlockSpec((1, 1, d, 2 * EXPERT_FF), lambda e, gs: (layer, e, 0, 0)),
                  pl.BlockSpec((1, 1, EXPERT_FF, d), lambda e, gs: (layer, e, 0, 0))],
        out_specs=pl.BlockSpec(memory_space=pl.ANY),
        scratch_shapes=[pltpu.VMEM((EXPERT_IN_RING, tm, dx), F32), pltpu.VMEM((2, tm, d), F32),
                        pltpu.SemaphoreType.DMA((EXPERT_IN_RING,)), pltpu.SemaphoreType.DMA((2,))],
    )
    return pl.pallas_call(
        body,
        grid_spec=grid_spec,
        out_shape=jax.ShapeDtypeStruct((n_rows, d), F32),
        compiler_params=_params("arbitrary"),
        name="experts",
    )(starts, xs, w_gate_up, w_down)


def _combine_body(p0_ref, p1_ref, ys_hbm, g_ref, beta_ref, o_ref, buf0, buf1, sem, *, tc, n_steps):
    i = pl.program_id(0)
    ring = buf0.shape[0]

    def start_row(base, r, sl):
        pltpu.make_async_copy(ys_hbm.at[pl.ds(p0_ref[base + r], 1), :], buf0.at[sl, pl.ds(r, 1), :],
                              sem.at[sl]).start(priority=0)
        pltpu.make_async_copy(ys_hbm.at[pl.ds(p1_ref[base + r], 1), :], buf1.at[sl, pl.ds(r, 1), :],
                              sem.at[sl]).start(priority=1)

    def wait_rows(sl):
        pltpu.make_async_copy(ys_hbm.at[pl.ds(0, tc), :], buf0.at[sl], sem.at[sl]).wait()
        pltpu.make_async_copy(ys_hbm.at[pl.ds(0, tc), :], buf1.at[sl], sem.at[sl]).wait()

    @pl.when(i == 0)
    def _():
        for sl in range(ring):
            def one(r, carry):
                start_row(sl * tc, r, sl)
                return carry

            lax.fori_loop(0, tc, one, 0)

    nxt = jnp.minimum(i + 1, n_steps - 1) * ring * tc
    for sl in range(ring):
        wait_rows(sl)
        o_ref[sl * tc:(sl + 1) * tc, :] = _layer_norm(buf0[sl] + buf1[sl], g_ref[...], beta_ref[...])
        for r in range(tc):
            start_row(nxt + sl * tc, r, sl)

    @pl.when(i == n_steps - 1)
    def _():
        for sl in range(ring):
            wait_rows(sl)


COMBINE_RING = 4


def _combine_ln(ys, pos0, pos1, g, beta, tc):
    t, d = pos0.shape[0], ys.shape[1]
    n_steps = t // (COMBINE_RING * tc)
    body = functools.partial(_combine_body, tc=tc, n_steps=n_steps)
    row = lambda i, p0, p1: (i, 0)
    const = lambda i, p0, p1: (0, 0)
    grid_spec = pltpu.PrefetchScalarGridSpec(
        num_scalar_prefetch=2,
        grid=(n_steps,),
        in_specs=[pl.BlockSpec(memory_space=pl.ANY), pl.BlockSpec((1, d), const), pl.BlockSpec((1, d), const)],
        out_specs=pl.BlockSpec((COMBINE_RING * tc, d), row),
        scratch_shapes=[pltpu.VMEM((COMBINE_RING, tc, d), F32), pltpu.VMEM((COMBINE_RING, tc, d), F32),
                        pltpu.SemaphoreType.DMA((COMBINE_RING,))],
    )
    return pl.pallas_call(
        body,
        grid_spec=grid_spec,
        out_shape=jax.ShapeDtypeStruct((t, d), F32),
        compiler_params=_params("arbitrary"),
        name="combine_ln",
    )(pos0, pos1, ys, g, beta)


def _moe_ln(h, wr_perm, bias_perm, w_gate_up, w_down, layer, g, beta, tm_router=512, tm_expert=128, tc=256):
    e01, gates = _router(h, wr_perm, bias_perm, tm_router)
    starts, ends, pos0, pos1 = _dispatch_plan(e01[:MOE_TOPK], tm_expert)
    xs = _dispatch(h, gates, starts, ends, pos0, pos1, tm_expert, tc)
    ys = _experts(xs, starts, w_gate_up, w_down, layer, tm_expert)
    return _combine_ln(ys, pos0, pos1, g, beta, tc)


def _swap_halves(w):
    half = w.shape[-1] // 2
    return jnp.concatenate([w[..., half:], w[..., :half]], axis=-1)


def _even_weights(w_in, w_q_up, w_kv_up, forget_bias):
    d = w_in.shape[0]
    o_kv = MLA_Q_LORA
    o_kr = o_kv + MLA_KV_LORA
    o_fq = o_kr + MLA_ROPE
    hd = FOX_HEADS * FOX_HEAD_DIM
    o_fl = o_fq + 3 * hd
    k_r = w_in[:, o_kr:o_fq]
    f_l = jnp.pad(w_in[:, o_fl:], ((0, 0), (0, LANES - FOX_HEADS)))
    w_small = jnp.concatenate([w_in[:, :o_kr], k_r, _swap_halves(k_r), f_l], axis=1).astype(BF)
    w_fox = w_in[:, o_fq:o_fl].astype(BF)
    cs_fox = jnp.concatenate([jnp.full((1, hd), FOX_SCALE, F32), jnp.ones((1, 2 * hd), F32)], axis=1)
    wq = w_q_up.reshape(MLA_Q_LORA, MLA_HEADS, MLA_NOPE + MLA_ROPE)
    wqn = wq[:, :, :MLA_NOPE].reshape(MLA_Q_LORA, -1).astype(BF)
    pe = wq[:, :, MLA_NOPE:]
    wqp = jnp.concatenate([pe, _swap_halves(pe)], axis=-1).reshape(MLA_Q_LORA, -1).astype(BF)
    wkv = w_kv_up.reshape(MLA_KV_LORA, MLA_HEADS, MLA_NOPE + MLA_V)
    wk = wkv[:, :, :MLA_NOPE].reshape(MLA_KV_LORA, -1).astype(BF)
    wv = wkv[:, :, MLA_NOPE:].reshape(MLA_KV_LORA, -1).astype(BF)
    fb = jnp.pad(forget_bias.astype(F32), (0, LANES - FOX_HEADS)).reshape(1, LANES)
    return w_small, w_fox, cs_fox, wqn, wqp, wk, wv, fb


def _rope_table(seq):
    half = MLA_ROPE // 2
    inv_freq = ROPE_THETA ** (-np.arange(half, dtype=np.float64) / half)
    ang = np.arange(seq, dtype=np.float64)[:, None] * inv_freq[None, :]
    cos, sin = np.cos(ang), np.sin(ang)
    return jnp.asarray(np.concatenate([cos, cos, -sin, sin], axis=1), dtype=F32)


def _router_weights(w_router, router_bias):
    r = np.arange(N_EXPERTS)
    perm = (r % N_GROUPS) * EXPERTS_PER_GROUP + r // N_GROUPS
    wr = jnp.pad(w_router[:, perm], ((0, 0), (0, LANES - N_EXPERTS)))
    w_hi = wr.astype(BF)
    w_lo = (wr - w_hi.astype(F32)).astype(BF)
    return jnp.concatenate([w_hi, w_lo], axis=1), router_bias.astype(F32)[perm].reshape(N_EXPERTS, 1)


def _even_layer(h, batch, seq, w_in, q_norm, w_q_up, kv_norm, w_kv_up, forget_bias, w_out, g, beta):
    w_small, w_fox, cs_fox, wqn, wqp, wk, wv, fb = _even_weights(w_in, w_q_up, w_kv_up, forget_bias)
    fox_qkv = _mm(h, w_fox, cs_fox, BF, ROW_TILE, w_fox.shape[1] // 2)
    qn, qp, kn, v, kp, forget_logits = _mla_prep(h, w_small, _rope_table(seq), q_norm.reshape(1, -1),
                                                 kv_norm.reshape(1, -1), wqn, wqp, wk, wv, seq, ROW_TILE)
    q_bias, k_bias, edge = _fox_cum(forget_logits, fb, batch, seq, ROW_TILE)
    a = _mla_attention(qn, qp, kn, v, kp, batch, seq, ROW_TILE, HEAD_GROUP)
    bo = _fox_attention(fox_qkv, q_bias, k_bias, edge, batch, seq, ROW_TILE, HEAD_GROUP)
    return _outproj_ln(a, bo, w_out.astype(BF), h, g, beta, ROW_TILE)


def _odd_layer(h, batch, seq, w_in, sinks, w_out, g, beta):
    n_sq = SWA_Q_HEADS * SWA_HEAD_DIM
    n_skv = SWA_KV_HEADS * SWA_HEAD_DIM
    n_m = MOBA_HEADS * MOBA_HEAD_DIM
    o_mq = n_sq + 2 * n_skv
    w_big = jnp.concatenate([w_in[:, :n_sq], w_in[:, o_mq:], w_in[:, n_sq:o_mq]], axis=1).astype(BF)
    cs = jnp.concatenate([jnp.full((1, n_sq), SWA_SCALE, F32), jnp.full((1, n_m), MOBA_SCALE, F32),
                          jnp.ones((1, 2 * n_m + 2 * n_skv), F32)], axis=1)
    big = _mm(h, w_big, cs, BF, ROW_TILE, w_big.shape[1] // 2)
    blocks = lambda cols: cols // LANES
    k_mean = _moba_kmean(big, batch, seq, (n_sq + n_m) // n_m)
    slopes = jnp.asarray(LOG2E * 2.0 ** (-8.0 * np.arange(1, MOBA_HEADS + 1) / MOBA_HEADS), dtype=F32)
    c = _swa_attention(big, sinks.astype(F32), batch, seq, 0, blocks(n_sq + 3 * n_m), blocks(n_sq + 3 * n_m) + 1)
    dd = _moba_attention(big, k_mean, slopes, batch, seq, blocks(n_sq), blocks(n_sq + n_m), blocks(n_sq + 2 * n_m),
                         ROW_TILE, HEAD_GROUP)
    return _outproj_ln(c, dd, w_out.astype(BF), h, g, beta, ROW_TILE)


def kernel(x, w_router, router_bias, even_w_in, even_q_norm, even_w_q_up, even_kv_norm, even_w_kv_up,
           even_forget_bias, even_w_out, odd_w_in, odd_sinks, odd_w_out, ln_mix_g, ln_mix_b, ln_ffn_g, ln_ffn_b,
           w_gate_up, w_down):
    batch, seq, d = x.shape
    h = x.reshape(batch * seq, d)
    wr_perm, bias_perm = _router_weights(w_router, router_bias)
    row = lambda p, layer: p[layer].reshape(1, d)
    for layer in range(DEPTH):
        i = layer // 2
        if layer % 2 == 0:
            h = _even_layer(h, batch, seq, even_w_in[i], even_q_norm[i], even_w_q_up[i], even_kv_norm[i],
                            even_w_kv_up[i], even_forget_bias[i], even_w_out[i], row(ln_mix_g, layer),
                            row(ln_mix_b, layer))
        else:
            h = _odd_layer(h, batch, seq, odd_w_in[i], odd_sinks[i], odd_w_out[i], row(ln_mix_g, layer),
                           row(ln_mix_b, layer))
        h = _moe_ln(h, wr_perm, bias_perm, w_gate_up, w_down, layer, row(ln_ffn_g, layer), row(ln_ffn_b, layer))
    return h.reshape(batch, seq, d)
```

```python
import functools

import numpy as np
import jax
import jax.numpy as jnp
from jax import lax
from jax.experimental import pallas as pl
from jax.experimental.pallas import tpu as pltpu

BF = jnp.bfloat16
F32 = jnp.float32
I32 = jnp.int32

LANES = 128
SUBLANES = 8
VMEM_LIMIT = 56 * 1024 * 1024

ROW_TILE = 512
HEAD_GROUP = 4

DEPTH = 2
LN_EPS = 1e-5
RMS_EPS = 1e-6
ALPHA = (2 * DEPTH) ** 0.25
MLA_HEADS, MLA_Q_LORA, MLA_KV_LORA, MLA_NOPE, MLA_ROPE, MLA_V = 8, 512, 256, 128, 64, 128
ROPE_THETA = 10000.0
FOX_HEADS, FOX_HEAD_DIM = 8, 128
SWA_Q_HEADS, SWA_KV_HEADS, SWA_HEAD_DIM, SWA_WINDOW = 16, 2, 64, 128
MOBA_HEADS, MOBA_HEAD_DIM, MOBA_BLOCK, MOBA_TOPK = 8, 128, 256, 3
N_EXPERTS, N_GROUPS, MOE_TOPK, EXPERT_FF = 32, 8, 2, 512
EXPERTS_PER_GROUP = N_EXPERTS // N_GROUPS

LOG2E = 1.4426950408889634
MLA_SCALE = (MLA_NOPE + MLA_ROPE) ** -0.5 * LOG2E
FOX_SCALE = FOX_HEAD_DIM ** -0.5 * LOG2E
MOBA_SCALE = MOBA_HEAD_DIM ** -0.5 * LOG2E
SWA_SCALE = SWA_HEAD_DIM ** -0.5 * LOG2E
NEG = -1e30

NT_DIMS = (((1,), (1,)), ((), ()))


def _params(*sem):
    return pltpu.CompilerParams(dimension_semantics=sem, vmem_limit_bytes=VMEM_LIMIT)


def _mm_body(x_ref, w_ref, cs_ref, o_ref):
    acc = jnp.dot(x_ref[...].astype(BF), w_ref[...], preferred_element_type=F32)
    o_ref[...] = (acc * cs_ref[...]).astype(o_ref.dtype)


def _mm(x, w, col_scale, out_dtype, tm, tn):
    m, k = x.shape
    n = w.shape[1]
    return pl.pallas_call(
        _mm_body,
        grid=(n // tn, m // tm),
        in_specs=[
            pl.BlockSpec((tm, k), lambda j, i: (i, 0)),
            pl.BlockSpec((k, tn), lambda j, i: (0, j)),
            pl.BlockSpec((1, tn), lambda j, i: (0, j)),
        ],
        out_specs=pl.BlockSpec((tm, tn), lambda j, i: (i, j)),
        out_shape=jax.ShapeDtypeStruct((m, n), out_dtype),
        compiler_params=_params("parallel", "parallel"),
        name="proj",
    )(x, w, col_scale)


def _rms(x, g):
    return x * lax.rsqrt(jnp.mean(x * x, axis=-1, keepdims=True) + RMS_EPS) * g


def _rope_pair(slab, table):
    r = slab * table
    return r + pltpu.roll(r, MLA_ROPE, axis=1)


def _mla_prep_body(x_ref, ws_ref, rope_ref, qg_ref, kvg_ref, wqn_ref, wqp_ref, wk_ref, wv_ref,
                   qn_ref, qp_ref, kn_ref, v_ref, kp_ref, fl_ref):
    small = jnp.dot(x_ref[...].astype(BF), ws_ref[...], preferred_element_type=F32)
    o_kv, o_kr, o_fl = MLA_Q_LORA, MLA_Q_LORA + MLA_KV_LORA, MLA_Q_LORA + MLA_KV_LORA + LANES
    fl_ref[...] = small[:, o_fl:]
    table = rope_ref[...]
    cqn = _rms(small[:, :o_kv], qg_ref[...]).astype(BF)
    qn_ref[...] = (jnp.dot(cqn, wqn_ref[...], preferred_element_type=F32) * MLA_SCALE).astype(BF)
    qp = jnp.dot(cqn, wqp_ref[...], preferred_element_type=F32)
    for h in range(MLA_HEADS):
        sl = slice(h * LANES, (h + 1) * LANES)
        qp_ref[:, sl] = (_rope_pair(qp[:, sl], table) * MLA_SCALE).astype(BF)
    ckvn = _rms(small[:, o_kv:o_kr], kvg_ref[...]).astype(BF)
    kn_ref[...] = jnp.dot(ckvn, wk_ref[...], preferred_element_type=F32).astype(BF)
    v_ref[...] = jnp.dot(ckvn, wv_ref[...], preferred_element_type=F32).astype(BF)
    kr = _rope_pair(small[:, o_kr:o_fl], table)
    lane = lax.broadcasted_iota(I32, kr.shape, 1)
    kp_ref[...] = jnp.where(lane < MLA_ROPE, kr, 0.0).astype(BF)


def _mla_prep(x, w_small, rope_table, q_norm, kv_norm, wqn, wqp, wk, wv, seq, tm):
    t, d = x.shape
    n_s = seq // tm
    hd = MLA_HEADS * LANES
    row = lambda i: (i, 0)
    const = lambda i: (0, 0)
    out = lambda w: pl.BlockSpec((tm, w), row)
    return pl.pallas_call(
        _mla_prep_body,
        grid=(t // tm,),
        in_specs=[
            pl.BlockSpec((tm, d), row),
            pl.BlockSpec(w_small.shape, const),
            pl.BlockSpec((tm, LANES), lambda i: (i % n_s, 0)),
            pl.BlockSpec((1, MLA_Q_LORA), const),
            pl.BlockSpec((1, MLA_KV_LORA), const),
            pl.BlockSpec((MLA_Q_LORA, hd), const),
            pl.BlockSpec((MLA_Q_LORA, hd), const),
            pl.BlockSpec((MLA_KV_LORA, hd), const),
            pl.BlockSpec((MLA_KV_LORA, hd), const),
        ],
        out_specs=[out(hd), out(hd), out(hd), out(hd), out(LANES), out(LANES)],
        out_shape=[jax.ShapeDtypeStruct((t, hd), BF)] * 4 + [jax.ShapeDtypeStruct((t, LANES), BF),
                                                             jax.ShapeDtypeStruct((t, LANES), F32)],
        compiler_params=_params("parallel"),
        name="mla_prep",
    )(x, w_small, rope_table, q_norm, kv_norm, wqn, wqp, wk, wv)


N_SPLIT = 3


def _split_piece(x, which):
    hi = x.astype(BF).astype(F32)
    rest = x - hi
    mid = rest.astype(BF).astype(F32)
    return jnp.where(which == 0, hi, jnp.where(which == 1, mid, rest - mid))


def _fox_cum_body(fl_ref, fb_ref, spread_ref, qe_ref, ke_ref, edge_ref, carry_ref):
    @pl.when(pl.program_id(1) == 0)
    def _():
        carry_ref[...] = jnp.zeros_like(carry_ref)

    z = fl_ref[...] + fb_ref[...]
    log_f = jnp.minimum(z, 0.0) - jnp.log1p(jnp.exp(-jnp.abs(z)))
    tm = z.shape[0]
    r = lax.broadcasted_iota(I32, (tm, tm), 0)
    c = lax.broadcasted_iota(I32, (tm, tm), 1)
    tri = jnp.where(r >= c, 1.0, 0.0).astype(BF)
    parts = jnp.dot(tri, jnp.concatenate([_split_piece(log_f, n).astype(BF) for n in range(N_SPLIT)], axis=1),
                    preferred_element_type=F32)
    cum = (parts[:, :LANES] + parts[:, LANES:2 * LANES]) + parts[:, 2 * LANES:] + carry_ref[...]
    carry_ref[...] = cum[tm - 1:tm, :]
    cum = cum * LOG2E
    width = qe_ref.shape[1]
    pieces = jnp.concatenate([_split_piece(cum, n).astype(BF) for n in range(N_SPLIT)], axis=1)
    placed = jnp.dot(pieces, spread_ref[...], preferred_element_type=F32)
    li = lax.broadcasted_iota(I32, (tm, width), 1) % LANES
    qe_ref[...] = jnp.where((li >= N_SPLIT) & (li < 2 * N_SPLIT), 1.0, placed[:, :width]).astype(BF)
    ke_ref[...] = jnp.where(li < N_SPLIT, 1.0, placed[:, width:]).astype(BF)
    edge_ref[0, 0:1, :] = cum[0:1, :]
    edge_ref[0, 1:2, :] = cum[tm - 1:tm, :]


def _fox_spread():
    width = FOX_HEADS * LANES
    m = np.zeros((N_SPLIT * LANES, 2 * width), np.float32)
    for j in range(N_SPLIT):
        for h in range(FOX_HEADS):
            m[j * LANES + h, h * LANES + j] = 1.0
            m[j * LANES + h, width + h * LANES + N_SPLIT + j] = -1.0
    return jnp.asarray(m, dtype=BF)


def _fox_cum(forget_logits, forget_bias_row, batch, seq, tm):
    t = forget_logits.shape[0]
    n_s = seq // tm
    width = FOX_HEADS * LANES
    q_bias, k_bias, edge = pl.pallas_call(
        _fox_cum_body,
        grid=(batch, n_s),
        in_specs=[
            pl.BlockSpec((tm, LANES), lambda b, i: (b * n_s + i, 0)),
            pl.BlockSpec((1, LANES), lambda b, i: (0, 0)),
            pl.BlockSpec((N_SPLIT * LANES, 2 * width), lambda b, i: (0, 0)),
        ],
        out_specs=[
            pl.BlockSpec((tm, width), lambda b, i: (b * n_s + i, 0)),
            pl.BlockSpec((tm, width), lambda b, i: (b * n_s + i, 0)),
            pl.BlockSpec((1, 2, LANES), lambda b, i: (b * n_s + i, 0, 0)),
        ],
        out_shape=[jax.ShapeDtypeStruct((t, width), BF), jax.ShapeDtypeStruct((t, width), BF),
                   jax.ShapeDtypeStruct((batch * n_s, 2, LANES), F32)],
        scratch_shapes=[pltpu.VMEM((1, LANES), F32)],
        compiler_params=_params("parallel", "arbitrary"),
        name="fox_cum",
    )(forget_logits, forget_bias_row, _fox_spread())
    return q_bias, k_bias, edge[:, :, :FOX_HEADS].reshape(-1)


def _softmax_update(s_all, v_all, m_ref, l_ref, acc_ref, rows=slice(None)):
    heads = range(len(s_all))
    m_prev = [m_ref[g, rows] for g in heads]
    m_new = [jnp.maximum(m_prev[g], jnp.max(s_all[g], axis=1, keepdims=True)) for g in heads]
    reps = s_all[0].shape[1] // LANES
    p = [jnp.exp2(s_all[g] - jnp.concatenate([m_new[g]] * reps, axis=1)) for g in heads]
    pv = [jnp.dot(p[g].astype(BF), v_all[g], preferred_element_type=F32) for g in heads]
    for g in heads:
        alpha = jnp.exp2(m_prev[g] - m_new[g])
        l_ref[g, rows] = alpha * l_ref[g, rows] + jnp.sum(p[g], axis=1, keepdims=True)
        acc_ref[g, rows] = alpha * acc_ref[g, rows] + pv[g]
        m_ref[g, rows] = m_new[g]


def _attn_body(*refs, mode, t, group):
    hg = pl.program_id(1)
    i = pl.program_id(2)
    heads = range(group)
    sl = lambda g: slice(g * LANES, (g + 1) * LANES)
    if mode == "moba":
        q_ref, k_ref, kx_ref, v_ref, km_ref, slope_ref, o_ref, m_ref, l_ref, acc_ref, qx_ref = refs
        for g in heads:
            qx_ref[g] = _moba_query_lanes(q_ref[:, sl(g)], km_ref[:, sl(g)], slope_ref[hg * group + g], i, t)
        q_extra = lambda g, rows: qx_ref[g, rows]
        k_extra = lambda g, ks, nk: kx_ref[pl.ds(ks, nk), :]
    elif mode == "mla":
        q_ref, qx_ref, k_ref, kx_ref, v_ref, o_ref, m_ref, l_ref, acc_ref = refs
        q_extra = lambda g, rows: qx_ref[rows, sl(g)]
        k_extra = lambda g, ks, nk: kx_ref[pl.ds(ks, nk), :]
    else:
        q_ref, qx_ref, k_ref, kx_ref, v_ref, edge_ref, o_ref, m_ref, l_ref, acc_ref, knorm_ref = refs
        q_extra = lambda g, rows: qx_ref[rows, sl(g)]
        k_extra = lambda g, ks, nk: kx_ref[pl.ds(ks, nk), sl(g)]
        first_tile = _fox_first_tile(q_ref, k_ref, edge_ref, knorm_ref, i, hg, t, group)

    def scores(g, ks, nk=t, rows=slice(None)):
        q = jnp.concatenate([q_ref[rows, sl(g)], q_extra(g, rows)], axis=1)
        k = jnp.concatenate([k_ref[pl.ds(ks, nk), sl(g)], k_extra(g, ks, nk)], axis=1)
        return lax.dot_general(q, k, NT_DIMS, preferred_element_type=F32)

    m_ref[...] = jnp.full(m_ref.shape, NEG, F32)
    l_ref[...] = jnp.zeros(l_ref.shape, F32)
    acc_ref[...] = jnp.zeros(acc_ref.shape, F32)

    values = lambda ks, nk=t: [v_ref[pl.ds(ks, nk), sl(g)] for g in heads]
    ks = pl.multiple_of(i * t, t)
    half = t // 2
    for rows, nk in ((slice(0, half), half), (slice(half, t), t)):
        r = lax.broadcasted_iota(I32, (half, nk), 0) + rows.start
        c = lax.broadcasted_iota(I32, (half, nk), 1)
        _softmax_update([jnp.where(c <= r, scores(g, ks, nk, rows), NEG) for g in heads], values(ks, nk),
                        m_ref, l_ref, acc_ref, rows)

    def past_tile(j, carry):
        ks = pl.multiple_of(j * t, t)
        _softmax_update([scores(g, ks) for g in heads], values(ks), m_ref, l_ref, acc_ref)
        return carry

    lax.fori_loop(first_tile if mode == "fox" else 0, i, past_tile, 0)
    for g in heads:
        o_ref[:, sl(g)] = (acc_ref[g] / l_ref[g]).astype(o_ref.dtype)


FOX_SKIP_GAP = 160.0


def _fox_first_tile(q_ref, k_ref, edge_ref, knorm_ref, i, hg, t, group):
    b = pl.program_id(0)
    n_q = pl.num_programs(2)
    sl = lambda g: slice(g * LANES, (g + 1) * LANES)

    def max_row_norm(x):
        x = x.astype(F32)
        return jnp.sqrt(jnp.max(jnp.sum(x * x, axis=1, keepdims=True)))

    @pl.when(i == 0)
    def _():
        for g in range(group):
            knorm_ref[g] = max_row_norm(k_ref[:, sl(g)])

    def edge(tile, last, g):
        return edge_ref[((b * n_q + tile) * 2 + last) * FOX_HEADS + hg * group + g]

    limit = [-(2.0 * 1.001 * max_row_norm(q_ref[:, sl(g)]) * knorm_ref[g] + FOX_SKIP_GAP) for g in range(group)]
    first = [edge(i, 0, g) for g in range(group)]

    def scan(j, lo):
        needed = first[0] - edge(j, 1, 0) >= limit[0]
        for g in range(1, group):
            needed = needed | (first[g] - edge(j, 1, g) >= limit[g])
        return jnp.where(needed, jnp.minimum(lo, j), lo)

    return lax.fori_loop(0, i, scan, i)


def _attention(mode, batch, seq, heads, t, group, operands, in_specs, extra_scratch=()):
    n_q = seq // t
    body = functools.partial(_attn_body, mode=mode, t=t, group=group)
    return pl.pallas_call(
        body,
        grid=(batch, heads // group, n_q),
        in_specs=in_specs,
        out_specs=pl.BlockSpec((t, group * LANES), lambda b, h, i: (b * n_q + i, h)),
        out_shape=jax.ShapeDtypeStruct((batch * seq, heads * LANES), BF),
        scratch_shapes=[pltpu.VMEM((group, t, LANES), F32), pltpu.VMEM((group, t, LANES), F32),
                        pltpu.VMEM((group, t, LANES), F32), *extra_scratch],
        compiler_params=_params("parallel", "parallel", "arbitrary"),
        name="attn_" + mode,
    )(*operands)


def _q_spec(t, n_q, group, col0=0):
    return pl.BlockSpec((t, group * LANES), lambda b, h, i: (b * n_q + i, col0 + h))


def _kv_spec(seq, group, col0=0):
    return pl.BlockSpec((seq, group * LANES), lambda b, h, i: (b, col0 + h))


def _mla_attention(qn, qp, kn, v, kp, batch, seq, t, group):
    n_q = seq // t
    specs = [_q_spec(t, n_q, group), _q_spec(t, n_q, group), _kv_spec(seq, group),
             pl.BlockSpec((seq, LANES), lambda b, h, i: (b, 0)), _kv_spec(seq, group)]
    return _attention("mla", batch, seq, MLA_HEADS, t, group, (qn, qp, kn, kp, v), specs)


def _fox_attention(qkv, q_bias, k_bias, edge, batch, seq, t, group):
    n_q = seq // t
    n_hg = FOX_HEADS // group
    specs = [_q_spec(t, n_q, group, 0), _q_spec(t, n_q, group, 0), _kv_spec(seq, group, n_hg),
             _kv_spec(seq, group, 0), _kv_spec(seq, group, 2 * n_hg), pl.BlockSpec(memory_space=pltpu.SMEM)]
    return _attention("fox", batch, seq, FOX_HEADS, t, group, (qkv, q_bias, qkv, k_bias, qkv, edge), specs,
                      (pltpu.SMEM((group,), F32),))


def _moba_key_lanes(seq):
    n_kb = seq // MOBA_BLOCK
    kpos = np.arange(seq)
    lanes = np.zeros((seq, LANES), np.float32)
    lanes[kpos, kpos // MOBA_BLOCK] = 1.0
    lanes[:, n_kb:n_kb + N_SPLIT] = 1.0
    lanes[:, n_kb + N_SPLIT:n_kb + 2 * N_SPLIT] = (MOBA_BLOCK * (kpos // MOBA_BLOCK))[:, None]
    lanes[:, n_kb + 2 * N_SPLIT:n_kb + 3 * N_SPLIT] = (kpos % MOBA_BLOCK)[:, None]
    return jnp.asarray(lanes, dtype=BF)


def _moba_query_lanes(q, k_mean, slope, i, t):
    n_kb = k_mean.shape[0]
    gate = lax.dot_general(k_mean.astype(BF), q, NT_DIMS, preferred_element_type=F32)
    blk = lax.broadcasted_iota(I32, (n_kb, t), 0)
    qpos_i = i * t + lax.broadcasted_iota(I32, (1, t), 1)
    own = qpos_i // MOBA_BLOCK
    beaten = jnp.zeros((n_kb, t), F32)
    for n in range(n_kb):
        g_n = gate[n:n + 1, :]
        wins = (g_n > gate) | ((g_n == gate) & (blk > n))
        beaten = beaten + jnp.where(wins & (own > n), 1.0, 0.0)
    attended = (blk == own) | ((blk < own) & (beaten < MOBA_TOPK))
    mask = jnp.concatenate([jnp.where(attended, 0.0, NEG), jnp.zeros((LANES - n_kb, t), F32)], axis=0)
    row = lax.broadcasted_iota(I32, (LANES, t), 0) - n_kb
    qpos = qpos_i.astype(F32)
    slope_row = jnp.full((1, t), slope, F32)
    which = row % N_SPLIT
    lanes = jnp.where(row < 0, mask,
                      jnp.where(row < N_SPLIT, _split_piece(-slope_row * qpos, which),
                                jnp.where(row < 3 * N_SPLIT, _split_piece(slope_row, which), 0.0)))
    return lanes.T.astype(BF)


def _moba_attention(big, k_mean, slopes, batch, seq, col_q, col_k, col_v, t, group):
    n_q = seq // t
    specs = [_q_spec(t, n_q, group, col_q // group), _kv_spec(seq, group, col_k // group),
             pl.BlockSpec((seq, LANES), lambda b, h, i: (0, 0)),
             _kv_spec(seq, group, col_v // group),
             pl.BlockSpec((seq // MOBA_BLOCK, group * LANES), lambda b, h, i: (b, h)),
             pl.BlockSpec(memory_space=pltpu.SMEM)]
    scratch = (pltpu.VMEM((group, t, LANES), BF),)
    return _attention("moba", batch, seq, MOBA_HEADS, t, group,
                      (big, big, _moba_key_lanes(seq), big, k_mean, slopes), specs, scratch)


def _kmean_body(k_ref, o_ref):
    n_kb = o_ref.shape[0]
    for n in range(n_kb):
        blk = k_ref[n * MOBA_BLOCK:(n + 1) * MOBA_BLOCK, :].astype(F32)
        o_ref[n:n + 1, :] = jnp.mean(blk, axis=0, keepdims=True)


def _moba_kmean(big, batch, seq, col_block):
    n_kb = seq // MOBA_BLOCK
    width = MOBA_HEADS * MOBA_HEAD_DIM
    return pl.pallas_call(
        _kmean_body,
        grid=(batch,),
        in_specs=[pl.BlockSpec((seq, width), lambda b: (b, col_block))],
        out_specs=pl.BlockSpec((n_kb, width), lambda b: (b, 0)),
        out_shape=jax.ShapeDtypeStruct((batch * n_kb, width), F32),
        compiler_params=_params("parallel"),
        name="moba_kmean",
    )(big)


def _swa_body(sink_ref, q_ref, kp_ref, kc_ref, vp_ref, vc_ref, o_ref):
    i = pl.program_id(1)
    w = SWA_WINDOW
    half = SWA_HEAD_DIM
    lane = lax.broadcasted_iota(I32, (w, LANES), 1)
    r = lax.broadcasted_iota(I32, (w, w), 0)
    c = lax.broadcasted_iota(I32, (w, w), 1)
    from_prev = c > r
    dist = (r - c + jnp.where(from_prev, w, 0)).astype(F32)
    valid = jnp.logical_not(from_prev) | (i > 0)
    heads_per_kv = SWA_Q_HEADS // SWA_KV_HEADS

    def lo_hi(ref, kvh):
        own = jnp.where((lane < half) == (kvh == 0), ref[...].astype(F32), 0.0)
        other = pltpu.roll(own, half, axis=1)
        lo, hi = (own, other) if kvh == 0 else (other, own)
        return lo.astype(BF), hi.astype(BF)

    for kvh in range(SWA_KV_HEADS):
        kp, kc, vp, vc = (lo_hi(ref, kvh) for ref in (kp_ref, kc_ref, vp_ref, vc_ref))
        heads = range(kvh * heads_per_kv, (kvh + 1) * heads_per_kv)
        q = {h: q_ref[:, (h // 2) * LANES:(h // 2 + 1) * LANES] for h in heads}
        s = {h: jnp.where(from_prev,
                          lax.dot_general(q[h], kp[h % 2], NT_DIMS, preferred_element_type=F32),
                          lax.dot_general(q[h], kc[h % 2], NT_DIMS, preferred_element_type=F32)) for h in heads}
        p, inv = {}, {}
        for h in heads:
            slope = float(LOG2E * 2.0 ** (-8.0 * (h + 1) / SWA_Q_HEADS))
            sink = sink_ref[h] * LOG2E
            logits = jnp.where(valid, s[h] - slope * dist, NEG)
            m = jnp.maximum(jnp.max(logits, axis=1, keepdims=True), sink)
            p[h] = jnp.exp2(logits - m)
            inv[h] = 1.0 / (jnp.sum(p[h], axis=1, keepdims=True) + jnp.exp2(sink - m))
        for pair in range(kvh * heads_per_kv // 2, (kvh + 1) * heads_per_kv // 2):
            out = jnp.zeros((w, LANES), F32)
            for h in (2 * pair, 2 * pair + 1):
                pn = p[h] * inv[h]
                out = out + jnp.dot(jnp.where(from_prev, pn, 0.0).astype(BF), vp[h % 2], preferred_element_type=F32)
                out = out + jnp.dot(jnp.where(from_prev, 0.0, pn).astype(BF), vc[h % 2], preferred_element_type=F32)
            o_ref[:, pair * LANES:(pair + 1) * LANES] = out.astype(o_ref.dtype)


def _swa_attention(big, sinks, batch, seq, col_q, col_k, col_v):
    w = SWA_WINDOW
    n_q = seq // w
    width = SWA_Q_HEADS * SWA_HEAD_DIM
    prev = lambda col: pl.BlockSpec((w, LANES), lambda b, i: (b * n_q + jnp.maximum(i - 1, 0), col))
    cur = lambda col: pl.BlockSpec((w, LANES), lambda b, i: (b * n_q + i, col))
    return pl.pallas_call(
        _swa_body,
        grid=(batch, n_q),
        in_specs=[pl.BlockSpec(memory_space=pltpu.SMEM),
                  pl.BlockSpec((w, width), lambda b, i: (b * n_q + i, col_q)),
                  prev(col_k), cur(col_k), prev(col_v), cur(col_v)],
        out_specs=pl.BlockSpec((w, width), lambda b, i: (b * n_q + i, 0)),
        out_shape=jax.ShapeDtypeStruct((batch * seq, width), BF),
        compiler_params=_params("parallel", "parallel"),
        name="swa",
    )(sinks, big, big, big, big, big)


def _layer_norm(z, g, b):
    mu = jnp.mean(z, axis=-1, keepdims=True)
    zc = z - mu
    var = jnp.mean(zc * zc, axis=-1, keepdims=True)
    return zc * lax.rsqrt(var + LN_EPS) * g + b


def _outproj_body(a_ref, b_ref, wa_ref, wb_ref, h_ref, g_ref, beta_ref, o_ref):
    half = a_ref.shape[0] // 2
    for rows in (slice(0, half), slice(half, 2 * half)):
        mix = jnp.dot(a_ref[rows, :], wa_ref[...], preferred_element_type=F32)
        mix = mix + jnp.dot(b_ref[rows, :], wb_ref[...], preferred_element_type=F32)
        o_ref[rows, :] = _layer_norm(ALPHA * h_ref[rows, :] + mix, g_ref[...], beta_ref[...])


def _outproj_ln(a, b, w_out, h, g, beta, tm):
    t, d = h.shape
    ka = a.shape[1]
    row = lambda i: (i, 0)
    return pl.pallas_call(
        _outproj_body,
        grid=(t // tm,),
        in_specs=[pl.BlockSpec((tm, ka), row), pl.BlockSpec((tm, ka), row),
                  pl.BlockSpec((ka, d), lambda i: (0, 0)), pl.BlockSpec((ka, d), lambda i: (1, 0)),
                  pl.BlockSpec((tm, d), row), pl.BlockSpec((1, d), lambda i: (0, 0)),
                  pl.BlockSpec((1, d), lambda i: (0, 0))],
        out_specs=pl.BlockSpec((tm, d), row),
        out_shape=jax.ShapeDtypeStruct((t, d), F32),
        compiler_params=_params("parallel"),
        name="outproj_ln",
    )(a, b, w_out, w_out, h, g, beta)


def _second_of_four(a, b, c, d):
    hi_ab, lo_ab = jnp.maximum(a, b), jnp.minimum(a, b)
    hi_cd, lo_cd = jnp.maximum(c, d), jnp.minimum(c, d)
    return jnp.maximum(jnp.maximum(lo_ab, lo_cd), jnp.minimum(hi_ab, hi_cd))


def _router_body(h_ref, wr_ref, bias_ref, e_ref, r_ref):
    tm = h_ref.shape[0]
    h = h_ref[...]
    h_hi = h.astype(BF)
    h_lo = (h - h_hi.astype(F32)).astype(BF)
    both = jnp.dot(h_hi, wr_ref[...], preferred_element_type=F32)
    logits = (both[:, :LANES] + both[:, LANES:]) + jnp.dot(h_lo, wr_ref[:, :LANES], preferred_element_type=F32)
    lt = logits.T
    aff = [jax.nn.sigmoid(lt[SUBLANES * j:SUBLANES * (j + 1), :]) for j in range(EXPERTS_PER_GROUP)]
    sel = [aff[j] + bias_ref[SUBLANES * j:SUBLANES * (j + 1), :] for j in range(EXPERTS_PER_GROUP)]
    top1 = jnp.maximum(jnp.maximum(sel[0], sel[1]), jnp.maximum(sel[2], sel[3]))
    score = top1 + _second_of_four(*sel)
    gid = lax.broadcasted_iota(I32, (N_GROUPS, tm), 0)
    best = jnp.min(jnp.where(score == jnp.max(score, axis=0, keepdims=True), gid, N_GROUPS),
                   axis=0, keepdims=True)
    in_grp = gid == best
    pick = lambda x: jnp.sum(jnp.where(in_grp, x, 0.0), axis=0, keepdims=True)
    s4 = [pick(x) for x in sel]
    a4 = [pick(x) for x in aff]

    def argmax4(vals):
        j, v = jnp.zeros((1, tm), I32), vals[0]
        for n in range(1, EXPERTS_PER_GROUP):
            better = vals[n] > v
            j, v = jnp.where(better, n, j), jnp.where(better, vals[n], v)
        return j

    j0 = argmax4(s4)
    j1 = argmax4([jnp.where(j0 == n, -jnp.inf, s4[n]) for n in range(EXPERTS_PER_GROUP)])
    take = lambda j: sum(jnp.where(j == n, a4[n], 0.0) for n in range(EXPERTS_PER_GROUP))
    w0, w1 = take(j0), take(j1)
    total = w0 + w1
    e0 = best * EXPERTS_PER_GROUP + j0
    e1 = best * EXPERTS_PER_GROUP + j1
    rid = lax.broadcasted_iota(I32, (SUBLANES, tm), 0)
    e_ref[...] = jnp.where(rid == 0, e0, jnp.where(rid == 1, e1, 0))
    rid = lax.broadcasted_iota(I32, (LANES, tm), 0)
    rows = jnp.where(rid == 0, w0 / total, jnp.where(rid == 1, w1 / total, 0.0))
    r_ref[...] = rows.T


def _router(h, wr_perm, bias_perm, tm):
    t, d = h.shape
    return pl.pallas_call(
        _router_body,
        grid=(t // tm,),
        in_specs=[pl.BlockSpec((tm, d), lambda i: (i, 0)), pl.BlockSpec((d, 2 * LANES), lambda i: (0, 0)),
                  pl.BlockSpec((N_EXPERTS, 1), lambda i: (0, 0))],
        out_specs=[pl.BlockSpec((SUBLANES, tm), lambda i: (0, i)), pl.BlockSpec((tm, LANES), lambda i: (i, 0))],
        out_shape=[jax.ShapeDtypeStruct((SUBLANES, t), I32), jax.ShapeDtypeStruct((t, LANES), F32)],
        compiler_params=_params("parallel"),
        name="router",
    )(h, wr_perm, bias_perm)


def _dispatch_plan(e01, tm):
    t = e01.shape[1]
    flat = e01.reshape(-1)
    onehot = (flat[:, None] == jnp.arange(N_EXPERTS, dtype=I32)[None, :]).astype(I32)
    cum = jnp.cumsum(onehot, axis=0)
    rank = jnp.sum(onehot * cum, axis=1) - 1
    counts = cum[-1]
    padded = ((counts + tm - 1) // tm) * tm
    starts = jnp.concatenate([jnp.zeros((1,), I32), jnp.cumsum(padded).astype(I32)])
    pos = jnp.sum(onehot * starts[None, :N_EXPERTS], axis=1) + rank
    return starts, starts[:N_EXPERTS] + counts, pos[:t], pos[t:]


def _dispatch_body(p0_ref, p1_ref, gs_ref, ge_ref, h_ref, r_ref, xs_hbm, xbuf, zbuf, sem, zsem, *, tm, tt,
                   n_steps):
    i = pl.program_id(0)
    slot = i % 2
    n_tail = (xs_hbm.shape[0] - gs_ref[N_EXPERTS]) // tm

    def zero_copy(row0, size):
        return pltpu.make_async_copy(zbuf.at[pl.ds(0, size), :], xs_hbm.at[pl.ds(row0, size), :], zsem.at[0])

    def for_each_fill(fn):
        def group_pad(e, carry):
            end = gs_ref[e + 1]
            pad = end - ge_ref[e]
            covered = 0
            size = tm // 2
            while size >= SUBLANES:
                take = pad & size

                @pl.when(take != 0)
                def _():
                    fn(zero_copy(pl.multiple_of(end - covered - size, size), size))

                covered = covered + take
                size //= 2
            for k in range(SUBLANES - 1):
                @pl.when(k < (pad & (SUBLANES - 1)))
                def _():
                    fn(zero_copy(ge_ref[e] + k, 1))
            return carry

        def tail_tile(k, carry):
            fn(zero_copy(pl.multiple_of(gs_ref[N_EXPERTS] + k * tm, tm), tm))
            return carry

        lax.fori_loop(0, N_EXPERTS, group_pad, 0)
        lax.fori_loop(0, n_tail, tail_tile, 0)

    def wait_rows(sl):
        for _ in range(2):
            pltpu.make_async_copy(xbuf.at[sl, 0], xs_hbm.at[pl.ds(0, tt), :], sem.at[sl]).wait()

    @pl.when(i == 0)
    def _():
        zbuf[...] = jnp.zeros(zbuf.shape, F32)
        for_each_fill(lambda cp: cp.start())

    @pl.when(i >= 2)
    def _():
        wait_rows(slot)

    d = h_ref.shape[1]
    gates = r_ref[...]
    lane = lax.broadcasted_iota(I32, gates.shape, 1)
    for k in range(MOE_TOPK):
        xbuf[slot, k, :, :d] = h_ref[...]
    xbuf[slot, 0, :, d:] = jnp.where(lane == 0, gates, jnp.where(lane == 1, ALPHA, 0.0))
    xbuf[slot, 1, :, d:] = jnp.where(lane == 0, pltpu.roll(gates, LANES - 1, axis=1), 0.0)
    base = i * tt
    for r in range(tt):
        pltpu.make_async_copy(xbuf.at[slot, 0, pl.ds(r, 1), :], xs_hbm.at[pl.ds(p0_ref[base + r], 1), :],
                              sem.at[slot]).start(priority=0)
        pltpu.make_async_copy(xbuf.at[slot, 1, pl.ds(r, 1), :], xs_hbm.at[pl.ds(p1_ref[base + r], 1), :],
                              sem.at[slot]).start(priority=1)

    @pl.when(i == n_steps - 1)
    def _():
        wait_rows(slot)
        if n_steps > 1:
            wait_rows(1 - slot)
        for_each_fill(lambda cp: cp.wait())


def _dispatch(h, gates, starts, ends, pos0, pos1, tm, tt):
    t, d = h.shape
    dx = d + LANES
    n_steps = t // tt
    n_rows = 2 * t + N_EXPERTS * tm
    body = functools.partial(_dispatch_body, tm=tm, tt=tt, n_steps=n_steps)
    grid_spec = pltpu.PrefetchScalarGridSpec(
        num_scalar_prefetch=4,
        grid=(n_steps,),
        in_specs=[pl.BlockSpec((tt, d), lambda i, *_: (i, 0)), pl.BlockSpec((tt, LANES), lambda i, *_: (i, 0))],
        out_specs=pl.BlockSpec(memory_space=pl.ANY),
        scratch_shapes=[pltpu.VMEM((2, MOE_TOPK, tt, dx), F32), pltpu.VMEM((tm, dx), F32),
                        pltpu.SemaphoreType.DMA((2,)), pltpu.SemaphoreType.DMA((1,))],
    )
    return pl.pallas_call(
        body,
        grid_spec=grid_spec,
        out_shape=jax.ShapeDtypeStruct((n_rows, dx), F32),
        compiler_params=_params("arbitrary"),
        name="dispatch",
    )(pos0, pos1, starts, ends, h, gates)


def _experts_body(gs_ref, xs_hbm, wgu_ref, wd_ref, ys_hbm, xbuf, ybuf, xsem, osem, *, tm):
    e = pl.program_id(0)
    first = gs_ref[e] // tm
    n_tiles = gs_ref[e + 1] // tm - first
    total = gs_ref[N_EXPERTS] // tm
    ring = xbuf.shape[0]

    def in_copy(g, slot):
        row0 = pl.multiple_of(g * tm, tm)
        return pltpu.make_async_copy(xs_hbm.at[pl.ds(row0, tm), :], xbuf.at[slot], xsem.at[slot])

    def out_copy(g, slot):
        row0 = pl.multiple_of(g * tm, tm)
        return pltpu.make_async_copy(ybuf.at[slot], ys_hbm.at[pl.ds(row0, tm), :], osem.at[slot])

    @pl.when(e == 0)
    def _():
        for g in range(ring - 1):
            in_copy(jnp.minimum(g, total - 1), g).start(priority=1)
        ybuf[...] = jnp.zeros(ybuf.shape, F32)
        out_copy(0, 0).start()
        out_copy(1, 1).start()

    @pl.when(n_tiles > 0)
    def _():
        def tile(k, carry):
            g = first + k
            slot = g % 2
            xslot = g % ring
            in_copy(g, xslot).wait()
            out_copy(g, slot).wait()
            in_copy(jnp.minimum(g + ring - 1, total - 1), (g + ring - 1) % ring).start(priority=1)
            d = ybuf.shape[2]
            x = xbuf[xslot, :, :d]
            row_gate = xbuf[xslot, :, d:d + 1]
            row_residual = xbuf[xslot, :, d + 1:d + 2]
            gu = jnp.dot(x.astype(BF), wgu_ref[0, 0].astype(BF), preferred_element_type=F32)
            gate, up = gu[:, :EXPERT_FF], gu[:, EXPERT_FF:]
            hidden = (gate * jax.nn.sigmoid(gate) * up).astype(BF)
            y = jnp.dot(hidden, wd_ref[0, 0].astype(BF), preferred_element_type=F32)
            ybuf[slot] = row_gate * y + row_residual * x
            out_copy(g, slot).start(priority=1)
            return carry

        lax.fori_loop(0, n_tiles, tile, 0)

    @pl.when(e == N_EXPERTS - 1)
    def _():
        for n in range(ring - 1):
            in_copy(0, (total + n) % ring).wait()
        out_copy(0, 0).wait()
        out_copy(1, 1).wait()
        end = gs_ref[N_EXPERTS]
        n_tail = (ys_hbm.shape[0] - end) // tm
        ybuf[0] = jnp.zeros(ybuf.shape[1:], F32)

        def tail_copy(k):
            row0 = pl.multiple_of(end + k * tm, tm)
            return pltpu.make_async_copy(ybuf.at[0], ys_hbm.at[pl.ds(row0, tm), :], osem.at[0])

        def start_one(k, carry):
            tail_copy(k).start()
            return carry

        def wait_one(k, carry):
            tail_copy(k).wait()
            return carry

        lax.fori_loop(0, n_tail, start_one, 0)
        lax.fori_loop(0, n_tail, wait_one, 0)


EXPERT_IN_RING = 12


def _experts(xs, starts, w_gate_up, w_down, layer, tm):
    n_rows, dx = xs.shape
    d = dx - LANES
    body = functools.partial(_experts_body, tm=tm)
    grid_spec = pltpu.PrefetchScalarGridSpec(
        num_scalar_prefetch=1,
        grid=(N_EXPERTS,),
        in_specs=[pl.BlockSpec(memory_space=pl.ANY),
                  pl.BlockSpec((1, 1, d, 2 * EXPERT_FF), lambda e, gs: (layer, e, 0, 0)),
                  pl.BlockSpec((1, 1, EXPERT_FF, d), lambda e, gs: (layer, e, 0, 0))],
        out_specs=pl.BlockSpec(memory_space=pl.ANY),
        scratch_shapes=[pltpu.VMEM((EXPERT_IN_RING, tm, dx), F32), pltpu.VMEM((2, tm, d), F32),
                        pltpu.SemaphoreType.DMA((EXPERT_IN_RING,)), pltpu.SemaphoreType.DMA((2,))],
    )
    return pl.pallas_call(
        body,
        grid_spec=grid_spec,
        out_shape=jax.ShapeDtypeStruct((n_rows, d), F32),
        compiler_params=_params("arbitrary"),
        name="experts",
    )(starts, xs, w_gate_up, w_down)


def _combine_body(p0_ref, p1_ref, ys_hbm, g_ref, beta_ref, o_ref, buf0, buf1, sem, *, tc, n_steps):
    i = pl.program_id(0)
    ring = buf0.shape[0]

    def start_row(base, r, sl):
        pltpu.make_async_copy(ys_hbm.at[pl.ds(p0_ref[base + r], 1), :], buf0.at[sl, pl.ds(r, 1), :],
                              sem.at[sl]).start(priority=0)
        pltpu.make_async_copy(ys_hbm.at[pl.ds(p1_ref[base + r], 1), :], buf1.at[sl, pl.ds(r, 1), :],
                              sem.at[sl]).start(priority=1)

    def wait_rows(sl):
        pltpu.make_async_copy(ys_hbm.at[pl.ds(0, tc), :], buf0.at[sl], sem.at[sl]).wait()
        pltpu.make_async_copy(ys_hbm.at[pl.ds(0, tc), :], buf1.at[sl], sem.at[sl]).wait()

    @pl.when(i == 0)
    def _():
        for sl in range(ring):
            def one(r, carry):
                start_row(sl * tc, r, sl)
                return carry

            lax.fori_loop(0, tc, one, 0)

    nxt = jnp.minimum(i + 1, n_steps - 1) * ring * tc
    for sl in range(ring):
        wait_rows(sl)
        o_ref[sl * tc:(sl + 1) * tc, :] = _layer_norm(buf0[sl] + buf1[sl], g_ref[...], beta_ref[...])
        for r in range(tc):
            start_row(nxt + sl * tc, r, sl)

    @pl.when(i == n_steps - 1)
    def _():
        for sl in range(ring):
            wait_rows(sl)


COMBINE_RING = 4


def _combine_ln(ys, pos0, pos1, g, beta, tc):
    t, d = pos0.shape[0], ys.shape[1]
    n_steps = t // (COMBINE_RING * tc)
    body = functools.partial(_combine_body, tc=tc, n_steps=n_steps)
    row = lambda i, p0, p1: (i, 0)
    const = lambda i, p0, p1: (0, 0)
    grid_spec = pltpu.PrefetchScalarGridSpec(
        num_scalar_prefetch=2,
        grid=(n_steps,),
        in_specs=[pl.BlockSpec(memory_space=pl.ANY), pl.BlockSpec((1, d), const), pl.BlockSpec((1, d), const)],
        out_specs=pl.BlockSpec((COMBINE_RING * tc, d), row),
        scratch_shapes=[pltpu.VMEM((COMBINE_RING, tc, d), F32), pltpu.VMEM((COMBINE_RING, tc, d), F32),
                        pltpu.SemaphoreType.DMA((COMBINE_RING,))],
    )
    return pl.pallas_call(
        body,
        grid_spec=grid_spec,
        out_shape=jax.ShapeDtypeStruct((t, d), F32),
        compiler_params=_params("arbitrary"),
        name="combine_ln",
    )(pos0, pos1, ys, g, beta)


def _moe_ln(h, wr_perm, bias_perm, w_gate_up, w_down, layer, g, beta, tm_router=512, tm_expert=128, tc=256):
    e01, gates = _router(h, wr_perm, bias_perm, tm_router)
    starts, ends, pos0, pos1 = _dispatch_plan(e01[:MOE_TOPK], tm_expert)
    xs = _dispatch(h, gates, starts, ends, pos0, pos1, tm_expert, tc)
    ys = _experts(xs, starts, w_gate_up, w_down, layer, tm_expert)
    return _combine_ln(ys, pos0, pos1, g, beta, tc)


def _swap_halves(w):
    half = w.shape[-1] // 2
    return jnp.concatenate([w[..., half:], w[..., :half]], axis=-1)


def _even_weights(w_in, w_q_up, w_kv_up, forget_bias):
    o_kv = MLA_Q_LORA
    o_kr = o_kv + MLA_KV_LORA
    o_fq = o_kr + MLA_ROPE
    hd = FOX_HEADS * FOX_HEAD_DIM
    o_fl = o_fq + 3 * hd
    k_r = w_in[:, o_kr:o_fq]
    f_l = jnp.pad(w_in[:, o_fl:], ((0, 0), (0, LANES - FOX_HEADS)))
    w_small = jnp.concatenate([w_in[:, :o_kr], k_r, _swap_halves(k_r), f_l], axis=1).astype(BF)
    w_fox = w_in[:, o_fq:o_fl].astype(BF)
    cs_fox = jnp.concatenate([jnp.full((1, hd), FOX_SCALE, F32), jnp.ones((1, 2 * hd), F32)], axis=1)
    wq = w_q_up.reshape(MLA_Q_LORA, MLA_HEADS, MLA_NOPE + MLA_ROPE)
    wqn = wq[:, :, :MLA_NOPE].reshape(MLA_Q_LORA, -1).astype(BF)
    pe = wq[:, :, MLA_NOPE:]
    wqp = jnp.concatenate([pe, _swap_halves(pe)], axis=-1).reshape(MLA_Q_LORA, -1).astype(BF)
    wkv = w_kv_up.reshape(MLA_KV_LORA, MLA_HEADS, MLA_NOPE + MLA_V)
    wk = wkv[:, :, :MLA_NOPE].reshape(MLA_KV_LORA, -1).astype(BF)
    wv = wkv[:, :, MLA_NOPE:].reshape(MLA_KV_LORA, -1).astype(BF)
    fb = jnp.pad(forget_bias.astype(F32), (0, LANES - FOX_HEADS)).reshape(1, LANES)
    return w_small, w_fox, cs_fox, wqn, wqp, wk, wv, fb


def _rope_table(seq):
    half = MLA_ROPE // 2
    inv_freq = ROPE_THETA ** (-np.arange(half, dtype=np.float64) / half)
    ang = np.arange(seq, dtype=np.float64)[:, None] * inv_freq[None, :]
    cos, sin = np.cos(ang), np.sin(ang)
    return jnp.asarray(np.concatenate([cos, cos, -sin, sin], axis=1), dtype=F32)


def _router_weights(w_router, router_bias):
    r = np.arange(N_EXPERTS)
    perm = (r % N_GROUPS) * EXPERTS_PER_GROUP + r // N_GROUPS
    wr = jnp.pad(w_router[:, perm], ((0, 0), (0, LANES - N_EXPERTS)))
    w_hi = wr.astype(BF)
    w_lo = (wr - w_hi.astype(F32)).astype(BF)
    return jnp.concatenate([w_hi, w_lo], axis=1), router_bias.astype(F32)[perm].reshape(N_EXPERTS, 1)


def _even_layer(h, batch, seq, w_in, q_norm, w_q_up, kv_norm, w_kv_up, forget_bias, w_out, g, beta):
    w_small, w_fox, cs_fox, wqn, wqp, wk, wv, fb = _even_weights(w_in, w_q_up, w_kv_up, forget_bias)
    fox_qkv = _mm(h, w_fox, cs_fox, BF, 2 * ROW_TILE, w_fox.shape[1] // 2)
    qn, qp, kn, v, kp, forget_logits = _mla_prep(h, w_small, _rope_table(seq), q_norm.reshape(1, -1),
                                                 kv_norm.reshape(1, -1), wqn, wqp, wk, wv, seq, ROW_TILE)
    q_bias, k_bias, edge = _fox_cum(forget_logits, fb, batch, seq, ROW_TILE)
    a = _mla_attention(qn, qp, kn, v, kp, batch, seq, ROW_TILE, HEAD_GROUP)
    bo = _fox_attention(fox_qkv, q_bias, k_bias, edge, batch, seq, ROW_TILE, HEAD_GROUP)
    return _outproj_ln(a, bo, w_out.astype(BF), h, g, beta, ROW_TILE)


def _odd_layer(h, batch, seq, w_in, sinks, w_out, g, beta):
    n_sq = SWA_Q_HEADS * SWA_HEAD_DIM
    n_skv = SWA_KV_HEADS * SWA_HEAD_DIM
    n_m = MOBA_HEADS * MOBA_HEAD_DIM
    o_mq = n_sq + 2 * n_skv
    w_big = jnp.concatenate([w_in[:, :n_sq], w_in[:, o_mq:], w_in[:, n_sq:o_mq]], axis=1).astype(BF)
    cs = jnp.concatenate([jnp.full((1, n_sq), SWA_SCALE, F32), jnp.full((1, n_m), MOBA_SCALE, F32),
                          jnp.ones((1, 2 * n_m + 2 * n_skv), F32)], axis=1)
    big = _mm(h, w_big, cs, BF, 2 * ROW_TILE, w_big.shape[1] // 2)
    blocks = lambda cols: cols // LANES
    k_mean = _moba_kmean(big, batch, seq, (n_sq + n_m) // n_m)
    slopes = jnp.asarray(LOG2E * 2.0 ** (-8.0 * np.arange(1, MOBA_HEADS + 1) / MOBA_HEADS), dtype=F32)
    c = _swa_attention(big, sinks.astype(F32), batch, seq, 0, blocks(n_sq + 3 * n_m), blocks(n_sq + 3 * n_m) + 1)
    dd = _moba_attention(big, k_mean, slopes, batch, seq, blocks(n_sq), blocks(n_sq + n_m), blocks(n_sq + 2 * n_m),
                         ROW_TILE, HEAD_GROUP)
    return _outproj_ln(c, dd, w_out.astype(BF), h, g, beta, ROW_TILE)


def kernel(x, w_router, router_bias, even_w_in, even_q_norm, even_w_q_up, even_kv_norm, even_w_kv_up,
           even_forget_bias, even_w_out, odd_w_in, odd_sinks, odd_w_out, ln_mix_g, ln_mix_b, ln_ffn_g, ln_ffn_b,
           w_gate_up, w_down):
    batch, seq, d = x.shape
    h = x.reshape(batch * seq, d)
    wr_perm, bias_perm = _router_weights(w_router, router_bias)
    row = lambda p, layer: p[layer].reshape(1, d)
    for layer in range(DEPTH):
        i = layer // 2
        if layer % 2 == 0:
            h = _even_layer(h, batch, seq, even_w_in[i], even_q_norm[i], even_w_q_up[i], even_kv_norm[i],
                            even_w_kv_up[i], even_forget_bias[i], even_w_out[i], row(ln_mix_g, layer),
                            row(ln_mix_b, layer))
        else:
            h = _odd_layer(h, batch, seq, odd_w_in[i], odd_sinks[i], odd_w_out[i], row(ln_mix_g, layer),
                           row(ln_mix_b, layer))
        h = _moe_ln(h, wr_perm, bias_perm, w_gate_up, w_down, layer, row(ln_ffn_g, layer), row(ln_ffn_b, layer))
    return h.reshape(batch, seq, d)
```

```python
import functools

import numpy as np
import jax
import jax.numpy as jnp
from jax import lax
from jax.experimental import pallas as pl
from jax.experimental.pallas import tpu as pltpu

BF = jnp.bfloat16
F32 = jnp.float32
I32 = jnp.int32

LANES = 128
SUBLANES = 8
VMEM_LIMIT = 56 * 1024 * 1024

ROW_TILE = 512
HEAD_GROUP = 4

DEPTH = 2
LN_EPS = 1e-5
RMS_EPS = 1e-6
ALPHA = (2 * DEPTH) ** 0.25
MLA_HEADS, MLA_Q_LORA, MLA_KV_LORA, MLA_NOPE, MLA_ROPE, MLA_V = 8, 512, 256, 128, 64, 128
ROPE_THETA = 10000.0
FOX_HEADS, FOX_HEAD_DIM = 8, 128
SWA_Q_HEADS, SWA_KV_HEADS, SWA_HEAD_DIM, SWA_WINDOW = 16, 2, 64, 128
MOBA_HEADS, MOBA_HEAD_DIM, MOBA_BLOCK, MOBA_TOPK = 8, 128, 256, 3
N_EXPERTS, N_GROUPS, MOE_TOPK, EXPERT_FF = 32, 8, 2, 512
EXPERTS_PER_GROUP = N_EXPERTS // N_GROUPS

LOG2E = 1.4426950408889634
MLA_SCALE = (MLA_NOPE + MLA_ROPE) ** -0.5 * LOG2E
FOX_SCALE = FOX_HEAD_DIM ** -0.5 * LOG2E
MOBA_SCALE = MOBA_HEAD_DIM ** -0.5 * LOG2E
SWA_SCALE = SWA_HEAD_DIM ** -0.5 * LOG2E
NEG = -1e30

NT_DIMS = (((1,), (1,)), ((), ()))


def _params(*sem):
    return pltpu.CompilerParams(dimension_semantics=sem, vmem_limit_bytes=VMEM_LIMIT)


def _mm_body(x_ref, w_ref, cs_ref, o_ref):
    acc = jnp.dot(x_ref[...].astype(BF), w_ref[...], preferred_element_type=F32)
    o_ref[...] = (acc * cs_ref[...]).astype(o_ref.dtype)


def _mm(x, w, col_scale, out_dtype, tm, tn):
    m, k = x.shape
    n = w.shape[1]
    return pl.pallas_call(
        _mm_body,
        grid=(n // tn, m // tm),
        in_specs=[
            pl.BlockSpec((tm, k), lambda j, i: (i, 0)),
            pl.BlockSpec((k, tn), lambda j, i: (0, j)),
            pl.BlockSpec((1, tn), lambda j, i: (0, j)),
        ],
        out_specs=pl.BlockSpec((tm, tn), lambda j, i: (i, j)),
        out_shape=jax.ShapeDtypeStruct((m, n), out_dtype),
        compiler_params=_params("parallel", "parallel"),
        name="proj",
    )(x, w, col_scale)


def _rms(x, g):
    return x * lax.rsqrt(jnp.mean(x * x, axis=-1, keepdims=True) + RMS_EPS) * g


def _rope_pair(slab, table):
    r = slab * table
    return r + pltpu.roll(r, MLA_ROPE, axis=1)


def _mla_prep_body(x_ref, ws_ref, rope_ref, qg_ref, kvg_ref, wqn_ref, wqp_ref, wk_ref, wv_ref,
                   qn_ref, qp_ref, kn_ref, v_ref, kp_ref, fl_ref):
    small = jnp.dot(x_ref[...].astype(BF), ws_ref[...], preferred_element_type=F32)
    o_kv, o_kr, o_fl = MLA_Q_LORA, MLA_Q_LORA + MLA_KV_LORA, MLA_Q_LORA + MLA_KV_LORA + LANES
    fl_ref[...] = small[:, o_fl:]
    table = rope_ref[...]
    cqn = _rms(small[:, :o_kv], qg_ref[...]).astype(BF)
    qn_ref[...] = (jnp.dot(cqn, wqn_ref[...], preferred_element_type=F32) * MLA_SCALE).astype(BF)
    qp = jnp.dot(cqn, wqp_ref[...], preferred_element_type=F32)
    for h in range(MLA_HEADS):
        sl = slice(h * LANES, (h + 1) * LANES)
        qp_ref[:, sl] = (_rope_pair(qp[:, sl], table) * MLA_SCALE).astype(BF)
    ckvn = _rms(small[:, o_kv:o_kr], kvg_ref[...]).astype(BF)
    kn_ref[...] = jnp.dot(ckvn, wk_ref[...], preferred_element_type=F32).astype(BF)
    v_ref[...] = jnp.dot(ckvn, wv_ref[...], preferred_element_type=F32).astype(BF)
    kr = _rope_pair(small[:, o_kr:o_fl], table)
    lane = lax.broadcasted_iota(I32, kr.shape, 1)
    kp_ref[...] = jnp.where(lane < MLA_ROPE, kr, 0.0).astype(BF)


def _mla_prep(x, w_small, rope_table, q_norm, kv_norm, wqn, wqp, wk, wv, seq, tm):
    t, d = x.shape
    n_s = seq // tm
    hd = MLA_HEADS * LANES
    row = lambda i: (i, 0)
    const = lambda i: (0, 0)
    out = lambda w: pl.BlockSpec((tm, w), row)
    return pl.pallas_call(
        _mla_prep_body,
        grid=(t // tm,),
        in_specs=[
            pl.BlockSpec((tm, d), row),
            pl.BlockSpec(w_small.shape, const),
            pl.BlockSpec((tm, LANES), lambda i: (i % n_s, 0)),
            pl.BlockSpec((1, MLA_Q_LORA), const),
            pl.BlockSpec((1, MLA_KV_LORA), const),
            pl.BlockSpec((MLA_Q_LORA, hd), const),
            pl.BlockSpec((MLA_Q_LORA, hd), const),
            pl.BlockSpec((MLA_KV_LORA, hd), const),
            pl.BlockSpec((MLA_KV_LORA, hd), const),
        ],
        out_specs=[out(hd), out(hd), out(hd), out(hd), out(LANES), out(LANES)],
        out_shape=[jax.ShapeDtypeStruct((t, hd), BF)] * 4 + [jax.ShapeDtypeStruct((t, LANES), BF),
                                                             jax.ShapeDtypeStruct((t, LANES), F32)],
        compiler_params=_params("parallel"),
        name="mla_prep",
    )(x, w_small, rope_table, q_norm, kv_norm, wqn, wqp, wk, wv)


N_SPLIT = 3


def _split_piece(x, which):
    hi = x.astype(BF).astype(F32)
    rest = x - hi
    mid = rest.astype(BF).astype(F32)
    return jnp.where(which == 0, hi, jnp.where(which == 1, mid, rest - mid))


def _fox_cum_body(fl_ref, fb_ref, spread_ref, qe_ref, ke_ref, edge_ref, carry_ref):
    @pl.when(pl.program_id(1) == 0)
    def _():
        carry_ref[...] = jnp.zeros_like(carry_ref)

    z = fl_ref[...] + fb_ref[...]
    log_f = jnp.minimum(z, 0.0) - jnp.log1p(jnp.exp(-jnp.abs(z)))
    tm = z.shape[0]
    r = lax.broadcasted_iota(I32, (tm, tm), 0)
    c = lax.broadcasted_iota(I32, (tm, tm), 1)
    tri = jnp.where(r >= c, 1.0, 0.0).astype(BF)
    parts = jnp.dot(tri, jnp.concatenate([_split_piece(log_f, n).astype(BF) for n in range(N_SPLIT)], axis=1),
                    preferred_element_type=F32)
    cum = (parts[:, :LANES] + parts[:, LANES:2 * LANES]) + parts[:, 2 * LANES:] + carry_ref[...]
    carry_ref[...] = cum[tm - 1:tm, :]
    cum = cum * LOG2E
    width = qe_ref.shape[1]
    pieces = jnp.concatenate([_split_piece(cum, n).astype(BF) for n in range(N_SPLIT)], axis=1)
    placed = jnp.dot(pieces, spread_ref[...], preferred_element_type=F32)
    li = lax.broadcasted_iota(I32, (tm, width), 1) % LANES
    qe_ref[...] = jnp.where((li >= N_SPLIT) & (li < 2 * N_SPLIT), 1.0, placed[:, :width]).astype(BF)
    ke_ref[...] = jnp.where(li < N_SPLIT, 1.0, placed[:, width:]).astype(BF)
    edge_ref[0, 0:1, :] = cum[0:1, :]
    edge_ref[0, 1:2, :] = cum[tm - 1:tm, :]


def _fox_spread():
    width = FOX_HEADS * LANES
    m = np.zeros((N_SPLIT * LANES, 2 * width), np.float32)
    for j in range(N_SPLIT):
        for h in range(FOX_HEADS):
            m[j * LANES + h, h * LANES + j] = 1.0
            m[j * LANES + h, width + h * LANES + N_SPLIT + j] = -1.0
    return jnp.asarray(m, dtype=BF)


def _fox_cum(forget_logits, forget_bias_row, batch, seq, tm):
    t = forget_logits.shape[0]
    n_s = seq // tm
    width = FOX_HEADS * LANES
    q_bias, k_bias, edge = pl.pallas_call(
        _fox_cum_body,
        grid=(batch, n_s),
        in_specs=[
            pl.BlockSpec((tm, LANES), lambda b, i: (b * n_s + i, 0)),
            pl.BlockSpec((1, LANES), lambda b, i: (0, 0)),
            pl.BlockSpec((N_SPLIT * LANES, 2 * width), lambda b, i: (0, 0)),
        ],
        out_specs=[
            pl.BlockSpec((tm, width), lambda b, i: (b * n_s + i, 0)),
            pl.BlockSpec((tm, width), lambda b, i: (b * n_s + i, 0)),
            pl.BlockSpec((1, 2, LANES), lambda b, i: (b * n_s + i, 0, 0)),
        ],
        out_shape=[jax.ShapeDtypeStruct((t, width), BF), jax.ShapeDtypeStruct((t, width), BF),
                   jax.ShapeDtypeStruct((batch * n_s, 2, LANES), F32)],
        scratch_shapes=[pltpu.VMEM((1, LANES), F32)],
        compiler_params=_params("parallel", "arbitrary"),
        name="fox_cum",
    )(forget_logits, forget_bias_row, _fox_spread())
    return q_bias, k_bias, edge[:, :, :FOX_HEADS].reshape(-1)


def _softmax_update(s_all, v_all, m_ref, l_ref, acc_ref, rows=slice(None)):
    heads = range(len(s_all))
    m_prev = [m_ref[g, rows] for g in heads]
    m_new = [jnp.maximum(m_prev[g], jnp.max(s_all[g], axis=1, keepdims=True)) for g in heads]
    reps = s_all[0].shape[1] // LANES
    p = [jnp.exp2(s_all[g] - jnp.concatenate([m_new[g]] * reps, axis=1)) for g in heads]
    pv = [jnp.dot(p[g].astype(BF), v_all[g], preferred_element_type=F32) for g in heads]
    for g in heads:
        alpha = jnp.exp2(m_prev[g] - m_new[g])
        l_ref[g, rows] = alpha * l_ref[g, rows] + jnp.sum(p[g], axis=1, keepdims=True)
        acc_ref[g, rows] = alpha * acc_ref[g, rows] + pv[g]
        m_ref[g, rows] = m_new[g]


def _attn_body(*refs, mode, t, group):
    hg = pl.program_id(1)
    i = pl.program_id(2)
    heads = range(group)
    sl = lambda g: slice(g * LANES, (g + 1) * LANES)
    if mode == "moba":
        q_ref, k_ref, kx_ref, v_ref, km_ref, slope_ref, o_ref, m_ref, l_ref, acc_ref, qx_ref = refs
        for g in heads:
            qx_ref[g] = _moba_query_lanes(q_ref[:, sl(g)], km_ref[:, sl(g)], slope_ref[hg * group + g], i, t)
        q_extra = lambda g, rows: qx_ref[g, rows]
        k_extra = lambda g, ks, nk: kx_ref[pl.ds(ks, nk), :]
    elif mode == "mla":
        q_ref, qx_ref, k_ref, kx_ref, v_ref, o_ref, m_ref, l_ref, acc_ref = refs
        q_extra = lambda g, rows: qx_ref[rows, sl(g)]
        k_extra = lambda g, ks, nk: kx_ref[pl.ds(ks, nk), :]
    else:
        q_ref, qx_ref, k_ref, kx_ref, v_ref, edge_ref, o_ref, m_ref, l_ref, acc_ref, knorm_ref = refs
        q_extra = lambda g, rows: qx_ref[rows, sl(g)]
        k_extra = lambda g, ks, nk: kx_ref[pl.ds(ks, nk), sl(g)]
        first_tile = _fox_first_tile(q_ref, k_ref, edge_ref, knorm_ref, i, hg, t, group)

    def scores(g, ks, nk=t, rows=slice(None)):
        q = jnp.concatenate([q_ref[rows, sl(g)], q_extra(g, rows)], axis=1)
        k = jnp.concatenate([k_ref[pl.ds(ks, nk), sl(g)], k_extra(g, ks, nk)], axis=1)
        return lax.dot_general(q, k, NT_DIMS, preferred_element_type=F32)

    m_ref[...] = jnp.full(m_ref.shape, NEG, F32)
    l_ref[...] = jnp.zeros(l_ref.shape, F32)
    acc_ref[...] = jnp.zeros(acc_ref.shape, F32)

    values = lambda ks, nk=t: [v_ref[pl.ds(ks, nk), sl(g)] for g in heads]
    ks = pl.multiple_of(i * t, t)
    half = t // 2
    for rows, nk in ((slice(0, half), half), (slice(half, t), t)):
        r = lax.broadcasted_iota(I32, (half, nk), 0) + rows.start
        c = lax.broadcasted_iota(I32, (half, nk), 1)
        _softmax_update([jnp.where(c <= r, scores(g, ks, nk, rows), NEG) for g in heads], values(ks, nk),
                        m_ref, l_ref, acc_ref, rows)

    def past_tile(j, carry):
        ks = pl.multiple_of(j * t, t)
        _softmax_update([scores(g, ks) for g in heads], values(ks), m_ref, l_ref, acc_ref)
        return carry

    lax.fori_loop(first_tile if mode == "fox" else 0, i, past_tile, 0)
    for g in heads:
        o_ref[:, sl(g)] = (acc_ref[g] / l_ref[g]).astype(o_ref.dtype)


FOX_SKIP_GAP = 160.0


def _fox_first_tile(q_ref, k_ref, edge_ref, knorm_ref, i, hg, t, group):
    b = pl.program_id(0)
    n_q = pl.num_programs(2)
    sl = lambda g: slice(g * LANES, (g + 1) * LANES)

    def max_row_norm(x):
        x = x.astype(F32)
        return jnp.sqrt(jnp.max(jnp.sum(x * x, axis=1, keepdims=True)))

    @pl.when(i == 0)
    def _():
        for g in range(group):
            knorm_ref[g] = max_row_norm(k_ref[:, sl(g)])

    def edge(tile, last, g):
        return edge_ref[((b * n_q + tile) * 2 + last) * FOX_HEADS + hg * group + g]

    limit = [-(2.0 * 1.001 * max_row_norm(q_ref[:, sl(g)]) * knorm_ref[g] + FOX_SKIP_GAP) for g in range(group)]
    first = [edge(i, 0, g) for g in range(group)]

    def scan(j, lo):
        needed = first[0] - edge(j, 1, 0) >= limit[0]
        for g in range(1, group):
            needed = needed | (first[g] - edge(j, 1, g) >= limit[g])
        return jnp.where(needed, jnp.minimum(lo, j), lo)

    return lax.fori_loop(0, i, scan, i)


def _attention(mode, batch, seq, heads, t, group, operands, in_specs, extra_scratch=()):
    n_q = seq // t
    body = functools.partial(_attn_body, mode=mode, t=t, group=group)
    return pl.pallas_call(
        body,
        grid=(batch, heads // group, n_q),
        in_specs=in_specs,
        out_specs=pl.BlockSpec((t, group * LANES), lambda b, h, i: (b * n_q + i, h)),
        out_shape=jax.ShapeDtypeStruct((batch * seq, heads * LANES), BF),
        scratch_shapes=[pltpu.VMEM((group, t, LANES), F32), pltpu.VMEM((group, t, LANES), F32),
                        pltpu.VMEM((group, t, LANES), F32), *extra_scratch],
        compiler_params=_params("parallel", "parallel", "arbitrary"),
        name="attn_" + mode,
    )(*operands)


def _q_spec(t, n_q, group, col0=0):
    return pl.BlockSpec((t, group * LANES), lambda b, h, i: (b * n_q + i, col0 + h))


def _kv_spec(seq, group, col0=0):
    return pl.BlockSpec((seq, group * LANES), lambda b, h, i: (b, col0 + h))


def _mla_attention(qn, qp, kn, v, kp, batch, seq, t, group):
    n_q = seq // t
    specs = [_q_spec(t, n_q, group), _q_spec(t, n_q, group), _kv_spec(seq, group),
             pl.BlockSpec((seq, LANES), lambda b, h, i: (b, 0)), _kv_spec(seq, group)]
    return _attention("mla", batch, seq, MLA_HEADS, t, group, (qn, qp, kn, kp, v), specs)


def _fox_attention(qkv, q_bias, k_bias, edge, batch, seq, t, group):
    n_q = seq // t
    n_hg = FOX_HEADS // group
    specs = [_q_spec(t, n_q, group, 0), _q_spec(t, n_q, group, 0), _kv_spec(seq, group, n_hg),
             _kv_spec(seq, group, 0), _kv_spec(seq, group, 2 * n_hg), pl.BlockSpec(memory_space=pltpu.SMEM)]
    return _attention("fox", batch, seq, FOX_HEADS, t, group, (qkv, q_bias, qkv, k_bias, qkv, edge), specs,
                      (pltpu.SMEM((group,), F32),))


def _moba_key_lanes(seq):
    n_kb = seq // MOBA_BLOCK
    kpos = np.arange(seq)
    lanes = np.zeros((seq, LANES), np.float32)
    lanes[kpos, kpos // MOBA_BLOCK] = 1.0
    lanes[:, n_kb:n_kb + N_SPLIT] = 1.0
    lanes[:, n_kb + N_SPLIT:n_kb + 2 * N_SPLIT] = (MOBA_BLOCK * (kpos // MOBA_BLOCK))[:, None]
    lanes[:, n_kb + 2 * N_SPLIT:n_kb + 3 * N_SPLIT] = (kpos % MOBA_BLOCK)[:, None]
    return jnp.asarray(lanes, dtype=BF)


def _moba_query_lanes(q, k_mean, slope, i, t):
    n_kb = k_mean.shape[0]
    gate = lax.dot_general(k_mean.astype(BF), q, NT_DIMS, preferred_element_type=F32)
    blk = lax.broadcasted_iota(I32, (n_kb, t), 0)
    qpos_i = i * t + lax.broadcasted_iota(I32, (1, t), 1)
    own = qpos_i // MOBA_BLOCK
    beaten = jnp.zeros((n_kb, t), F32)
    for n in range(n_kb):
        g_n = gate[n:n + 1, :]
        wins = (g_n > gate) | ((g_n == gate) & (blk > n))
        beaten = beaten + jnp.where(wins & (own > n), 1.0, 0.0)
    attended = (blk == own) | ((blk < own) & (beaten < MOBA_TOPK))
    mask = jnp.concatenate([jnp.where(attended, 0.0, NEG), jnp.zeros((LANES - n_kb, t), F32)], axis=0)
    row = lax.broadcasted_iota(I32, (LANES, t), 0) - n_kb
    qpos = qpos_i.astype(F32)
    slope_row = jnp.full((1, t), slope, F32)
    which = row % N_SPLIT
    lanes = jnp.where(row < 0, mask,
                      jnp.where(row < N_SPLIT, _split_piece(-slope_row * qpos, which),
                                jnp.where(row < 3 * N_SPLIT, _split_piece(slope_row, which), 0.0)))
    return lanes.T.astype(BF)


def _moba_attention(big, k_mean, slopes, batch, seq, col_q, col_k, col_v, t, group):
    n_q = seq // t
    specs = [_q_spec(t, n_q, group, col_q // group), _kv_spec(seq, group, col_k // group),
             pl.BlockSpec((seq, LANES), lambda b, h, i: (0, 0)),
             _kv_spec(seq, group, col_v // group),
             pl.BlockSpec((seq // MOBA_BLOCK, group * LANES), lambda b, h, i: (b, h)),
             pl.BlockSpec(memory_space=pltpu.SMEM)]
    scratch = (pltpu.VMEM((group, t, LANES), BF),)
    return _attention("moba", batch, seq, MOBA_HEADS, t, group,
                      (big, big, _moba_key_lanes(seq), big, k_mean, slopes), specs, scratch)


def _kmean_body(k_ref, o_ref):
    n_kb = o_ref.shape[0]
    for n in range(n_kb):
        blk = k_ref[n * MOBA_BLOCK:(n + 1) * MOBA_BLOCK, :].astype(F32)
        o_ref[n:n + 1, :] = jnp.mean(blk, axis=0, keepdims=True)


def _moba_kmean(big, batch, seq, col_block):
    n_kb = seq // MOBA_BLOCK
    width = MOBA_HEADS * MOBA_HEAD_DIM
    return pl.pallas_call(
        _kmean_body,
        grid=(batch,),
        in_specs=[pl.BlockSpec((seq, width), lambda b: (b, col_block))],
        out_specs=pl.BlockSpec((n_kb, width), lambda b: (b, 0)),
        out_shape=jax.ShapeDtypeStruct((batch * n_kb, width), F32),
        compiler_params=_params("parallel"),
        name="moba_kmean",
    )(big)


def _swa_body(sink_ref, q_ref, kp_ref, kc_ref, vp_ref, vc_ref, o_ref):
    i = pl.program_id(1)
    w = SWA_WINDOW
    half = SWA_HEAD_DIM
    lane = lax.broadcasted_iota(I32, (w, LANES), 1)
    r = lax.broadcasted_iota(I32, (w, w), 0)
    c = lax.broadcasted_iota(I32, (w, w), 1)
    from_prev = c > r
    dist = (r - c + jnp.where(from_prev, w, 0)).astype(F32)
    valid = jnp.logical_not(from_prev) | (i > 0)
    heads_per_kv = SWA_Q_HEADS // SWA_KV_HEADS

    def lo_hi(ref, kvh):
        own = jnp.where((lane < half) == (kvh == 0), ref[...].astype(F32), 0.0)
        other = pltpu.roll(own, half, axis=1)
        lo, hi = (own, other) if kvh == 0 else (other, own)
        return lo.astype(BF), hi.astype(BF)

    for kvh in range(SWA_KV_HEADS):
        kp, kc, vp, vc = (lo_hi(ref, kvh) for ref in (kp_ref, kc_ref, vp_ref, vc_ref))
        heads = range(kvh * heads_per_kv, (kvh + 1) * heads_per_kv)
        q = {h: q_ref[:, (h // 2) * LANES:(h // 2 + 1) * LANES] for h in heads}
        s = {h: jnp.where(from_prev,
                          lax.dot_general(q[h], kp[h % 2], NT_DIMS, preferred_element_type=F32),
                          lax.dot_general(q[h], kc[h % 2], NT_DIMS, preferred_element_type=F32)) for h in heads}
        p, inv = {}, {}
        for h in heads:
            slope = float(LOG2E * 2.0 ** (-8.0 * (h + 1) / SWA_Q_HEADS))
            sink = sink_ref[h] * LOG2E
            logits = jnp.where(valid, s[h] - slope * dist, NEG)
            m = jnp.maximum(jnp.max(logits, axis=1, keepdims=True), sink)
            p[h] = jnp.exp2(logits - m)
            inv[h] = 1.0 / (jnp.sum(p[h], axis=1, keepdims=True) + jnp.exp2(sink - m))
        for pair in range(kvh * heads_per_kv // 2, (kvh + 1) * heads_per_kv // 2):
            out = jnp.zeros((w, LANES), F32)
            for h in (2 * pair, 2 * pair + 1):
                pn = p[h] * inv[h]
                out = out + jnp.dot(jnp.where(from_prev, pn, 0.0).astype(BF), vp[h % 2], preferred_element_type=F32)
                out = out + jnp.dot(jnp.where(from_prev, 0.0, pn).astype(BF), vc[h % 2], preferred_element_type=F32)
            o_ref[:, pair * LANES:(pair + 1) * LANES] = out.astype(o_ref.dtype)


def _swa_attention(big, sinks, batch, seq, col_q, col_k, col_v):
    w = SWA_WINDOW
    n_q = seq // w
    width = SWA_Q_HEADS * SWA_HEAD_DIM
    prev = lambda col: pl.BlockSpec((w, LANES), lambda b, i: (b * n_q + jnp.maximum(i - 1, 0), col))
    cur = lambda col: pl.BlockSpec((w, LANES), lambda b, i: (b * n_q + i, col))
    return pl.pallas_call(
        _swa_body,
        grid=(batch, n_q),
        in_specs=[pl.BlockSpec(memory_space=pltpu.SMEM),
                  pl.BlockSpec((w, width), lambda b, i: (b * n_q + i, col_q)),
                  prev(col_k), cur(col_k), prev(col_v), cur(col_v)],
        out_specs=pl.BlockSpec((w, width), lambda b, i: (b * n_q + i, 0)),
        out_shape=jax.ShapeDtypeStruct((batch * seq, width), BF),
        compiler_params=_params("parallel", "parallel"),
        name="swa",
    )(sinks, big, big, big, big, big)


def _layer_norm(z, g, b):
    mu = jnp.mean(z, axis=-1, keepdims=True)
    zc = z - mu
    var = jnp.mean(zc * zc, axis=-1, keepdims=True)
    return zc * lax.rsqrt(var + LN_EPS) * g + b


def _outproj_body(a_ref, b_ref, wa_ref, wb_ref, h_ref, g_ref, beta_ref, o_ref):
    half = a_ref.shape[0] // 2
    for rows in (slice(0, half), slice(half, 2 * half)):
        mix = jnp.dot(a_ref[rows, :], wa_ref[...], preferred_element_type=F32)
        mix = mix + jnp.dot(b_ref[rows, :], wb_ref[...], preferred_element_type=F32)
        o_ref[rows, :] = _layer_norm(ALPHA * h_ref[rows, :] + mix, g_ref[...], beta_ref[...])


def _outproj_ln(a, b, w_out, h, g, beta, tm):
    t, d = h.shape
    ka = a.shape[1]
    row = lambda i: (i, 0)
    return pl.pallas_call(
        _outproj_body,
        grid=(t // tm,),
        in_specs=[pl.BlockSpec((tm, ka), row), pl.BlockSpec((tm, ka), row),
                  pl.BlockSpec((ka, d), lambda i: (0, 0)), pl.BlockSpec((ka, d), lambda i: (1, 0)),
                  pl.BlockSpec((tm, d), row), pl.BlockSpec((1, d), lambda i: (0, 0)),
                  pl.BlockSpec((1, d), lambda i: (0, 0))],
        out_specs=pl.BlockSpec((tm, d), row),
        out_shape=jax.ShapeDtypeStruct((t, d), F32),
        compiler_params=_params("parallel"),
        name="outproj_ln",
    )(a, b, w_out, w_out, h, g, beta)


def _second_of_four(a, b, c, d):
    hi_ab, lo_ab = jnp.maximum(a, b), jnp.minimum(a, b)
    hi_cd, lo_cd = jnp.maximum(c, d), jnp.minimum(c, d)
    return jnp.maximum(jnp.maximum(lo_ab, lo_cd), jnp.minimum(hi_ab, hi_cd))


def _router_body(h_ref, wr_ref, bias_ref, e_ref, r_ref):
    tm = h_ref.shape[0]
    h = h_ref[...]
    h_hi = h.astype(BF)
    h_lo = (h - h_hi.astype(F32)).astype(BF)
    both = jnp.dot(h_hi, wr_ref[...], preferred_element_type=F32)
    logits = (both[:, :LANES] + both[:, LANES:]) + jnp.dot(h_lo, wr_ref[:, :LANES], preferred_element_type=F32)
    lt = logits.T
    aff = [jax.nn.sigmoid(lt[SUBLANES * j:SUBLANES * (j + 1), :]) for j in range(EXPERTS_PER_GROUP)]
    sel = [aff[j] + bias_ref[SUBLANES * j:SUBLANES * (j + 1), :] for j in range(EXPERTS_PER_GROUP)]
    top1 = jnp.maximum(jnp.maximum(sel[0], sel[1]), jnp.maximum(sel[2], sel[3]))
    score = top1 + _second_of_four(*sel)
    gid = lax.broadcasted_iota(I32, (N_GROUPS, tm), 0)
    best = jnp.min(jnp.where(score == jnp.max(score, axis=0, keepdims=True), gid, N_GROUPS),
                   axis=0, keepdims=True)
    in_grp = gid == best
    pick = lambda x: jnp.sum(jnp.where(in_grp, x, 0.0), axis=0, keepdims=True)
    s4 = [pick(x) for x in sel]
    a4 = [pick(x) for x in aff]

    def argmax4(vals):
        j, v = jnp.zeros((1, tm), I32), vals[0]
        for n in range(1, EXPERTS_PER_GROUP):
            better = vals[n] > v
            j, v = jnp.where(better, n, j), jnp.where(better, vals[n], v)
        return j

    j0 = argmax4(s4)
    j1 = argmax4([jnp.where(j0 == n, -jnp.inf, s4[n]) for n in range(EXPERTS_PER_GROUP)])
    take = lambda j: sum(jnp.where(j == n, a4[n], 0.0) for n in range(EXPERTS_PER_GROUP))
    w0, w1 = take(j0), take(j1)
    total = w0 + w1
    e0 = best * EXPERTS_PER_GROUP + j0
    e1 = best * EXPERTS_PER_GROUP + j1
    rid = lax.broadcasted_iota(I32, (SUBLANES, tm), 0)
    e_ref[...] = jnp.where(rid == 0, e0, jnp.where(rid == 1, e1, 0))
    rid = lax.broadcasted_iota(I32, (LANES, tm), 0)
    rows = jnp.where(rid == 0, w0 / total, jnp.where(rid == 1, w1 / total, 0.0))
    r_ref[...] = rows.T


def _router(h, wr_perm, bias_perm, tm):
    t, d = h.shape
    return pl.pallas_call(
        _router_body,
        grid=(t // tm,),
        in_specs=[pl.BlockSpec((tm, d), lambda i: (i, 0)), pl.BlockSpec((d, 2 * LANES), lambda i: (0, 0)),
                  pl.BlockSpec((N_EXPERTS, 1), lambda i: (0, 0))],
        out_specs=[pl.BlockSpec((SUBLANES, tm), lambda i: (0, i)), pl.BlockSpec((tm, LANES), lambda i: (i, 0))],
        out_shape=[jax.ShapeDtypeStruct((SUBLANES, t), I32), jax.ShapeDtypeStruct((t, LANES), F32)],
        compiler_params=_params("parallel"),
        name="router",
    )(h, wr_perm, bias_perm)


def _dispatch_plan(e01, tm):
    t = e01.shape[1]
    flat = e01.reshape(-1)
    onehot = (flat[:, None] == jnp.arange(N_EXPERTS, dtype=I32)[None, :]).astype(I32)
    cum = jnp.cumsum(onehot, axis=0)
    rank = jnp.sum(onehot * cum, axis=1) - 1
    counts = cum[-1]
    padded = ((counts + tm - 1) // tm) * tm
    starts = jnp.concatenate([jnp.zeros((1,), I32), jnp.cumsum(padded).astype(I32)])
    pos = jnp.sum(onehot * starts[None, :N_EXPERTS], axis=1) + rank
    return starts, starts[:N_EXPERTS] + counts, pos[:t], pos[t:]


def _dispatch_body(p0_ref, p1_ref, gs_ref, ge_ref, h_ref, r_ref, xs_hbm, xbuf, zbuf, sem, zsem, *, tm, tt,
                   n_steps):
    i = pl.program_id(0)
    slot = i % 2
    n_tail = (xs_hbm.shape[0] - gs_ref[N_EXPERTS]) // tm

    def zero_copy(row0, size):
        return pltpu.make_async_copy(zbuf.at[pl.ds(0, size), :], xs_hbm.at[pl.ds(row0, size), :], zsem.at[0])

    def for_each_fill(fn):
        def group_pad(e, carry):
            end = gs_ref[e + 1]
            pad = end - ge_ref[e]
            covered = 0
            size = tm // 2
            while size >= SUBLANES:
                take = pad & size

                @pl.when(take != 0)
                def _():
                    fn(zero_copy(pl.multiple_of(end - covered - size, size), size))

                covered = covered + take
                size //= 2
            for k in range(SUBLANES - 1):
                @pl.when(k < (pad & (SUBLANES - 1)))
                def _():
                    fn(zero_copy(ge_ref[e] + k, 1))
            return carry

        def tail_tile(k, carry):
            fn(zero_copy(pl.multiple_of(gs_ref[N_EXPERTS] + k * tm, tm), tm))
            return carry

        lax.fori_loop(0, N_EXPERTS, group_pad, 0)
        lax.fori_loop(0, n_tail, tail_tile, 0)

    def wait_rows(sl):
        for _ in range(2):
            pltpu.make_async_copy(xbuf.at[sl, 0], xs_hbm.at[pl.ds(0, tt), :], sem.at[sl]).wait()

    @pl.when(i == 0)
    def _():
        zbuf[...] = jnp.zeros(zbuf.shape, F32)
        for_each_fill(lambda cp: cp.start())

    @pl.when(i >= 2)
    def _():
        wait_rows(slot)

    d = h_ref.shape[1]
    gates = r_ref[...]
    lane = lax.broadcasted_iota(I32, gates.shape, 1)
    for k in range(MOE_TOPK):
        xbuf[slot, k, :, :d] = h_ref[...]
    xbuf[slot, 0, :, d:] = jnp.where(lane == 0, gates, jnp.where(lane == 1, ALPHA, 0.0))
    xbuf[slot, 1, :, d:] = jnp.where(lane == 0, pltpu.roll(gates, LANES - 1, axis=1), 0.0)
    base = i * tt
    for r in range(tt):
        pltpu.make_async_copy(xbuf.at[slot, 0, pl.ds(r, 1), :], xs_hbm.at[pl.ds(p0_ref[base + r], 1), :],
                              sem.at[slot]).start(priority=0)
        pltpu.make_async_copy(xbuf.at[slot, 1, pl.ds(r, 1), :], xs_hbm.at[pl.ds(p1_ref[base + r], 1), :],
                              sem.at[slot]).start(priority=1)

    @pl.when(i == n_steps - 1)
    def _():
        wait_rows(slot)
        if n_steps > 1:
            wait_rows(1 - slot)
        for_each_fill(lambda cp: cp.wait())


def _dispatch(h, gates, starts, ends, pos0, pos1, tm, tt):
    t, d = h.shape
    dx = d + LANES
    n_steps = t // tt
    n_rows = 2 * t + N_EXPERTS * tm
    body = functools.partial(_dispatch_body, tm=tm, tt=tt, n_steps=n_steps)
    grid_spec = pltpu.PrefetchScalarGridSpec(
        num_scalar_prefetch=4,
        grid=(n_steps,),
        in_specs=[pl.BlockSpec((tt, d), lambda i, *_: (i, 0)), pl.BlockSpec((tt, LANES), lambda i, *_: (i, 0))],
        out_specs=pl.BlockSpec(memory_space=pl.ANY),
        scratch_shapes=[pltpu.VMEM((2, MOE_TOPK, tt, dx), F32), pltpu.VMEM((tm, dx), F32),
                        pltpu.SemaphoreType.DMA((2,)), pltpu.SemaphoreType.DMA((1,))],
    )
    return pl.pallas_call(
        body,
        grid_spec=grid_spec,
        out_shape=jax.ShapeDtypeStruct((n_rows, dx), F32),
        compiler_params=_params("arbitrary"),
        name="dispatch",
    )(pos0, pos1, starts, ends, h, gates)


def _experts_body(gs_ref, xs_hbm, wgu_ref, wd_ref, ys_hbm, xbuf, ybuf, xsem, osem, *, tm):
    e = pl.program_id(0)
    first = gs_ref[e] // tm
    n_tiles = gs_ref[e + 1] // tm - first
    total = gs_ref[N_EXPERTS] // tm
    ring = xbuf.shape[0]

    def in_copy(g, slot):
        row0 = pl.multiple_of(g * tm, tm)
        return pltpu.make_async_copy(xs_hbm.at[pl.ds(row0, tm), :], xbuf.at[slot], xsem.at[slot])

    def out_copy(g, slot):
        row0 = pl.multiple_of(g * tm, tm)
        return pltpu.make_async_copy(ybuf.at[slot], ys_hbm.at[pl.ds(row0, tm), :], osem.at[slot])

    @pl.when(e == 0)
    def _():
        for g in range(ring - 1):
            in_copy(jnp.minimum(g, total - 1), g).start(priority=1)
        ybuf[...] = jnp.zeros(ybuf.shape, F32)
        out_copy(0, 0).start()
        out_copy(1, 1).start()

    @pl.when(n_tiles > 0)
    def _():
        def tile(k, carry):
            g = first + k
            slot = g % 2
            xslot = g % ring
            in_copy(g, xslot).wait()
            out_copy(g, slot).wait()
            in_copy(jnp.minimum(g + ring - 1, total - 1), (g + ring - 1) % ring).start(priority=1)
            d = ybuf.shape[2]
            x = xbuf[xslot, :, :d]
            row_gate = xbuf[xslot, :, d:d + 1]
            row_residual = xbuf[xslot, :, d + 1:d + 2]
            gu = jnp.dot(x.astype(BF), wgu_ref[0, 0].astype(BF), preferred_element_type=F32)
            gate, up = gu[:, :EXPERT_FF], gu[:, EXPERT_FF:]
            hidden = (gate * jax.nn.sigmoid(gate) * up).astype(BF)
            y = jnp.dot(hidden, wd_ref[0, 0].astype(BF), preferred_element_type=F32)
            ybuf[slot] = row_gate * y + row_residual * x
            out_copy(g, slot).start(priority=1)
            return carry

        lax.fori_loop(0, n_tiles, tile, 0)

    @pl.when(e == N_EXPERTS - 1)
    def _():
        for n in range(ring - 1):
            in_copy(0, (total + n) % ring).wait()
        out_copy(0, 0).wait()
        out_copy(1, 1).wait()
        end = gs_ref[N_EXPERTS]
        n_tail = (ys_hbm.shape[0] - end) // tm
        ybuf[0] = jnp.zeros(ybuf.shape[1:], F32)

        def tail_copy(k):
            row0 = pl.multiple_of(end + k * tm, tm)
            return pltpu.make_async_copy(ybuf.at[0], ys_hbm.at[pl.ds(row0, tm), :], osem.at[0])

        def start_one(k, carry):
            tail_copy(k).start()
            return carry

        def wait_one(k, carry):
            tail_copy(k).wait()
            return carry

        lax.fori_loop(0, n_tail, start_one, 0)
        lax.fori_loop(0, n_tail, wait_one, 0)


EXPERT_IN_RING = 12


def _experts(xs, starts, w_gate_up, w_down, layer, tm):
    n_rows, dx = xs.shape
    d = dx - LANES
    body = functools.partial(_experts_body, tm=tm)
    grid_spec = pltpu.PrefetchScalarGridSpec(
        num_scalar_prefetch=1,
        grid=(N_EXPERTS,),
        in_specs=[pl.BlockSpec(memory_space=pl.ANY),
                  pl.BlockSpec((1, 1, d, 2 * EXPERT_FF), lambda e, gs: (layer, e, 0, 0)),
                  pl.BlockSpec((1, 1, EXPERT_FF, d), lambda e, gs: (layer, e, 0, 0))],
        out_specs=pl.BlockSpec(memory_space=pl.ANY),
        scratch_shapes=[pltpu.VMEM((EXPERT_IN_RING, tm, dx), F32), pltpu.VMEM((2, tm, d), F32),
                        pltpu.SemaphoreType.DMA((EXPERT_IN_RING,)), pltpu.SemaphoreType.DMA((2,))],
    )
    return pl.pallas_call(
        body,
        grid_spec=grid_spec,
        out_shape=jax.ShapeDtypeStruct((n_rows, d), F32),
        compiler_params=_params("arbitrary"),
        name="experts",
    )(starts, xs, w_gate_up, w_down)


def _combine_body(p0_ref, p1_ref, ys_hbm, g_ref, beta_ref, o_ref, buf0, buf1, sem, *, tc, n_steps):
    i = pl.program_id(0)
    ring = buf0.shape[0]

    def start_row(base, r, sl):
        pltpu.make_async_copy(ys_hbm.at[pl.ds(p0_ref[base + r], 1), :], buf0.at[sl, pl.ds(r, 1), :],
                              sem.at[sl]).start(priority=0)
        pltpu.make_async_copy(ys_hbm.at[pl.ds(p1_ref[base + r], 1), :], buf1.at[sl, pl.ds(r, 1), :],
                              sem.at[sl]).start(priority=1)

    def wait_rows(sl):
        pltpu.make_async_copy(ys_hbm.at[pl.ds(0, tc), :], buf0.at[sl], sem.at[sl]).wait()
        pltpu.make_async_copy(ys_hbm.at[pl.ds(0, tc), :], buf1.at[sl], sem.at[sl]).wait()

    @pl.when(i == 0)
    def _():
        for sl in range(ring):
            def one(r, carry):
                start_row(sl * tc, r, sl)
                return carry

            lax.fori_loop(0, tc, one, 0)

    nxt = jnp.minimum(i + 1, n_steps - 1) * ring * tc
    for sl in range(ring):
        wait_rows(sl)
        o_ref[sl * tc:(sl + 1) * tc, :] = _layer_norm(buf0[sl] + buf1[sl], g_ref[...], beta_ref[...])
        for r in range(tc):
            start_row(nxt + sl * tc, r, sl)

    @pl.when(i == n_steps - 1)
    def _():
        for sl in range(ring):
            wait_rows(sl)


COMBINE_RING = 4


def _combine_ln(ys, pos0, pos1, g, beta, tc):
    t, d = pos0.shape[0], ys.shape[1]
    n_steps = t // (COMBINE_RING * tc)
    body = functools.partial(_combine_body, tc=tc, n_steps=n_steps)
    row = lambda i, p0, p1: (i, 0)
    const = lambda i, p0, p1: (0, 0)
    grid_spec = pltpu.PrefetchScalarGridSpec(
        num_scalar_prefetch=2,
        grid=(n_steps,),
        in_specs=[pl.BlockSpec(memory_space=pl.ANY), pl.BlockSpec((1, d), const), pl.BlockSpec((1, d), const)],
        out_specs=pl.BlockSpec((COMBINE_RING * tc, d), row),
        scratch_shapes=[pltpu.VMEM((COMBINE_RING, tc, d), F32), pltpu.VMEM((COMBINE_RING, tc, d), F32),
                        pltpu.SemaphoreType.DMA((COMBINE_RING,))],
    )
    return pl.pallas_call(
        body,
        grid_spec=grid_spec,
        out_shape=jax.ShapeDtypeStruct((t, d), F32),
        compiler_params=_params("arbitrary"),
        name="combine_ln",
    )(pos0, pos1, ys, g, beta)


def _moe_ln(h, wr_perm, bias_perm, w_gate_up, w_down, layer, g, beta, tm_router=512, tm_expert=128, tt=512,
            tc=256):
    e01, gates = _router(h, wr_perm, bias_perm, tm_router)
    starts, ends, pos0, pos1 = _dispatch_plan(e01[:MOE_TOPK], tm_expert)
    xs = _dispatch(h, gates, starts, ends, pos0, pos1, tm_expert, tt)
    ys = _experts(xs, starts, w_gate_up, w_down, layer, tm_expert)
    return _combine_ln(ys, pos0, pos1, g, beta, tc)


def _swap_halves(w):
    half = w.shape[-1] // 2
    return jnp.concatenate([w[..., half:], w[..., :half]], axis=-1)


def _even_weights(w_in, w_q_up, w_kv_up, forget_bias):
    o_kv = MLA_Q_LORA
    o_kr = o_kv + MLA_KV_LORA
    o_fq = o_kr + MLA_ROPE
    hd = FOX_HEADS * FOX_HEAD_DIM
    o_fl = o_fq + 3 * hd
    k_r = w_in[:, o_kr:o_fq]
    f_l = jnp.pad(w_in[:, o_fl:], ((0, 0), (0, LANES - FOX_HEADS)))
    w_small = jnp.concatenate([w_in[:, :o_kr], k_r, _swap_halves(k_r), f_l], axis=1).astype(BF)
    w_fox = w_in[:, o_fq:o_fl].astype(BF)
    cs_fox = jnp.concatenate([jnp.full((1, hd), FOX_SCALE, F32), jnp.ones((1, 2 * hd), F32)], axis=1)
    wq = w_q_up.reshape(MLA_Q_LORA, MLA_HEADS, MLA_NOPE + MLA_ROPE)
    wqn = wq[:, :, :MLA_NOPE].reshape(MLA_Q_LORA, -1).astype(BF)
    pe = wq[:, :, MLA_NOPE:]
    wqp = jnp.concatenate([pe, _swap_halves(pe)], axis=-1).reshape(MLA_Q_LORA, -1).astype(BF)
    wkv = w_kv_up.reshape(MLA_KV_LORA, MLA_HEADS, MLA_NOPE + MLA_V)
    wk = wkv[:, :, :MLA_NOPE].reshape(MLA_KV_LORA, -1).astype(BF)
    wv = wkv[:, :, MLA_NOPE:].reshape(MLA_KV_LORA, -1).astype(BF)
    fb = jnp.pad(forget_bias.astype(F32), (0, LANES - FOX_HEADS)).reshape(1, LANES)
    return w_small, w_fox, cs_fox, wqn, wqp, wk, wv, fb


def _rope_table(seq):
    half = MLA_ROPE // 2
    inv_freq = ROPE_THETA ** (-np.arange(half, dtype=np.float64) / half)
    ang = np.arange(seq, dtype=np.float64)[:, None] * inv_freq[None, :]
    cos, sin = np.cos(ang), np.sin(ang)
    return jnp.asarray(np.concatenate([cos, cos, -sin, sin], axis=1), dtype=F32)


def _router_weights(w_router, router_bias):
    r = np.arange(N_EXPERTS)
    perm = (r % N_GROUPS) * EXPERTS_PER_GROUP + r // N_GROUPS
    wr = jnp.pad(w_router[:, perm], ((0, 0), (0, LANES - N_EXPERTS)))
    w_hi = wr.astype(BF)
    w_lo = (wr - w_hi.astype(F32)).astype(BF)
    return jnp.concatenate([w_hi, w_lo], axis=1), router_bias.astype(F32)[perm].reshape(N_EXPERTS, 1)


def _even_layer(h, batch, seq, w_in, q_norm, w_q_up, kv_norm, w_kv_up, forget_bias, w_out, g, beta):
    w_small, w_fox, cs_fox, wqn, wqp, wk, wv, fb = _even_weights(w_in, w_q_up, w_kv_up, forget_bias)
    fox_qkv = _mm(h, w_fox, cs_fox, BF, 2 * ROW_TILE, w_fox.shape[1] // 2)
    qn, qp, kn, v, kp, forget_logits = _mla_prep(h, w_small, _rope_table(seq), q_norm.reshape(1, -1),
                                                 kv_norm.reshape(1, -1), wqn, wqp, wk, wv, seq, ROW_TILE)
    q_bias, k_bias, edge = _fox_cum(forget_logits, fb, batch, seq, ROW_TILE)
    a = _mla_attention(qn, qp, kn, v, kp, batch, seq, ROW_TILE, HEAD_GROUP)
    bo = _fox_attention(fox_qkv, q_bias, k_bias, edge, batch, seq, ROW_TILE, HEAD_GROUP)
    return _outproj_ln(a, bo, w_out.astype(BF), h, g, beta, ROW_TILE)


def _odd_layer(h, batch, seq, w_in, sinks, w_out, g, beta):
    n_sq = SWA_Q_HEADS * SWA_HEAD_DIM
    n_skv = SWA_KV_HEADS * SWA_HEAD_DIM
    n_m = MOBA_HEADS * MOBA_HEAD_DIM
    o_mq = n_sq + 2 * n_skv
    w_big = jnp.concatenate([w_in[:, :n_sq], w_in[:, o_mq:], w_in[:, n_sq:o_mq]], axis=1).astype(BF)
    cs = jnp.concatenate([jnp.full((1, n_sq), SWA_SCALE, F32), jnp.full((1, n_m), MOBA_SCALE, F32),
                          jnp.ones((1, 2 * n_m + 2 * n_skv), F32)], axis=1)
    big = _mm(h, w_big, cs, BF, 2 * ROW_TILE, w_big.shape[1] // 2)
    blocks = lambda cols: cols // LANES
    k_mean = _moba_kmean(big, batch, seq, (n_sq + n_m) // n_m)
    slopes = jnp.asarray(LOG2E * 2.0 ** (-8.0 * np.arange(1, MOBA_HEADS + 1) / MOBA_HEADS), dtype=F32)
    c = _swa_attention(big, sinks.astype(F32), batch, seq, 0, blocks(n_sq + 3 * n_m), blocks(n_sq + 3 * n_m) + 1)
    dd = _moba_attention(big, k_mean, slopes, batch, seq, blocks(n_sq), blocks(n_sq + n_m), blocks(n_sq + 2 * n_m),
                         ROW_TILE, HEAD_GROUP)
    return _outproj_ln(c, dd, w_out.astype(BF), h, g, beta, ROW_TILE)


def kernel(x, w_router, router_bias, even_w_in, even_q_norm, even_w_q_up, even_kv_norm, even_w_kv_up,
           even_forget_bias, even_w_out, odd_w_in, odd_sinks, odd_w_out, ln_mix_g, ln_mix_b, ln_ffn_g, ln_ffn_b,
           w_gate_up, w_down):
    batch, seq, d = x.shape
    h = x.reshape(batch * seq, d)
    wr_perm, bias_perm = _router_weights(w_router, router_bias)
    row = lambda p, layer: p[layer].reshape(1, d)
    for layer in range(DEPTH):
        i = layer // 2
        if layer % 2 == 0:
            h = _even_layer(h, batch, seq, even_w_in[i], even_q_norm[i], even_w_q_up[i], even_kv_norm[i],
                            even_w_kv_up[i], even_forget_bias[i], even_w_out[i], row(ln_mix_g, layer),
                            row(ln_mix_b, layer))
        else:
            h = _odd_layer(h, batch, seq, odd_w_in[i], odd_sinks[i], odd_w_out[i], row(ln_mix_g, layer),
                           row(ln_mix_b, layer))
        h = _moe_ln(h, wr_perm, bias_perm, w_gate_up, w_down, layer, row(ln_ffn_g, layer), row(ln_ffn_b, layer))
    return h.reshape(batch, seq, d)
```

```python
import functools

import numpy as np
import jax
import jax.numpy as jnp
from jax import lax
from jax.experimental import pallas as pl
from jax.experimental.pallas import tpu as pltpu

BF = jnp.bfloat16
F32 = jnp.float32
I32 = jnp.int32

LANES = 128
SUBLANES = 8
VMEM_LIMIT = 56 * 1024 * 1024

ROW_TILE = 512
HEAD_GROUP = 4

DEPTH = 2
LN_EPS = 1e-5
RMS_EPS = 1e-6
ALPHA = (2 * DEPTH) ** 0.25
MLA_HEADS, MLA_Q_LORA, MLA_KV_LORA, MLA_NOPE, MLA_ROPE, MLA_V = 8, 512, 256, 128, 64, 128
ROPE_THETA = 10000.0
FOX_HEADS, FOX_HEAD_DIM = 8, 128
SWA_Q_HEADS, SWA_KV_HEADS, SWA_HEAD_DIM, SWA_WINDOW = 16, 2, 64, 128
MOBA_HEADS, MOBA_HEAD_DIM, MOBA_BLOCK, MOBA_TOPK = 8, 128, 256, 3
N_EXPERTS, N_GROUPS, MOE_TOPK, EXPERT_FF = 32, 8, 2, 512
EXPERTS_PER_GROUP = N_EXPERTS // N_GROUPS

LOG2E = 1.4426950408889634
MLA_SCALE = (MLA_NOPE + MLA_ROPE) ** -0.5 * LOG2E
FOX_SCALE = FOX_HEAD_DIM ** -0.5 * LOG2E
MOBA_SCALE = MOBA_HEAD_DIM ** -0.5 * LOG2E
SWA_SCALE = SWA_HEAD_DIM ** -0.5 * LOG2E
NEG = -1e30

NT_DIMS = (((1,), (1,)), ((), ()))


def _params(*sem):
    return pltpu.CompilerParams(dimension_semantics=sem, vmem_limit_bytes=VMEM_LIMIT)


def _mm_body(x_ref, w_ref, cs_ref, o_ref):
    acc = jnp.dot(x_ref[...].astype(BF), w_ref[...], preferred_element_type=F32)
    o_ref[...] = (acc * cs_ref[...]).astype(o_ref.dtype)


def _mm(x, w, col_scale, out_dtype, tm, tn):
    m, k = x.shape
    n = w.shape[1]
    return pl.pallas_call(
        _mm_body,
        grid=(n // tn, m // tm),
        in_specs=[
            pl.BlockSpec((tm, k), lambda j, i: (i, 0)),
            pl.BlockSpec((k, tn), lambda j, i: (0, j)),
            pl.BlockSpec((1, tn), lambda j, i: (0, j)),
        ],
        out_specs=pl.BlockSpec((tm, tn), lambda j, i: (i, j)),
        out_shape=jax.ShapeDtypeStruct((m, n), out_dtype),
        compiler_params=_params("parallel", "parallel"),
        name="proj",
    )(x, w, col_scale)


def _rms(x, g):
    return x * lax.rsqrt(jnp.mean(x * x, axis=-1, keepdims=True) + RMS_EPS) * g


def _rope_pair(slab, table):
    r = slab * table
    return r + pltpu.roll(r, MLA_ROPE, axis=1)


def _mla_prep_body(x_ref, ws_ref, rope_ref, qg_ref, kvg_ref, wqn_ref, wqp_ref, wk_ref, wv_ref,
                   qn_ref, qp_ref, kn_ref, v_ref, kp_ref, fl_ref):
    small = jnp.dot(x_ref[...].astype(BF), ws_ref[...], preferred_element_type=F32)
    o_kv, o_kr, o_fl = MLA_Q_LORA, MLA_Q_LORA + MLA_KV_LORA, MLA_Q_LORA + MLA_KV_LORA + LANES
    fl_ref[...] = small[:, o_fl:]
    table = rope_ref[...]
    cqn = _rms(small[:, :o_kv], qg_ref[...]).astype(BF)
    qn_ref[...] = (jnp.dot(cqn, wqn_ref[...], preferred_element_type=F32) * MLA_SCALE).astype(BF)
    qp = jnp.dot(cqn, wqp_ref[...], preferred_element_type=F32)
    for h in range(MLA_HEADS):
        sl = slice(h * LANES, (h + 1) * LANES)
        qp_ref[:, sl] = (_rope_pair(qp[:, sl], table) * MLA_SCALE).astype(BF)
    ckvn = _rms(small[:, o_kv:o_kr], kvg_ref[...]).astype(BF)
    kn_ref[...] = jnp.dot(ckvn, wk_ref[...], preferred_element_type=F32).astype(BF)
    v_ref[...] = jnp.dot(ckvn, wv_ref[...], preferred_element_type=F32).astype(BF)
    kr = _rope_pair(small[:, o_kr:o_fl], table)
    lane = lax.broadcasted_iota(I32, kr.shape, 1)
    kp_ref[...] = jnp.where(lane < MLA_ROPE, kr, 0.0).astype(BF)


def _mla_prep(x, w_small, rope_table, q_norm, kv_norm, wqn, wqp, wk, wv, seq, tm):
    t, d = x.shape
    n_s = seq // tm
    hd = MLA_HEADS * LANES
    row = lambda i: (i, 0)
    const = lambda i: (0, 0)
    out = lambda w: pl.BlockSpec((tm, w), row)
    return pl.pallas_call(
        _mla_prep_body,
        grid=(t // tm,),
        in_specs=[
            pl.BlockSpec((tm, d), row),
            pl.BlockSpec(w_small.shape, const),
            pl.BlockSpec((tm, LANES), lambda i: (i % n_s, 0)),
            pl.BlockSpec((1, MLA_Q_LORA), const),
            pl.BlockSpec((1, MLA_KV_LORA), const),
            pl.BlockSpec((MLA_Q_LORA, hd), const),
            pl.BlockSpec((MLA_Q_LORA, hd), const),
            pl.BlockSpec((MLA_KV_LORA, hd), const),
            pl.BlockSpec((MLA_KV_LORA, hd), const),
        ],
        out_specs=[out(hd), out(hd), out(hd), out(hd), out(LANES), out(LANES)],
        out_shape=[jax.ShapeDtypeStruct((t, hd), BF)] * 4 + [jax.ShapeDtypeStruct((t, LANES), BF),
                                                             jax.ShapeDtypeStruct((t, LANES), F32)],
        compiler_params=_params("parallel"),
        name="mla_prep",
    )(x, w_small, rope_table, q_norm, kv_norm, wqn, wqp, wk, wv)


N_SPLIT = 3


def _split_piece(x, which):
    hi = x.astype(BF).astype(F32)
    rest = x - hi
    mid = rest.astype(BF).astype(F32)
    return jnp.where(which == 0, hi, jnp.where(which == 1, mid, rest - mid))


def _fox_cum_body(fl_ref, fb_ref, spread_ref, qe_ref, ke_ref, edge_ref, carry_ref):
    @pl.when(pl.program_id(1) == 0)
    def _():
        carry_ref[...] = jnp.zeros_like(carry_ref)

    z = fl_ref[...] + fb_ref[...]
    log_f = jnp.minimum(z, 0.0) - jnp.log1p(jnp.exp(-jnp.abs(z)))
    tm = z.shape[0]
    r = lax.broadcasted_iota(I32, (tm, tm), 0)
    c = lax.broadcasted_iota(I32, (tm, tm), 1)
    tri = jnp.where(r >= c, 1.0, 0.0).astype(BF)
    parts = jnp.dot(tri, jnp.concatenate([_split_piece(log_f, n).astype(BF) for n in range(N_SPLIT)], axis=1),
                    preferred_element_type=F32)
    cum = (parts[:, :LANES] + parts[:, LANES:2 * LANES]) + parts[:, 2 * LANES:] + carry_ref[...]
    carry_ref[...] = cum[tm - 1:tm, :]
    cum = cum * LOG2E
    width = qe_ref.shape[1]
    pieces = jnp.concatenate([_split_piece(cum, n).astype(BF) for n in range(N_SPLIT)], axis=1)
    placed = jnp.dot(pieces, spread_ref[...], preferred_element_type=F32)
    li = lax.broadcasted_iota(I32, (tm, width), 1) % LANES
    qe_ref[...] = jnp.where((li >= N_SPLIT) & (li < 2 * N_SPLIT), 1.0, placed[:, :width]).astype(BF)
    ke_ref[...] = jnp.where(li < N_SPLIT, 1.0, placed[:, width:]).astype(BF)
    edge_ref[0, 0:1, :] = cum[0:1, :]
    edge_ref[0, 1:2, :] = cum[tm - 1:tm, :]


def _fox_spread():
    width = FOX_HEADS * LANES
    m = np.zeros((N_SPLIT * LANES, 2 * width), np.float32)
    for j in range(N_SPLIT):
        for h in range(FOX_HEADS):
            m[j * LANES + h, h * LANES + j] = 1.0
            m[j * LANES + h, width + h * LANES + N_SPLIT + j] = -1.0
    return jnp.asarray(m, dtype=BF)


def _fox_cum(forget_logits, forget_bias_row, batch, seq, tm):
    t = forget_logits.shape[0]
    n_s = seq // tm
    width = FOX_HEADS * LANES
    q_bias, k_bias, edge = pl.pallas_call(
        _fox_cum_body,
        grid=(batch, n_s),
        in_specs=[
            pl.BlockSpec((tm, LANES), lambda b, i: (b * n_s + i, 0)),
            pl.BlockSpec((1, LANES), lambda b, i: (0, 0)),
            pl.BlockSpec((N_SPLIT * LANES, 2 * width), lambda b, i: (0, 0)),
        ],
        out_specs=[
            pl.BlockSpec((tm, width), lambda b, i: (b * n_s + i, 0)),
            pl.BlockSpec((tm, width), lambda b, i: (b * n_s + i, 0)),
            pl.BlockSpec((1, 2, LANES), lambda b, i: (b * n_s + i, 0, 0)),
        ],
        out_shape=[jax.ShapeDtypeStruct((t, width), BF), jax.ShapeDtypeStruct((t, width), BF),
                   jax.ShapeDtypeStruct((batch * n_s, 2, LANES), F32)],
        scratch_shapes=[pltpu.VMEM((1, LANES), F32)],
        compiler_params=_params("parallel", "arbitrary"),
        name="fox_cum",
    )(forget_logits, forget_bias_row, _fox_spread())
    return q_bias, k_bias, edge[:, :, :FOX_HEADS].reshape(-1)


def _softmax_update(s_all, v_all, m_ref, l_ref, acc_ref, rows=slice(None)):
    heads = range(len(s_all))
    m_prev = [m_ref[g, rows] for g in heads]
    m_new = [jnp.maximum(m_prev[g], jnp.max(s_all[g], axis=1, keepdims=True)) for g in heads]
    reps = s_all[0].shape[1] // LANES
    p = [jnp.exp2(s_all[g] - jnp.concatenate([m_new[g]] * reps, axis=1)) for g in heads]
    pv = [jnp.dot(p[g].astype(BF), v_all[g], preferred_element_type=F32) for g in heads]
    for g in heads:
        alpha = jnp.exp2(m_prev[g] - m_new[g])
        l_ref[g, rows] = alpha * l_ref[g, rows] + jnp.sum(p[g], axis=1, keepdims=True)
        acc_ref[g, rows] = alpha * acc_ref[g, rows] + pv[g]
        m_ref[g, rows] = m_new[g]


def _attn_body(*refs, mode, t, group):
    hg = pl.program_id(1)
    i = pl.program_id(2)
    heads = range(group)
    sl = lambda g: slice(g * LANES, (g + 1) * LANES)
    if mode == "moba":
        q_ref, k_ref, kx_ref, v_ref, km_ref, slope_ref, o_ref, m_ref, l_ref, acc_ref, qx_ref = refs
        for g in heads:
            qx_ref[g] = _moba_query_lanes(q_ref[:, sl(g)], km_ref[:, sl(g)], slope_ref[hg * group + g], i, t)
        q_extra = lambda g, rows: qx_ref[g, rows]
        k_extra = lambda g, ks, nk: kx_ref[pl.ds(ks, nk), :]
    elif mode == "mla":
        q_ref, qx_ref, k_ref, kx_ref, v_ref, o_ref, m_ref, l_ref, acc_ref = refs
        q_extra = lambda g, rows: qx_ref[rows, sl(g)]
        k_extra = lambda g, ks, nk: kx_ref[pl.ds(ks, nk), :]
    else:
        q_ref, qx_ref, k_ref, kx_ref, v_ref, edge_ref, o_ref, m_ref, l_ref, acc_ref, knorm_ref = refs
        q_extra = lambda g, rows: qx_ref[rows, sl(g)]
        k_extra = lambda g, ks, nk: kx_ref[pl.ds(ks, nk), sl(g)]
        first_tile = _fox_first_tile(q_ref, k_ref, edge_ref, knorm_ref, i, hg, t, group)

    def scores(g, ks, nk=t, rows=slice(None)):
        q = jnp.concatenate([q_ref[rows, sl(g)], q_extra(g, rows)], axis=1)
        k = jnp.concatenate([k_ref[pl.ds(ks, nk), sl(g)], k_extra(g, ks, nk)], axis=1)
        return lax.dot_general(q, k, NT_DIMS, preferred_element_type=F32)

    m_ref[...] = jnp.full(m_ref.shape, NEG, F32)
    l_ref[...] = jnp.zeros(l_ref.shape, F32)
    acc_ref[...] = jnp.zeros(acc_ref.shape, F32)

    values = lambda ks, nk=t: [v_ref[pl.ds(ks, nk), sl(g)] for g in heads]
    ks = pl.multiple_of(i * t, t)
    half = t // 2
    for rows, nk in ((slice(0, half), half), (slice(half, t), t)):
        r = lax.broadcasted_iota(I32, (half, nk), 0) + rows.start
        c = lax.broadcasted_iota(I32, (half, nk), 1)
        _softmax_update([jnp.where(c <= r, scores(g, ks, nk, rows), NEG) for g in heads], values(ks, nk),
                        m_ref, l_ref, acc_ref, rows)

    def past_tile(j, carry):
        ks = pl.multiple_of(j * t, t)
        _softmax_update([scores(g, ks) for g in heads], values(ks), m_ref, l_ref, acc_ref)
        return carry

    lax.fori_loop(first_tile if mode == "fox" else 0, i, past_tile, 0)
    for g in heads:
        o_ref[:, sl(g)] = (acc_ref[g] / l_ref[g]).astype(o_ref.dtype)


FOX_SKIP_GAP = 160.0


def _fox_first_tile(q_ref, k_ref, edge_ref, knorm_ref, i, hg, t, group):
    b = pl.program_id(0)
    n_q = pl.num_programs(2)
    sl = lambda g: slice(g * LANES, (g + 1) * LANES)

    def max_row_norm(x):
        x = x.astype(F32)
        return jnp.sqrt(jnp.max(jnp.sum(x * x, axis=1, keepdims=True)))

    @pl.when(i == 0)
    def _():
        for g in range(group):
            knorm_ref[g] = max_row_norm(k_ref[:, sl(g)])

    def edge(tile, last, g):
        return edge_ref[((b * n_q + tile) * 2 + last) * FOX_HEADS + hg * group + g]

    limit = [-(2.0 * 1.001 * max_row_norm(q_ref[:, sl(g)]) * knorm_ref[g] + FOX_SKIP_GAP) for g in range(group)]
    first = [edge(i, 0, g) for g in range(group)]

    def scan(j, lo):
        needed = first[0] - edge(j, 1, 0) >= limit[0]
        for g in range(1, group):
            needed = needed | (first[g] - edge(j, 1, g) >= limit[g])
        return jnp.where(needed, jnp.minimum(lo, j), lo)

    return lax.fori_loop(0, i, scan, i)


def _attention(mode, batch, seq, heads, t, group, operands, in_specs, extra_scratch=()):
    n_q = seq // t
    body = functools.partial(_attn_body, mode=mode, t=t, group=group)
    return pl.pallas_call(
        body,
        grid=(batch, heads // group, n_q),
        in_specs=in_specs,
        out_specs=pl.BlockSpec((t, group * LANES), lambda b, h, i: (b * n_q + i, h)),
        out_shape=jax.ShapeDtypeStruct((batch * seq, heads * LANES), BF),
        scratch_shapes=[pltpu.VMEM((group, t, LANES), F32), pltpu.VMEM((group, t, LANES), F32),
                        pltpu.VMEM((group, t, LANES), F32), *extra_scratch],
        compiler_params=_params("parallel", "parallel", "arbitrary"),
        name="attn_" + mode,
    )(*operands)


def _q_spec(t, n_q, group, col0=0):
    return pl.BlockSpec((t, group * LANES), lambda b, h, i: (b * n_q + i, col0 + h))


def _kv_spec(seq, group, col0=0):
    return pl.BlockSpec((seq, group * LANES), lambda b, h, i: (b, col0 + h))


def _mla_attention(qn, qp, kn, v, kp, batch, seq, t, group):
    n_q = seq // t
    specs = [_q_spec(t, n_q, group), _q_spec(t, n_q, group), _kv_spec(seq, group),
             pl.BlockSpec((seq, LANES), lambda b, h, i: (b, 0)), _kv_spec(seq, group)]
    return _attention("mla", batch, seq, MLA_HEADS, t, group, (qn, qp, kn, kp, v), specs)


def _fox_attention(qkv, q_bias, k_bias, edge, batch, seq, t, group):
    n_q = seq // t
    n_hg = FOX_HEADS // group
    specs = [_q_spec(t, n_q, group, 0), _q_spec(t, n_q, group, 0), _kv_spec(seq, group, n_hg),
             _kv_spec(seq, group, 0), _kv_spec(seq, group, 2 * n_hg), pl.BlockSpec(memory_space=pltpu.SMEM)]
    return _attention("fox", batch, seq, FOX_HEADS, t, group, (qkv, q_bias, qkv, k_bias, qkv, edge), specs,
                      (pltpu.SMEM((group,), F32),))


def _moba_key_lanes(seq):
    n_kb = seq // MOBA_BLOCK
    kpos = np.arange(seq)
    lanes = np.zeros((seq, LANES), np.float32)
    lanes[kpos, kpos // MOBA_BLOCK] = 1.0
    lanes[:, n_kb:n_kb + N_SPLIT] = 1.0
    lanes[:, n_kb + N_SPLIT:n_kb + 2 * N_SPLIT] = (MOBA_BLOCK * (kpos // MOBA_BLOCK))[:, None]
    lanes[:, n_kb + 2 * N_SPLIT:n_kb + 3 * N_SPLIT] = (kpos % MOBA_BLOCK)[:, None]
    return jnp.asarray(lanes, dtype=BF)


def _moba_query_lanes(q, k_mean, slope, i, t):
    n_kb = k_mean.shape[0]
    gate = lax.dot_general(k_mean.astype(BF), q, NT_DIMS, preferred_element_type=F32)
    blk = lax.broadcasted_iota(I32, (n_kb, t), 0)
    qpos_i = i * t + lax.broadcasted_iota(I32, (1, t), 1)
    own = qpos_i // MOBA_BLOCK
    beaten = jnp.zeros((n_kb, t), F32)
    for n in range(n_kb):
        g_n = gate[n:n + 1, :]
        wins = (g_n > gate) | ((g_n == gate) & (blk > n))
        beaten = beaten + jnp.where(wins & (own > n), 1.0, 0.0)
    attended = (blk == own) | ((blk < own) & (beaten < MOBA_TOPK))
    mask = jnp.concatenate([jnp.where(attended, 0.0, NEG), jnp.zeros((LANES - n_kb, t), F32)], axis=0)
    row = lax.broadcasted_iota(I32, (LANES, t), 0) - n_kb
    qpos = qpos_i.astype(F32)
    slope_row = jnp.full((1, t), slope, F32)
    which = row % N_SPLIT
    lanes = jnp.where(row < 0, mask,
                      jnp.where(row < N_SPLIT, _split_piece(-slope_row * qpos, which),
                                jnp.where(row < 3 * N_SPLIT, _split_piece(slope_row, which), 0.0)))
    return lanes.T.astype(BF)


def _moba_attention(big, k_mean, slopes, batch, seq, col_q, col_k, col_v, t, group):
    n_q = seq // t
    specs = [_q_spec(t, n_q, group, col_q // group), _kv_spec(seq, group, col_k // group),
             pl.BlockSpec((seq, LANES), lambda b, h, i: (0, 0)),
             _kv_spec(seq, group, col_v // group),
             pl.BlockSpec((seq // MOBA_BLOCK, group * LANES), lambda b, h, i: (b, h)),
             pl.BlockSpec(memory_space=pltpu.SMEM)]
    scratch = (pltpu.VMEM((group, t, LANES), BF),)
    return _attention("moba", batch, seq, MOBA_HEADS, t, group,
                      (big, big, _moba_key_lanes(seq), big, k_mean, slopes), specs, scratch)


def _kmean_body(k_ref, o_ref):
    n_kb = o_ref.shape[0]
    for n in range(n_kb):
        blk = k_ref[n * MOBA_BLOCK:(n + 1) * MOBA_BLOCK, :].astype(F32)
        o_ref[n:n + 1, :] = jnp.mean(blk, axis=0, keepdims=True)


def _moba_kmean(big, batch, seq, col_block):
    n_kb = seq // MOBA_BLOCK
    width = MOBA_HEADS * MOBA_HEAD_DIM
    return pl.pallas_call(
        _kmean_body,
        grid=(batch,),
        in_specs=[pl.BlockSpec((seq, width), lambda b: (b, col_block))],
        out_specs=pl.BlockSpec((n_kb, width), lambda b: (b, 0)),
        out_shape=jax.ShapeDtypeStruct((batch * n_kb, width), F32),
        compiler_params=_params("parallel"),
        name="moba_kmean",
    )(big)


def _swa_body(sink_ref, q_ref, kp_ref, kc_ref, vp_ref, vc_ref, o_ref):
    i = pl.program_id(1)
    w = SWA_WINDOW
    half = SWA_HEAD_DIM
    lane = lax.broadcasted_iota(I32, (w, LANES), 1)
    r = lax.broadcasted_iota(I32, (w, w), 0)
    c = lax.broadcasted_iota(I32, (w, w), 1)
    from_prev = c > r
    dist = (r - c + jnp.where(from_prev, w, 0)).astype(F32)
    valid = jnp.logical_not(from_prev) | (i > 0)
    heads_per_kv = SWA_Q_HEADS // SWA_KV_HEADS

    def lo_hi(ref, kvh):
        own = jnp.where((lane < half) == (kvh == 0), ref[...].astype(F32), 0.0)
        other = pltpu.roll(own, half, axis=1)
        lo, hi = (own, other) if kvh == 0 else (other, own)
        return lo.astype(BF), hi.astype(BF)

    for kvh in range(SWA_KV_HEADS):
        kp, kc, vp, vc = (lo_hi(ref, kvh) for ref in (kp_ref, kc_ref, vp_ref, vc_ref))
        heads = range(kvh * heads_per_kv, (kvh + 1) * heads_per_kv)
        q = {h: q_ref[:, (h // 2) * LANES:(h // 2 + 1) * LANES] for h in heads}
        s = {h: jnp.where(from_prev,
                          lax.dot_general(q[h], kp[h % 2], NT_DIMS, preferred_element_type=F32),
                          lax.dot_general(q[h], kc[h % 2], NT_DIMS, preferred_element_type=F32)) for h in heads}
        p, inv = {}, {}
        for h in heads:
            slope = float(LOG2E * 2.0 ** (-8.0 * (h + 1) / SWA_Q_HEADS))
            sink = sink_ref[h] * LOG2E
            logits = jnp.where(valid, s[h] - slope * dist, NEG)
            m = jnp.maximum(jnp.max(logits, axis=1, keepdims=True), sink)
            p[h] = jnp.exp2(logits - m)
            inv[h] = 1.0 / (jnp.sum(p[h], axis=1, keepdims=True) + jnp.exp2(sink - m))
        for pair in range(kvh * heads_per_kv // 2, (kvh + 1) * heads_per_kv // 2):
            out = jnp.zeros((w, LANES), F32)
            for h in (2 * pair, 2 * pair + 1):
                pn = p[h] * inv[h]
                out = out + jnp.dot(jnp.where(from_prev, pn, 0.0).astype(BF), vp[h % 2], preferred_element_type=F32)
                out = out + jnp.dot(jnp.where(from_prev, 0.0, pn).astype(BF), vc[h % 2], preferred_element_type=F32)
            o_ref[:, pair * LANES:(pair + 1) * LANES] = out.astype(o_ref.dtype)


def _swa_attention(big, sinks, batch, seq, col_q, col_k, col_v):
    w = SWA_WINDOW
    n_q = seq // w
    width = SWA_Q_HEADS * SWA_HEAD_DIM
    prev = lambda col: pl.BlockSpec((w, LANES), lambda b, i: (b * n_q + jnp.maximum(i - 1, 0), col))
    cur = lambda col: pl.BlockSpec((w, LANES), lambda b, i: (b * n_q + i, col))
    return pl.pallas_call(
        _swa_body,
        grid=(batch, n_q),
        in_specs=[pl.BlockSpec(memory_space=pltpu.SMEM),
                  pl.BlockSpec((w, width), lambda b, i: (b * n_q + i, col_q)),
                  prev(col_k), cur(col_k), prev(col_v), cur(col_v)],
        out_specs=pl.BlockSpec((w, width), lambda b, i: (b * n_q + i, 0)),
        out_shape=jax.ShapeDtypeStruct((batch * seq, width), BF),
        compiler_params=_params("parallel", "parallel"),
        name="swa",
    )(sinks, big, big, big, big, big)


def _layer_norm(z, g, b):
    mu = jnp.mean(z, axis=-1, keepdims=True)
    zc = z - mu
    var = jnp.mean(zc * zc, axis=-1, keepdims=True)
    return zc * lax.rsqrt(var + LN_EPS) * g + b


def _outproj_body(a_ref, b_ref, wa_ref, wb_ref, h_ref, g_ref, beta_ref, o_ref):
    half = a_ref.shape[0] // 2
    for rows in (slice(0, half), slice(half, 2 * half)):
        mix = jnp.dot(a_ref[rows, :], wa_ref[...], preferred_element_type=F32)
        mix = mix + jnp.dot(b_ref[rows, :], wb_ref[...], preferred_element_type=F32)
        o_ref[rows, :] = _layer_norm(ALPHA * h_ref[rows, :] + mix, g_ref[...], beta_ref[...])


def _outproj_ln(a, b, w_out, h, g, beta, tm):
    t, d = h.shape
    ka = a.shape[1]
    row = lambda i: (i, 0)
    return pl.pallas_call(
        _outproj_body,
        grid=(t // tm,),
        in_specs=[pl.BlockSpec((tm, ka), row), pl.BlockSpec((tm, ka), row),
                  pl.BlockSpec((ka, d), lambda i: (0, 0)), pl.BlockSpec((ka, d), lambda i: (1, 0)),
                  pl.BlockSpec((tm, d), row), pl.BlockSpec((1, d), lambda i: (0, 0)),
                  pl.BlockSpec((1, d), lambda i: (0, 0))],
        out_specs=pl.BlockSpec((tm, d), row),
        out_shape=jax.ShapeDtypeStruct((t, d), F32),
        compiler_params=_params("parallel"),
        name="outproj_ln",
    )(a, b, w_out, w_out, h, g, beta)


def _second_of_four(a, b, c, d):
    hi_ab, lo_ab = jnp.maximum(a, b), jnp.minimum(a, b)
    hi_cd, lo_cd = jnp.maximum(c, d), jnp.minimum(c, d)
    return jnp.maximum(jnp.maximum(lo_ab, lo_cd), jnp.minimum(hi_ab, hi_cd))


def _router_body(h_ref, wr_ref, bias_ref, e_ref, r_ref):
    tm = h_ref.shape[0]
    h = h_ref[...]
    h_hi = h.astype(BF)
    h_lo = (h - h_hi.astype(F32)).astype(BF)
    both = jnp.dot(h_hi, wr_ref[...], preferred_element_type=F32)
    logits = (both[:, :LANES] + both[:, LANES:]) + jnp.dot(h_lo, wr_ref[:, :LANES], preferred_element_type=F32)
    lt = logits.T
    aff = [jax.nn.sigmoid(lt[SUBLANES * j:SUBLANES * (j + 1), :]) for j in range(EXPERTS_PER_GROUP)]
    sel = [aff[j] + bias_ref[SUBLANES * j:SUBLANES * (j + 1), :] for j in range(EXPERTS_PER_GROUP)]
    top1 = jnp.maximum(jnp.maximum(sel[0], sel[1]), jnp.maximum(sel[2], sel[3]))
    score = top1 + _second_of_four(*sel)
    gid = lax.broadcasted_iota(I32, (N_GROUPS, tm), 0)
    best = jnp.min(jnp.where(score == jnp.max(score, axis=0, keepdims=True), gid, N_GROUPS),
                   axis=0, keepdims=True)
    in_grp = gid == best
    pick = lambda x: jnp.sum(jnp.where(in_grp, x, 0.0), axis=0, keepdims=True)
    s4 = [pick(x) for x in sel]
    a4 = [pick(x) for x in aff]

    def argmax4(vals):
        j, v = jnp.zeros((1, tm), I32), vals[0]
        for n in range(1, EXPERTS_PER_GROUP):
            better = vals[n] > v
            j, v = jnp.where(better, n, j), jnp.where(better, vals[n], v)
        return j

    j0 = argmax4(s4)
    j1 = argmax4([jnp.where(j0 == n, -jnp.inf, s4[n]) for n in range(EXPERTS_PER_GROUP)])
    take = lambda j: sum(jnp.where(j == n, a4[n], 0.0) for n in range(EXPERTS_PER_GROUP))
    w0, w1 = take(j0), take(j1)
    total = w0 + w1
    e0 = best * EXPERTS_PER_GROUP + j0
    e1 = best * EXPERTS_PER_GROUP + j1
    rid = lax.broadcasted_iota(I32, (SUBLANES, tm), 0)
    e_ref[...] = jnp.where(rid == 0, e0, jnp.where(rid == 1, e1, 0))
    rid = lax.broadcasted_iota(I32, (LANES, tm), 0)
    rows = jnp.where(rid == 0, w0 / total, jnp.where(rid == 1, w1 / total, 0.0))
    r_ref[...] = rows.T


def _router(h, wr_perm, bias_perm, tm):
    t, d = h.shape
    return pl.pallas_call(
        _router_body,
        grid=(t // tm,),
        in_specs=[pl.BlockSpec((tm, d), lambda i: (i, 0)), pl.BlockSpec((d, 2 * LANES), lambda i: (0, 0)),
                  pl.BlockSpec((N_EXPERTS, 1), lambda i: (0, 0))],
        out_specs=[pl.BlockSpec((SUBLANES, tm), lambda i: (0, i)), pl.BlockSpec((tm, LANES), lambda i: (i, 0))],
        out_shape=[jax.ShapeDtypeStruct((SUBLANES, t), I32), jax.ShapeDtypeStruct((t, LANES), F32)],
        compiler_params=_params("parallel"),
        name="router",
    )(h, wr_perm, bias_perm)


def _dispatch_plan(e01, tm):
    t = e01.shape[1]
    flat = e01.reshape(-1)
    onehot = (flat[:, None] == jnp.arange(N_EXPERTS, dtype=I32)[None, :]).astype(I32)
    cum = jnp.cumsum(onehot, axis=0)
    rank = jnp.sum(onehot * cum, axis=1) - 1
    counts = cum[-1]
    padded = ((counts + tm - 1) // tm) * tm
    starts = jnp.concatenate([jnp.zeros((1,), I32), jnp.cumsum(padded).astype(I32)])
    pos = jnp.sum(onehot * starts[None, :N_EXPERTS], axis=1) + rank
    return starts, starts[:N_EXPERTS] + counts, pos[:t], pos[t:]


def _dispatch_body(p0_ref, p1_ref, gs_ref, ge_ref, h_ref, r_ref, xs_hbm, xbuf, zbuf, sem, zsem, *, tm, tt,
                   n_steps):
    i = pl.program_id(0)
    slot = i % 2
    n_tail = (xs_hbm.shape[0] - gs_ref[N_EXPERTS]) // tm

    def zero_copy(row0, size):
        return pltpu.make_async_copy(zbuf.at[pl.ds(0, size), :], xs_hbm.at[pl.ds(row0, size), :], zsem.at[0])

    def for_each_fill(fn):
        def group_pad(e, carry):
            end = gs_ref[e + 1]
            pad = end - ge_ref[e]
            covered = 0
            size = tm // 2
            while size >= SUBLANES:
                take = pad & size

                @pl.when(take != 0)
                def _():
                    fn(zero_copy(pl.multiple_of(end - covered - size, size), size))

                covered = covered + take
                size //= 2
            for k in range(SUBLANES - 1):
                @pl.when(k < (pad & (SUBLANES - 1)))
                def _():
                    fn(zero_copy(ge_ref[e] + k, 1))
            return carry

        def tail_tile(k, carry):
            fn(zero_copy(pl.multiple_of(gs_ref[N_EXPERTS] + k * tm, tm), tm))
            return carry

        lax.fori_loop(0, N_EXPERTS, group_pad, 0)
        lax.fori_loop(0, n_tail, tail_tile, 0)

    def wait_rows(sl):
        for _ in range(2):
            pltpu.make_async_copy(xbuf.at[sl, 0], xs_hbm.at[pl.ds(0, tt), :], sem.at[sl]).wait()

    @pl.when(i == 0)
    def _():
        zbuf[...] = jnp.zeros(zbuf.shape, F32)
        for_each_fill(lambda cp: cp.start())

    @pl.when(i >= 2)
    def _():
        wait_rows(slot)

    d = h_ref.shape[1]
    gates = r_ref[...]
    lane = lax.broadcasted_iota(I32, gates.shape, 1)
    for k in range(MOE_TOPK):
        xbuf[slot, k, :, :d] = h_ref[...]
    xbuf[slot, 0, :, d:] = jnp.where(lane == 0, gates, jnp.where(lane == 1, ALPHA, 0.0))
    xbuf[slot, 1, :, d:] = jnp.where(lane == 0, pltpu.roll(gates, LANES - 1, axis=1), 0.0)
    base = i * tt
    for r in range(tt):
        pltpu.make_async_copy(xbuf.at[slot, 0, pl.ds(r, 1), :], xs_hbm.at[pl.ds(p0_ref[base + r], 1), :],
                              sem.at[slot]).start(priority=0)
        pltpu.make_async_copy(xbuf.at[slot, 1, pl.ds(r, 1), :], xs_hbm.at[pl.ds(p1_ref[base + r], 1), :],
                              sem.at[slot]).start(priority=1)

    @pl.when(i == n_steps - 1)
    def _():
        wait_rows(slot)
        if n_steps > 1:
            wait_rows(1 - slot)
        for_each_fill(lambda cp: cp.wait())


def _dispatch(h, gates, starts, ends, pos0, pos1, tm, tt):
    t, d = h.shape
    dx = d + LANES
    n_steps = t // tt
    n_rows = 2 * t + N_EXPERTS * tm
    body = functools.partial(_dispatch_body, tm=tm, tt=tt, n_steps=n_steps)
    grid_spec = pltpu.PrefetchScalarGridSpec(
        num_scalar_prefetch=4,
        grid=(n_steps,),
        in_specs=[pl.BlockSpec((tt, d), lambda i, *_: (i, 0)), pl.BlockSpec((tt, LANES), lambda i, *_: (i, 0))],
        out_specs=pl.BlockSpec(memory_space=pl.ANY),
        scratch_shapes=[pltpu.VMEM((2, MOE_TOPK, tt, dx), F32), pltpu.VMEM((tm, dx), F32),
                        pltpu.SemaphoreType.DMA((2,)), pltpu.SemaphoreType.DMA((1,))],
    )
    return pl.pallas_call(
        body,
        grid_spec=grid_spec,
        out_shape=jax.ShapeDtypeStruct((n_rows, dx), F32),
        compiler_params=_params("arbitrary"),
        name="dispatch",
    )(pos0, pos1, starts, ends, h, gates)


def _experts_body(gs_ref, xs_hbm, wgu_hbm, wd_hbm, ys_hbm, xbuf, ybuf, wgu_buf, wd_buf, xsem, osem, wsem, *, tm,
                  layer):
    e = pl.program_id(0)
    first = gs_ref[e] // tm
    n_tiles = gs_ref[e + 1] // tm - first
    total = gs_ref[N_EXPERTS] // tm
    ring = xbuf.shape[0]
    wring = wgu_buf.shape[0]
    wslot = e % wring

    def weight_copies(expert, slot):
        return (pltpu.make_async_copy(wgu_hbm.at[layer, expert], wgu_buf.at[slot], wsem.at[0, slot]),
                pltpu.make_async_copy(wd_hbm.at[layer, expert], wd_buf.at[slot], wsem.at[1, slot]))

    @pl.when(e == 0)
    def _():
        for n in range(wring - 1):
            for cp in weight_copies(n, n):
                cp.start()

    for cp in weight_copies(jnp.minimum(e + wring - 1, N_EXPERTS - 1), (e + wring - 1) % wring):
        cp.start()
    for cp in weight_copies(e, wslot):
        cp.wait()
    wgu_ref = wgu_buf.at[wslot]
    wd_ref = wd_buf.at[wslot]

    def in_copy(g, slot):
        row0 = pl.multiple_of(g * tm, tm)
        return pltpu.make_async_copy(xs_hbm.at[pl.ds(row0, tm), :], xbuf.at[slot], xsem.at[slot])

    def out_copy(g, slot):
        row0 = pl.multiple_of(g * tm, tm)
        return pltpu.make_async_copy(ybuf.at[slot], ys_hbm.at[pl.ds(row0, tm), :], osem.at[slot])

    @pl.when(e == 0)
    def _():
        for g in range(ring - 1):
            in_copy(jnp.minimum(g, total - 1), g).start(priority=1)
        ybuf[...] = jnp.zeros(ybuf.shape, F32)
        out_copy(0, 0).start()
        out_copy(1, 1).start()

    @pl.when(n_tiles > 0)
    def _():
        def tile(k, carry):
            g = first + k
            slot = g % 2
            xslot = g % ring
            in_copy(g, xslot).wait()
            out_copy(g, slot).wait()
            in_copy(jnp.minimum(g + ring - 1, total - 1), (g + ring - 1) % ring).start(priority=1)
            d = ybuf.shape[2]
            x = xbuf[xslot, :, :d]
            row_gate = xbuf[xslot, :, d:d + 1]
            row_residual = xbuf[xslot, :, d + 1:d + 2]
            gu = jnp.dot(x.astype(BF), wgu_ref[...].astype(BF), preferred_element_type=F32)
            gate, up = gu[:, :EXPERT_FF], gu[:, EXPERT_FF:]
            hidden = (gate * jax.nn.sigmoid(gate) * up).astype(BF)
            y = jnp.dot(hidden, wd_ref[...].astype(BF), preferred_element_type=F32)
            ybuf[slot] = row_gate * y + row_residual * x
            out_copy(g, slot).start(priority=1)
            return carry

        lax.fori_loop(0, n_tiles, tile, 0)

    @pl.when(e == N_EXPERTS - 1)
    def _():
        for n in range(ring - 1):
            in_copy(0, (total + n) % ring).wait()
        for n in range(1, wring):
            for cp in weight_copies(0, (e + n) % wring):
                cp.wait()
        out_copy(0, 0).wait()
        out_copy(1, 1).wait()
        end = gs_ref[N_EXPERTS]
        n_tail = (ys_hbm.shape[0] - end) // tm
        ybuf[0] = jnp.zeros(ybuf.shape[1:], F32)

        def tail_copy(k):
            row0 = pl.multiple_of(end + k * tm, tm)
            return pltpu.make_async_copy(ybuf.at[0], ys_hbm.at[pl.ds(row0, tm), :], osem.at[0])

        def start_one(k, carry):
            tail_copy(k).start()
            return carry

        def wait_one(k, carry):
            tail_copy(k).wait()
            return carry

        lax.fori_loop(0, n_tail, start_one, 0)
        lax.fori_loop(0, n_tail, wait_one, 0)


EXPERT_WEIGHT_RING = 3
EXPERT_IN_RING = 12


def _experts(xs, starts, w_gate_up, w_down, layer, tm):
    n_rows, dx = xs.shape
    d = dx - LANES
    body = functools.partial(_experts_body, tm=tm, layer=layer)
    grid_spec = pltpu.PrefetchScalarGridSpec(
        num_scalar_prefetch=1,
        grid=(N_EXPERTS,),
        in_specs=[pl.BlockSpec(memory_space=pl.ANY), pl.BlockSpec(memory_space=pl.ANY),
                  pl.BlockSpec(memory_space=pl.ANY)],
        out_specs=pl.BlockSpec(memory_space=pl.ANY),
        scratch_shapes=[pltpu.VMEM((EXPERT_IN_RING, tm, dx), F32), pltpu.VMEM((2, tm, d), F32),
                        pltpu.VMEM((EXPERT_WEIGHT_RING, d, 2 * EXPERT_FF), F32),
                        pltpu.VMEM((EXPERT_WEIGHT_RING, EXPERT_FF, d), F32),
                        pltpu.SemaphoreType.DMA((EXPERT_IN_RING,)), pltpu.SemaphoreType.DMA((2,)),
                        pltpu.SemaphoreType.DMA((2, EXPERT_WEIGHT_RING))],
    )
    return pl.pallas_call(
        body,
        grid_spec=grid_spec,
        out_shape=jax.ShapeDtypeStruct((n_rows, d), F32),
        compiler_params=_params("arbitrary"),
        name="experts",
    )(starts, xs, w_gate_up, w_down)


def _combine_body(p0_ref, p1_ref, ys_hbm, g_ref, beta_ref, o_ref, buf0, buf1, sem, *, tc, n_steps):
    i = pl.program_id(0)
    ring = buf0.shape[0]

    def start_row(base, r, sl):
        pltpu.make_async_copy(ys_hbm.at[pl.ds(p0_ref[base + r], 1), :], buf0.at[sl, pl.ds(r, 1), :],
                              sem.at[sl]).start(priority=0)
        pltpu.make_async_copy(ys_hbm.at[pl.ds(p1_ref[base + r], 1), :], buf1.at[sl, pl.ds(r, 1), :],
                              sem.at[sl]).start(priority=1)

    def wait_rows(sl):
        pltpu.make_async_copy(ys_hbm.at[pl.ds(0, tc), :], buf0.at[sl], sem.at[sl]).wait()
        pltpu.make_async_copy(ys_hbm.at[pl.ds(0, tc), :], buf1.at[sl], sem.at[sl]).wait()

    @pl.when(i == 0)
    def _():
        for sl in range(ring):
            def one(r, carry):
                start_row(sl * tc, r, sl)
                return carry

            lax.fori_loop(0, tc, one, 0)

    nxt = jnp.minimum(i + 1, n_steps - 1) * ring * tc
    for sl in range(ring):
        wait_rows(sl)
        o_ref[sl * tc:(sl + 1) * tc, :] = _layer_norm(buf0[sl] + buf1[sl], g_ref[...], beta_ref[...])
        for r in range(tc):
            start_row(nxt + sl * tc, r, sl)

    @pl.when(i == n_steps - 1)
    def _():
        for sl in range(ring):
            wait_rows(sl)


COMBINE_RING = 4


def _combine_ln(ys, pos0, pos1, g, beta, tc):
    t, d = pos0.shape[0], ys.shape[1]
    n_steps = t // (COMBINE_RING * tc)
    body = functools.partial(_combine_body, tc=tc, n_steps=n_steps)
    row = lambda i, p0, p1: (i, 0)
    const = lambda i, p0, p1: (0, 0)
    grid_spec = pltpu.PrefetchScalarGridSpec(
        num_scalar_prefetch=2,
        grid=(n_steps,),
        in_specs=[pl.BlockSpec(memory_space=pl.ANY), pl.BlockSpec((1, d), const), pl.BlockSpec((1, d), const)],
        out_specs=pl.BlockSpec((COMBINE_RING * tc, d), row),
        scratch_shapes=[pltpu.VMEM((COMBINE_RING, tc, d), F32), pltpu.VMEM((COMBINE_RING, tc, d), F32),
                        pltpu.SemaphoreType.DMA((COMBINE_RING,))],
    )
    return pl.pallas_call(
        body,
        grid_spec=grid_spec,
        out_shape=jax.ShapeDtypeStruct((t, d), F32),
        compiler_params=_params("arbitrary"),
        name="combine_ln",
    )(pos0, pos1, ys, g, beta)


def _moe_ln(h, wr_perm, bias_perm, w_gate_up, w_down, layer, g, beta, tm_router=512, tm_expert=128, tt=512,
            tc=256):
    e01, gates = _router(h, wr_perm, bias_perm, tm_router)
    starts, ends, pos0, pos1 = _dispatch_plan(e01[:MOE_TOPK], tm_expert)
    xs = _dispatch(h, gates, starts, ends, pos0, pos1, tm_expert, tt)
    ys = _experts(xs, starts, w_gate_up, w_down, layer, tm_expert)
    return _combine_ln(ys, pos0, pos1, g, beta, tc)


def _swap_halves(w):
    half = w.shape[-1] // 2
    return jnp.concatenate([w[..., half:], w[..., :half]], axis=-1)


def _even_weights(w_in, w_q_up, w_kv_up, forget_bias):
    o_kv = MLA_Q_LORA
    o_kr = o_kv + MLA_KV_LORA
    o_fq = o_kr + MLA_ROPE
    hd = FOX_HEADS * FOX_HEAD_DIM
    o_fl = o_fq + 3 * hd
    k_r = w_in[:, o_kr:o_fq]
    f_l = jnp.pad(w_in[:, o_fl:], ((0, 0), (0, LANES - FOX_HEADS)))
    w_small = jnp.concatenate([w_in[:, :o_kr], k_r, _swap_halves(k_r), f_l], axis=1).astype(BF)
    w_fox = w_in[:, o_fq:o_fl].astype(BF)
    cs_fox = jnp.concatenate([jnp.full((1, hd), FOX_SCALE, F32), jnp.ones((1, 2 * hd), F32)], axis=1)
    wq = w_q_up.reshape(MLA_Q_LORA, MLA_HEADS, MLA_NOPE + MLA_ROPE)
    wqn = wq[:, :, :MLA_NOPE].reshape(MLA_Q_LORA, -1).astype(BF)
    pe = wq[:, :, MLA_NOPE:]
    wqp = jnp.concatenate([pe, _swap_halves(pe)], axis=-1).reshape(MLA_Q_LORA, -1).astype(BF)
    wkv = w_kv_up.reshape(MLA_KV_LORA, MLA_HEADS, MLA_NOPE + MLA_V)
    wk = wkv[:, :, :MLA_NOPE].reshape(MLA_KV_LORA, -1).astype(BF)
    wv = wkv[:, :, MLA_NOPE:].reshape(MLA_KV_LORA, -1).astype(BF)
    fb = jnp.pad(forget_bias.astype(F32), (0, LANES - FOX_HEADS)).reshape(1, LANES)
    return w_small, w_fox, cs_fox, wqn, wqp, wk, wv, fb


def _rope_table(seq):
    half = MLA_ROPE // 2
    inv_freq = ROPE_THETA ** (-np.arange(half, dtype=np.float64) / half)
    ang = np.arange(seq, dtype=np.float64)[:, None] * inv_freq[None, :]
    cos, sin = np.cos(ang), np.sin(ang)
    return jnp.asarray(np.concatenate([cos, cos, -sin, sin], axis=1), dtype=F32)


def _router_weights(w_router, router_bias):
    r = np.arange(N_EXPERTS)
    perm = (r % N_GROUPS) * EXPERTS_PER_GROUP + r // N_GROUPS
    wr = jnp.pad(w_router[:, perm], ((0, 0), (0, LANES - N_EXPERTS)))
    w_hi = wr.astype(BF)
    w_lo = (wr - w_hi.astype(F32)).astype(BF)
    return jnp.concatenate([w_hi, w_lo], axis=1), router_bias.astype(F32)[perm].reshape(N_EXPERTS, 1)


def _even_layer(h, batch, seq, w_in, q_norm, w_q_up, kv_norm, w_kv_up, forget_bias, w_out, g, beta):
    w_small, w_fox, cs_fox, wqn, wqp, wk, wv, fb = _even_weights(w_in, w_q_up, w_kv_up, forget_bias)
    fox_qkv = _mm(h, w_fox, cs_fox, BF, 2 * ROW_TILE, w_fox.shape[1] // 2)
    qn, qp, kn, v, kp, forget_logits = _mla_prep(h, w_small, _rope_table(seq), q_norm.reshape(1, -1),
                                                 kv_norm.reshape(1, -1), wqn, wqp, wk, wv, seq, ROW_TILE)
    q_bias, k_bias, edge = _fox_cum(forget_logits, fb, batch, seq, ROW_TILE)
    a = _mla_attention(qn, qp, kn, v, kp, batch, seq, ROW_TILE, HEAD_GROUP)
    bo = _fox_attention(fox_qkv, q_bias, k_bias, edge, batch, seq, ROW_TILE, HEAD_GROUP)
    return _outproj_ln(a, bo, w_out.astype(BF), h, g, beta, ROW_TILE)


def _odd_layer(h, batch, seq, w_in, sinks, w_out, g, beta):
    n_sq = SWA_Q_HEADS * SWA_HEAD_DIM
    n_skv = SWA_KV_HEADS * SWA_HEAD_DIM
    n_m = MOBA_HEADS * MOBA_HEAD_DIM
    o_mq = n_sq + 2 * n_skv
    w_big = jnp.concatenate([w_in[:, :n_sq], w_in[:, o_mq:], w_in[:, n_sq:o_mq]], axis=1).astype(BF)
    cs = jnp.concatenate([jnp.full((1, n_sq), SWA_SCALE, F32), jnp.full((1, n_m), MOBA_SCALE, F32),
                          jnp.ones((1, 2 * n_m + 2 * n_skv), F32)], axis=1)
    big = _mm(h, w_big, cs, BF, 2 * ROW_TILE, w_big.shape[1] // 2)
    blocks = lambda cols: cols // LANES
    k_mean = _moba_kmean(big, batch, seq, (n_sq + n_m) // n_m)
    slopes = jnp.asarray(LOG2E * 2.0 ** (-8.0 * np.arange(1, MOBA_HEADS + 1) / MOBA_HEADS), dtype=F32)
    c = _swa_attention(big, sinks.astype(F32), batch, seq, 0, blocks(n_sq + 3 * n_m), blocks(n_sq + 3 * n_m) + 1)
    dd = _moba_attention(big, k_mean, slopes, batch, seq, blocks(n_sq), blocks(n_sq + n_m), blocks(n_sq + 2 * n_m),
                         ROW_TILE, HEAD_GROUP)
    return _outproj_ln(c, dd, w_out.astype(BF), h, g, beta, ROW_TILE)


def kernel(x, w_router, router_bias, even_w_in, even_q_norm, even_w_q_up, even_kv_norm, even_w_kv_up,
           even_forget_bias, even_w_out, odd_w_in, odd_sinks, odd_w_out, ln_mix_g, ln_mix_b, ln_ffn_g, ln_ffn_b,
           w_gate_up, w_down):
    batch, seq, d = x.shape
    h = x.reshape(batch * seq, d)
    wr_perm, bias_perm = _router_weights(w_router, router_bias)
    row = lambda p, layer: p[layer].reshape(1, d)
    for layer in range(DEPTH):
        i = layer // 2
        if layer % 2 == 0:
            h = _even_layer(h, batch, seq, even_w_in[i], even_q_norm[i], even_w_q_up[i], even_kv_norm[i],
                            even_w_kv_up[i], even_forget_bias[i], even_w_out[i], row(ln_mix_g, layer),
                            row(ln_mix_b, layer))
        else:
            h = _odd_layer(h, batch, seq, odd_w_in[i], odd_sinks[i], odd_w_out[i], row(ln_mix_g, layer),
                           row(ln_mix_b, layer))
        h = _moe_ln(h, wr_perm, bias_perm, w_gate_up, w_down, layer, row(ln_ffn_g, layer), row(ln_ffn_b, layer))
    return h.reshape(batch, seq, d)
```
